```python
import math
import jax, jax.numpy as jnp
from jax import lax
import numpy as np

D_MODEL = 1024
BATCH = 8
SEQ = 2048
DEPTH = 4

N_MIXERS = 2
N_A_LAYERS = (DEPTH + N_MIXERS - 1) // N_MIXERS
N_B_LAYERS = DEPTH // N_MIXERS

SSM_EXPAND = 2
D_INNER = SSM_EXPAND * D_MODEL
HEAD_DIM = 64
N_HEADS = D_INNER // HEAD_DIM
N_GROUPS = 4
HEADS_PER_GROUP = N_HEADS // N_GROUPS
D_STATE = 128
SSM_CONV = 4
CONV_DIM = D_INNER + 2 * N_GROUPS * D_STATE
D_IN_PROJ = 2 * D_INNER + 2 * N_GROUPS * D_STATE + N_HEADS
CHUNK = 128
DT_MIN = 0.001
DT_MAX = 0.1

CONV_KERNEL = 31

D_FF = -(-8 * D_MODEL // (3 * 256)) * 256

EPS = 1e-5

kernel_name = "hybrid_ssd_conformer_trunk"


def rmsnorm(x, g):
    xf = x.astype(jnp.float32)
    y = xf * lax.rsqrt(jnp.mean(xf * xf, axis=-1, keepdims=True) + EPS)
    return (y * g).astype(x.dtype)


def layernorm(x, g, b):
    xf = x.astype(jnp.float32)
    mu = jnp.mean(xf, axis=-1, keepdims=True)
    xc = xf - mu
    var = jnp.mean(xc * xc, axis=-1, keepdims=True)
    return (xc * lax.rsqrt(var + EPS) * g + b).astype(x.dtype)


def gated_group_rmsnorm(y, z, g):
    h = (y * jax.nn.silu(z)).astype(jnp.float32)
    shp = h.shape
    h = h.reshape(shp[:-1] + (N_GROUPS, shp[-1] // N_GROUPS))
    h = h * lax.rsqrt(jnp.mean(h * h, axis=-1, keepdims=True) + EPS)
    return (h.reshape(shp) * g).astype(y.dtype)


def causal_depthwise_conv(x, w, b):
    k = w.shape[0]
    y = lax.conv_general_dilated(
        x, w[:, None, :], window_strides=(1,), padding=[(k - 1, 0)],
        dimension_numbers=("NWC", "WIO", "NWC"), feature_group_count=x.shape[-1])
    return y + b


def segsum(a):
    t = a.shape[-1]
    ar = jnp.broadcast_to(a[..., :, None], a.shape + (t,))
    ar = jnp.where(jnp.tril(jnp.ones((t, t), dtype=bool), -1), ar, 0.0)
    cs = jnp.cumsum(ar, axis=-2)
    return jnp.where(jnp.tril(jnp.ones((t, t), dtype=bool), 0), cs, -jnp.inf)


def ssd_chunked(x, da, bm, cm):
    b, l, h, p = x.shape
    c = l // CHUNK
    dt_ = x.dtype
    x = x.reshape(b, c, CHUNK, N_GROUPS, HEADS_PER_GROUP, p)
    bm = bm.reshape(b, c, CHUNK, N_GROUPS, D_STATE)
    cm = cm.reshape(b, c, CHUNK, N_GROUPS, D_STATE)
    a = da.reshape(b, c, CHUNK, N_GROUPS, HEADS_PER_GROUP).transpose(0, 3, 4, 1, 2)
    a_cs = jnp.cumsum(a, axis=-1)

    lmat = jnp.exp(segsum(a)).astype(dt_)
    cb = jnp.einsum("bclgn,bcsgn->bgcls", cm, bm)
    y_diag = jnp.einsum("bgrcls,bcsgrp->bclgrp", cb[:, :, None] * lmat, x)

    decay_states = jnp.exp(a_cs[..., -1:] - a_cs).astype(dt_)
    xd = x * decay_states.transpose(0, 3, 4, 1, 2)[..., None]
    states = jnp.einsum("bclgn,bclgrp->bcgrpn", bm, xd)

    chunk_tot = jnp.pad(a_cs[..., -1], ((0, 0), (0, 0), (0, 0), (1, 0)))
    decay_chunk = jnp.exp(segsum(chunk_tot))[..., :c, 1:].astype(dt_)
    states_in = jnp.einsum("bgrzj,bjgrpn->bzgrpn", decay_chunk, states)

    state_decay_out = jnp.exp(a_cs).astype(dt_).transpose(0, 3, 4, 1, 2)
    y_off = jnp.einsum("bclgn,bcgrpn->bclgrp", cm, states_in) * state_decay_out[..., None]
    return (y_diag + y_off).reshape(b, l, h, p)


def mamba2_mixer(u, w_in, conv_w, conv_b, dt_bias, a_log, d_skip, gate_norm, w_out):
    b, l, _ = u.shape
    zxbcdt = u @ w_in
    z, xbc, dt = jnp.split(zxbcdt, [D_INNER, D_INNER + CONV_DIM], axis=-1)
    xbc = jax.nn.silu(causal_depthwise_conv(xbc, conv_w, conv_b))
    xs, bm, cm = jnp.split(xbc, [D_INNER, D_INNER + N_GROUPS * D_STATE], axis=-1)
    xs = xs.reshape(b, l, N_HEADS, HEAD_DIM)
    bm = bm.reshape(b, l, N_GROUPS, D_STATE)
    cm = cm.reshape(b, l, N_GROUPS, D_STATE)
    dt = jax.nn.softplus(dt.astype(jnp.float32) + dt_bias.astype(jnp.float32))
    a = -jnp.exp(a_log.astype(jnp.float32))
    y = ssd_chunked(xs * dt[..., None].astype(xs.dtype), dt * a, bm, cm)
    y = y + xs * d_skip[:, None]
    y = gated_group_rmsnorm(y.reshape(b, l, D_INNER), z, gate_norm)
    return y @ w_out


def conformer_conv_module(u, w_pw1, b_pw1, dw_w, dw_b, ln_g, ln_b, w_pw2, b_pw2):
    h = u @ w_pw1 + b_pw1
    a, g = jnp.split(h, 2, axis=-1)
    h = a * jax.nn.sigmoid(g)
    h = causal_depthwise_conv(h, dw_w, dw_b)
    h = jax.nn.silu(layernorm(h, ln_g, ln_b))
    return h @ w_pw2 + b_pw2


def swiglu_ffn(u, w_gate, w_up, w_down):
    return (jax.nn.silu(u @ w_gate) * (u @ w_up)) @ w_down


def _fwd_setup_inputs(seed: int = 0) -> dict:
    key = jax.random.key(seed)
    ks = iter(jax.random.split(key, 40))

    def nrm(shape, scale):
        return jax.random.normal(next(ks), shape, jnp.float32) * scale

    na, nb = N_A_LAYERS, N_B_LAYERS
    x = nrm((BATCH, SEQ, D_MODEL), 1.0)

    u = jax.random.uniform(next(ks), (na, N_HEADS), jnp.float32)
    dt0 = jnp.exp(u * (math.log(DT_MAX) - math.log(DT_MIN)) + math.log(DT_MIN))
    dt0 = jnp.maximum(dt0, 1e-4)
    ssm_dt_bias = dt0 + jnp.log(-jnp.expm1(-dt0))
    ssm_a_log = jnp.log(jax.random.uniform(next(ks), (na, N_HEADS), jnp.float32, 1.0, 16.0))

    return {
        "x": x,
        "ssm_norm": 1.0 + nrm((na, D_MODEL), 0.05),
        "ssm_w_in": nrm((na, D_MODEL, D_IN_PROJ), D_MODEL ** -0.5),
        "ssm_conv_w": nrm((na, SSM_CONV, CONV_DIM), SSM_CONV ** -0.5),
        "ssm_conv_b": nrm((na, CONV_DIM), 0.02),
        "ssm_dt_bias": ssm_dt_bias,
        "ssm_a_log": ssm_a_log,
        "ssm_d": 1.0 + nrm((na, N_HEADS), 0.1),
        "ssm_gate_norm": 1.0 + nrm((na, D_INNER), 0.05),
        "ssm_w_out": nrm((na, D_INNER, D_MODEL), D_INNER ** -0.5),
        "cv_norm": 1.0 + nrm((nb, D_MODEL), 0.05),
        "cv_w_pw1": nrm((nb, D_MODEL, 2 * D_MODEL), D_MODEL ** -0.5),
        "cv_b_pw1": nrm((nb, 2 * D_MODEL), 0.02),
        "cv_dw_w": nrm((nb, CONV_KERNEL, D_MODEL), CONV_KERNEL ** -0.5),
        "cv_dw_b": nrm((nb, D_MODEL), 0.02),
        "cv_ln_g": 1.0 + nrm((nb, D_MODEL), 0.05),
        "cv_ln_b": nrm((nb, D_MODEL), 0.02),
        "cv_w_pw2": nrm((nb, D_MODEL, D_MODEL), D_MODEL ** -0.5),
        "cv_b_pw2": nrm((nb, D_MODEL), 0.02),
        "ffn_norm": 1.0 + nrm((DEPTH, D_MODEL), 0.05),
        "ffn_w_gate": nrm((DEPTH, D_MODEL, D_FF), D_MODEL ** -0.5),
        "ffn_w_up": nrm((DEPTH, D_MODEL, D_FF), D_MODEL ** -0.5),
        "ffn_w_down": nrm((DEPTH, D_FF, D_MODEL), D_FF ** -0.5),
        "final_norm": 1.0 + nrm((D_MODEL,), 0.05),
    }


def _fwd_reference(x, ssm_norm, ssm_w_in, ssm_conv_w, ssm_conv_b, ssm_dt_bias, ssm_a_log, ssm_d,
              ssm_gate_norm, ssm_w_out, cv_norm, cv_w_pw1, cv_b_pw1, cv_dw_w, cv_dw_b, cv_ln_g,
              cv_ln_b, cv_w_pw2, cv_b_pw2, ffn_norm, ffn_w_gate, ffn_w_up, ffn_w_down, final_norm):
    h = x
    for i in range(DEPTH):
        j = i // N_MIXERS
        if i % N_MIXERS == 0:
            h = h + mamba2_mixer(rmsnorm(h, ssm_norm[j]), ssm_w_in[j], ssm_conv_w[j], ssm_conv_b[j],
                                 ssm_dt_bias[j], ssm_a_log[j], ssm_d[j], ssm_gate_norm[j], ssm_w_out[j])
        else:
            h = h + conformer_conv_module(rmsnorm(h, cv_norm[j]), cv_w_pw1[j], cv_b_pw1[j], cv_dw_w[j],
                                          cv_dw_b[j], cv_ln_g[j], cv_ln_b[j], cv_w_pw2[j], cv_b_pw2[j])
        h = h + swiglu_ffn(rmsnorm(h, ffn_norm[i]), ffn_w_gate[i], ffn_w_up[i], ffn_w_down[i])
    return rmsnorm(h, final_norm)


import jax as _jax
import jax.numpy as _jnp

TWIN_FORMAT = 'train_step'
FWD_PARAMS = ['x', 'ssm_norm', 'ssm_w_in', 'ssm_conv_w', 'ssm_conv_b', 'ssm_dt_bias', 'ssm_a_log', 'ssm_d', 'ssm_gate_norm', 'ssm_w_out', 'cv_norm', 'cv_w_pw1', 'cv_b_pw1', 'cv_dw_w', 'cv_dw_b', 'cv_ln_g', 'cv_ln_b', 'cv_w_pw2', 'cv_b_pw2', 'ffn_norm', 'ffn_w_gate', 'ffn_w_up', 'ffn_w_down', 'final_norm']
TWIN_WEIGHTS = ['ssm_norm', 'ssm_w_in', 'ssm_conv_w', 'ssm_conv_b', 'ssm_dt_bias', 'ssm_a_log', 'ssm_d', 'ssm_gate_norm', 'ssm_w_out', 'cv_norm', 'cv_w_pw1', 'cv_b_pw1', 'cv_dw_w', 'cv_dw_b', 'cv_ln_g', 'cv_ln_b', 'cv_w_pw2', 'cv_b_pw2', 'ffn_norm', 'ffn_w_gate', 'ffn_w_up', 'ffn_w_down', 'final_norm']
TWIN_DIFF_INPUT = 'x'
TWIN_INPUTS = ['x', 'ssm_norm', 'ssm_w_in', 'ssm_conv_w', 'ssm_conv_b', 'ssm_dt_bias', 'ssm_a_log', 'ssm_d', 'ssm_gate_norm', 'ssm_w_out', 'cv_norm', 'cv_w_pw1', 'cv_b_pw1', 'cv_dw_w', 'cv_dw_b', 'cv_ln_g', 'cv_ln_b', 'cv_w_pw2', 'cv_b_pw2', 'ffn_norm', 'ffn_w_gate', 'ffn_w_up', 'ffn_w_down', 'final_norm', 'loss_target', 'm_ssm_norm', 'm_ssm_w_in', 'm_ssm_conv_w', 'm_ssm_conv_b', 'm_ssm_dt_bias', 'm_ssm_a_log', 'm_ssm_d', 'm_ssm_gate_norm', 'm_ssm_w_out', 'm_cv_norm', 'm_cv_w_pw1', 'm_cv_b_pw1', 'm_cv_dw_w', 'm_cv_dw_b', 'm_cv_ln_g', 'm_cv_ln_b', 'm_cv_w_pw2', 'm_cv_b_pw2', 'm_ffn_norm', 'm_ffn_w_gate', 'm_ffn_w_up', 'm_ffn_w_down', 'm_final_norm', 'v_ssm_norm', 'v_ssm_w_in', 'v_ssm_conv_w', 'v_ssm_conv_b', 'v_ssm_dt_bias', 'v_ssm_a_log', 'v_ssm_d', 'v_ssm_gate_norm', 'v_ssm_w_out', 'v_cv_norm', 'v_cv_w_pw1', 'v_cv_b_pw1', 'v_cv_dw_w', 'v_cv_dw_b', 'v_cv_ln_g', 'v_cv_ln_b', 'v_cv_w_pw2', 'v_cv_b_pw2', 'v_ffn_norm', 'v_ffn_w_gate', 'v_ffn_w_up', 'v_ffn_w_down', 'v_final_norm']
TWIN_OUTPUTS = ['loss', 'grad_x', 'grad_ssm_norm', 'grad_ssm_w_in', 'grad_ssm_conv_w', 'grad_ssm_conv_b', 'grad_ssm_dt_bias', 'grad_ssm_a_log', 'grad_ssm_d', 'grad_ssm_gate_norm', 'grad_ssm_w_out', 'grad_cv_norm', 'grad_cv_w_pw1', 'grad_cv_b_pw1', 'grad_cv_dw_w', 'grad_cv_dw_b', 'grad_cv_ln_g', 'grad_cv_ln_b', 'grad_cv_w_pw2', 'grad_cv_b_pw2', 'grad_ffn_norm', 'grad_ffn_w_gate', 'grad_ffn_w_up', 'grad_ffn_w_down', 'grad_final_norm', 'delta_ssm_norm', 'delta_ssm_w_in', 'delta_ssm_conv_w', 'delta_ssm_conv_b', 'delta_ssm_dt_bias', 'delta_ssm_a_log', 'delta_ssm_d', 'delta_ssm_gate_norm', 'delta_ssm_w_out', 'delta_cv_norm', 'delta_cv_w_pw1', 'delta_cv_b_pw1', 'delta_cv_dw_w', 'delta_cv_dw_b', 'delta_cv_ln_g', 'delta_cv_ln_b', 'delta_cv_w_pw2', 'delta_cv_b_pw2', 'delta_ffn_norm', 'delta_ffn_w_gate', 'delta_ffn_w_up', 'delta_ffn_w_down', 'delta_final_norm', 'new_m_ssm_norm', 'new_m_ssm_w_in', 'new_m_ssm_conv_w', 'new_m_ssm_conv_b', 'new_m_ssm_dt_bias', 'new_m_ssm_a_log', 'new_m_ssm_d', 'new_m_ssm_gate_norm', 'new_m_ssm_w_out', 'new_m_cv_norm', 'new_m_cv_w_pw1', 'new_m_cv_b_pw1', 'new_m_cv_dw_w', 'new_m_cv_dw_b', 'new_m_cv_ln_g', 'new_m_cv_ln_b', 'new_m_cv_w_pw2', 'new_m_cv_b_pw2', 'new_m_ffn_norm', 'new_m_ffn_w_gate', 'new_m_ffn_w_up', 'new_m_ffn_w_down', 'new_m_final_norm', 'new_v_ssm_norm', 'new_v_ssm_w_in', 'new_v_ssm_conv_w', 'new_v_ssm_conv_b', 'new_v_ssm_dt_bias', 'new_v_ssm_a_log', 'new_v_ssm_d', 'new_v_ssm_gate_norm', 'new_v_ssm_w_out', 'new_v_cv_norm', 'new_v_cv_w_pw1', 'new_v_cv_b_pw1', 'new_v_cv_dw_w', 'new_v_cv_dw_b', 'new_v_cv_ln_g', 'new_v_cv_ln_b', 'new_v_cv_w_pw2', 'new_v_cv_b_pw2', 'new_v_ffn_norm', 'new_v_ffn_w_gate', 'new_v_ffn_w_up', 'new_v_ffn_w_down', 'new_v_final_norm']
TWIN_LEAF_KINDS = {'loss': 'loss', 'grad_x': 'grad_x', 'grad_ssm_norm': 'grad_w', 'grad_ssm_w_in': 'grad_w', 'grad_ssm_conv_w': 'grad_w', 'grad_ssm_conv_b': 'grad_w', 'grad_ssm_dt_bias': 'grad_w', 'grad_ssm_a_log': 'grad_w', 'grad_ssm_d': 'grad_w', 'grad_ssm_gate_norm': 'grad_w', 'grad_ssm_w_out': 'grad_w', 'grad_cv_norm': 'grad_w', 'grad_cv_w_pw1': 'grad_w', 'grad_cv_b_pw1': 'grad_w', 'grad_cv_dw_w': 'grad_w', 'grad_cv_dw_b': 'grad_w', 'grad_cv_ln_g': 'grad_w', 'grad_cv_ln_b': 'grad_w', 'grad_cv_w_pw2': 'grad_w', 'grad_cv_b_pw2': 'grad_w', 'grad_ffn_norm': 'grad_w', 'grad_ffn_w_gate': 'grad_w', 'grad_ffn_w_up': 'grad_w', 'grad_ffn_w_down': 'grad_w', 'grad_final_norm': 'grad_w', 'delta_ssm_norm': 'delta_w', 'delta_ssm_w_in': 'delta_w', 'delta_ssm_conv_w': 'delta_w', 'delta_ssm_conv_b': 'delta_w', 'delta_ssm_dt_bias': 'delta_w', 'delta_ssm_a_log': 'delta_w', 'delta_ssm_d': 'delta_w', 'delta_ssm_gate_norm': 'delta_w', 'delta_ssm_w_out': 'delta_w', 'delta_cv_norm': 'delta_w', 'delta_cv_w_pw1': 'delta_w', 'delta_cv_b_pw1': 'delta_w', 'delta_cv_dw_w': 'delta_w', 'delta_cv_dw_b': 'delta_w', 'delta_cv_ln_g': 'delta_w', 'delta_cv_ln_b': 'delta_w', 'delta_cv_w_pw2': 'delta_w', 'delta_cv_b_pw2': 'delta_w', 'delta_ffn_norm': 'delta_w', 'delta_ffn_w_gate': 'delta_w', 'delta_ffn_w_up': 'delta_w', 'delta_ffn_w_down': 'delta_w', 'delta_final_norm': 'delta_w', 'new_m_ssm_norm': 'new_m', 'new_m_ssm_w_in': 'new_m', 'new_m_ssm_conv_w': 'new_m', 'new_m_ssm_conv_b': 'new_m', 'new_m_ssm_dt_bias': 'new_m', 'new_m_ssm_a_log': 'new_m', 'new_m_ssm_d': 'new_m', 'new_m_ssm_gate_norm': 'new_m', 'new_m_ssm_w_out': 'new_m', 'new_m_cv_norm': 'new_m', 'new_m_cv_w_pw1': 'new_m', 'new_m_cv_b_pw1': 'new_m', 'new_m_cv_dw_w': 'new_m', 'new_m_cv_dw_b': 'new_m', 'new_m_cv_ln_g': 'new_m', 'new_m_cv_ln_b': 'new_m', 'new_m_cv_w_pw2': 'new_m', 'new_m_cv_b_pw2': 'new_m', 'new_m_ffn_norm': 'new_m', 'new_m_ffn_w_gate': 'new_m', 'new_m_ffn_w_up': 'new_m', 'new_m_ffn_w_down': 'new_m', 'new_m_final_norm': 'new_m', 'new_v_ssm_norm': 'new_v', 'new_v_ssm_w_in': 'new_v', 'new_v_ssm_conv_w': 'new_v', 'new_v_ssm_conv_b': 'new_v', 'new_v_ssm_dt_bias': 'new_v', 'new_v_ssm_a_log': 'new_v', 'new_v_ssm_d': 'new_v', 'new_v_ssm_gate_norm': 'new_v', 'new_v_ssm_w_out': 'new_v', 'new_v_cv_norm': 'new_v', 'new_v_cv_w_pw1': 'new_v', 'new_v_cv_b_pw1': 'new_v', 'new_v_cv_dw_w': 'new_v', 'new_v_cv_dw_b': 'new_v', 'new_v_cv_ln_g': 'new_v', 'new_v_cv_ln_b': 'new_v', 'new_v_cv_w_pw2': 'new_v', 'new_v_cv_b_pw2': 'new_v', 'new_v_ffn_norm': 'new_v', 'new_v_ffn_w_gate': 'new_v', 'new_v_ffn_w_up': 'new_v', 'new_v_ffn_w_down': 'new_v', 'new_v_final_norm': 'new_v'}


def _forward(args):
    return _fwd_reference(*[args[k] for k in FWD_PARAMS])


def _output_shape():
    out = _jax.eval_shape(lambda: _forward(_fwd_setup_inputs(0)))
    return out.shape, out.dtype

N_MICROBATCH = 1
ADAM_LR = 0.001
ADAM_B1 = 0.9
ADAM_B2 = 0.999
ADAM_EPS = 1e-08
ADAM_WD = 0.01
ADAM_STEP = 10
PER_EXAMPLE_BATCH_AXIS = {'x': 0, 'loss_target': 0}
SHARED_INPUTS = []
_WEIGHT_DTYPES = {'ssm_norm': _jnp.float32, 'ssm_w_in': _jnp.float32, 'ssm_conv_w': _jnp.float32, 'ssm_conv_b': _jnp.float32, 'ssm_dt_bias': _jnp.float32, 'ssm_a_log': _jnp.float32, 'ssm_d': _jnp.float32, 'ssm_gate_norm': _jnp.float32, 'ssm_w_out': _jnp.float32, 'cv_norm': _jnp.float32, 'cv_w_pw1': _jnp.float32, 'cv_b_pw1': _jnp.float32, 'cv_dw_w': _jnp.float32, 'cv_dw_b': _jnp.float32, 'cv_ln_g': _jnp.float32, 'cv_ln_b': _jnp.float32, 'cv_w_pw2': _jnp.float32, 'cv_b_pw2': _jnp.float32, 'ffn_norm': _jnp.float32, 'ffn_w_gate': _jnp.float32, 'ffn_w_up': _jnp.float32, 'ffn_w_down': _jnp.float32, 'final_norm': _jnp.float32}
MOMENT_SCALE = {'ssm_norm': 1.552894e-01, 'ssm_w_in': 6.828164e-02, 'ssm_conv_w': 6.318761e-02, 'ssm_conv_b': 8.906048e-02, 'ssm_dt_bias': 1.462807e-01, 'ssm_a_log': 1.543763e-01, 'ssm_d': 3.973929e-01, 'ssm_gate_norm': 7.512682e-02, 'ssm_w_out': 1.045498e-01, 'cv_norm': 5.588510e-02, 'cv_w_pw1': 3.903073e-02, 'cv_b_pw1': 4.898551e-02, 'cv_dw_w': 5.159156e-02, 'cv_dw_b': 1.169500e-01, 'cv_ln_g': 6.716518e-02, 'cv_ln_b': 7.083991e-02, 'cv_w_pw2': 5.396566e-02, 'cv_b_pw2': 1.325053e-01, 'ffn_norm': 6.951752e-02, 'ffn_w_gate': 3.044402e-02, 'ffn_w_up': 2.960386e-02, 'ffn_w_down': 4.900945e-02, 'final_norm': 1.602930e+01}


def _to_microbatches(a, axis):
    t = _jnp.moveaxis(a, axis, 0)
    t = t.reshape((N_MICROBATCH, t.shape[0] // N_MICROBATCH) + t.shape[1:])
    return _jnp.moveaxis(t, 1, axis + 1)


def setup_inputs(seed: int = 0) -> dict:
    inp = _fwd_setup_inputs(seed)
    key = _jax.random.fold_in(_jax.random.key(seed), 7919)
    shape, _ = _output_shape()
    out = dict(inp)
    out["loss_target"] = _jax.random.normal(_jax.random.fold_in(key, 0), shape, _jnp.float32)
    for i, name in enumerate(TWIN_WEIGHTS):
        w = inp[name].astype(_jnp.float32)
        if MOMENT_SCALE is None:
            s = _jnp.sqrt(_jnp.mean(_jnp.square(w)) + 1e-30)
        else:
            s = MOMENT_SCALE[name]
        km, kv = _jax.random.split(_jax.random.fold_in(key, i + 1))
        out[name] = w
        out["m_" + name] = s * _jax.random.normal(km, w.shape, _jnp.float32)
        out["v_" + name] = (s * s) * _jax.random.uniform(kv, w.shape, _jnp.float32, 0.5, 1.5)
    if N_MICROBATCH > 1:
        for name, axis in PER_EXAMPLE_BATCH_AXIS.items():
            out[name] = _to_microbatches(out[name], axis)
    return {'x': out['x'], 'ssm_norm': out['ssm_norm'], 'ssm_w_in': out['ssm_w_in'], 'ssm_conv_w': out['ssm_conv_w'], 'ssm_conv_b': out['ssm_conv_b'], 'ssm_dt_bias': out['ssm_dt_bias'], 'ssm_a_log': out['ssm_a_log'], 'ssm_d': out['ssm_d'], 'ssm_gate_norm': out['ssm_gate_norm'], 'ssm_w_out': out['ssm_w_out'], 'cv_norm': out['cv_norm'], 'cv_w_pw1': out['cv_w_pw1'], 'cv_b_pw1': out['cv_b_pw1'], 'cv_dw_w': out['cv_dw_w'], 'cv_dw_b': out['cv_dw_b'], 'cv_ln_g': out['cv_ln_g'], 'cv_ln_b': out['cv_ln_b'], 'cv_w_pw2': out['cv_w_pw2'], 'cv_b_pw2': out['cv_b_pw2'], 'ffn_norm': out['ffn_norm'], 'ffn_w_gate': out['ffn_w_gate'], 'ffn_w_up': out['ffn_w_up'], 'ffn_w_down': out['ffn_w_down'], 'final_norm': out['final_norm'], 'loss_target': out['loss_target'], 'm_ssm_norm': out['m_ssm_norm'], 'm_ssm_w_in': out['m_ssm_w_in'], 'm_ssm_conv_w': out['m_ssm_conv_w'], 'm_ssm_conv_b': out['m_ssm_conv_b'], 'm_ssm_dt_bias': out['m_ssm_dt_bias'], 'm_ssm_a_log': out['m_ssm_a_log'], 'm_ssm_d': out['m_ssm_d'], 'm_ssm_gate_norm': out['m_ssm_gate_norm'], 'm_ssm_w_out': out['m_ssm_w_out'], 'm_cv_norm': out['m_cv_norm'], 'm_cv_w_pw1': out['m_cv_w_pw1'], 'm_cv_b_pw1': out['m_cv_b_pw1'], 'm_cv_dw_w': out['m_cv_dw_w'], 'm_cv_dw_b': out['m_cv_dw_b'], 'm_cv_ln_g': out['m_cv_ln_g'], 'm_cv_ln_b': out['m_cv_ln_b'], 'm_cv_w_pw2': out['m_cv_w_pw2'], 'm_cv_b_pw2': out['m_cv_b_pw2'], 'm_ffn_norm': out['m_ffn_norm'], 'm_ffn_w_gate': out['m_ffn_w_gate'], 'm_ffn_w_up': out['m_ffn_w_up'], 'm_ffn_w_down': out['m_ffn_w_down'], 'm_final_norm': out['m_final_norm'], 'v_ssm_norm': out['v_ssm_norm'], 'v_ssm_w_in': out['v_ssm_w_in'], 'v_ssm_conv_w': out['v_ssm_conv_w'], 'v_ssm_conv_b': out['v_ssm_conv_b'], 'v_ssm_dt_bias': out['v_ssm_dt_bias'], 'v_ssm_a_log': out['v_ssm_a_log'], 'v_ssm_d': out['v_ssm_d'], 'v_ssm_gate_norm': out['v_ssm_gate_norm'], 'v_ssm_w_out': out['v_ssm_w_out'], 'v_cv_norm': out['v_cv_norm'], 'v_cv_w_pw1': out['v_cv_w_pw1'], 'v_cv_b_pw1': out['v_cv_b_pw1'], 'v_cv_dw_w': out['v_cv_dw_w'], 'v_cv_dw_b': out['v_cv_dw_b'], 'v_cv_ln_g': out['v_cv_ln_g'], 'v_cv_ln_b': out['v_cv_ln_b'], 'v_cv_w_pw2': out['v_cv_w_pw2'], 'v_cv_b_pw2': out['v_cv_b_pw2'], 'v_ffn_norm': out['v_ffn_norm'], 'v_ffn_w_gate': out['v_ffn_w_gate'], 'v_ffn_w_up': out['v_ffn_w_up'], 'v_ffn_w_down': out['v_ffn_w_down'], 'v_final_norm': out['v_final_norm']}


def _loss(weights, diff, rest, loss_target):
    with _jax.named_scope("forward"):
        args = {**rest, TWIN_DIFF_INPUT: diff, **{k: w.astype(_WEIGHT_DTYPES[k]) for k, w in weights.items()}}
        y = _forward(args)
    with _jax.named_scope("loss_head"):
        err = _jnp.square(y.astype(_jnp.float32) - loss_target)
        return 0.5 * _jnp.sum(_jnp.mean(err, axis=-1)) if err.ndim else 0.5 * err


def _adamw(w, g, m, v):
    m = ADAM_B1 * m + (1.0 - ADAM_B1) * g
    v = ADAM_B2 * v + (1.0 - ADAM_B2) * _jnp.square(g)
    m_hat = m / (1.0 - ADAM_B1 ** ADAM_STEP)
    v_hat = v / (1.0 - ADAM_B2 ** ADAM_STEP)
    delta = -ADAM_LR * (m_hat / (_jnp.sqrt(v_hat) + ADAM_EPS) + ADAM_WD * w)
    return delta, m, v


def reference(x, ssm_norm, ssm_w_in, ssm_conv_w, ssm_conv_b, ssm_dt_bias, ssm_a_log, ssm_d, ssm_gate_norm, ssm_w_out, cv_norm, cv_w_pw1, cv_b_pw1, cv_dw_w, cv_dw_b, cv_ln_g, cv_ln_b, cv_w_pw2, cv_b_pw2, ffn_norm, ffn_w_gate, ffn_w_up, ffn_w_down, final_norm, loss_target, m_ssm_norm, m_ssm_w_in, m_ssm_conv_w, m_ssm_conv_b, m_ssm_dt_bias, m_ssm_a_log, m_ssm_d, m_ssm_gate_norm, m_ssm_w_out, m_cv_norm, m_cv_w_pw1, m_cv_b_pw1, m_cv_dw_w, m_cv_dw_b, m_cv_ln_g, m_cv_ln_b, m_cv_w_pw2, m_cv_b_pw2, m_ffn_norm, m_ffn_w_gate, m_ffn_w_up, m_ffn_w_down, m_final_norm, v_ssm_norm, v_ssm_w_in, v_ssm_conv_w, v_ssm_conv_b, v_ssm_dt_bias, v_ssm_a_log, v_ssm_d, v_ssm_gate_norm, v_ssm_w_out, v_cv_norm, v_cv_w_pw1, v_cv_b_pw1, v_cv_dw_w, v_cv_dw_b, v_cv_ln_g, v_cv_ln_b, v_cv_w_pw2, v_cv_b_pw2, v_ffn_norm, v_ffn_w_gate, v_ffn_w_up, v_ffn_w_down, v_final_norm):
    given = dict(x=x, ssm_norm=ssm_norm, ssm_w_in=ssm_w_in, ssm_conv_w=ssm_conv_w, ssm_conv_b=ssm_conv_b, ssm_dt_bias=ssm_dt_bias, ssm_a_log=ssm_a_log, ssm_d=ssm_d, ssm_gate_norm=ssm_gate_norm, ssm_w_out=ssm_w_out, cv_norm=cv_norm, cv_w_pw1=cv_w_pw1, cv_b_pw1=cv_b_pw1, cv_dw_w=cv_dw_w, cv_dw_b=cv_dw_b, cv_ln_g=cv_ln_g, cv_ln_b=cv_ln_b, cv_w_pw2=cv_w_pw2, cv_b_pw2=cv_b_pw2, ffn_norm=ffn_norm, ffn_w_gate=ffn_w_gate, ffn_w_up=ffn_w_up, ffn_w_down=ffn_w_down, final_norm=final_norm, loss_target=loss_target, m_ssm_norm=m_ssm_norm, m_ssm_w_in=m_ssm_w_in, m_ssm_conv_w=m_ssm_conv_w, m_ssm_conv_b=m_ssm_conv_b, m_ssm_dt_bias=m_ssm_dt_bias, m_ssm_a_log=m_ssm_a_log, m_ssm_d=m_ssm_d, m_ssm_gate_norm=m_ssm_gate_norm, m_ssm_w_out=m_ssm_w_out, m_cv_norm=m_cv_norm, m_cv_w_pw1=m_cv_w_pw1, m_cv_b_pw1=m_cv_b_pw1, m_cv_dw_w=m_cv_dw_w, m_cv_dw_b=m_cv_dw_b, m_cv_ln_g=m_cv_ln_g, m_cv_ln_b=m_cv_ln_b, m_cv_w_pw2=m_cv_w_pw2, m_cv_b_pw2=m_cv_b_pw2, m_ffn_norm=m_ffn_norm, m_ffn_w_gate=m_ffn_w_gate, m_ffn_w_up=m_ffn_w_up, m_ffn_w_down=m_ffn_w_down, m_final_norm=m_final_norm, v_ssm_norm=v_ssm_norm, v_ssm_w_in=v_ssm_w_in, v_ssm_conv_w=v_ssm_conv_w, v_ssm_conv_b=v_ssm_conv_b, v_ssm_dt_bias=v_ssm_dt_bias, v_ssm_a_log=v_ssm_a_log, v_ssm_d=v_ssm_d, v_ssm_gate_norm=v_ssm_gate_norm, v_ssm_w_out=v_ssm_w_out, v_cv_norm=v_cv_norm, v_cv_w_pw1=v_cv_w_pw1, v_cv_b_pw1=v_cv_b_pw1, v_cv_dw_w=v_cv_dw_w, v_cv_dw_b=v_cv_dw_b, v_cv_ln_g=v_cv_ln_g, v_cv_ln_b=v_cv_ln_b, v_cv_w_pw2=v_cv_w_pw2, v_cv_b_pw2=v_cv_b_pw2, v_ffn_norm=v_ffn_norm, v_ffn_w_gate=v_ffn_w_gate, v_ffn_w_up=v_ffn_w_up, v_ffn_w_down=v_ffn_w_down, v_final_norm=v_final_norm)
    weights = {n: given[n] for n in TWIN_WEIGHTS}
    shared = {n: given[n] for n in SHARED_INPUTS}
    per_example = {n: given[n] for n in ['x']}
    grad_fn = _jax.value_and_grad(_loss, argnums=(0, 1))

    def one_microbatch(ex, loss_target):
        ex = dict(ex)
        diff = ex.pop(TWIN_DIFF_INPUT)
        return grad_fn(weights, diff, {**shared, **ex}, loss_target)

    if N_MICROBATCH == 1:
        loss, (grad_w, grad_x) = one_microbatch(per_example, given["loss_target"])
    else:
        def body(carry, xs):
            loss_sum, grad_sum = carry
            l_k, (gw_k, gx_k) = one_microbatch(xs[0], xs[1])
            with _jax.named_scope("update"):
                return (loss_sum + l_k, _jax.tree.map(_jnp.add, grad_sum, gw_k)), gx_k

        init = (_jnp.zeros((), _jnp.float32), _jax.tree.map(_jnp.zeros_like, weights))
        (loss, grad_w), grad_x = _jax.lax.scan(body, init, (per_example, given["loss_target"]))
    with _jax.named_scope("update"):
        delta_w, new_m, new_v = {}, {}, {}
        for n in TWIN_WEIGHTS:
            delta_w[n], new_m[n], new_v[n] = _adamw(weights[n], grad_w[n], given["m_" + n], given["v_" + n])
    return (loss, grad_x, *[grad_w[n] for n in TWIN_WEIGHTS], *[delta_w[n] for n in TWIN_WEIGHTS],
            *[new_m[n] for n in TWIN_WEIGHTS], *[new_v[n] for n in TWIN_WEIGHTS])
```

```python
import functools
import math

import jax
import jax.numpy as jnp
from jax import lax
from jax.experimental import pallas as pl
from jax.experimental.pallas import tpu as pltpu

F32 = jnp.float32
BF16 = jnp.bfloat16

N_DEV = 8
T = 2048
D = 1024
DI = 2048
NH = 32
HD = 64
NG = 4
GW = DI // NG
DS = 128
CONVD = DI + 2 * NG * DS
DINP = 2 * DI + 2 * NG * DS + NH
DINP_PAD = 5376
CH = 128
NCH = T // CH
DFF = 2816
KSSM = 4
KCV = 31
EPS = 1e-5
LANES = 128
VMEM_LIMIT = 56 * 1024 * 1024

ADAM_LR = 0.001
ADAM_B1 = 0.9
ADAM_B2 = 0.999
ADAM_EPS = 1e-08
ADAM_WD = 0.01
ADAM_STEP = 10

MESH = pl.DeviceIdType.MESH
ANY = pl.BlockSpec(memory_space=pl.ANY)


def _pcall(body, **kw):
    return pl.pallas_call(body, **kw)


def _cparams(sem):
    return pltpu.CompilerParams(dimension_semantics=sem, vmem_limit_bytes=VMEM_LIMIT)


def _pick(n, cands):
    for c in cands:
        if n % c == 0:
            return c
    raise ValueError(f"no tile for {n}")


def _sigmoid(x):
    return 1.0 / (1.0 + jnp.exp(-x))


def _silu(x):
    return x * _sigmoid(x)


def _dsilu(x):
    s = _sigmoid(x)
    return s * (1.0 + x * (1.0 - s))


_DIMS = {"nn": (((1,), (0,)), ((), ())), "nt": (((1,), (1,)), ((), ())), "tn": (((0,), (0,)), ((), ()))}


def matmul(a, b, mode, *, name, bias=None, residual=None, out_dtype=F32):
    assert a.dtype == BF16 and b.dtype == BF16
    if mode == "nn":
        (M, K), (K2, N) = a.shape, b.shape
    elif mode == "nt":
        (M, K), (N, K2) = a.shape, b.shape
    else:
        (K, M), (K2, N) = a.shape, b.shape
    assert K == K2
    tm = _pick(M, (512, 256, 128))
    tn = _pick(N, (768, 512, 384, 256, 128))
    tk = _pick(K, (1024, 768, 512, 256, 128))
    nk = K // tk
    dims = _DIMS[mode]
    has_bias, has_res = bias is not None, residual is not None

    def body(*refs):
        a_ref, b_ref = refs[0], refs[1]
        pos = 2
        bias_ref = res_ref = None
        if has_bias:
            bias_ref = refs[pos]
            pos += 1
        if has_res:
            res_ref = refs[pos]
            pos += 1
        o_ref, acc = refs[pos], refs[pos + 1]
        k = pl.program_id(2)

        @pl.when(k == 0)
        def _():
            acc[...] = jnp.zeros_like(acc)

        acc[...] += lax.dot_general(a_ref[...], b_ref[...], dims, preferred_element_type=F32)

        @pl.when(k == nk - 1)
        def _():
            out = acc[...]
            if has_bias:
                out = out + bias_ref[...]
            if has_res:
                out = out + res_ref[...]
            o_ref[...] = out.astype(o_ref.dtype)

    if mode == "tn":
        a_spec = pl.BlockSpec((tk, tm), lambda i, j, k: (k, i))
    else:
        a_spec = pl.BlockSpec((tm, tk), lambda i, j, k: (i, k))
    if mode == "nt":
        b_spec = pl.BlockSpec((tn, tk), lambda i, j, k: (j, k))
    else:
        b_spec = pl.BlockSpec((tk, tn), lambda i, j, k: (k, j))
    in_specs, args = [a_spec, b_spec], [a, b]
    if has_bias:
        in_specs.append(pl.BlockSpec((1, tn), lambda i, j, k: (0, j)))
        args.append(bias.reshape(1, N).astype(F32))
    if has_res:
        in_specs.append(pl.BlockSpec((tm, tn), lambda i, j, k: (i, j)))
        args.append(residual)
    return _pcall(
        body, name=name, grid=(M // tm, N // tn, nk), in_specs=in_specs,
        out_specs=pl.BlockSpec((tm, tn), lambda i, j, k: (i, j)),
        out_shape=jax.ShapeDtypeStruct((M, N), out_dtype),
        scratch_shapes=[pltpu.VMEM((tm, tn), F32)],
        compiler_params=_cparams(("parallel", "parallel", "arbitrary")),
    )(*args)


def rowwise(fn, rows, bcasts, outs, accs=(), *, name, tm=256):
    n_rows, n_b, n_o, n_a = len(rows), len(bcasts), len(outs), len(accs)
    nt = T // tm

    def body(*refs):
        ins = [r[...] for r in refs[:n_rows + n_b]]
        res = fn(*ins)
        o_refs = refs[n_rows + n_b:n_rows + n_b + n_o]
        a_refs = refs[n_rows + n_b + n_o:]
        for r, v in zip(o_refs, res[:n_o]):
            r[...] = v.astype(r.dtype)
        if n_a:
            i = pl.program_id(0)

            @pl.when(i == 0)
            def _():
                for r in a_refs:
                    r[...] = jnp.zeros_like(r)

            for r, v in zip(a_refs, res[n_o:]):
                r[...] += v

    in_specs = [pl.BlockSpec((tm, w), functools.partial(lambda i, cb: (i, cb), cb=cb)) for (_, w, cb) in rows]
    in_specs += [pl.BlockSpec(b.shape, lambda i: (0, 0)) for b in bcasts]
    out_specs = [pl.BlockSpec((tm, w), lambda i: (i, 0)) for (w, _) in outs]
    out_specs += [pl.BlockSpec((1, w), lambda i: (0, 0)) for w in accs]
    out_shape = [jax.ShapeDtypeStruct((T, w), dt) for (w, dt) in outs]
    out_shape += [jax.ShapeDtypeStruct((1, w), F32) for w in accs]
    return _pcall(
        body, name=name, grid=(nt,), in_specs=in_specs, out_specs=out_specs, out_shape=out_shape,
        compiler_params=_cparams(("arbitrary",)),
    )(*[r[0] for r in rows], *bcasts)


def _full(a):
    return (a, a.shape[1], 0)


def _rsum(v):
    return jnp.sum(v, axis=0, keepdims=True)


def rms_fwd(h, g, name):
    def fn(x, g):
        r = lax.rsqrt(jnp.mean(x * x, axis=-1, keepdims=True) + EPS)
        return (x * r * g,)
    return rowwise(fn, [_full(h)], [g], [(D, BF16)], name=name)[0]


def rms_bwd(du, h, g, dres, name):
    def fn(du, x, dres, g):
        r = lax.rsqrt(jnp.mean(x * x, axis=-1, keepdims=True) + EPS)
        xh = x * r
        dxh = du * g
        dx = r * (dxh - xh * jnp.mean(dxh * xh, axis=-1, keepdims=True))
        dh = dres + dx
        return dh, dh, _rsum(du * xh)
    return rowwise(fn, [_full(du), _full(h), _full(dres)], [g], [(D, F32), (D, BF16)], [D], name=name)


def loss_head(h, g, tgt, name):
    def fn(x, tgt, g):
        r = lax.rsqrt(jnp.mean(x * x, axis=-1, keepdims=True) + EPS)
        xh = x * r
        err = xh * g - tgt
        lsum = jnp.sum(jnp.sum(err * err, axis=-1, keepdims=True), axis=0, keepdims=True) * (0.5 / D)
        dy = err * (1.0 / D)
        dxh = dy * g
        dx = r * (dxh - xh * jnp.mean(dxh * xh, axis=-1, keepdims=True))
        return dx, dx, _rsum(dy * xh), jnp.broadcast_to(lsum, (1, LANES))
    return rowwise(fn, [_full(h), _full(tgt)], [g], [(D, F32), (D, BF16)], [D, LANES], name=name)


def swiglu_fwd(gu, name):
    def fn(g, u):
        return (_silu(g) * u,)
    return rowwise(fn, [(gu, DFF, 0), (gu, DFF, 1)], [], [(DFF, BF16)], name=name)[0]


def swiglu_bwd(dact, gu, name):
    def fn(da, g, u):
        return (jnp.concatenate([da * u * _dsilu(g), da * _silu(g)], axis=1),)
    return rowwise(fn, [_full(dact), (gu, DFF, 0), (gu, DFF, 1)], [], [(2 * DFF, BF16)], name=name)[0]


def glu_fwd(hh, name):
    def fn(a, g):
        return (a * _sigmoid(g),)
    return rowwise(fn, [(hh, D, 0), (hh, D, 1)], [], [(D, F32)], name=name)[0]


def glu_bwd(dgl, hh, name):
    def fn(dgl, a, g):
        s = _sigmoid(g)
        dhh = jnp.concatenate([dgl * s, dgl * a * s * (1.0 - s)], axis=1)
        return dhh, _rsum(dhh)
    return rowwise(fn, [_full(dgl), (hh, D, 0), (hh, D, 1)], [], [(2 * D, BF16)], [2 * D], name=name)


def ln_silu_fwd(c2, g, b, name):
    def fn(x, g, b):
        mu = jnp.mean(x, axis=-1, keepdims=True)
        xc = x - mu
        r = lax.rsqrt(jnp.mean(xc * xc, axis=-1, keepdims=True) + EPS)
        return (_silu(xc * r * g + b),)
    return rowwise(fn, [_full(c2)], [g, b], [(D, BF16)], name=name)[0]


def ln_silu_bwd(ds, c2, dh, g, b, name):
    def fn(ds, x, dh, g, b):
        mu = jnp.mean(x, axis=-1, keepdims=True)
        xc = x - mu
        r = lax.rsqrt(jnp.mean(xc * xc, axis=-1, keepdims=True) + EPS)
        xh = xc * r
        dn = ds * _dsilu(xh * g + b)
        dxh = dn * g
        dx = r * (dxh - jnp.mean(dxh, axis=-1, keepdims=True) - xh * jnp.mean(dxh * xh, axis=-1, keepdims=True))
        return dx, _rsum(dn * xh), _rsum(dn), _rsum(dh)
    return rowwise(fn, [_full(ds), _full(c2), _full(dh)], [g, b], [(D, F32)], [D, D, D], name=name)


def gatenorm_fwd(y, zx, gn, name):
    def fn(y, z, gn):
        hg = y * _silu(z)
        parts = []
        for k in range(NG):
            hk = hg[:, k * GW:(k + 1) * GW]
            parts.append(hk * lax.rsqrt(jnp.mean(hk * hk, axis=-1, keepdims=True) + EPS))
        return (jnp.concatenate(parts, axis=1) * gn,)
    return rowwise(fn, [_full(y), (zx, DI, 0)], [gn], [(DI, BF16)], name=name)[0]


def gatenorm_bwd(dyn, y, zx, gn, name):
    def fn(dyn, y, z, gn):
        sz = _silu(z)
        hg = y * sz
        dxh = dyn * gn
        dhg, xhs = [], []
        for k in range(NG):
            sl = slice(k * GW, (k + 1) * GW)
            hk = hg[:, sl]
            r = lax.rsqrt(jnp.mean(hk * hk, axis=-1, keepdims=True) + EPS)
            xh = hk * r
            dk = dxh[:, sl]
            dhg.append(r * (dk - xh * jnp.mean(dk * xh, axis=-1, keepdims=True)))
            xhs.append(xh)
        dhg = jnp.concatenate(dhg, axis=1)
        xh = jnp.concatenate(xhs, axis=1)
        return dhg * sz, dhg * y * _dsilu(z), _rsum(dyn * xh)
    return rowwise(fn, [_full(dyn), _full(y), (zx, DI, 0)], [gn], [(DI, F32), (DI, BF16)], [DI], name=name)


def _softplus(x):
    return jnp.maximum(x, 0.0) + jnp.log(1.0 + jnp.exp(-jnp.abs(x)))


def dt_fwd(zx, dt_bias, a_log, name):
    def fn(raw, bias, a_log):
        dt = _softplus(raw + bias)
        return dt, dt * (-jnp.exp(a_log))
    return rowwise(fn, [(zx, LANES, (2 * DI + 2 * NG * DS) // LANES)], [dt_bias, a_log], [(LANES, F32), (LANES, F32)], name=name)


def dt_bwd(ddt, dda, dt, zx, dt_bias, a_log, name):
    def fn(ddt, dda, dt, raw, bias, a_log):
        a = -jnp.exp(a_log)
        draw = (ddt + dda * a) * _sigmoid(raw + bias)
        return draw, _rsum(draw), _rsum(dda * dt) * a
    return rowwise(fn, [_full(ddt), _full(dda), _full(dt), (zx, LANES, (2 * DI + 2 * NG * DS) // LANES)],
                   [dt_bias, a_log], [(LANES, BF16)], [LANES, LANES], name=name)


def headsum(v, name):
    def body(v_ref, o_ref):
        o_ref[...] = jnp.sum(v_ref[...], axis=1, keepdims=True)
    return _pcall(body, name=name, out_shape=jax.ShapeDtypeStruct((v.shape[0], 1), F32))(v)


CONV_ROWS = 256


def _shifted(win, o, rows):
    if o == 0:
        return win[0:rows]
    n = win.shape[0]
    return pltpu.roll(win, shift=n - o, axis=0)[0:rows]


def dwconv_fwd(x, x_cb0, w, b, K, ct, act, name):
    C = w.shape[1]
    pad = 8 if K <= 8 else 32
    KP = w.shape[0]
    n_out = 2 if act else 1

    def body(x_ref, w_ref, b_ref, *rest):
        o_refs, px = rest[:n_out], rest[n_out]
        px[0:pad, :] = jnp.zeros((pad, ct), F32)
        px[pad:pad + T, :] = x_ref[...]
        wv = w_ref[...]
        bv = b_ref[...]
        for r0 in range(0, T, CONV_ROWS):
            win = px[r0:r0 + CONV_ROWS + pad, :]
            acc = jnp.broadcast_to(bv, (CONV_ROWS, ct))
            for k in range(K):
                acc = acc + wv[k:k + 1, :] * _shifted(win, pad - (K - 1) + k, CONV_ROWS)
            o_refs[0][r0:r0 + CONV_ROWS, :] = acc
            if act:
                o_refs[1][r0:r0 + CONV_ROWS, :] = _silu(acc)

    return _pcall(
        body, name=name, grid=(C // ct,),
        in_specs=[pl.BlockSpec((T, ct), lambda j: (0, x_cb0 + j)), pl.BlockSpec((KP, ct), lambda j: (0, j)),
                  pl.BlockSpec((1, ct), lambda j: (0, j))],
        out_specs=[pl.BlockSpec((T, ct), lambda j: (0, j))] * n_out,
        out_shape=[jax.ShapeDtypeStruct((T, C), F32)] * n_out,
        scratch_shapes=[pltpu.VMEM((T + pad, ct), F32)],
        compiler_params=_cparams(("parallel",)),
    )(x, w, b)


def dwconv_bwd(dout, cpre, x, x_cb0, w, K, ct, act, out_dtype, name):
    C = w.shape[1]
    pad = 8 if K <= 8 else 32
    KP = w.shape[0]

    def body(*refs):
        if act:
            d_ref, c_ref, x_ref, w_ref, dx_ref, dw_ref, db_ref, px, pd = refs
        else:
            d_ref, x_ref, w_ref, dx_ref, dw_ref, db_ref, px, pd = refs
        px[0:pad, :] = jnp.zeros((pad, ct), F32)
        px[pad:pad + T, :] = x_ref[...]
        pd[T:T + pad, :] = jnp.zeros((pad, ct), F32)
        if act:
            pd[0:T, :] = d_ref[...] * _dsilu(c_ref[...])
        else:
            pd[0:T, :] = d_ref[...]
        wv = w_ref[...]
        dws = [jnp.zeros((1, ct), F32) for _ in range(K)]
        db = jnp.zeros((1, ct), F32)
        for r0 in range(0, T, CONV_ROWS):
            dwin = pd[r0:r0 + CONV_ROWS + pad, :]
            xwin = px[r0:r0 + CONV_ROWS + pad, :]
            dc = dwin[0:CONV_ROWS]
            db = db + _rsum(dc)
            acc = jnp.zeros((CONV_ROWS, ct), F32)
            for k in range(K):
                acc = acc + wv[k:k + 1, :] * _shifted(dwin, K - 1 - k, CONV_ROWS)
                dws[k] = dws[k] + _rsum(dc * _shifted(xwin, pad - (K - 1) + k, CONV_ROWS))
            dx_ref[r0:r0 + CONV_ROWS, :] = acc.astype(dx_ref.dtype)
        dw_ref[...] = jnp.zeros((KP, ct), F32)
        for k in range(K):
            dw_ref[k:k + 1, :] = dws[k]
        db_ref[...] = db

    col = pl.BlockSpec((T, ct), lambda j: (0, j))
    in_specs = [col] + ([col] if act else []) + [pl.BlockSpec((T, ct), lambda j: (0, x_cb0 + j)),
                                                 pl.BlockSpec((KP, ct), lambda j: (0, j))]
    args = [dout] + ([cpre] if act else []) + [x, w]
    return _pcall(
        body, name=name, grid=(C // ct,), in_specs=in_specs,
        out_specs=[col, pl.BlockSpec((KP, ct), lambda j: (0, j)), pl.BlockSpec((1, ct), lambda j: (0, j))],
        out_shape=[jax.ShapeDtypeStruct((T, C), out_dtype), jax.ShapeDtypeStruct((KP, C), F32),
                   jax.ShapeDtypeStruct((1, C), F32)],
        scratch_shapes=[pltpu.VMEM((T + pad, ct), F32), pltpu.VMEM((T + pad, ct), F32)],
        compiler_params=_cparams(("parallel",)),
    )(*args)


def _scan(a, axis, reverse=False):
    n = a.shape[axis]
    idx = lax.broadcasted_iota(jnp.int32, a.shape, axis)
    s = 1
    while s < n:
        if reverse:
            a = a + jnp.where(idx < n - s, pltpu.roll(a, shift=n - s, axis=axis), 0.0)
        else:
            a = a + jnp.where(idx >= s, pltpu.roll(a, shift=s, axis=axis), 0.0)
        s *= 2
    return a


_NT = _DIMS["nt"]
_TN = _DIMS["tn"]


def _dot(a, b, dims=_DIMS["nn"]):
    return lax.dot_general(a, b, dims, preferred_element_type=F32)


def ssd_fwd(xbc, dtx, dax, daT, dfull, name):
    def body(xbc_ref, dtx_ref, dax_ref, daT_ref, df_ref, y_ref, st_ref, S):
        ci = pl.program_id(0)

        @pl.when(ci == 0)
        def _():
            S[...] = jnp.zeros_like(S)

        row = lax.broadcasted_iota(jnp.int32, (CH, CH), 0)
        lane = lax.broadcasted_iota(jnp.int32, (CH, CH), 1)
        acsT = _scan(daT_ref[...], 1)
        for g in range(NG):
            c0 = g * GW
            xs = xbc_ref[:, c0:c0 + GW]
            acs = _scan(dax_ref[:, c0:c0 + GW], 0)
            Bm = xbc_ref[:, DI + g * DS:DI + (g + 1) * DS].astype(BF16)
            Cm = xbc_ref[:, DI + NG * DS + g * DS:DI + NG * DS + (g + 1) * DS].astype(BF16)
            xdt = xs * dtx_ref[:, c0:c0 + GW]
            atot = acs[CH - 1:CH, :]
            Sg = S[:, c0:c0 + GW]
            st_ref[:, c0:c0 + GW] = Sg
            CB = _dot(Cm, Bm, _NT)
            yg = jnp.exp(acs) * _dot(Cm, Sg.astype(BF16)) + xs * df_ref[:, c0:c0 + GW]
            xd = (xdt * jnp.exp(atot - acs)).astype(BF16)
            S[:, c0:c0 + GW] = jnp.exp(atot) * Sg + _dot(Bm, xd, _TN)
            xdt_b = xdt.astype(BF16)
            for r in range(NH // NG):
                h = g * (NH // NG) + r
                hs = slice(r * HD, (r + 1) * HD)
                seg = acs[:, r * HD:r * HD + 1] - acsT[h:h + 1, :]
                Lm = jnp.where(row >= lane, jnp.exp(jnp.minimum(seg, 0.0)), 0.0)
                yd = _dot((CB * Lm).astype(BF16), xdt_b[:, hs])
                y_ref[:, c0 + r * HD:c0 + (r + 1) * HD] = yg[:, hs] + yd

    return _pcall(
        body, name=name, grid=(NCH,),
        in_specs=[pl.BlockSpec((CH, CONVD), lambda i: (i, 0)), pl.BlockSpec((CH, DI), lambda i: (i, 0)),
                  pl.BlockSpec((CH, DI), lambda i: (i, 0)), pl.BlockSpec((NH, CH), lambda i: (0, i)),
                  pl.BlockSpec((1, DI), lambda i: (0, 0))],
        out_specs=[pl.BlockSpec((CH, DI), lambda i: (i, 0)), pl.BlockSpec((None, DS, DI), lambda i: (i, 0, 0))],
        out_shape=[jax.ShapeDtypeStruct((T, DI), F32), jax.ShapeDtypeStruct((NCH, DS, DI), F32)],
        scratch_shapes=[pltpu.VMEM((DS, DI), F32)],
        compiler_params=_cparams(("arbitrary",)),
    )(xbc, dtx, dax, daT, dfull)


def ssd_bwd(dy, xbc, dtx, dax, daT, dfull, states, name):
    def body(dy_ref, xbc_ref, dtx_ref, dax_ref, daT_ref, df_ref, st_ref, dxbc_ref, ddt_ref, dda_ref, dD_ref, dS):
        i = pl.program_id(0)

        @pl.when(i == 0)
        def _():
            dS[...] = jnp.zeros_like(dS)
            dD_ref[...] = jnp.zeros_like(dD_ref)

        row = lax.broadcasted_iota(jnp.int32, (CH, CH), 0)
        lane = lax.broadcasted_iota(jnp.int32, (CH, CH), 1)
        rowc = lax.broadcasted_iota(jnp.int32, (CH, 1), 0)
        acsT = _scan(daT_ref[...], 1)
        ddt_all = jnp.zeros((CH, LANES), F32)
        dacs_all = jnp.zeros((CH, LANES), F32)
        for g in range(NG):
            c0 = g * GW
            xs = xbc_ref[:, c0:c0 + GW]
            dtx = dtx_ref[:, c0:c0 + GW]
            acs = _scan(dax_ref[:, c0:c0 + GW], 0)
            Bm = xbc_ref[:, DI + g * DS:DI + (g + 1) * DS].astype(BF16)
            Cm = xbc_ref[:, DI + NG * DS + g * DS:DI + NG * DS + (g + 1) * DS].astype(BF16)
            xdt = xs * dtx
            atot = acs[CH - 1:CH, :]
            Sin = st_ref[:, c0:c0 + GW]
            dyg = dy_ref[:, c0:c0 + GW]
            dSo = dS[:, c0:c0 + GW]
            E = jnp.exp(acs)
            Etot = jnp.exp(atot)
            dec = jnp.exp(atot - acs)
            dD_ref[:, c0:c0 + GW] += _rsum(dyg * xs)
            dxs = dyg * df_ref[:, c0:c0 + GW]
            Sin_b = Sin.astype(BF16)
            dSo_b = dSo.astype(BF16)
            dY0 = dyg * E
            dY0_b = dY0.astype(BF16)
            dC = _dot(dY0_b, Sin_b, _NT)
            dS[:, c0:c0 + GW] = _dot(Cm, dY0_b, _TN) + Etot * dSo
            XD = xdt * dec
            dXD = _dot(Bm, dSo_b)
            dB = _dot(XD.astype(BF16), dSo_b, _NT)
            dxdt = dXD * dec
            Gq = dXD * XD
            dacs_x = dY0 * _dot(Cm, Sin_b) - Gq
            datot_x = _rsum(dSo * Sin) * Etot + _rsum(Gq)
            CB = _dot(Cm, Bm, _NT)
            CBT = _dot(Bm, Cm, _NT)
            dCB = jnp.zeros((CH, CH), F32)
            xdt_b = xdt.astype(BF16)
            dy_b = dyg.astype(BF16)
            for r in range(NH // NG):
                h = g * (NH // NG) + r
                hs = slice(r * HD, (r + 1) * HD)
                seg = acs[:, r * HD:r * HD + 1] - acsT[h:h + 1, :]
                Lm = jnp.where(row >= lane, jnp.exp(jnp.minimum(seg, 0.0)), 0.0)
                LmT = jnp.where(lane >= row, jnp.exp(jnp.minimum(-seg, 0.0)), 0.0)
                xr = xdt_b[:, hs]
                dyr = dy_b[:, hs]
                dM = _dot(dyr, xr, _NT)
                dMT = _dot(xr, dyr, _NT)
                dxdt_r = dxdt[:, hs] + _dot((CB * Lm).astype(BF16), dyr, _TN)
                dCB = dCB + dM * Lm
                dcol = (jnp.sum(dM * CB * Lm, axis=1, keepdims=True) - jnp.sum(dMT * CBT * LmT, axis=1, keepdims=True)
                        + jnp.sum(dacs_x[:, hs], axis=1, keepdims=True))
                dtot = jnp.sum(datot_x[:, hs], axis=1, keepdims=True)
                dcol = dcol + jnp.where(rowc == CH - 1, dtot, 0.0)
                ddt_col = jnp.sum(dxdt_r * xs[:, hs], axis=1, keepdims=True)
                ddt_all = ddt_all + jnp.where(lane == h, ddt_col, 0.0)
                dacs_all = dacs_all + jnp.where(lane == h, dcol, 0.0)
                dxbc_ref[:, c0 + r * HD:c0 + (r + 1) * HD] = dxs[:, hs] + dxdt_r * dtx[:, hs]
            dCB_b = dCB.astype(BF16)
            dxbc_ref[:, DI + g * DS:DI + (g + 1) * DS] = dB + _dot(dCB_b, Cm, _TN)
            dxbc_ref[:, DI + NG * DS + g * DS:DI + NG * DS + (g + 1) * DS] = dC + _dot(dCB_b, Bm)
        ddt_ref[...] = ddt_all
        dda_ref[...] = _scan(dacs_all, 0, reverse=True)

    last = NCH - 1
    return _pcall(
        body, name=name, grid=(NCH,),
        in_specs=[pl.BlockSpec((CH, DI), lambda i: (last - i, 0)), pl.BlockSpec((CH, CONVD), lambda i: (last - i, 0)),
                  pl.BlockSpec((CH, DI), lambda i: (last - i, 0)), pl.BlockSpec((CH, DI), lambda i: (last - i, 0)),
                  pl.BlockSpec((NH, CH), lambda i: (0, last - i)), pl.BlockSpec((1, DI), lambda i: (0, 0)),
                  pl.BlockSpec((None, DS, DI), lambda i: (last - i, 0, 0))],
        out_specs=[pl.BlockSpec((CH, CONVD), lambda i: (last - i, 0)), pl.BlockSpec((CH, LANES), lambda i: (last - i, 0)),
                   pl.BlockSpec((CH, LANES), lambda i: (last - i, 0)), pl.BlockSpec((1, DI), lambda i: (0, 0))],
        out_shape=[jax.ShapeDtypeStruct((T, CONVD), F32), jax.ShapeDtypeStruct((T, LANES), F32),
                   jax.ShapeDtypeStruct((T, LANES), F32), jax.ShapeDtypeStruct((1, DI), F32)],
        scratch_shapes=[pltpu.VMEM((DS, DI), F32)],
        compiler_params=_cparams(("arbitrary",)),
    )(dy, xbc, dtx, dax, daT, dfull, states)


def _as3d(shape):
    if len(shape) == 1:
        return (1, 1, shape[0])
    if len(shape) == 2:
        return (1, shape[0], shape[1])
    return (math.prod(shape[:-2]), shape[-2], shape[-1])


def _row_tile(R, C):
    if R * C <= 512 * 1024:
        return R
    return _pick(R, (512, 256, 128, 64, 32, 16, 8))


def adamw(parts, w, m, v, name):
    shape = w.shape
    L, R, C = _as3d(shape)
    P = parts.shape[0]
    tr = _row_tile(R, C)
    bc1 = 1.0 - ADAM_B1 ** ADAM_STEP
    bc2 = 1.0 - ADAM_B2 ** ADAM_STEP

    def body(p_ref, w_ref, m_ref, v_ref, g_out, d_out, m_out, v_out):
        g = p_ref[0].astype(F32)
        for k in range(1, P):
            g = g + p_ref[k].astype(F32)
        mn = ADAM_B1 * m_ref[...] + (1.0 - ADAM_B1) * g
        vn = ADAM_B2 * v_ref[...] + (1.0 - ADAM_B2) * (g * g)
        g_out[...] = g
        m_out[...] = mn
        v_out[...] = vn
        d_out[...] = -ADAM_LR * ((mn / bc1) / (jnp.sqrt(vn / bc2) + ADAM_EPS) + ADAM_WD * w_ref[...])

    blk = pl.BlockSpec((None, tr, C), lambda l, r: (l, r, 0))
    outs = _pcall(
        body, name=name, grid=(L, R // tr),
        in_specs=[pl.BlockSpec((P, None, tr, C), lambda l, r: (0, l, r, 0)), blk, blk, blk],
        out_specs=[blk] * 4, out_shape=[jax.ShapeDtypeStruct((L, R, C), F32)] * 4,
        compiler_params=_cparams(("parallel", "parallel")),
    )(parts.reshape(P, L, R, C), w.reshape(L, R, C), m.reshape(L, R, C), v.reshape(L, R, C))
    return [o.reshape(shape) for o in outs]


def sum_leading(parts, name):
    P, R, C = parts.shape

    def body(p_ref, o_ref):
        s = p_ref[0]
        for k in range(1, P):
            s = s + p_ref[k]
        o_ref[...] = s

    return _pcall(body, name=name, out_shape=jax.ShapeDtypeStruct((R, C), F32))(parts)


def pair_sum(gsend, recv, name):
    S = gsend.shape[1:]
    L, R, C = _as3d(S)
    tr = _row_tile(R, C)

    def body(g_ref, r_ref, o_ref):
        o_ref[...] = (g_ref[...].astype(F32) + r_ref[...].astype(F32)).astype(o_ref.dtype)

    blk = pl.BlockSpec((None, None, tr, C), lambda q, l, r: (q, l, r, 0))
    own = pl.BlockSpec((None, None, tr, C), lambda q, l, r: (2 * q + lax.axis_index("c"), l, r, 0))
    out = _pcall(
        body, name=name, grid=(4, L, R // tr), in_specs=[own, blk], out_specs=blk,
        out_shape=jax.ShapeDtypeStruct((4, L, R, C), BF16),
        compiler_params=_cparams(("parallel", "parallel", "parallel")),
    )(gsend.reshape(8, L, R, C), recv.reshape(4, L, R, C))
    return out.reshape((4,) + S)


def _place():
    return lax.axis_index("x"), lax.axis_index("y"), lax.axis_index("c")


def _other_chips(x, y):
    return [(1 - x, y), (x, 1 - y), (1 - x, 1 - y)]


def all_gather(arrs, name):
    n = len(arrs)

    def body(*refs):
        ins, outs = refs[:n], refs[n:2 * n]
        send_sems, recv_sems, local_sems = refs[2 * n:]
        x, y, c = _place()
        me, sibling = (x, y, c), (x, y, 1 - c)
        chips = _other_chips(x, y)

        def slot(a, px, py, pc):
            return outs[a].at[4 * px + 2 * py + pc]

        def copy(a, k, block, to, src=None):
            return pltpu.make_async_remote_copy(
                src_ref=slot(a, *block) if src is None else src, dst_ref=slot(a, *block),
                send_sem=send_sems.at[a, k], recv_sem=recv_sems.at[a, k], device_id=to, device_id_type=MESH)

        mine, first, passed = [], [], []
        for a in range(n):
            cp = pltpu.make_async_copy(ins[a], slot(a, *me), local_sems.at[a])
            cp.start()
            mine.append(cp)
            first.append(copy(a, 0, me, sibling, src=ins[a]))
            first += [copy(a, 1 + j, me, (*chip, c), src=ins[a]) for j, chip in enumerate(chips)]
        for cp in first:
            cp.start()
        for j, chip in enumerate(chips):
            for a in range(n):
                copy(a, 1 + j, (*chip, c), me).wait_recv()
                cp = copy(a, 4 + j, (*chip, c), sibling)
                cp.start()
                passed.append(cp)
        for a in range(n):
            copy(a, 0, sibling, me).wait_recv()
            for j, chip in enumerate(chips):
                copy(a, 4 + j, (*chip, 1 - c), me).wait_recv()
        for cp in first + passed:
            cp.wait_send()
        for cp in mine:
            cp.wait()

    return _pcall(
        body, name=name, in_specs=[ANY] * n, out_specs=[ANY] * n,
        out_shape=[jax.ShapeDtypeStruct((N_DEV,) + a.shape, a.dtype) for a in arrs],
        scratch_shapes=[pltpu.SemaphoreType.DMA((n, 7)), pltpu.SemaphoreType.DMA((n, 7)), pltpu.SemaphoreType.DMA((n,))],
    )(*arrs)


def sibling_exchange(gsends, name):
    n = len(gsends)

    def body(*refs):
        ins, outs = refs[:n], refs[n:2 * n]
        send_sems, recv_sems = refs[2 * n:]
        x, y, c = _place()
        copies = []
        for a in range(n):
            for q in range(4):
                cp = pltpu.make_async_remote_copy(
                    src_ref=ins[a].at[2 * q + 1 - c], dst_ref=outs[a].at[q],
                    send_sem=send_sems.at[a, q], recv_sem=recv_sems.at[a, q],
                    device_id=(x, y, 1 - c), device_id_type=MESH)
                cp.start()
                copies.append(cp)
        for cp in copies:
            cp.wait()

    return _pcall(
        body, name=name, in_specs=[ANY] * n, out_specs=[ANY] * n,
        out_shape=[jax.ShapeDtypeStruct((4,) + g.shape[1:], g.dtype) for g in gsends],
        scratch_shapes=[pltpu.SemaphoreType.DMA((n, 4)), pltpu.SemaphoreType.DMA((n, 4))],
    )(*gsends)


def chip_exchange(parts, name):
    n = len(parts)

    def body(*refs):
        ins, outs = refs[:n], refs[n:2 * n]
        send_sems, recv_sems, local_sems = refs[2 * n:]
        x, y, c = _place()
        my_chip = 2 * x + y
        copies, mine = [], []
        for a in range(n):
            cp = pltpu.make_async_copy(ins[a].at[my_chip], outs[a].at[my_chip], local_sems.at[a])
            cp.start()
            mine.append(cp)
            for j, (px, py) in enumerate(_other_chips(x, y)):
                cp = pltpu.make_async_remote_copy(
                    src_ref=ins[a].at[2 * px + py], dst_ref=outs[a].at[my_chip],
                    send_sem=send_sems.at[a, j], recv_sem=recv_sems.at[a, j],
                    device_id=(px, py, c), device_id_type=MESH)
                cp.start()
                copies.append(cp)
        for cp in copies:
            cp.wait()
        for cp in mine:
            cp.wait()

    return _pcall(
        body, name=name, in_specs=[ANY] * n, out_specs=[ANY] * n,
        out_shape=[jax.ShapeDtypeStruct(p.shape, p.dtype) for p in parts],
        scratch_shapes=[pltpu.SemaphoreType.DMA((n, 3)), pltpu.SemaphoreType.DMA((n, 3)), pltpu.SemaphoreType.DMA((n,))],
    )(*parts)


def _unshard(g, axis):
    nd = g.ndim - 1
    axis = axis % nd
    t = jnp.moveaxis(g, 0, axis)
    shp = list(g.shape[1:])
    shp[axis] *= N_DEV
    return t.reshape(shp)


def _to_shards(full, axis):
    axis = axis % full.ndim
    shp = list(full.shape)
    shp[axis:axis + 1] = [N_DEV, shp[axis] // N_DEV]
    return jnp.moveaxis(full.reshape(shp), axis, 0)


def _pack(arrs, rows):
    flat = jnp.concatenate([a.reshape(-1).astype(F32) for a in arrs])
    return jnp.pad(flat, (0, rows * LANES - flat.shape[0])).reshape(rows, LANES)


def _unpack(buf, shapes):
    flat = buf.reshape(-1)
    out, off = [], 0
    for s in shapes:
        n = math.prod(s)
        out.append(flat[off:off + n].reshape(s))
        off += n
    return out


def _rows_for(shapes):
    n = sum(math.prod(s) for s in shapes)
    return -(-n // (8 * LANES)) * 8


def _row(v, width=None):
    v = v.reshape(1, -1).astype(F32)
    if width is not None and v.shape[1] < width:
        v = jnp.pad(v, ((0, 0), (0, width - v.shape[1])))
    return v


def ffn_layer_fwd(h, norm_g, w_gu, w_down, tag):
    u = rms_fwd(h, norm_g, f"{tag}_rms")
    gu = matmul(u, w_gu, "nn", name=f"{tag}_gu")
    act = swiglu_fwd(gu, f"{tag}_act")
    h2 = matmul(act, w_down, "nn", residual=h, name=f"{tag}_down")
    return h2, (h, u, gu, act)


def ffn_layer_bwd(dh, dh_b, saved, norm_g, w_gu, w_down, tag):
    h, u, gu, act = saved
    dact = matmul(dh_b, w_down, "nt", name=f"{tag}_dact")
    d_down = matmul(act, dh_b, "tn", name=f"{tag}_dwd")
    dgu = swiglu_bwd(dact, gu, f"{tag}_dgu")
    du = matmul(dgu, w_gu, "nt", name=f"{tag}_du")
    d_gu = matmul(u, dgu, "tn", name=f"{tag}_dwgu")
    dh2, dh2_b, d_norm = rms_bwd(du, h, norm_g, dh, f"{tag}_drms")
    return dh2, dh2_b, dict(norm=d_norm, w_gate=d_gu[:, :DFF], w_up=d_gu[:, DFF:], w_down=d_down)


def conv_layer_fwd(h, p, tag):
    u = rms_fwd(h, p["norm"], f"{tag}_rms")
    hh = matmul(u, p["w_pw1"], "nn", bias=p["b_pw1"], name=f"{tag}_pw1")
    gl = glu_fwd(hh, f"{tag}_glu")
    c2 = dwconv_fwd(gl, 0, p["dw_w"], p["dw_b"], KCV, 128, False, f"{tag}_dw")[0]
    s = ln_silu_fwd(c2, p["ln_g"], p["ln_b"], f"{tag}_ln")
    h2 = matmul(s, p["w_pw2"], "nn", bias=p["b_pw2"], residual=h, name=f"{tag}_pw2")
    return h2, (h, u, hh, gl, c2, s)


def conv_layer_bwd(dh, dh_b, saved, p, tag):
    h, u, hh, gl, c2, s = saved
    ds = matmul(dh_b, p["w_pw2"], "nt", name=f"{tag}_ds")
    d_pw2 = matmul(s, dh_b, "tn", name=f"{tag}_dwpw2")
    dc2, d_lng, d_lnb, d_bpw2 = ln_silu_bwd(ds, c2, dh, p["ln_g"], p["ln_b"], f"{tag}_dln")
    dgl, d_dww, d_dwb = dwconv_bwd(dc2, None, gl, 0, p["dw_w"], KCV, 128, False, F32, f"{tag}_ddw")
    dhh, d_bpw1 = glu_bwd(dgl, hh, f"{tag}_dglu")
    du = matmul(dhh, p["w_pw1"], "nt", name=f"{tag}_du")
    d_pw1 = matmul(u, dhh, "tn", name=f"{tag}_dwpw1")
    dh2, dh2_b, d_norm = rms_bwd(du, h, p["norm"], dh, f"{tag}_drms")
    grads = dict(norm=d_norm, w_pw1=d_pw1, b_pw1=d_bpw1, dw_w=d_dww[:KCV], dw_b=d_dwb, ln_g=d_lng, ln_b=d_lnb,
                 w_pw2=d_pw2, b_pw2=d_bpw2)
    return dh2, dh2_b, grads


def _repeat_heads(v):
    return jnp.repeat(v[:, :NH], HD, axis=1)


def ssm_layer_fwd(h, p, tag):
    u = rms_fwd(h, p["norm"], f"{tag}_rms")
    zx = matmul(u, p["w_in"], "nn", name=f"{tag}_in")
    cpre, xbc = dwconv_fwd(zx, DI // 512, p["conv_w"], p["conv_b"], KSSM, 512, True, f"{tag}_conv")
    dt, da = dt_fwd(zx, p["dt_bias"], p["a_log"], f"{tag}_dt")
    dtx, dax, daT = _repeat_heads(dt), _repeat_heads(da), da[:, :NH].T
    y, states = ssd_fwd(xbc, dtx, dax, daT, p["d_full"], f"{tag}_ssd")
    yn = gatenorm_fwd(y, zx, p["gate_norm"], f"{tag}_gn")
    h2 = matmul(yn, p["w_out"], "nn", residual=h, name=f"{tag}_out")
    return h2, (h, u, zx, cpre, xbc, dt, dtx, dax, daT, y, states, yn)


def ssm_layer_bwd(dh, dh_b, saved, p, tag):
    h, u, zx, cpre, xbc, dt, dtx, dax, daT, y, states, yn = saved
    dyn = matmul(dh_b, p["w_out"], "nt", name=f"{tag}_dyn")
    d_wout = matmul(yn, dh_b, "tn", name=f"{tag}_dwout")
    dy, dz, d_gn = gatenorm_bwd(dyn, y, zx, p["gate_norm"], f"{tag}_dgn")
    dxbc, ddt, dda, dD = ssd_bwd(dy, xbc, dtx, dax, daT, p["d_full"], states, f"{tag}_dssd")
    draw, d_dtb, d_alog = dt_bwd(ddt, dda, dt, zx, p["dt_bias"], p["a_log"], f"{tag}_ddt")
    dxpre, d_cw, d_cb = dwconv_bwd(dxbc, cpre, zx, DI // 512, p["conv_w"], KSSM, 512, True, BF16, f"{tag}_dconv")
    dzx = jnp.concatenate([dz, dxpre, draw, jnp.zeros((T, DINP_PAD - 2 * DI - 2 * NG * DS - LANES), BF16)], axis=1)
    du = matmul(dzx, p["w_in"], "nt", name=f"{tag}_du")
    d_win = matmul(u, dzx, "tn", name=f"{tag}_dwin")
    dh2, dh2_b, d_norm = rms_bwd(du, h, p["norm"], dh, f"{tag}_drms")
    d_d = headsum(dD.reshape(NH, HD), f"{tag}_dD").reshape(NH)
    grads = dict(norm=d_norm, w_in=d_win[:, :DINP], conv_w=d_cw[:KSSM], conv_b=d_cb, dt_bias=d_dtb[0, :NH],
                 a_log=d_alog[0, :NH], d=d_d, gate_norm=d_gn, w_out=d_wout)
    return dh2, dh2_b, grads


BIG = ["ssm_w_in", "ssm_w_out", "cv_w_pw1", "cv_w_pw2", "ffn_w_gate", "ffn_w_up", "ffn_w_down"]
BIG_AXIS = {"ssm_w_in": -1, "ssm_w_out": 1, "cv_w_pw1": -1, "cv_w_pw2": 1, "ffn_w_gate": -1, "ffn_w_up": -1,
            "ffn_w_down": 1}
SMALL_SHARDED = ["ssm_conv_w", "cv_norm", "cv_b_pw1", "cv_dw_w", "cv_dw_b", "cv_ln_g", "cv_ln_b", "cv_b_pw2"]
SMALL_REPL = ["ssm_norm", "ssm_conv_b", "ssm_dt_bias", "ssm_a_log", "ssm_d", "ssm_gate_norm", "ffn_norm", "final_norm"]
WEIGHTS = ["ssm_norm", "ssm_w_in", "ssm_conv_w", "ssm_conv_b", "ssm_dt_bias", "ssm_a_log", "ssm_d", "ssm_gate_norm",
           "ssm_w_out", "cv_norm", "cv_w_pw1", "cv_b_pw1", "cv_dw_w", "cv_dw_b", "cv_ln_g", "cv_ln_b", "cv_w_pw2",
           "cv_b_pw2", "ffn_norm", "ffn_w_gate", "ffn_w_up", "ffn_w_down", "final_norm"]
SMALL = [n for n in WEIGHTS if n not in BIG]


def _local(x, tgt, full):
    def ssm_params(j):
        w_in = jnp.pad(full["ssm_w_in"][j], ((0, 0), (0, DINP_PAD - DINP)))
        return dict(norm=_row(full["ssm_norm"][j]), w_in=w_in, conv_w=jnp.pad(full["ssm_conv_w"][j], ((0, 8 - KSSM), (0, 0))),
                    conv_b=_row(full["ssm_conv_b"][j]), dt_bias=_row(full["ssm_dt_bias"][j], LANES),
                    a_log=_row(full["ssm_a_log"][j], LANES), d_full=_row(jnp.repeat(full["ssm_d"][j], HD)),
                    gate_norm=_row(full["ssm_gate_norm"][j]), w_out=full["ssm_w_out"][j])

    def cv_params(j):
        return dict(norm=_row(full["cv_norm"][j]), w_pw1=full["cv_w_pw1"][j], b_pw1=_row(full["cv_b_pw1"][j]),
                    dw_w=jnp.pad(full["cv_dw_w"][j], ((0, 32 - KCV), (0, 0))), dw_b=_row(full["cv_dw_b"][j]),
                    ln_g=_row(full["cv_ln_g"][j]), ln_b=_row(full["cv_ln_b"][j]), w_pw2=full["cv_w_pw2"][j],
                    b_pw2=_row(full["cv_b_pw2"][j]))

    def ffn_params(i):
        return (_row(full["ffn_norm"][i]), jnp.concatenate([full["ffn_w_gate"][i], full["ffn_w_up"][i]], axis=1),
                full["ffn_w_down"][i])

    h = x
    tape = []
    for i in range(4):
        j = i // 2
        if i % 2 == 0:
            pm = ssm_params(j)
            h, sv = ssm_layer_fwd(h, pm, f"ssm{j}")
        else:
            pm = cv_params(j)
            h, sv = conv_layer_fwd(h, pm, f"cv{j}")
        pf = ffn_params(i)
        h, sf = ffn_layer_fwd(h, *pf, f"ffn{i}")
        tape.append((pm, sv, pf, sf))

    dh, dh_b, d_final, loss_row = loss_head(h, _row(full["final_norm"]), tgt, "loss_head")

    gl = {n: [None] * full[n].shape[0] for n in WEIGHTS if n != "final_norm"}
    for i in reversed(range(4)):
        j = i // 2
        pm, sv, pf, sf = tape[i]
        dh, dh_b, gf = ffn_layer_bwd(dh, dh_b, sf, *pf, f"ffn{i}")
        gl["ffn_norm"][i], gl["ffn_w_gate"][i], gl["ffn_w_up"][i], gl["ffn_w_down"][i] = (
            gf["norm"].reshape(D), gf["w_gate"], gf["w_up"], gf["w_down"])
        if i % 2 == 0:
            dh, dh_b, gm = ssm_layer_bwd(dh, dh_b, sv, pm, f"ssm{j}")
            for k, val in gm.items():
                gl["ssm_" + k][j] = val.reshape(full["ssm_" + k].shape[1:])
        else:
            dh, dh_b, gm = conv_layer_bwd(dh, dh_b, sv, pm, f"cv{j}")
            for k, val in gm.items():
                gl["cv_" + k][j] = val.reshape(full["cv_" + k].shape[1:])
    grads = {n: jnp.stack(vs) for n, vs in gl.items()}
    grads["final_norm"] = d_final.reshape(D)
    return loss_row, dh, grads


def _step(x, tgt, w, m, v):
    idx = 4 * lax.axis_index("x") + 2 * lax.axis_index("y") + lax.axis_index("c")

    small_shapes = [w[n].shape for n in SMALL_SHARDED]
    small_pack = _pack([w[n] for n in SMALL_SHARDED], _rows_for(small_shapes))
    gathered = all_gather([w[n].astype(BF16) for n in BIG] + [small_pack], "gather_params")
    full = {n: _unshard(g, BIG_AXIS[n]) for n, g in zip(BIG, gathered[:-1])}
    per_dev = [_unpack(gathered[-1][k], small_shapes) for k in range(N_DEV)]
    for i, n in enumerate(SMALL_SHARDED):
        full[n] = _unshard(jnp.stack([per_dev[k][i] for k in range(N_DEV)]), -1)
    for n in SMALL_REPL:
        full[n] = w[n]

    loss_row, dh, grads = _local(x, tgt, full)

    gsend = [_to_shards(grads[n].astype(BF16), BIG_AXIS[n]) for n in BIG]
    from_sibling = sibling_exchange(gsend, "grads_to_sibling")
    by_chip = [pair_sum(g, r, f"pair_sum_{n}") for n, g, r in zip(BIG, gsend, from_sibling)]
    parts = chip_exchange(by_chip, "grads_to_chips")
    out = {}
    for n, p in zip(BIG, parts):
        out[n] = adamw(p, w[n], m[n], v[n], f"adamw_{n}")

    small_full_shapes = [full[n].shape for n in SMALL] + [(1,)]
    rows = _rows_for(small_full_shapes)
    packed = _pack([grads[n] for n in SMALL] + [loss_row[0, :1]], rows)
    summed = sum_leading(all_gather([packed], "gather_small_grads")[0], "sum_small_grads")
    small_g = _unpack(summed, small_full_shapes)
    loss = small_g[-1][0]
    for n, g in zip(SMALL, small_g[:-1]):
        if n in SMALL_SHARDED:
            s = w[n].shape[-1]
            g = lax.dynamic_slice_in_dim(g, idx * s, s, axis=g.ndim - 1)
        out[n] = adamw(g[None], w[n], m[n], v[n], f"adamw_{n}")
    return loss, dh, out


def kernel(x, ssm_norm, ssm_w_in, ssm_conv_w, ssm_conv_b, ssm_dt_bias, ssm_a_log, ssm_d, ssm_gate_norm, ssm_w_out, cv_norm, cv_w_pw1, cv_b_pw1, cv_dw_w, cv_dw_b, cv_ln_g, cv_ln_b, cv_w_pw2, cv_b_pw2, ffn_norm, ffn_w_gate, ffn_w_up, ffn_w_down, final_norm, loss_target, m_ssm_norm, m_ssm_w_in, m_ssm_conv_w, m_ssm_conv_b, m_ssm_dt_bias, m_ssm_a_log, m_ssm_d, m_ssm_gate_norm, m_ssm_w_out, m_cv_norm, m_cv_w_pw1, m_cv_b_pw1, m_cv_dw_w, m_cv_dw_b, m_cv_ln_g, m_cv_ln_b, m_cv_w_pw2, m_cv_b_pw2, m_ffn_norm, m_ffn_w_gate, m_ffn_w_up, m_ffn_w_down, m_final_norm, v_ssm_norm, v_ssm_w_in, v_ssm_conv_w, v_ssm_conv_b, v_ssm_dt_bias, v_ssm_a_log, v_ssm_d, v_ssm_gate_norm, v_ssm_w_out, v_cv_norm, v_cv_w_pw1, v_cv_b_pw1, v_cv_dw_w, v_cv_dw_b, v_cv_ln_g, v_cv_ln_b, v_cv_w_pw2, v_cv_b_pw2, v_ffn_norm, v_ffn_w_gate, v_ffn_w_up, v_ffn_w_down, v_final_norm):
    args = locals()
    w = {n: args[n] for n in WEIGHTS}
    m = {n: args["m_" + n] for n in WEIGHTS}
    v = {n: args["v_" + n] for n in WEIGHTS}
    loss, grad_x, out = _step(x[0], loss_target[0], w, m, v)
    res = [loss, grad_x[None]]
    for k in range(4):
        res += [out[n][k] for n in WEIGHTS]
    return tuple(res)
```

```python
import functools
import math

import jax
import jax.numpy as jnp
from jax import lax
from jax.experimental import pallas as pl
from jax.experimental.pallas import tpu as pltpu

F32 = jnp.float32
BF16 = jnp.bfloat16

N_DEV = 8
T = 2048
D = 1024
DI = 2048
NH = 32
HD = 64
NG = 4
GW = DI // NG
DS = 128
CONVD = DI + 2 * NG * DS
DINP = 2 * DI + 2 * NG * DS + NH
DINP_PAD = 5376
CH = 128
NCH = T // CH
DFF = 2816
KSSM = 4
KCV = 31
EPS = 1e-5
LANES = 128
VMEM_LIMIT = 56 * 1024 * 1024

ADAM_LR = 0.001
ADAM_B1 = 0.9
ADAM_B2 = 0.999
ADAM_EPS = 1e-08
ADAM_WD = 0.01
ADAM_STEP = 10

MESH = pl.DeviceIdType.MESH
ANY = pl.BlockSpec(memory_space=pl.ANY)


def _pcall(body, **kw):
    return pl.pallas_call(body, **kw)


def _cparams(sem):
    return pltpu.CompilerParams(dimension_semantics=sem, vmem_limit_bytes=VMEM_LIMIT)


def _pick(n, cands):
    for c in cands:
        if n % c == 0:
            return c
    raise ValueError(f"no tile for {n}")


def _sigmoid(x):
    return 1.0 / (1.0 + jnp.exp(-x))


def _silu(x):
    return x * _sigmoid(x)


def _dsilu(x):
    s = _sigmoid(x)
    return s * (1.0 + x * (1.0 - s))


_DIMS = {"nn": (((1,), (0,)), ((), ())), "nt": (((1,), (1,)), ((), ())), "tn": (((0,), (0,)), ((), ()))}


MM_VMEM_BUDGET = 40 * 1024 * 1024
MM_MAX_K = 3072


def _mm_tiles(M, N, K, out_bytes, has_res):
    tk = K if K <= MM_MAX_K else K // 2
    assert K % tk == 0 and tk % LANES == 0
    nk = K // tk
    best = None
    for tm in (2048, 1024, 512, 256, 128):
        if M % tm:
            continue
        for tn in (1408, 1024, 768, 512, 384, 256, 128):
            if N % tn:
                continue
            blocks = tm * tk * 2 + tk * tn * 2 + tm * tn * out_bytes + (tm * tn * 4 if has_res else 0)
            vmem = 2 * blocks + tm * tn * 4 * (2 if nk > 1 else 1)
            if vmem > MM_VMEM_BUDGET:
                continue
            traffic = (N // tn if nk > 1 else 1) * M * K + (M // tm) * N * K
            key = (-traffic, tm * tn)
            if best is None or key > best[0]:
                best = (key, tm, tn)
    assert best is not None, (M, N, K)
    return best[1], best[2], tk


def matmul(a, b, mode, *, name, bias=None, residual=None, out_dtype=F32):
    assert a.dtype == BF16 and b.dtype == BF16
    if mode == "nn":
        (M, K), (K2, N) = a.shape, b.shape
    elif mode == "nt":
        (M, K), (N, K2) = a.shape, b.shape
    else:
        (K, M), (K2, N) = a.shape, b.shape
    assert K == K2
    has_bias, has_res = bias is not None, residual is not None
    tm, tn, tk = _mm_tiles(M, N, K, jnp.dtype(out_dtype).itemsize, has_res)
    nk = K // tk
    dims = _DIMS[mode]

    def body(*refs):
        a_ref, b_ref = refs[0], refs[1]
        pos = 2
        bias_ref = res_ref = None
        if has_bias:
            bias_ref = refs[pos]
            pos += 1
        if has_res:
            res_ref = refs[pos]
            pos += 1
        o_ref = refs[pos]

        def finish(out):
            if has_bias:
                out = out + bias_ref[...]
            if has_res:
                out = out + res_ref[...]
            o_ref[...] = out.astype(o_ref.dtype)

        part = lax.dot_general(a_ref[...], b_ref[...], dims, preferred_element_type=F32)
        if nk == 1:
            finish(part)
            return
        acc = refs[pos + 1]
        k = pl.program_id(2)

        @pl.when(k == 0)
        def _():
            acc[...] = part

        @pl.when(jnp.logical_and(k > 0, k < nk - 1))
        def _():
            acc[...] += part

        @pl.when(k == nk - 1)
        def _():
            finish(acc[...] + part)

    if mode == "tn":
        a_spec = pl.BlockSpec((tk, tm), lambda i, j, k: (k, i))
    else:
        a_spec = pl.BlockSpec((tm, tk), lambda i, j, k: (i, k))
    if mode == "nt":
        b_spec = pl.BlockSpec((tn, tk), lambda i, j, k: (j, k))
    else:
        b_spec = pl.BlockSpec((tk, tn), lambda i, j, k: (k, j))
    in_specs, args = [a_spec, b_spec], [a, b]
    if has_bias:
        in_specs.append(pl.BlockSpec((1, tn), lambda i, j, k: (0, j)))
        args.append(bias.reshape(1, N).astype(F32))
    if has_res:
        in_specs.append(pl.BlockSpec((tm, tn), lambda i, j, k: (i, j)))
        args.append(residual)
    return _pcall(
        body, name=name, grid=(M // tm, N // tn, nk), in_specs=in_specs,
        out_specs=pl.BlockSpec((tm, tn), lambda i, j, k: (i, j)),
        out_shape=jax.ShapeDtypeStruct((M, N), out_dtype),
        scratch_shapes=[pltpu.VMEM((tm, tn), F32)] if nk > 1 else [],
        compiler_params=_cparams(("parallel", "parallel", "arbitrary")),
    )(*args)


def rowwise(fn, rows, bcasts, outs, accs=(), *, name, tm=256):
    n_rows, n_b, n_o, n_a = len(rows), len(bcasts), len(outs), len(accs)
    nt = T // tm

    def body(*refs):
        ins = [r[...] for r in refs[:n_rows + n_b]]
        res = fn(*ins)
        o_refs = refs[n_rows + n_b:n_rows + n_b + n_o]
        a_refs = refs[n_rows + n_b + n_o:]
        for r, v in zip(o_refs, res[:n_o]):
            r[...] = v.astype(r.dtype)
        if n_a:
            i = pl.program_id(0)

            @pl.when(i == 0)
            def _():
                for r in a_refs:
                    r[...] = jnp.zeros_like(r)

            for r, v in zip(a_refs, res[n_o:]):
                r[...] += v

    in_specs = [pl.BlockSpec((tm, w), functools.partial(lambda i, cb: (i, cb), cb=cb)) for (_, w, cb) in rows]
    in_specs += [pl.BlockSpec(b.shape, lambda i: (0, 0)) for b in bcasts]
    out_specs = [pl.BlockSpec((tm, w), lambda i: (i, 0)) for (w, _) in outs]
    out_specs += [pl.BlockSpec((1, w), lambda i: (0, 0)) for w in accs]
    out_shape = [jax.ShapeDtypeStruct((T, w), dt) for (w, dt) in outs]
    out_shape += [jax.ShapeDtypeStruct((1, w), F32) for w in accs]
    return _pcall(
        body, name=name, grid=(nt,), in_specs=in_specs, out_specs=out_specs, out_shape=out_shape,
        compiler_params=_cparams(("arbitrary",)),
    )(*[r[0] for r in rows], *bcasts)


def _full(a):
    return (a, a.shape[1], 0)


def _rsum(v):
    return jnp.sum(v, axis=0, keepdims=True)


def rms_fwd(h, g, name):
    def fn(x, g):
        r = lax.rsqrt(jnp.mean(x * x, axis=-1, keepdims=True) + EPS)
        return (x * r * g,)
    return rowwise(fn, [_full(h)], [g], [(D, BF16)], name=name)[0]


def rms_bwd(du, h, g, dres, name):
    def fn(du, x, dres, g):
        r = lax.rsqrt(jnp.mean(x * x, axis=-1, keepdims=True) + EPS)
        xh = x * r
        dxh = du * g
        dx = r * (dxh - xh * jnp.mean(dxh * xh, axis=-1, keepdims=True))
        dh = dres + dx
        return dh, dh, _rsum(du * xh)
    return rowwise(fn, [_full(du), _full(h), _full(dres)], [g], [(D, F32), (D, BF16)], [D], name=name)


def loss_head(h, g, tgt, name):
    def fn(x, tgt, g):
        r = lax.rsqrt(jnp.mean(x * x, axis=-1, keepdims=True) + EPS)
        xh = x * r
        err = xh * g - tgt
        lsum = jnp.sum(jnp.sum(err * err, axis=-1, keepdims=True), axis=0, keepdims=True) * (0.5 / D)
        dy = err * (1.0 / D)
        dxh = dy * g
        dx = r * (dxh - xh * jnp.mean(dxh * xh, axis=-1, keepdims=True))
        return dx, dx, _rsum(dy * xh), jnp.broadcast_to(lsum, (1, LANES))
    return rowwise(fn, [_full(h), _full(tgt)], [g], [(D, F32), (D, BF16)], [D, LANES], name=name)


def swiglu_fwd(gu, name):
    def fn(g, u):
        return (_silu(g) * u,)
    return rowwise(fn, [(gu, DFF, 0), (gu, DFF, 1)], [], [(DFF, BF16)], name=name)[0]


def swiglu_bwd(dact, gu, name):
    def fn(da, g, u):
        return (jnp.concatenate([da * u * _dsilu(g), da * _silu(g)], axis=1),)
    return rowwise(fn, [_full(dact), (gu, DFF, 0), (gu, DFF, 1)], [], [(2 * DFF, BF16)], name=name)[0]


def glu_fwd(hh, name):
    def fn(a, g):
        return (a * _sigmoid(g),)
    return rowwise(fn, [(hh, D, 0), (hh, D, 1)], [], [(D, F32)], name=name)[0]


def glu_bwd(dgl, hh, name):
    def fn(dgl, a, g):
        s = _sigmoid(g)
        dhh = jnp.concatenate([dgl * s, dgl * a * s * (1.0 - s)], axis=1)
        return dhh, _rsum(dhh)
    return rowwise(fn, [_full(dgl), (hh, D, 0), (hh, D, 1)], [], [(2 * D, BF16)], [2 * D], name=name)


def ln_silu_fwd(c2, g, b, name):
    def fn(x, g, b):
        mu = jnp.mean(x, axis=-1, keepdims=True)
        xc = x - mu
        r = lax.rsqrt(jnp.mean(xc * xc, axis=-1, keepdims=True) + EPS)
        return (_silu(xc * r * g + b),)
    return rowwise(fn, [_full(c2)], [g, b], [(D, BF16)], name=name)[0]


def ln_silu_bwd(ds, c2, dh, g, b, name):
    def fn(ds, x, dh, g, b):
        mu = jnp.mean(x, axis=-1, keepdims=True)
        xc = x - mu
        r = lax.rsqrt(jnp.mean(xc * xc, axis=-1, keepdims=True) + EPS)
        xh = xc * r
        dn = ds * _dsilu(xh * g + b)
        dxh = dn * g
        dx = r * (dxh - jnp.mean(dxh, axis=-1, keepdims=True) - xh * jnp.mean(dxh * xh, axis=-1, keepdims=True))
        return dx, _rsum(dn * xh), _rsum(dn), _rsum(dh)
    return rowwise(fn, [_full(ds), _full(c2), _full(dh)], [g, b], [(D, F32)], [D, D, D], name=name)


def gatenorm_fwd(y, zx, gn, name):
    def fn(y, z, gn):
        hg = y * _silu(z)
        parts = []
        for k in range(NG):
            hk = hg[:, k * GW:(k + 1) * GW]
            parts.append(hk * lax.rsqrt(jnp.mean(hk * hk, axis=-1, keepdims=True) + EPS))
        return (jnp.concatenate(parts, axis=1) * gn,)
    return rowwise(fn, [_full(y), (zx, DI, 0)], [gn], [(DI, BF16)], name=name)[0]


def gatenorm_bwd(dyn, y, zx, gn, name):
    def fn(dyn, y, z, gn):
        sz = _silu(z)
        hg = y * sz
        dxh = dyn * gn
        dhg, xhs = [], []
        for k in range(NG):
            sl = slice(k * GW, (k + 1) * GW)
            hk = hg[:, sl]
            r = lax.rsqrt(jnp.mean(hk * hk, axis=-1, keepdims=True) + EPS)
            xh = hk * r
            dk = dxh[:, sl]
            dhg.append(r * (dk - xh * jnp.mean(dk * xh, axis=-1, keepdims=True)))
            xhs.append(xh)
        dhg = jnp.concatenate(dhg, axis=1)
        xh = jnp.concatenate(xhs, axis=1)
        return dhg * sz, dhg * y * _dsilu(z), _rsum(dyn * xh)
    return rowwise(fn, [_full(dyn), _full(y), (zx, DI, 0)], [gn], [(DI, F32), (DI, BF16)], [DI], name=name)


def _softplus(x):
    return jnp.maximum(x, 0.0) + jnp.log(1.0 + jnp.exp(-jnp.abs(x)))


def dt_fwd(zx, dt_bias, a_log, name):
    def fn(raw, bias, a_log):
        dt = _softplus(raw + bias)
        return dt, dt * (-jnp.exp(a_log))
    return rowwise(fn, [(zx, LANES, (2 * DI + 2 * NG * DS) // LANES)], [dt_bias, a_log], [(LANES, F32), (LANES, F32)], name=name)


def dt_bwd(ddt, dda, dt, zx, dt_bias, a_log, name):
    def fn(ddt, dda, dt, raw, bias, a_log):
        a = -jnp.exp(a_log)
        draw = (ddt + dda * a) * _sigmoid(raw + bias)
        return draw, _rsum(draw), _rsum(dda * dt) * a
    return rowwise(fn, [_full(ddt), _full(dda), _full(dt), (zx, LANES, (2 * DI + 2 * NG * DS) // LANES)],
                   [dt_bias, a_log], [(LANES, BF16)], [LANES, LANES], name=name)


def headsum(v, name):
    def body(v_ref, o_ref):
        o_ref[...] = jnp.sum(v_ref[...], axis=1, keepdims=True)
    return _pcall(body, name=name, out_shape=jax.ShapeDtypeStruct((v.shape[0], 1), F32))(v)


CONV_ROWS = 256


def _shifted(win, o, rows):
    if o == 0:
        return win[0:rows]
    n = win.shape[0]
    return pltpu.roll(win, shift=n - o, axis=0)[0:rows]


def dwconv_fwd(x, x_cb0, w, b, K, ct, act, name):
    C = w.shape[1]
    pad = 8 if K <= 8 else 32
    KP = w.shape[0]
    n_out = 2 if act else 1

    def body(x_ref, w_ref, b_ref, *rest):
        o_refs, px = rest[:n_out], rest[n_out]
        px[0:pad, :] = jnp.zeros((pad, ct), F32)
        px[pad:pad + T, :] = x_ref[...]
        wv = w_ref[...]
        bv = b_ref[...]
        for r0 in range(0, T, CONV_ROWS):
            win = px[r0:r0 + CONV_ROWS + pad, :]
            acc = jnp.broadcast_to(bv, (CONV_ROWS, ct))
            for k in range(K):
                acc = acc + wv[k:k + 1, :] * _shifted(win, pad - (K - 1) + k, CONV_ROWS)
            o_refs[0][r0:r0 + CONV_ROWS, :] = acc
            if act:
                o_refs[1][r0:r0 + CONV_ROWS, :] = _silu(acc)

    return _pcall(
        body, name=name, grid=(C // ct,),
        in_specs=[pl.BlockSpec((T, ct), lambda j: (0, x_cb0 + j)), pl.BlockSpec((KP, ct), lambda j: (0, j)),
                  pl.BlockSpec((1, ct), lambda j: (0, j))],
        out_specs=[pl.BlockSpec((T, ct), lambda j: (0, j))] * n_out,
        out_shape=[jax.ShapeDtypeStruct((T, C), F32)] * n_out,
        scratch_shapes=[pltpu.VMEM((T + pad, ct), F32)],
        compiler_params=_cparams(("parallel",)),
    )(x, w, b)


def dwconv_bwd(dout, cpre, x, x_cb0, w, K, ct, act, out_dtype, name):
    C = w.shape[1]
    pad = 8 if K <= 8 else 32
    KP = w.shape[0]

    def body(*refs):
        if act:
            d_ref, c_ref, x_ref, w_ref, dx_ref, dw_ref, db_ref, px, pd = refs
        else:
            d_ref, x_ref, w_ref, dx_ref, dw_ref, db_ref, px, pd = refs
        px[0:pad, :] = jnp.zeros((pad, ct), F32)
        px[pad:pad + T, :] = x_ref[...]
        pd[T:T + pad, :] = jnp.zeros((pad, ct), F32)
        if act:
            pd[0:T, :] = d_ref[...] * _dsilu(c_ref[...])
        else:
            pd[0:T, :] = d_ref[...]
        wv = w_ref[...]
        dws = [jnp.zeros((1, ct), F32) for _ in range(K)]
        db = jnp.zeros((1, ct), F32)
        for r0 in range(0, T, CONV_ROWS):
            dwin = pd[r0:r0 + CONV_ROWS + pad, :]
            xwin = px[r0:r0 + CONV_ROWS + pad, :]
            dc = dwin[0:CONV_ROWS]
            db = db + _rsum(dc)
            acc = jnp.zeros((CONV_ROWS, ct), F32)
            for k in range(K):
                acc = acc + wv[k:k + 1, :] * _shifted(dwin, K - 1 - k, CONV_ROWS)
                dws[k] = dws[k] + _rsum(dc * _shifted(xwin, pad - (K - 1) + k, CONV_ROWS))
            dx_ref[r0:r0 + CONV_ROWS, :] = acc.astype(dx_ref.dtype)
        dw_ref[...] = jnp.zeros((KP, ct), F32)
        for k in range(K):
            dw_ref[k:k + 1, :] = dws[k]
        db_ref[...] = db

    col = pl.BlockSpec((T, ct), lambda j: (0, j))
    in_specs = [col] + ([col] if act else []) + [pl.BlockSpec((T, ct), lambda j: (0, x_cb0 + j)),
                                                 pl.BlockSpec((KP, ct), lambda j: (0, j))]
    args = [dout] + ([cpre] if act else []) + [x, w]
    return _pcall(
        body, name=name, grid=(C // ct,), in_specs=in_specs,
        out_specs=[col, pl.BlockSpec((KP, ct), lambda j: (0, j)), pl.BlockSpec((1, ct), lambda j: (0, j))],
        out_shape=[jax.ShapeDtypeStruct((T, C), out_dtype), jax.ShapeDtypeStruct((KP, C), F32),
                   jax.ShapeDtypeStruct((1, C), F32)],
        scratch_shapes=[pltpu.VMEM((T + pad, ct), F32), pltpu.VMEM((T + pad, ct), F32)],
        compiler_params=_cparams(("parallel",)),
    )(*args)


def _scan(a, axis, reverse=False):
    n = a.shape[axis]
    idx = lax.broadcasted_iota(jnp.int32, a.shape, axis)
    s = 1
    while s < n:
        if reverse:
            a = a + jnp.where(idx < n - s, pltpu.roll(a, shift=n - s, axis=axis), 0.0)
        else:
            a = a + jnp.where(idx >= s, pltpu.roll(a, shift=s, axis=axis), 0.0)
        s *= 2
    return a


_NT = _DIMS["nt"]
_TN = _DIMS["tn"]


def _dot(a, b, dims=_DIMS["nn"]):
    return lax.dot_general(a, b, dims, preferred_element_type=F32)


def ssd_fwd(xbc, dtx, dax, daT, dfull, name):
    def body(xbc_ref, dtx_ref, dax_ref, daT_ref, df_ref, y_ref, st_ref, S):
        ci = pl.program_id(0)

        @pl.when(ci == 0)
        def _():
            S[...] = jnp.zeros_like(S)

        row = lax.broadcasted_iota(jnp.int32, (CH, CH), 0)
        lane = lax.broadcasted_iota(jnp.int32, (CH, CH), 1)
        acsT = _scan(daT_ref[...], 1)
        for g in range(NG):
            c0 = g * GW
            xs = xbc_ref[:, c0:c0 + GW]
            acs = _scan(dax_ref[:, c0:c0 + GW], 0)
            Bm = xbc_ref[:, DI + g * DS:DI + (g + 1) * DS].astype(BF16)
            Cm = xbc_ref[:, DI + NG * DS + g * DS:DI + NG * DS + (g + 1) * DS].astype(BF16)
            xdt = xs * dtx_ref[:, c0:c0 + GW]
            atot = acs[CH - 1:CH, :]
            Sg = S[:, c0:c0 + GW]
            st_ref[:, c0:c0 + GW] = Sg
            CB = _dot(Cm, Bm, _NT)
            yg = jnp.exp(acs) * _dot(Cm, Sg.astype(BF16)) + xs * df_ref[:, c0:c0 + GW]
            xd = (xdt * jnp.exp(atot - acs)).astype(BF16)
            S[:, c0:c0 + GW] = jnp.exp(atot) * Sg + _dot(Bm, xd, _TN)
            xdt_b = xdt.astype(BF16)
            for r in range(NH // NG):
                h = g * (NH // NG) + r
                hs = slice(r * HD, (r + 1) * HD)
                seg = acs[:, r * HD:r * HD + 1] - acsT[h:h + 1, :]
                Lm = jnp.where(row >= lane, jnp.exp(jnp.minimum(seg, 0.0)), 0.0)
                yd = _dot((CB * Lm).astype(BF16), xdt_b[:, hs])
                y_ref[:, c0 + r * HD:c0 + (r + 1) * HD] = yg[:, hs] + yd

    return _pcall(
        body, name=name, grid=(NCH,),
        in_specs=[pl.BlockSpec((CH, CONVD), lambda i: (i, 0)), pl.BlockSpec((CH, DI), lambda i: (i, 0)),
                  pl.BlockSpec((CH, DI), lambda i: (i, 0)), pl.BlockSpec((NH, CH), lambda i: (0, i)),
                  pl.BlockSpec((1, DI), lambda i: (0, 0))],
        out_specs=[pl.BlockSpec((CH, DI), lambda i: (i, 0)), pl.BlockSpec((None, DS, DI), lambda i: (i, 0, 0))],
        out_shape=[jax.ShapeDtypeStruct((T, DI), F32), jax.ShapeDtypeStruct((NCH, DS, DI), F32)],
        scratch_shapes=[pltpu.VMEM((DS, DI), F32)],
        compiler_params=_cparams(("arbitrary",)),
    )(xbc, dtx, dax, daT, dfull)


def ssd_bwd(dy, xbc, dtx, dax, daT, dfull, states, name):
    def body(dy_ref, xbc_ref, dtx_ref, dax_ref, daT_ref, df_ref, st_ref, dxbc_ref, ddt_ref, dda_ref, dD_ref, dS):
        i = pl.program_id(0)

        @pl.when(i == 0)
        def _():
            dS[...] = jnp.zeros_like(dS)
            dD_ref[...] = jnp.zeros_like(dD_ref)

        row = lax.broadcasted_iota(jnp.int32, (CH, CH), 0)
        lane = lax.broadcasted_iota(jnp.int32, (CH, CH), 1)
        rowc = lax.broadcasted_iota(jnp.int32, (CH, 1), 0)
        acsT = _scan(daT_ref[...], 1)
        ddt_all = jnp.zeros((CH, LANES), F32)
        dacs_all = jnp.zeros((CH, LANES), F32)
        for g in range(NG):
            c0 = g * GW
            xs = xbc_ref[:, c0:c0 + GW]
            dtx = dtx_ref[:, c0:c0 + GW]
            acs = _scan(dax_ref[:, c0:c0 + GW], 0)
            Bm = xbc_ref[:, DI + g * DS:DI + (g + 1) * DS].astype(BF16)
            Cm = xbc_ref[:, DI + NG * DS + g * DS:DI + NG * DS + (g + 1) * DS].astype(BF16)
            xdt = xs * dtx
            atot = acs[CH - 1:CH, :]
            Sin = st_ref[:, c0:c0 + GW]
            dyg = dy_ref[:, c0:c0 + GW]
            dSo = dS[:, c0:c0 + GW]
            E = jnp.exp(acs)
            Etot = jnp.exp(atot)
            dec = jnp.exp(atot - acs)
            dD_ref[:, c0:c0 + GW] += _rsum(dyg * xs)
            dxs = dyg * df_ref[:, c0:c0 + GW]
            Sin_b = Sin.astype(BF16)
            dSo_b = dSo.astype(BF16)
            dY0 = dyg * E
            dY0_b = dY0.astype(BF16)
            dC = _dot(dY0_b, Sin_b, _NT)
            dS[:, c0:c0 + GW] = _dot(Cm, dY0_b, _TN) + Etot * dSo
            XD = xdt * dec
            dXD = _dot(Bm, dSo_b)
            dB = _dot(XD.astype(BF16), dSo_b, _NT)
            dxdt = dXD * dec
            Gq = dXD * XD
            dacs_x = dY0 * _dot(Cm, Sin_b) - Gq
            datot_x = _rsum(dSo * Sin) * Etot + _rsum(Gq)
            CB = _dot(Cm, Bm, _NT)
            CBT = _dot(Bm, Cm, _NT)
            dCB = jnp.zeros((CH, CH), F32)
            xdt_b = xdt.astype(BF16)
            dy_b = dyg.astype(BF16)
            for r in range(NH // NG):
                h = g * (NH // NG) + r
                hs = slice(r * HD, (r + 1) * HD)
                seg = acs[:, r * HD:r * HD + 1] - acsT[h:h + 1, :]
                Lm = jnp.where(row >= lane, jnp.exp(jnp.minimum(seg, 0.0)), 0.0)
                LmT = jnp.where(lane >= row, jnp.exp(jnp.minimum(-seg, 0.0)), 0.0)
                xr = xdt_b[:, hs]
                dyr = dy_b[:, hs]
                dM = _dot(dyr, xr, _NT)
                dMT = _dot(xr, dyr, _NT)
                dxdt_r = dxdt[:, hs] + _dot((CB * Lm).astype(BF16), dyr, _TN)
                dCB = dCB + dM * Lm
                dcol = (jnp.sum(dM * CB * Lm, axis=1, keepdims=True) - jnp.sum(dMT * CBT * LmT, axis=1, keepdims=True)
                        + jnp.sum(dacs_x[:, hs], axis=1, keepdims=True))
                dtot = jnp.sum(datot_x[:, hs], axis=1, keepdims=True)
                dcol = dcol + jnp.where(rowc == CH - 1, dtot, 0.0)
                ddt_col = jnp.sum(dxdt_r * xs[:, hs], axis=1, keepdims=True)
                ddt_all = ddt_all + jnp.where(lane == h, ddt_col, 0.0)
                dacs_all = dacs_all + jnp.where(lane == h, dcol, 0.0)
                dxbc_ref[:, c0 + r * HD:c0 + (r + 1) * HD] = dxs[:, hs] + dxdt_r * dtx[:, hs]
            dCB_b = dCB.astype(BF16)
            dxbc_ref[:, DI + g * DS:DI + (g + 1) * DS] = dB + _dot(dCB_b, Cm, _TN)
            dxbc_ref[:, DI + NG * DS + g * DS:DI + NG * DS + (g + 1) * DS] = dC + _dot(dCB_b, Bm)
        ddt_ref[...] = ddt_all
        dda_ref[...] = _scan(dacs_all, 0, reverse=True)

    last = NCH - 1
    return _pcall(
        body, name=name, grid=(NCH,),
        in_specs=[pl.BlockSpec((CH, DI), lambda i: (last - i, 0)), pl.BlockSpec((CH, CONVD), lambda i: (last - i, 0)),
                  pl.BlockSpec((CH, DI), lambda i: (last - i, 0)), pl.BlockSpec((CH, DI), lambda i: (last - i, 0)),
                  pl.BlockSpec((NH, CH), lambda i: (0, last - i)), pl.BlockSpec((1, DI), lambda i: (0, 0)),
                  pl.BlockSpec((None, DS, DI), lambda i: (last - i, 0, 0))],
        out_specs=[pl.BlockSpec((CH, CONVD), lambda i: (last - i, 0)), pl.BlockSpec((CH, LANES), lambda i: (last - i, 0)),
                   pl.BlockSpec((CH, LANES), lambda i: (last - i, 0)), pl.BlockSpec((1, DI), lambda i: (0, 0))],
        out_shape=[jax.ShapeDtypeStruct((T, CONVD), F32), jax.ShapeDtypeStruct((T, LANES), F32),
                   jax.ShapeDtypeStruct((T, LANES), F32), jax.ShapeDtypeStruct((1, DI), F32)],
        scratch_shapes=[pltpu.VMEM((DS, DI), F32)],
        compiler_params=_cparams(("arbitrary",)),
    )(dy, xbc, dtx, dax, daT, dfull, states)


def _as3d(shape):
    if len(shape) == 1:
        return (1, 1, shape[0])
    if len(shape) == 2:
        return (1, shape[0], shape[1])
    return (math.prod(shape[:-2]), shape[-2], shape[-1])


def _row_tile(R, C):
    if R * C <= 512 * 1024:
        return R
    return _pick(R, (512, 256, 128, 64, 32, 16, 8))


def adamw(parts, w, m, v, name):
    shape = w.shape
    L, R, C = _as3d(shape)
    P = parts.shape[0]
    tr = _row_tile(R, C)
    bc1 = 1.0 - ADAM_B1 ** ADAM_STEP
    bc2 = 1.0 - ADAM_B2 ** ADAM_STEP

    def body(p_ref, w_ref, m_ref, v_ref, g_out, d_out, m_out, v_out):
        g = p_ref[0].astype(F32)
        for k in range(1, P):
            g = g + p_ref[k].astype(F32)
        mn = ADAM_B1 * m_ref[...] + (1.0 - ADAM_B1) * g
        vn = ADAM_B2 * v_ref[...] + (1.0 - ADAM_B2) * (g * g)
        g_out[...] = g
        m_out[...] = mn
        v_out[...] = vn
        d_out[...] = -ADAM_LR * ((mn / bc1) / (jnp.sqrt(vn / bc2) + ADAM_EPS) + ADAM_WD * w_ref[...])

    blk = pl.BlockSpec((None, tr, C), lambda l, r: (l, r, 0))
    outs = _pcall(
        body, name=name, grid=(L, R // tr),
        in_specs=[pl.BlockSpec((P, None, tr, C), lambda l, r: (0, l, r, 0)), blk, blk, blk],
        out_specs=[blk] * 4, out_shape=[jax.ShapeDtypeStruct((L, R, C), F32)] * 4,
        compiler_params=_cparams(("parallel", "parallel")),
    )(parts.reshape(P, L, R, C), w.reshape(L, R, C), m.reshape(L, R, C), v.reshape(L, R, C))
    return [o.reshape(shape) for o in outs]


def adamw_layer(recv, own, w, m, v, layer, prev, name):
    L, R, C = w.shape
    P = recv.shape[0]
    tr = _row_tile(R, C)
    bc1 = 1.0 - ADAM_B1 ** ADAM_STEP
    bc2 = 1.0 - ADAM_B2 ** ADAM_STEP

    def body(r_ref, o_ref, w_ref, m_ref, v_ref, *rest):
        g_out, d_out, m_out, v_out = rest[-4:]
        g = o_ref[...].astype(F32)
        for k in range(P):
            g = g + r_ref[k].astype(F32)
        mn = ADAM_B1 * m_ref[...] + (1.0 - ADAM_B1) * g
        vn = ADAM_B2 * v_ref[...] + (1.0 - ADAM_B2) * (g * g)
        g_out[...] = g
        m_out[...] = mn
        v_out[...] = vn
        d_out[...] = -ADAM_LR * ((mn / bc1) / (jnp.sqrt(vn / bc2) + ADAM_EPS) + ADAM_WD * w_ref[...])

    slot = pl.BlockSpec((None, tr, C), lambda r: (layer, r, 0))
    own_spec = pl.BlockSpec((None, tr, C), lambda r: (2 * lax.axis_index("x") + lax.axis_index("y"), r, 0))
    in_specs = [pl.BlockSpec((P, tr, C), lambda r: (0, r, 0)), own_spec, slot, slot, slot]
    args = [recv, own, w, m, v]
    aliases = {}
    if prev is not None:
        in_specs += [ANY] * 4
        args += list(prev)
        aliases = {5 + k: k for k in range(4)}
    return _pcall(
        body, name=name, grid=(R // tr,), in_specs=in_specs, out_specs=[slot] * 4,
        out_shape=[jax.ShapeDtypeStruct((L, R, C), F32)] * 4, input_output_aliases=aliases,
        compiler_params=_cparams(("parallel",)),
    )(*args)


def sum_leading(parts, name):
    P, R, C = parts.shape

    def body(p_ref, o_ref):
        s = p_ref[0]
        for k in range(1, P):
            s = s + p_ref[k]
        o_ref[...] = s

    return _pcall(body, name=name, out_shape=jax.ShapeDtypeStruct((R, C), F32))(parts)


def pair_sum(gsend, recv, name):
    S = gsend.shape[1:]
    L, R, C = _as3d(S)
    tr = _row_tile(R, C)

    def body(g_ref, r_ref, o_ref):
        o_ref[...] = (g_ref[...].astype(F32) + r_ref[...].astype(F32)).astype(o_ref.dtype)

    blk = pl.BlockSpec((None, None, tr, C), lambda q, l, r: (q, l, r, 0))
    own = pl.BlockSpec((None, None, tr, C), lambda q, l, r: (2 * q + lax.axis_index("c"), l, r, 0))
    out = _pcall(
        body, name=name, grid=(4, L, R // tr), in_specs=[own, blk], out_specs=blk,
        out_shape=jax.ShapeDtypeStruct((4, L, R, C), BF16),
        compiler_params=_cparams(("parallel", "parallel", "parallel")),
    )(gsend.reshape(8, L, R, C), recv.reshape(4, L, R, C))
    return out.reshape((4,) + S)


def _place():
    return lax.axis_index("x"), lax.axis_index("y"), lax.axis_index("c")


def _other_chips(x, y):
    return [(1 - x, y), (x, 1 - y), (1 - x, 1 - y)]


def all_gather(arrs, name):
    n = len(arrs)

    def body(*refs):
        ins, outs = refs[:n], refs[n:2 * n]
        send_sems, recv_sems, local_sems = refs[2 * n:]
        x, y, c = _place()
        me, sibling = (x, y, c), (x, y, 1 - c)
        chips = _other_chips(x, y)

        def slot(a, px, py, pc):
            return outs[a].at[4 * px + 2 * py + pc]

        def copy(a, k, block, to, src=None):
            return pltpu.make_async_remote_copy(
                src_ref=slot(a, *block) if src is None else src, dst_ref=slot(a, *block),
                send_sem=send_sems.at[a, k], recv_sem=recv_sems.at[a, k], device_id=to, device_id_type=MESH)

        mine, first, passed = [], [], []
        for a in range(n):
            cp = pltpu.make_async_copy(ins[a], slot(a, *me), local_sems.at[a])
            cp.start()
            mine.append(cp)
            first.append(copy(a, 0, me, sibling, src=ins[a]))
            first += [copy(a, 1 + j, me, (*chip, c), src=ins[a]) for j, chip in enumerate(chips)]
        for cp in first:
            cp.start()
        for j, chip in enumerate(chips):
            for a in range(n):
                copy(a, 1 + j, (*chip, c), me).wait_recv()
                cp = copy(a, 4 + j, (*chip, c), sibling)
                cp.start()
                passed.append(cp)
        for a in range(n):
            copy(a, 0, sibling, me).wait_recv()
            for j, chip in enumerate(chips):
                copy(a, 4 + j, (*chip, 1 - c), me).wait_recv()
        for cp in first + passed:
            cp.wait_send()
        for cp in mine:
            cp.wait()

    return _pcall(
        body, name=name, in_specs=[ANY] * n, out_specs=[ANY] * n,
        out_shape=[jax.ShapeDtypeStruct((N_DEV,) + a.shape, a.dtype) for a in arrs],
        scratch_shapes=[pltpu.SemaphoreType.DMA((n, 7)), pltpu.SemaphoreType.DMA((n, 7)), pltpu.SemaphoreType.DMA((n,))],
    )(*arrs)


def sibling_exchange(gsends, name):
    n = len(gsends)

    def body(*refs):
        ins, outs = refs[:n], refs[n:2 * n]
        send_sems, recv_sems = refs[2 * n:]
        x, y, c = _place()
        copies = []
        for a in range(n):
            for q in range(4):
                cp = pltpu.make_async_remote_copy(
                    src_ref=ins[a].at[2 * q + 1 - c], dst_ref=outs[a].at[q],
                    send_sem=send_sems.at[a, q], recv_sem=recv_sems.at[a, q],
                    device_id=(x, y, 1 - c), device_id_type=MESH)
                cp.start()
                copies.append(cp)
        for cp in copies:
            cp.wait()

    return _pcall(
        body, name=name, in_specs=[ANY] * n, out_specs=[ANY] * n,
        out_shape=[jax.ShapeDtypeStruct((4,) + g.shape[1:], g.dtype) for g in gsends],
        scratch_shapes=[pltpu.SemaphoreType.DMA((n, 4)), pltpu.SemaphoreType.DMA((n, 4))],
    )(*gsends)


HBM =pl.BlockSpec(memory_space=pltpu.HBM)
SEM = pl.BlockSpec(memory_space=pltpu.SEMAPHORE)
EFFECT = pltpu.SideEffectType.DATAFLOW_SIDE_EFFECTING


def _in_hbm(a):
    return pltpu.with_memory_space_constraint(a, pltpu.HBM)


def _gather_peers(x, y, c):
    to = [(x, y, 1 - c)] + [(px, py, c) for px, py in _other_chips(x, y)]
    return to, [4 * px + 2 * py + pc for px, py, pc in to]


def gather_start(arrs, after, name):
    n = len(arrs)
    n_in = 2 * n + (1 if after is not None else 0)

    def body(*refs):
        srcs, lands = refs[:n], refs[n:2 * n]
        send_sems, recv_sems = refs[n_in], refs[n_in + 1]
        token = refs[-1]
        x, y, c = _place()
        to, _ = _gather_peers(x, y, c)
        me = 4 * x + 2 * y + c
        for a in range(n):
            for k, dev in enumerate(to):
                pltpu.make_async_remote_copy(
                    src_ref=srcs[a], dst_ref=lands[a].at[me], send_sem=send_sems.at[4 * a + k], recv_sem=recv_sems.at[4 * a + k],
                    device_id=dev, device_id_type=MESH).start()
        token[...] = jnp.zeros_like(token)

    zones = [lax.empty((N_DEV,) + a.shape, a.dtype) for a in arrs]
    args = [_in_hbm(a) for a in arrs] + [_in_hbm(z) for z in zones] + ([after] if after is not None else [])
    outs = _pcall(
        body, name=name,
        out_shape=(pltpu.SemaphoreType.DMA((4 * n,)), pltpu.SemaphoreType.DMA((4 * n,)),
                   *[pltpu.HBM(a.shape, a.dtype) for a in arrs], *[pltpu.HBM(z.shape, z.dtype) for z in zones],
                   jax.ShapeDtypeStruct((8, LANES), F32)),
        in_specs=[HBM] * (2 * n) + ([ANY] if after is not None else []),
        out_specs=(SEM, SEM, *[HBM] * (2 * n), pl.BlockSpec(memory_space=pltpu.VMEM)),
        input_output_aliases={i: 2 + i for i in range(2 * n)},
        compiler_params=pltpu.CompilerParams(has_side_effects=EFFECT),
    )(*args)
    return dict(send=outs[0], recv=outs[1], srcs=list(outs[2:2 + n]), lands=list(outs[2 + n:2 + 2 * n]), token=outs[-1])


def gather_wait(st, after, name):
    n = len(st["srcs"])

    def body(*refs):
        srcs, lands = refs[:n], refs[n:2 * n]
        send_sems, recv_sems = refs[2 * n], refs[2 * n + 1]
        x, y, c = _place()
        to, slots = _gather_peers(x, y, c)
        for a in range(n):
            for k, dev in enumerate(to):
                cp = pltpu.make_async_remote_copy(
                    src_ref=srcs[a], dst_ref=lands[a].at[slots[k]], send_sem=send_sems.at[4 * a + k],
                    recv_sem=recv_sems.at[4 * a + k], device_id=dev, device_id_type=MESH)
                cp.wait_send()
                cp.wait_recv()

    outs = _pcall(
        body, name=name,
        out_shape=(*[pltpu.HBM(a.shape, a.dtype) for a in st["srcs"]], *[pltpu.HBM(z.shape, z.dtype) for z in st["lands"]]),
        in_specs=[HBM] * (2 * n) + [SEM, SEM, ANY], out_specs=tuple([HBM] * (2 * n)),
        input_output_aliases={i: i for i in range(2 * n)},
        compiler_params=pltpu.CompilerParams(has_side_effects=EFFECT),
    )(*st["srcs"], *st["lands"], st["send"], st["recv"], after)
    return list(outs[n:])


def gather_pass(arrs, lands, name):
    n = len(arrs)

    def body(*refs):
        srcs, zones = refs[:n], refs[n:2 * n]
        send_sems, recv_sems, local_sems = refs[3 * n:]
        x, y, c = _place()
        mine, copies = [], []
        for a in range(n):
            cp = pltpu.make_async_copy(srcs[a], zones[a].at[4 * x + 2 * y + c], local_sems.at[a])
            cp.start()
            mine.append(cp)
            for j, (px, py) in enumerate(_other_chips(x, y)):
                cp = pltpu.make_async_remote_copy(
                    src_ref=zones[a].at[4 * px + 2 * py + c], dst_ref=zones[a].at[4 * px + 2 * py + c],
                    send_sem=send_sems.at[a, j], recv_sem=recv_sems.at[a, j], device_id=(x, y, 1 - c), device_id_type=MESH)
                cp.start()
                copies.append(cp)
        for a in range(n):
            for j, (px, py) in enumerate(_other_chips(x, y)):
                blk = zones[a].at[4 * px + 2 * py + 1 - c]
                pltpu.make_async_remote_copy(
                    src_ref=blk, dst_ref=blk, send_sem=send_sems.at[a, j], recv_sem=recv_sems.at[a, j],
                    device_id=(x, y, 1 - c), device_id_type=MESH).wait_recv()
        for cp in copies:
            cp.wait_send()
        for cp in mine:
            cp.wait()

    return _pcall(
        body, name=name, in_specs=[ANY] * (2 * n), out_specs=[ANY] * n,
        out_shape=[jax.ShapeDtypeStruct(z.shape, z.dtype) for z in lands],
        input_output_aliases={n + a: a for a in range(n)},
        scratch_shapes=[pltpu.SemaphoreType.DMA((n, 3)), pltpu.SemaphoreType.DMA((n, 3)), pltpu.SemaphoreType.DMA((n,))],
    )(*arrs, *lands)


def scatter_start(parts, name):
    n = len(parts)

    def body(*refs):
        srcs, lands = refs[:n], refs[n:2 * n]
        send_sems, recv_sems = refs[2 * n], refs[2 * n + 1]
        token = refs[-1]
        x, y, c = _place()
        for a in range(n):
            for j, (px, py) in enumerate(_other_chips(x, y)):
                pltpu.make_async_remote_copy(
                    src_ref=srcs[a].at[2 * px + py], dst_ref=lands[a].at[j], send_sem=send_sems.at[3 * a + j],
                    recv_sem=recv_sems.at[3 * a + j], device_id=(px, py, c), device_id_type=MESH).start()
        token[...] = jnp.zeros_like(token)

    zones = [lax.empty((3,) + p.shape[1:], p.dtype) for p in parts]
    outs = _pcall(
        body, name=name,
        out_shape=(pltpu.SemaphoreType.DMA((3 * n,)), pltpu.SemaphoreType.DMA((3 * n,)),
                   *[pltpu.HBM(p.shape, p.dtype) for p in parts], *[pltpu.HBM(z.shape, z.dtype) for z in zones],
                   jax.ShapeDtypeStruct((8, LANES), F32)),
        in_specs=[HBM] * (2 * n), out_specs=(SEM, SEM, *[HBM] * (2 * n), pl.BlockSpec(memory_space=pltpu.VMEM)),
        input_output_aliases={i: 2 + i for i in range(2 * n)},
        compiler_params=pltpu.CompilerParams(has_side_effects=EFFECT),
    )(*[_in_hbm(p) for p in parts], *[_in_hbm(z) for z in zones])
    return dict(send=outs[0], recv=outs[1], srcs=list(outs[2:2 + n]), lands=list(outs[2 + n:2 + 2 * n]), token=outs[-1])


def scatter_wait(st, after, name):
    n = len(st["srcs"])

    def body(*refs):
        srcs, lands = refs[:n], refs[n:2 * n]
        send_sems, recv_sems = refs[2 * n], refs[2 * n + 1]
        x, y, c = _place()
        for a in range(n):
            for j, (px, py) in enumerate(_other_chips(x, y)):
                cp = pltpu.make_async_remote_copy(
                    src_ref=srcs[a].at[2 * px + py], dst_ref=lands[a].at[j], send_sem=send_sems.at[3 * a + j],
                    recv_sem=recv_sems.at[3 * a + j], device_id=(px, py, c), device_id_type=MESH)
                cp.wait_send()
                cp.wait_recv()

    outs = _pcall(
        body, name=name,
        out_shape=(*[pltpu.HBM(a.shape, a.dtype) for a in st["srcs"]], *[pltpu.HBM(z.shape, z.dtype) for z in st["lands"]]),
        in_specs=[HBM] * (2 * n) + [SEM, SEM, ANY], out_specs=tuple([HBM] * (2 * n)),
        input_output_aliases={i: i for i in range(2 * n)},
        compiler_params=pltpu.CompilerParams(has_side_effects=EFFECT),
    )(*st["srcs"], *st["lands"], st["send"], st["recv"], after)
    return list(outs[:n]), list(outs[n:])


def _unshard(g, axis):
    nd = g.ndim - 1
    axis = axis % nd
    t = jnp.moveaxis(g, 0, axis)
    shp = list(g.shape[1:])
    shp[axis] *= N_DEV
    return t.reshape(shp)


def _to_shards(full, axis):
    axis = axis % full.ndim
    shp = list(full.shape)
    shp[axis:axis + 1] = [N_DEV, shp[axis] // N_DEV]
    return jnp.moveaxis(full.reshape(shp), axis, 0)


def _pack(arrs, rows):
    flat = jnp.concatenate([a.reshape(-1).astype(F32) for a in arrs])
    return jnp.pad(flat, (0, rows * LANES - flat.shape[0])).reshape(rows, LANES)


def _unpack(buf, shapes):
    flat = buf.reshape(-1)
    out, off = [], 0
    for s in shapes:
        n = math.prod(s)
        out.append(flat[off:off + n].reshape(s))
        off += n
    return out


def _rows_for(shapes):
    n = sum(math.prod(s) for s in shapes)
    return -(-n // (8 * LANES)) * 8


def _row(v, width=None):
    v = v.reshape(1, -1).astype(F32)
    if width is not None and v.shape[1] < width:
        v = jnp.pad(v, ((0, 0), (0, width - v.shape[1])))
    return v


def ffn_layer_fwd(h, norm_g, w_gu, w_down, tag):
    u = rms_fwd(h, norm_g, f"{tag}_rms")
    gu = matmul(u, w_gu, "nn", name=f"{tag}_gu")
    act = swiglu_fwd(gu, f"{tag}_act")
    h2 = matmul(act, w_down, "nn", residual=h, name=f"{tag}_down")
    return h2, (h, u, gu, act)


def ffn_layer_bwd(dh, dh_b, saved, norm_g, w_gu, w_down, tag):
    h, u, gu, act = saved
    dact = matmul(dh_b, w_down, "nt", name=f"{tag}_dact")
    d_down = matmul(act, dh_b, "tn", out_dtype=BF16, name=f"{tag}_dwd")
    dgu = swiglu_bwd(dact, gu, f"{tag}_dgu")
    du = matmul(dgu, w_gu, "nt", name=f"{tag}_du")
    d_gu = matmul(u, dgu, "tn", out_dtype=BF16, name=f"{tag}_dwgu")
    dh2, dh2_b, d_norm = rms_bwd(du, h, norm_g, dh, f"{tag}_drms")
    return dh2, dh2_b, dict(norm=d_norm, w_gate=d_gu[:, :DFF], w_up=d_gu[:, DFF:], w_down=d_down)


def conv_layer_fwd(h, p, tag):
    u = rms_fwd(h, p["norm"], f"{tag}_rms")
    hh = matmul(u, p["w_pw1"], "nn", bias=p["b_pw1"], name=f"{tag}_pw1")
    gl = glu_fwd(hh, f"{tag}_glu")
    c2 = dwconv_fwd(gl, 0, p["dw_w"], p["dw_b"], KCV, 128, False, f"{tag}_dw")[0]
    s = ln_silu_fwd(c2, p["ln_g"], p["ln_b"], f"{tag}_ln")
    h2 = matmul(s, p["w_pw2"], "nn", bias=p["b_pw2"], residual=h, name=f"{tag}_pw2")
    return h2, (h, u, hh, gl, c2, s)


def conv_layer_bwd(dh, dh_b, saved, p, tag):
    h, u, hh, gl, c2, s = saved
    ds = matmul(dh_b, p["w_pw2"], "nt", name=f"{tag}_ds")
    d_pw2 = matmul(s, dh_b, "tn", out_dtype=BF16, name=f"{tag}_dwpw2")
    dc2, d_lng, d_lnb, d_bpw2 = ln_silu_bwd(ds, c2, dh, p["ln_g"], p["ln_b"], f"{tag}_dln")
    dgl, d_dww, d_dwb = dwconv_bwd(dc2, None, gl, 0, p["dw_w"], KCV, 128, False, F32, f"{tag}_ddw")
    dhh, d_bpw1 = glu_bwd(dgl, hh, f"{tag}_dglu")
    du = matmul(dhh, p["w_pw1"], "nt", name=f"{tag}_du")
    d_pw1 = matmul(u, dhh, "tn", out_dtype=BF16, name=f"{tag}_dwpw1")
    dh2, dh2_b, d_norm = rms_bwd(du, h, p["norm"], dh, f"{tag}_drms")
    grads = dict(norm=d_norm, w_pw1=d_pw1, b_pw1=d_bpw1, dw_w=d_dww[:KCV], dw_b=d_dwb, ln_g=d_lng, ln_b=d_lnb,
                 w_pw2=d_pw2, b_pw2=d_bpw2)
    return dh2, dh2_b, grads


def _repeat_heads(v):
    return jnp.repeat(v[:, :NH], HD, axis=1)


def ssm_layer_fwd(h, p, tag):
    u = rms_fwd(h, p["norm"], f"{tag}_rms")
    zx = matmul(u, p["w_in"], "nn", name=f"{tag}_in")
    cpre, xbc = dwconv_fwd(zx, DI // 512, p["conv_w"], p["conv_b"], KSSM, 512, True, f"{tag}_conv")
    dt, da = dt_fwd(zx, p["dt_bias"], p["a_log"], f"{tag}_dt")
    dtx, dax, daT = _repeat_heads(dt), _repeat_heads(da), da[:, :NH].T
    y, states = ssd_fwd(xbc, dtx, dax, daT, p["d_full"], f"{tag}_ssd")
    yn = gatenorm_fwd(y, zx, p["gate_norm"], f"{tag}_gn")
    h2 = matmul(yn, p["w_out"], "nn", residual=h, name=f"{tag}_out")
    return h2, (h, u, zx, cpre, xbc, dt, dtx, dax, daT, y, states, yn)


def ssm_layer_bwd(dh, dh_b, saved, p, tag):
    h, u, zx, cpre, xbc, dt, dtx, dax, daT, y, states, yn = saved
    dyn = matmul(dh_b, p["w_out"], "nt", name=f"{tag}_dyn")
    d_wout = matmul(yn, dh_b, "tn", out_dtype=BF16, name=f"{tag}_dwout")
    dy, dz, d_gn = gatenorm_bwd(dyn, y, zx, p["gate_norm"], f"{tag}_dgn")
    dxbc, ddt, dda, dD = ssd_bwd(dy, xbc, dtx, dax, daT, p["d_full"], states, f"{tag}_dssd")
    draw, d_dtb, d_alog = dt_bwd(ddt, dda, dt, zx, p["dt_bias"], p["a_log"], f"{tag}_ddt")
    dxpre, d_cw, d_cb = dwconv_bwd(dxbc, cpre, zx, DI // 512, p["conv_w"], KSSM, 512, True, BF16, f"{tag}_dconv")
    dzx = jnp.concatenate([dz, dxpre, draw, jnp.zeros((T, DINP_PAD - 2 * DI - 2 * NG * DS - LANES), BF16)], axis=1)
    du = matmul(dzx, p["w_in"], "nt", name=f"{tag}_du")
    d_win = matmul(u, dzx, "tn", out_dtype=BF16, name=f"{tag}_dwin")
    dh2, dh2_b, d_norm = rms_bwd(du, h, p["norm"], dh, f"{tag}_drms")
    d_d = headsum(dD.reshape(NH, HD), f"{tag}_dD").reshape(NH)
    grads = dict(norm=d_norm, w_in=d_win[:, :DINP], conv_w=d_cw[:KSSM], conv_b=d_cb, dt_bias=d_dtb[0, :NH],
                 a_log=d_alog[0, :NH], d=d_d, gate_norm=d_gn, w_out=d_wout)
    return dh2, dh2_b, grads


BIG = ["ssm_w_in", "ssm_w_out", "cv_w_pw1", "cv_w_pw2", "ffn_w_gate", "ffn_w_up", "ffn_w_down"]
LAYER_AXIS = {"ssm_w_in": -1, "ssm_w_out": 0, "cv_w_pw1": -1, "cv_w_pw2": 0, "ffn_w_gate": -1, "ffn_w_up": -1,
              "ffn_w_down": 0}
SMALL_SHARDED = ["ssm_conv_w", "cv_norm", "cv_b_pw1", "cv_dw_w", "cv_dw_b", "cv_ln_g", "cv_ln_b", "cv_b_pw2"]
SMALL_REPL = ["ssm_norm", "ssm_conv_b", "ssm_dt_bias", "ssm_a_log", "ssm_d", "ssm_gate_norm", "ffn_norm", "final_norm"]
WEIGHTS = ["ssm_norm", "ssm_w_in", "ssm_conv_w", "ssm_conv_b", "ssm_dt_bias", "ssm_a_log", "ssm_d", "ssm_gate_norm",
           "ssm_w_out", "cv_norm", "cv_w_pw1", "cv_b_pw1", "cv_dw_w", "cv_dw_b", "cv_ln_g", "cv_ln_b", "cv_w_pw2",
           "cv_b_pw2", "ffn_norm", "ffn_w_gate", "ffn_w_up", "ffn_w_down", "final_norm"]
SMALL = [n for n in WEIGHTS if n not in BIG]


def _group(i):
    j = i // 2
    mix = [("ssm_w_in", j), ("ssm_w_out", j)] if i % 2 == 0 else [("cv_w_pw1", j), ("cv_w_pw2", j)]
    return mix + [("ffn_w_gate", i), ("ffn_w_up", i), ("ffn_w_down", i)]


def _layer_params(i, big, small):
    j = i // 2
    if i % 2 == 0:
        pm = dict(norm=_row(small["ssm_norm"][j]), w_in=jnp.pad(big["ssm_w_in"], ((0, 0), (0, DINP_PAD - DINP))),
                  conv_w=jnp.pad(small["ssm_conv_w"][j], ((0, 8 - KSSM), (0, 0))), conv_b=_row(small["ssm_conv_b"][j]),
                  dt_bias=_row(small["ssm_dt_bias"][j], LANES), a_log=_row(small["ssm_a_log"][j], LANES),
                  d_full=_row(jnp.repeat(small["ssm_d"][j], HD)), gate_norm=_row(small["ssm_gate_norm"][j]),
                  w_out=big["ssm_w_out"])
    else:
        pm = dict(norm=_row(small["cv_norm"][j]), w_pw1=big["cv_w_pw1"], b_pw1=_row(small["cv_b_pw1"][j]),
                  dw_w=jnp.pad(small["cv_dw_w"][j], ((0, 32 - KCV), (0, 0))), dw_b=_row(small["cv_dw_b"][j]),
                  ln_g=_row(small["cv_ln_g"][j]), ln_b=_row(small["cv_ln_b"][j]), w_pw2=big["cv_w_pw2"],
                  b_pw2=_row(small["cv_b_pw2"][j]))
    pf = (_row(small["ffn_norm"][i]), jnp.concatenate([big["ffn_w_gate"], big["ffn_w_up"]], axis=1), big["ffn_w_down"])
    return pm, pf


def _layer_fwd(i, h, pm, pf):
    j = i // 2
    if i % 2 == 0:
        h, sv = ssm_layer_fwd(h, pm, f"ssm{j}")
    else:
        h, sv = conv_layer_fwd(h, pm, f"cv{j}")
    h, sf = ffn_layer_fwd(h, *pf, f"ffn{i}")
    return h, (sv, sf)


def _layer_bwd(i, dh, dh_b, pm, pf, saved):
    j = i // 2
    sv, sf = saved
    dh, dh_b, gf = ffn_layer_bwd(dh, dh_b, sf, *pf, f"ffn{i}")
    g = {"ffn_" + k: val for k, val in gf.items()}
    if i % 2 == 0:
        dh, dh_b, gm = ssm_layer_bwd(dh, dh_b, sv, pm, f"ssm{j}")
        g.update({"ssm_" + k: val for k, val in gm.items()})
    else:
        dh, dh_b, gm = conv_layer_bwd(dh, dh_b, sv, pm, f"cv{j}")
        g.update({"cv_" + k: val for k, val in gm.items()})
    return dh, dh_b, g


def _local(x, tgt, full):
    h, tape = x, []
    for i in range(4):
        pm, pf = _layer_params(i, {n: full[n][l] for n, l in _group(i)}, full)
        h, saved = _layer_fwd(i, h, pm, pf)
        tape.append((pm, pf, saved))
    dh, dh_b, d_final, loss_row = loss_head(h, _row(full["final_norm"]), tgt, "loss_head")
    gl = {n: [None] * full[n].shape[0] for n in WEIGHTS if n != "final_norm"}
    for i in reversed(range(4)):
        dh, dh_b, g = _layer_bwd(i, dh, dh_b, *tape[i])
        for n, val in g.items():
            gl[n][i if n.startswith("ffn_") else i // 2] = val.reshape(full[n].shape[1:])
    grads = {n: jnp.stack(vs) for n, vs in gl.items()}
    grads["final_norm"] = d_final.reshape(D)
    return loss_row, dh, grads


def _step(x, tgt, w, m, v):
    idx = 4 * lax.axis_index("x") + 2 * lax.axis_index("y") + lax.axis_index("c")
    small_shapes = [w[n].shape for n in SMALL_SHARDED]
    small_pack = _pack([w[n] for n in SMALL_SHARDED], _rows_for(small_shapes))

    def blocks(i):
        return [w[n][l].astype(BF16) for n, l in _group(i)] + ([small_pack] if i == 0 else [])

    small = {n: w[n] for n in SMALL_REPL}
    h, tape = x, []
    arrs = blocks(0)
    st = gather_start(arrs, None, "gather0_start")
    after = st["token"]
    for i in range(4):
        lands = gather_pass(arrs, gather_wait(st, after, f"gather{i}_wait"), f"gather{i}_pass")
        if i < 3:
            arrs = blocks(i + 1)
            st = gather_start(arrs, lands[0], f"gather{i + 1}_start")
        if i == 0:
            per_dev = [_unpack(lands[-1][k], small_shapes) for k in range(N_DEV)]
            for q, n in enumerate(SMALL_SHARDED):
                small[n] = _unshard(jnp.stack([per_dev[k][q] for k in range(N_DEV)]), -1)
        big = {n: _unshard(z, LAYER_AXIS[n]) for (n, _), z in zip(_group(i), lands)}
        pm, pf = _layer_params(i, big, small)
        h, saved = _layer_fwd(i, h, pm, pf)
        tape.append((pm, pf, saved))
        after = h

    dh, dh_b, d_final, loss_row = loss_head(h, _row(w["final_norm"]), tgt, "loss_head")

    out = {}
    small_g = {n: [None] * w[n].shape[0] for n in SMALL if n != "final_norm"}

    def finish(i, st, after):
        by_chip, recv = scatter_wait(st, after, f"scatter{i}_wait")
        for (n, l), own, r in zip(_group(i), by_chip, recv):
            out[n] = adamw_layer(r, own, w[n], m[n], v[n], l, out.get(n), f"adamw_{n}{l}")

    pending = None
    for i in reversed(range(4)):
        dh, dh_b, g = _layer_bwd(i, dh, dh_b, *tape[i])
        if pending is not None:
            finish(*pending, dh)
        gsend = [_to_shards(g[n], LAYER_AXIS[n]) for n, _ in _group(i)]
        from_sibling = sibling_exchange(gsend, f"scatter{i}_sibling")
        by_chip = [pair_sum(a, r, f"pair_sum_{n}{l}") for (n, l), a, r in zip(_group(i), gsend, from_sibling)]
        pending = (i, scatter_start(by_chip, f"scatter{i}_start"))
        for n, val in g.items():
            if n not in BIG:
                small_g[n][i if n.startswith("ffn_") else i // 2] = val.reshape(small[n].shape[1:])
    finish(pending[0], pending[1], pending[1]["token"])

    grads = {n: jnp.stack(vs) for n, vs in small_g.items()}
    grads["final_norm"] = d_final.reshape(D)
    small_full_shapes = [grads[n].shape for n in SMALL] + [(1,)]
    packed = _pack([grads[n] for n in SMALL] + [loss_row[0, :1]], _rows_for(small_full_shapes))
    summed = sum_leading(all_gather([packed], "gather_small_grads")[0], "sum_small_grads")
    parts = _unpack(summed, small_full_shapes)
    loss = parts[-1][0]
    for n, g in zip(SMALL, parts[:-1]):
        if n in SMALL_SHARDED:
            s = w[n].shape[-1]
            g = lax.dynamic_slice_in_dim(g, idx * s, s, axis=g.ndim - 1)
        out[n] = adamw(g[None], w[n], m[n], v[n], f"adamw_{n}")
    return loss, dh, out


def kernel(x, ssm_norm, ssm_w_in, ssm_conv_w, ssm_conv_b, ssm_dt_bias, ssm_a_log, ssm_d, ssm_gate_norm, ssm_w_out, cv_norm, cv_w_pw1, cv_b_pw1, cv_dw_w, cv_dw_b, cv_ln_g, cv_ln_b, cv_w_pw2, cv_b_pw2, ffn_norm, ffn_w_gate, ffn_w_up, ffn_w_down, final_norm, loss_target, m_ssm_norm, m_ssm_w_in, m_ssm_conv_w, m_ssm_conv_b, m_ssm_dt_bias, m_ssm_a_log, m_ssm_d, m_ssm_gate_norm, m_ssm_w_out, m_cv_norm, m_cv_w_pw1, m_cv_b_pw1, m_cv_dw_w, m_cv_dw_b, m_cv_ln_g, m_cv_ln_b, m_cv_w_pw2, m_cv_b_pw2, m_ffn_norm, m_ffn_w_gate, m_ffn_w_up, m_ffn_w_down, m_final_norm, v_ssm_norm, v_ssm_w_in, v_ssm_conv_w, v_ssm_conv_b, v_ssm_dt_bias, v_ssm_a_log, v_ssm_d, v_ssm_gate_norm, v_ssm_w_out, v_cv_norm, v_cv_w_pw1, v_cv_b_pw1, v_cv_dw_w, v_cv_dw_b, v_cv_ln_g, v_cv_ln_b, v_cv_w_pw2, v_cv_b_pw2, v_ffn_norm, v_ffn_w_gate, v_ffn_w_up, v_ffn_w_down, v_final_norm):
    args = locals()
    w = {n: args[n] for n in WEIGHTS}
    m = {n: args["m_" + n] for n in WEIGHTS}
    v = {n: args["v_" + n] for n in WEIGHTS}
    loss, grad_x, out = _step(x[0], loss_target[0], w, m, v)
    res = [loss, grad_x[None]]
    for k in range(4):
        res += [out[n][k] for n in WEIGHTS]
    return tuple(res)
```

```python
import functools
import math

import jax
import jax.numpy as jnp
from jax import lax
from jax.experimental import pallas as pl
from jax.experimental.pallas import tpu as pltpu

F32 = jnp.float32
BF16 = jnp.bfloat16

N_DEV = 8
T = 2048
D = 1024
DI = 2048
NH = 32
HD = 64
NG = 4
GW = DI // NG
DS = 128
CONVD = DI + 2 * NG * DS
DINP = 2 * DI + 2 * NG * DS + NH
DINP_PAD = 5376
CH = 128
NCH = T // CH
DFF = 2816
KSSM = 4
KCV = 31
EPS = 1e-5
LANES = 128
VMEM_LIMIT = 56 * 1024 * 1024

ADAM_LR = 0.001
ADAM_B1 = 0.9
ADAM_B2 = 0.999
ADAM_EPS = 1e-08
ADAM_WD = 0.01
ADAM_STEP = 10

MESH = pl.DeviceIdType.MESH
ANY = pl.BlockSpec(memory_space=pl.ANY)


def _pcall(body, **kw):
    return pl.pallas_call(body, **kw)


def _cparams(sem):
    return pltpu.CompilerParams(dimension_semantics=sem, vmem_limit_bytes=VMEM_LIMIT)


def _pick(n, cands):
    for c in cands:
        if n % c == 0:
            return c
    raise ValueError(f"no tile for {n}")


def _sigmoid(x):
    return 1.0 / (1.0 + jnp.exp(-x))


def _silu(x):
    return x * _sigmoid(x)


def _dsilu(x):
    s = _sigmoid(x)
    return s * (1.0 + x * (1.0 - s))


_DIMS = {"nn": (((1,), (0,)), ((), ())), "nt": (((1,), (1,)), ((), ())), "tn": (((0,), (0,)), ((), ()))}


MM_VMEM_BUDGET = 40 * 1024 * 1024
MM_MAX_K = 3072


def _mm_tiles(M, N, K, out_bytes, has_res):
    tk = K if K <= MM_MAX_K else K // 2
    assert K % tk == 0 and tk % LANES == 0
    nk = K // tk
    best = None
    for tm in (2048, 1408, 1024, 512, 256, 128):
        if M % tm:
            continue
        for tn in (1408, 1024, 768, 512, 384, 256, 128):
            if N % tn:
                continue
            blocks = tm * tk * 2 + tk * tn * 2 + tm * tn * out_bytes + (tm * tn * 4 if has_res else 0)
            vmem = 2 * blocks + tm * tn * 4 * (2 if nk > 1 else 1)
            if vmem > MM_VMEM_BUDGET:
                continue
            traffic = (N // tn if nk > 1 else 1) * M * K + (M // tm) * N * K
            key = (-traffic, tm * tn)
            if best is None or key > best[0]:
                best = (key, tm, tn)
    assert best is not None, (M, N, K)
    return best[1], best[2], tk


def matmul(a, b, mode, *, name, bias=None, residual=None, out_dtype=F32):
    assert a.dtype == BF16 and b.dtype == BF16
    if mode == "nn":
        (M, K), (K2, N) = a.shape, b.shape
    elif mode == "nt":
        (M, K), (N, K2) = a.shape, b.shape
    else:
        (K, M), (K2, N) = a.shape, b.shape
    assert K == K2
    has_bias, has_res = bias is not None, residual is not None
    tm, tn, tk = _mm_tiles(M, N, K, jnp.dtype(out_dtype).itemsize, has_res)
    nk = K // tk
    dims = _DIMS[mode]

    def body(*refs):
        a_ref, b_ref = refs[0], refs[1]
        pos = 2
        bias_ref = res_ref = None
        if has_bias:
            bias_ref = refs[pos]
            pos += 1
        if has_res:
            res_ref = refs[pos]
            pos += 1
        o_ref = refs[pos]

        def finish(out):
            if has_bias:
                out = out + bias_ref[...]
            if has_res:
                out = out + res_ref[...]
            o_ref[...] = out.astype(o_ref.dtype)

        part = lax.dot_general(a_ref[...], b_ref[...], dims, preferred_element_type=F32)
        if nk == 1:
            finish(part)
            return
        acc = refs[pos + 1]
        k = pl.program_id(2)

        @pl.when(k == 0)
        def _():
            acc[...] = part

        @pl.when(jnp.logical_and(k > 0, k < nk - 1))
        def _():
            acc[...] += part

        @pl.when(k == nk - 1)
        def _():
            finish(acc[...] + part)

    if mode == "tn":
        a_spec = pl.BlockSpec((tk, tm), lambda i, j, k: (k, i))
    else:
        a_spec = pl.BlockSpec((tm, tk), lambda i, j, k: (i, k))
    if mode == "nt":
        b_spec = pl.BlockSpec((tn, tk), lambda i, j, k: (j, k))
    else:
        b_spec = pl.BlockSpec((tk, tn), lambda i, j, k: (k, j))
    in_specs, args = [a_spec, b_spec], [a, b]
    if has_bias:
        in_specs.append(pl.BlockSpec((1, tn), lambda i, j, k: (0, j)))
        args.append(bias.reshape(1, N).astype(F32))
    if has_res:
        in_specs.append(pl.BlockSpec((tm, tn), lambda i, j, k: (i, j)))
        args.append(residual)
    return _pcall(
        body, name=name, grid=(M // tm, N // tn, nk), in_specs=in_specs,
        out_specs=pl.BlockSpec((tm, tn), lambda i, j, k: (i, j)),
        out_shape=jax.ShapeDtypeStruct((M, N), out_dtype),
        scratch_shapes=[pltpu.VMEM((tm, tn), F32)] if nk > 1 else [],
        compiler_params=_cparams(("parallel", "parallel", "arbitrary")),
    )(*args)


def rowwise(fn, rows, bcasts, outs, accs=(), *, name, tm=256):
    n_rows, n_b, n_o, n_a = len(rows), len(bcasts), len(outs), len(accs)
    nt = T // tm

    def body(*refs):
        ins = [r[...] for r in refs[:n_rows + n_b]]
        res = fn(*ins)
        o_refs = refs[n_rows + n_b:n_rows + n_b + n_o]
        a_refs = refs[n_rows + n_b + n_o:]
        for r, v in zip(o_refs, res[:n_o]):
            r[...] = v.astype(r.dtype)
        if n_a:
            i = pl.program_id(0)

            @pl.when(i == 0)
            def _():
                for r in a_refs:
                    r[...] = jnp.zeros_like(r)

            for r, v in zip(a_refs, res[n_o:]):
                r[...] += v

    in_specs = [pl.BlockSpec((tm, w), functools.partial(lambda i, cb: (i, cb), cb=cb)) for (_, w, cb) in rows]
    in_specs += [pl.BlockSpec(b.shape, lambda i: (0, 0)) for b in bcasts]
    out_specs = [pl.BlockSpec((tm, w), lambda i: (i, 0)) for (w, _) in outs]
    out_specs += [pl.BlockSpec((1, w), lambda i: (0, 0)) for w in accs]
    out_shape = [jax.ShapeDtypeStruct((T, w), dt) for (w, dt) in outs]
    out_shape += [jax.ShapeDtypeStruct((1, w), F32) for w in accs]
    return _pcall(
        body, name=name, grid=(nt,), in_specs=in_specs, out_specs=out_specs, out_shape=out_shape,
        compiler_params=_cparams(("arbitrary",)),
    )(*[r[0] for r in rows], *bcasts)


def _full(a):
    return (a, a.shape[1], 0)


def _rsum(v):
    return jnp.sum(v, axis=0, keepdims=True)


def rms_fwd(h, g, name):
    def fn(x, g):
        r = lax.rsqrt(jnp.mean(x * x, axis=-1, keepdims=True) + EPS)
        return (x * r * g,)
    return rowwise(fn, [_full(h)], [g], [(D, BF16)], name=name)[0]


def rms_bwd(du, h, g, dres, name):
    def fn(du, x, dres, g):
        r = lax.rsqrt(jnp.mean(x * x, axis=-1, keepdims=True) + EPS)
        xh = x * r
        dxh = du * g
        dx = r * (dxh - xh * jnp.mean(dxh * xh, axis=-1, keepdims=True))
        dh = dres + dx
        return dh, dh, _rsum(du * xh)
    return rowwise(fn, [_full(du), _full(h), _full(dres)], [g], [(D, F32), (D, BF16)], [D], name=name)


def loss_head(h, g, tgt, name):
    def fn(x, tgt, g):
        r = lax.rsqrt(jnp.mean(x * x, axis=-1, keepdims=True) + EPS)
        xh = x * r
        err = xh * g - tgt
        lsum = jnp.sum(jnp.sum(err * err, axis=-1, keepdims=True), axis=0, keepdims=True) * (0.5 / D)
        dy = err * (1.0 / D)
        dxh = dy * g
        dx = r * (dxh - xh * jnp.mean(dxh * xh, axis=-1, keepdims=True))
        return dx, dx, _rsum(dy * xh), jnp.broadcast_to(lsum, (1, LANES))
    return rowwise(fn, [_full(h), _full(tgt)], [g], [(D, F32), (D, BF16)], [D, LANES], name=name)


def swiglu_fwd(g, u, name):
    def fn(g, u):
        return (_silu(g) * u,)
    return rowwise(fn, [_full(g), _full(u)], [], [(DFF, BF16)], name=name)[0]


def swiglu_bwd(dact, g, u, name):
    def fn(da, g, u):
        return da * u * _dsilu(g), da * _silu(g)
    return rowwise(fn, [_full(dact), _full(g), _full(u)], [], [(DFF, BF16), (DFF, BF16)], name=name)


def glu_fwd(hh, name):
    def fn(a, g):
        return (a * _sigmoid(g),)
    return rowwise(fn, [(hh, D, 0), (hh, D, 1)], [], [(D, F32)], name=name)[0]


def glu_bwd(dgl, hh, name):
    def fn(dgl, a, g):
        s = _sigmoid(g)
        dhh = jnp.concatenate([dgl * s, dgl * a * s * (1.0 - s)], axis=1)
        return dhh, _rsum(dhh)
    return rowwise(fn, [_full(dgl), (hh, D, 0), (hh, D, 1)], [], [(2 * D, BF16)], [2 * D], name=name)


def ln_silu_fwd(c2, g, b, name):
    def fn(x, g, b):
        mu = jnp.mean(x, axis=-1, keepdims=True)
        xc = x - mu
        r = lax.rsqrt(jnp.mean(xc * xc, axis=-1, keepdims=True) + EPS)
        return (_silu(xc * r * g + b),)
    return rowwise(fn, [_full(c2)], [g, b], [(D, BF16)], name=name)[0]


def ln_silu_bwd(ds, c2, dh, g, b, name):
    def fn(ds, x, dh, g, b):
        mu = jnp.mean(x, axis=-1, keepdims=True)
        xc = x - mu
        r = lax.rsqrt(jnp.mean(xc * xc, axis=-1, keepdims=True) + EPS)
        xh = xc * r
        dn = ds * _dsilu(xh * g + b)
        dxh = dn * g
        dx = r * (dxh - jnp.mean(dxh, axis=-1, keepdims=True) - xh * jnp.mean(dxh * xh, axis=-1, keepdims=True))
        return dx, _rsum(dn * xh), _rsum(dn), _rsum(dh)
    return rowwise(fn, [_full(ds), _full(c2), _full(dh)], [g, b], [(D, F32)], [D, D, D], name=name)


def gatenorm_fwd(y, zx, gn, name):
    def fn(y, z, gn):
        hg = y * _silu(z)
        parts = []
        for k in range(NG):
            hk = hg[:, k * GW:(k + 1) * GW]
            parts.append(hk * lax.rsqrt(jnp.mean(hk * hk, axis=-1, keepdims=True) + EPS))
        return (jnp.concatenate(parts, axis=1) * gn,)
    return rowwise(fn, [_full(y), (zx, DI, 0)], [gn], [(DI, BF16)], name=name)[0]


def gatenorm_bwd(dyn, y, zx, gn, name):
    def fn(dyn, y, z, gn):
        sz = _silu(z)
        hg = y * sz
        dxh = dyn * gn
        dhg, xhs = [], []
        for k in range(NG):
            sl = slice(k * GW, (k + 1) * GW)
            hk = hg[:, sl]
            r = lax.rsqrt(jnp.mean(hk * hk, axis=-1, keepdims=True) + EPS)
            xh = hk * r
            dk = dxh[:, sl]
            dhg.append(r * (dk - xh * jnp.mean(dk * xh, axis=-1, keepdims=True)))
            xhs.append(xh)
        dhg = jnp.concatenate(dhg, axis=1)
        xh = jnp.concatenate(xhs, axis=1)
        return dhg * sz, dhg * y * _dsilu(z), _rsum(dyn * xh)
    return rowwise(fn, [_full(dyn), _full(y), (zx, DI, 0)], [gn], [(DI, F32), (DI, BF16)], [DI], name=name)


def _softplus(x):
    return jnp.maximum(x, 0.0) + jnp.log(1.0 + jnp.exp(-jnp.abs(x)))


def _spread(v, e):
    hi = v.astype(BF16)
    r = v - hi.astype(F32)
    mid = r.astype(BF16)
    lo = (r - mid.astype(F32)).astype(BF16)
    return _dot(hi, e) + _dot(mid, e) + _dot(lo, e)


def dt_fwd(zx, dt_bias, a_log, name):
    heads = (jnp.arange(DI)[None, :] // HD == jnp.arange(LANES)[:, None]).astype(BF16)

    def fn(raw, bias, a_log, e):
        dt = _softplus(raw + bias)
        da = dt * (-jnp.exp(a_log))
        return dt, da, _spread(dt, e), _spread(da, e)

    return rowwise(fn, [(zx, LANES, (2 * DI + 2 * NG * DS) // LANES)], [dt_bias, a_log, heads],
                   [(LANES, F32), (LANES, F32), (DI, F32), (DI, F32)], name=name)


def dt_bwd(ddt, dda, dt, zx, dt_bias, a_log, name):
    def fn(ddt, dda, dt, raw, bias, a_log):
        a = -jnp.exp(a_log)
        draw = (ddt + dda * a) * _sigmoid(raw + bias)
        return draw, _rsum(draw), _rsum(dda * dt) * a
    return rowwise(fn, [_full(ddt), _full(dda), _full(dt), (zx, LANES, (2 * DI + 2 * NG * DS) // LANES)],
                   [dt_bias, a_log], [(LANES, BF16)], [LANES, LANES], name=name)


def headsum(v, name):
    def body(v_ref, o_ref):
        o_ref[...] = jnp.sum(v_ref[...], axis=1, keepdims=True)
    return _pcall(body, name=name, out_shape=jax.ShapeDtypeStruct((v.shape[0], 1), F32))(v)


CONV_ROWS = 256


def _shifted(win, o, rows):
    if o == 0:
        return win[0:rows]
    n = win.shape[0]
    return pltpu.roll(win, shift=n - o, axis=0)[0:rows]


def dwconv_fwd(x, x_cb0, w, b, K, ct, act, name):
    C = w.shape[1]
    pad = 8 if K <= 8 else 32
    KP = w.shape[0]
    n_out = 2 if act else 1

    def body(x_ref, w_ref, b_ref, *rest):
        o_refs, px = rest[:n_out], rest[n_out]
        px[0:pad, :] = jnp.zeros((pad, ct), F32)
        px[pad:pad + T, :] = x_ref[...]
        wv = w_ref[...]
        bv = b_ref[...]
        for r0 in range(0, T, CONV_ROWS):
            win = px[r0:r0 + CONV_ROWS + pad, :]
            acc = jnp.broadcast_to(bv, (CONV_ROWS, ct))
            for k in range(K):
                acc = acc + wv[k:k + 1, :] * _shifted(win, pad - (K - 1) + k, CONV_ROWS)
            o_refs[0][r0:r0 + CONV_ROWS, :] = acc
            if act:
                o_refs[1][r0:r0 + CONV_ROWS, :] = _silu(acc)

    return _pcall(
        body, name=name, grid=(C // ct,),
        in_specs=[pl.BlockSpec((T, ct), lambda j: (0, x_cb0 + j)), pl.BlockSpec((KP, ct), lambda j: (0, j)),
                  pl.BlockSpec((1, ct), lambda j: (0, j))],
        out_specs=[pl.BlockSpec((T, ct), lambda j: (0, j))] * n_out,
        out_shape=[jax.ShapeDtypeStruct((T, C), F32)] * n_out,
        scratch_shapes=[pltpu.VMEM((T + pad, ct), F32)],
        compiler_params=_cparams(("parallel",)),
    )(x, w, b)


def dwconv_bwd(dout, cpre, x, x_cb0, w, K, ct, act, out_dtype, name):
    C = w.shape[1]
    pad = 8 if K <= 8 else 32
    KP = w.shape[0]

    def body(*refs):
        if act:
            d_ref, c_ref, x_ref, w_ref, dx_ref, dw_ref, db_ref, px, pd = refs
        else:
            d_ref, x_ref, w_ref, dx_ref, dw_ref, db_ref, px, pd = refs
        px[0:pad, :] = jnp.zeros((pad, ct), F32)
        px[pad:pad + T, :] = x_ref[...]
        pd[T:T + pad, :] = jnp.zeros((pad, ct), F32)
        if act:
            pd[0:T, :] = d_ref[...] * _dsilu(c_ref[...])
        else:
            pd[0:T, :] = d_ref[...]
        wv = w_ref[...]
        dws = [jnp.zeros((1, ct), F32) for _ in range(K)]
        db = jnp.zeros((1, ct), F32)
        for r0 in range(0, T, CONV_ROWS):
            dwin = pd[r0:r0 + CONV_ROWS + pad, :]
            xwin = px[r0:r0 + CONV_ROWS + pad, :]
            dc = dwin[0:CONV_ROWS]
            db = db + _rsum(dc)
            acc = jnp.zeros((CONV_ROWS, ct), F32)
            for k in range(K):
                acc = acc + wv[k:k + 1, :] * _shifted(dwin, K - 1 - k, CONV_ROWS)
                dws[k] = dws[k] + _rsum(dc * _shifted(xwin, pad - (K - 1) + k, CONV_ROWS))
            dx_ref[r0:r0 + CONV_ROWS, :] = acc.astype(dx_ref.dtype)
        dw_ref[...] = jnp.zeros((KP, ct), F32)
        for k in range(K):
            dw_ref[k:k + 1, :] = dws[k]
        db_ref[...] = db

    col = pl.BlockSpec((T, ct), lambda j: (0, j))
    in_specs = [col] + ([col] if act else []) + [pl.BlockSpec((T, ct), lambda j: (0, x_cb0 + j)),
                                                 pl.BlockSpec((KP, ct), lambda j: (0, j))]
    args = [dout] + ([cpre] if act else []) + [x, w]
    return _pcall(
        body, name=name, grid=(C // ct,), in_specs=in_specs,
        out_specs=[col, pl.BlockSpec((KP, ct), lambda j: (0, j)), pl.BlockSpec((1, ct), lambda j: (0, j))],
        out_shape=[jax.ShapeDtypeStruct((T, C), out_dtype), jax.ShapeDtypeStruct((KP, C), F32),
                   jax.ShapeDtypeStruct((1, C), F32)],
        scratch_shapes=[pltpu.VMEM((T + pad, ct), F32), pltpu.VMEM((T + pad, ct), F32)],
        compiler_params=_cparams(("parallel",)),
    )(*args)


def _scan(a, axis, reverse=False):
    n = a.shape[axis]
    idx = lax.broadcasted_iota(jnp.int32, a.shape, axis)
    s = 1
    while s < n:
        if reverse:
            a = a + jnp.where(idx < n - s, pltpu.roll(a, shift=n - s, axis=axis), 0.0)
        else:
            a = a + jnp.where(idx >= s, pltpu.roll(a, shift=s, axis=axis), 0.0)
        s *= 2
    return a


_NT = _DIMS["nt"]
_TN = _DIMS["tn"]


def _dot(a, b, dims=_DIMS["nn"]):
    return lax.dot_general(a, b, dims, preferred_element_type=F32)


def ssd_fwd(xbc, dtx, dax, daT, dfull, name):
    def body(xbc_ref, dtx_ref, dax_ref, daT_ref, df_ref, y_ref, st_ref, S):
        ci = pl.program_id(0)

        @pl.when(ci == 0)
        def _():
            S[...] = jnp.zeros_like(S)

        row = lax.broadcasted_iota(jnp.int32, (CH, CH), 0)
        lane = lax.broadcasted_iota(jnp.int32, (CH, CH), 1)
        acsT = _scan(daT_ref[...], 1)
        for g in range(NG):
            c0 = g * GW
            xs = xbc_ref[:, c0:c0 + GW]
            acs = _scan(dax_ref[:, c0:c0 + GW], 0)
            Bm = xbc_ref[:, DI + g * DS:DI + (g + 1) * DS].astype(BF16)
            Cm = xbc_ref[:, DI + NG * DS + g * DS:DI + NG * DS + (g + 1) * DS].astype(BF16)
            xdt = xs * dtx_ref[:, c0:c0 + GW]
            atot = acs[CH - 1:CH, :]
            Sg = S[:, c0:c0 + GW]
            st_ref[:, c0:c0 + GW] = Sg
            CB = _dot(Cm, Bm, _NT)
            yg = jnp.exp(acs) * _dot(Cm, Sg.astype(BF16)) + xs * df_ref[:, c0:c0 + GW]
            xd = (xdt * jnp.exp(atot - acs)).astype(BF16)
            S[:, c0:c0 + GW] = jnp.exp(atot) * Sg + _dot(Bm, xd, _TN)
            xdt_b = xdt.astype(BF16)
            for r in range(NH // NG):
                h = g * (NH // NG) + r
                hs = slice(r * HD, (r + 1) * HD)
                seg = acs[:, r * HD:r * HD + 1] - acsT[h:h + 1, :]
                Lm = jnp.where(row >= lane, jnp.exp(jnp.minimum(seg, 0.0)), 0.0)
                yd = _dot((CB * Lm).astype(BF16), xdt_b[:, hs])
                y_ref[:, c0 + r * HD:c0 + (r + 1) * HD] = yg[:, hs] + yd

    return _pcall(
        body, name=name, grid=(NCH,),
        in_specs=[pl.BlockSpec((CH, CONVD), lambda i: (i, 0)), pl.BlockSpec((CH, DI), lambda i: (i, 0)),
                  pl.BlockSpec((CH, DI), lambda i: (i, 0)), pl.BlockSpec((NH, CH), lambda i: (0, i)),
                  pl.BlockSpec((1, DI), lambda i: (0, 0))],
        out_specs=[pl.BlockSpec((CH, DI), lambda i: (i, 0)), pl.BlockSpec((None, DS, DI), lambda i: (i, 0, 0))],
        out_shape=[jax.ShapeDtypeStruct((T, DI), F32), jax.ShapeDtypeStruct((NCH, DS, DI), F32)],
        scratch_shapes=[pltpu.VMEM((DS, DI), F32)],
        compiler_params=_cparams(("arbitrary",)),
    )(xbc, dtx, dax, daT, dfull)


def ssd_bwd(dy, xbc, dtx, dax, daT, dfull, states, name):
    def body(dy_ref, xbc_ref, dtx_ref, dax_ref, daT_ref, df_ref, st_ref, dxbc_ref, ddt_ref, dda_ref, dD_ref, dS):
        i = pl.program_id(0)

        @pl.when(i == 0)
        def _():
            dS[...] = jnp.zeros_like(dS)
            dD_ref[...] = jnp.zeros_like(dD_ref)

        row = lax.broadcasted_iota(jnp.int32, (CH, CH), 0)
        lane = lax.broadcasted_iota(jnp.int32, (CH, CH), 1)
        rowc = lax.broadcasted_iota(jnp.int32, (CH, 1), 0)
        acsT = _scan(daT_ref[...], 1)
        ddt_all = jnp.zeros((CH, LANES), F32)
        dacs_all = jnp.zeros((CH, LANES), F32)
        for g in range(NG):
            c0 = g * GW
            xs = xbc_ref[:, c0:c0 + GW]
            dtx = dtx_ref[:, c0:c0 + GW]
            acs = _scan(dax_ref[:, c0:c0 + GW], 0)
            Bm = xbc_ref[:, DI + g * DS:DI + (g + 1) * DS].astype(BF16)
            Cm = xbc_ref[:, DI + NG * DS + g * DS:DI + NG * DS + (g + 1) * DS].astype(BF16)
            xdt = xs * dtx
            atot = acs[CH - 1:CH, :]
            Sin = st_ref[:, c0:c0 + GW]
            dyg = dy_ref[:, c0:c0 + GW]
            dSo = dS[:, c0:c0 + GW]
            E = jnp.exp(acs)
            Etot = jnp.exp(atot)
            dec = jnp.exp(atot - acs)
            dD_ref[:, c0:c0 + GW] += _rsum(dyg * xs)
            dxs = dyg * df_ref[:, c0:c0 + GW]
            Sin_b = Sin.astype(BF16)
            dSo_b = dSo.astype(BF16)
            dY0 = dyg * E
            dY0_b = dY0.astype(BF16)
            dC = _dot(dY0_b, Sin_b, _NT)
            dS[:, c0:c0 + GW] = _dot(Cm, dY0_b, _TN) + Etot * dSo
            XD = xdt * dec
            dXD = _dot(Bm, dSo_b)
            dB = _dot(XD.astype(BF16), dSo_b, _NT)
            dxdt = dXD * dec
            Gq = dXD * XD
            dacs_x = dY0 * _dot(Cm, Sin_b) - Gq
            datot_x = _rsum(dSo * Sin) * Etot + _rsum(Gq)
            CB = _dot(Cm, Bm, _NT)
            CBT = _dot(Bm, Cm, _NT)
            dCB = jnp.zeros((CH, CH), F32)
            xdt_b = xdt.astype(BF16)
            dy_b = dyg.astype(BF16)
            for r in range(NH // NG):
                h = g * (NH // NG) + r
                hs = slice(r * HD, (r + 1) * HD)
                seg = acs[:, r * HD:r * HD + 1] - acsT[h:h + 1, :]
                Lm = jnp.where(row >= lane, jnp.exp(jnp.minimum(seg, 0.0)), 0.0)
                LmT = jnp.where(lane >= row, jnp.exp(jnp.minimum(-seg, 0.0)), 0.0)
                xr = xdt_b[:, hs]
                dyr = dy_b[:, hs]
                dM = _dot(dyr, xr, _NT)
                dMT = _dot(xr, dyr, _NT)
                dxdt_r = dxdt[:, hs] + _dot((CB * Lm).astype(BF16), dyr, _TN)
                dCB = dCB + dM * Lm
                dcol = (jnp.sum(dM * CB * Lm, axis=1, keepdims=True) - jnp.sum(dMT * CBT * LmT, axis=1, keepdims=True)
                        + jnp.sum(dacs_x[:, hs], axis=1, keepdims=True))
                dtot = jnp.sum(datot_x[:, hs], axis=1, keepdims=True)
                dcol = dcol + jnp.where(rowc == CH - 1, dtot, 0.0)
                ddt_col = jnp.sum(dxdt_r * xs[:, hs], axis=1, keepdims=True)
                ddt_all = ddt_all + jnp.where(lane == h, ddt_col, 0.0)
                dacs_all = dacs_all + jnp.where(lane == h, dcol, 0.0)
                dxbc_ref[:, c0 + r * HD:c0 + (r + 1) * HD] = dxs[:, hs] + dxdt_r * dtx[:, hs]
            dCB_b = dCB.astype(BF16)
            dxbc_ref[:, DI + g * DS:DI + (g + 1) * DS] = dB + _dot(dCB_b, Cm, _TN)
            dxbc_ref[:, DI + NG * DS + g * DS:DI + NG * DS + (g + 1) * DS] = dC + _dot(dCB_b, Bm)
        ddt_ref[...] = ddt_all
        dda_ref[...] = _scan(dacs_all, 0, reverse=True)

    last = NCH - 1
    return _pcall(
        body, name=name, grid=(NCH,),
        in_specs=[pl.BlockSpec((CH, DI), lambda i: (last - i, 0)), pl.BlockSpec((CH, CONVD), lambda i: (last - i, 0)),
                  pl.BlockSpec((CH, DI), lambda i: (last - i, 0)), pl.BlockSpec((CH, DI), lambda i: (last - i, 0)),
                  pl.BlockSpec((NH, CH), lambda i: (0, last - i)), pl.BlockSpec((1, DI), lambda i: (0, 0)),
                  pl.BlockSpec((None, DS, DI), lambda i: (last - i, 0, 0))],
        out_specs=[pl.BlockSpec((CH, CONVD), lambda i: (last - i, 0)), pl.BlockSpec((CH, LANES), lambda i: (last - i, 0)),
                   pl.BlockSpec((CH, LANES), lambda i: (last - i, 0)), pl.BlockSpec((1, DI), lambda i: (0, 0))],
        out_shape=[jax.ShapeDtypeStruct((T, CONVD), F32), jax.ShapeDtypeStruct((T, LANES), F32),
                   jax.ShapeDtypeStruct((T, LANES), F32), jax.ShapeDtypeStruct((1, DI), F32)],
        scratch_shapes=[pltpu.VMEM((DS, DI), F32)],
        compiler_params=_cparams(("arbitrary",)),
    )(dy, xbc, dtx, dax, daT, dfull, states)


def _as3d(shape):
    if len(shape) == 1:
        return (1, 1, shape[0])
    if len(shape) == 2:
        return (1, shape[0], shape[1])
    return (math.prod(shape[:-2]), shape[-2], shape[-1])


def _row_tile(R, C):
    if R * C <= 512 * 1024:
        return R
    return _pick(R, (512, 256, 128, 64, 32, 16, 8))


def adamw(parts, w, m, v, name):
    shape = w.shape
    L, R, C = _as3d(shape)
    P = parts.shape[0]
    tr = _row_tile(R, C)
    bc1 = 1.0 - ADAM_B1 ** ADAM_STEP
    bc2 = 1.0 - ADAM_B2 ** ADAM_STEP

    def body(p_ref, w_ref, m_ref, v_ref, g_out, d_out, m_out, v_out):
        g = p_ref[0].astype(F32)
        for k in range(1, P):
            g = g + p_ref[k].astype(F32)
        mn = ADAM_B1 * m_ref[...] + (1.0 - ADAM_B1) * g
        vn = ADAM_B2 * v_ref[...] + (1.0 - ADAM_B2) * (g * g)
        g_out[...] = g
        m_out[...] = mn
        v_out[...] = vn
        d_out[...] = -ADAM_LR * ((mn / bc1) / (jnp.sqrt(vn / bc2) + ADAM_EPS) + ADAM_WD * w_ref[...])

    blk = pl.BlockSpec((None, tr, C), lambda l, r: (l, r, 0))
    outs = _pcall(
        body, name=name, grid=(L, R // tr),
        in_specs=[pl.BlockSpec((P, None, tr, C), lambda l, r: (0, l, r, 0)), blk, blk, blk],
        out_specs=[blk] * 4, out_shape=[jax.ShapeDtypeStruct((L, R, C), F32)] * 4,
        compiler_params=_cparams(("parallel", "parallel")),
    )(parts.reshape(P, L, R, C), w.reshape(L, R, C), m.reshape(L, R, C), v.reshape(L, R, C))
    return [o.reshape(shape) for o in outs]


def adamw_layer(recv, own, w, m, v, layer, prev, name):
    L, R, C = w.shape
    P = recv.shape[0]
    tr = _row_tile(R, C)
    bc1 = 1.0 - ADAM_B1 ** ADAM_STEP
    bc2 = 1.0 - ADAM_B2 ** ADAM_STEP

    def body(r_ref, o_ref, w_ref, m_ref, v_ref, *rest):
        g_out, d_out, m_out, v_out = rest[-4:]
        g = o_ref[...].astype(F32)
        for k in range(P):
            g = g + r_ref[k].astype(F32)
        mn = ADAM_B1 * m_ref[...] + (1.0 - ADAM_B1) * g
        vn = ADAM_B2 * v_ref[...] + (1.0 - ADAM_B2) * (g * g)
        g_out[...] = g
        m_out[...] = mn
        v_out[...] = vn
        d_out[...] = -ADAM_LR * ((mn / bc1) / (jnp.sqrt(vn / bc2) + ADAM_EPS) + ADAM_WD * w_ref[...])

    slot = pl.BlockSpec((None, tr, C), lambda r: (layer, r, 0))
    own_spec = pl.BlockSpec((None, tr, C), lambda r: (2 * lax.axis_index("x") + lax.axis_index("y"), r, 0))
    in_specs = [pl.BlockSpec((P, tr, C), lambda r: (0, r, 0)), own_spec, slot, slot, slot]
    args = [recv, own, w, m, v]
    aliases = {}
    if prev is not None:
        in_specs += [ANY] * 4
        args += list(prev)
        aliases = {5 + k: k for k in range(4)}
    return _pcall(
        body, name=name, grid=(R // tr,), in_specs=in_specs, out_specs=[slot] * 4,
        out_shape=[jax.ShapeDtypeStruct((L, R, C), F32)] * 4, input_output_aliases=aliases,
        compiler_params=_cparams(("parallel",)),
    )(*args)


def sum_leading(parts, name):
    P, R, C = parts.shape

    def body(p_ref, o_ref):
        s = p_ref[0]
        for k in range(1, P):
            s = s + p_ref[k]
        o_ref[...] = s

    return _pcall(body, name=name, out_shape=jax.ShapeDtypeStruct((R, C), F32))(parts)


def pair_sum(gsend, recv, name):
    S = gsend.shape[1:]
    L, R, C = _as3d(S)
    tr = _row_tile(R, C)

    def body(g_ref, r_ref, o_ref):
        o_ref[...] = (g_ref[...].astype(F32) + r_ref[...].astype(F32)).astype(o_ref.dtype)

    blk = pl.BlockSpec((None, None, tr, C), lambda q, l, r: (q, l, r, 0))
    own = pl.BlockSpec((None, None, tr, C), lambda q, l, r: (2 * q + lax.axis_index("c"), l, r, 0))
    out = _pcall(
        body, name=name, grid=(4, L, R // tr), in_specs=[own, blk], out_specs=blk,
        out_shape=jax.ShapeDtypeStruct((4, L, R, C), BF16),
        compiler_params=_cparams(("parallel", "parallel", "parallel")),
    )(gsend.reshape(8, L, R, C), recv.reshape(4, L, R, C))
    return out.reshape((4,) + S)


def _place():
    return lax.axis_index("x"), lax.axis_index("y"), lax.axis_index("c")


def _other_chips(x, y):
    return [(1 - x, y), (x, 1 - y), (1 - x, 1 - y)]


def all_gather(arrs, name):
    n = len(arrs)

    def body(*refs):
        ins, outs = refs[:n], refs[n:2 * n]
        send_sems, recv_sems, local_sems = refs[2 * n:]
        x, y, c = _place()
        me, sibling = (x, y, c), (x, y, 1 - c)
        chips = _other_chips(x, y)

        def slot(a, px, py, pc):
            return outs[a].at[4 * px + 2 * py + pc]

        def copy(a, k, block, to, src=None):
            return pltpu.make_async_remote_copy(
                src_ref=slot(a, *block) if src is None else src, dst_ref=slot(a, *block),
                send_sem=send_sems.at[a, k], recv_sem=recv_sems.at[a, k], device_id=to, device_id_type=MESH)

        mine, first, passed = [], [], []
        for a in range(n):
            cp = pltpu.make_async_copy(ins[a], slot(a, *me), local_sems.at[a])
            cp.start()
            mine.append(cp)
            first.append(copy(a, 0, me, sibling, src=ins[a]))
            first += [copy(a, 1 + j, me, (*chip, c), src=ins[a]) for j, chip in enumerate(chips)]
        for cp in first:
            cp.start()
        for j, chip in enumerate(chips):
            for a in range(n):
                copy(a, 1 + j, (*chip, c), me).wait_recv()
                cp = copy(a, 4 + j, (*chip, c), sibling)
                cp.start()
                passed.append(cp)
        for a in range(n):
            copy(a, 0, sibling, me).wait_recv()
            for j, chip in enumerate(chips):
                copy(a, 4 + j, (*chip, 1 - c), me).wait_recv()
        for cp in first + passed:
            cp.wait_send()
        for cp in mine:
            cp.wait()

    return _pcall(
        body, name=name, in_specs=[ANY] * n, out_specs=[ANY] * n,
        out_shape=[jax.ShapeDtypeStruct((N_DEV,) + a.shape, a.dtype) for a in arrs],
        scratch_shapes=[pltpu.SemaphoreType.DMA((n, 7)), pltpu.SemaphoreType.DMA((n, 7)), pltpu.SemaphoreType.DMA((n,))],
    )(*arrs)


def sibling_exchange(gsends, name):
    n = len(gsends)

    def body(*refs):
        ins, outs = refs[:n], refs[n:2 * n]
        send_sems, recv_sems = refs[2 * n:]
        x, y, c = _place()
        copies = []
        for a in range(n):
            for q in range(4):
                cp = pltpu.make_async_remote_copy(
                    src_ref=ins[a].at[2 * q + 1 - c], dst_ref=outs[a].at[q],
                    send_sem=send_sems.at[a, q], recv_sem=recv_sems.at[a, q],
                    device_id=(x, y, 1 - c), device_id_type=MESH)
                cp.start()
                copies.append(cp)
        for cp in copies:
            cp.wait()

    return _pcall(
        body, name=name, in_specs=[ANY] * n, out_specs=[ANY] * n,
        out_shape=[jax.ShapeDtypeStruct((4,) + g.shape[1:], g.dtype) for g in gsends],
        scratch_shapes=[pltpu.SemaphoreType.DMA((n, 4)), pltpu.SemaphoreType.DMA((n, 4))],
    )(*gsends)


HBM =pl.BlockSpec(memory_space=pltpu.HBM)
SEM = pl.BlockSpec(memory_space=pltpu.SEMAPHORE)
EFFECT = pltpu.SideEffectType.DATAFLOW_SIDE_EFFECTING


def _in_hbm(a):
    return pltpu.with_memory_space_constraint(a, pltpu.HBM)


def _gather_peers(x, y, c):
    to = [(x, y, 1 - c)] + [(px, py, c) for px, py in _other_chips(x, y)]
    return to, [4 * px + 2 * py + pc for px, py, pc in to]


def gather_start(arrs, after, name):
    n = len(arrs)
    n_in = 2 * n + (1 if after is not None else 0)

    def body(*refs):
        srcs, lands = refs[:n], refs[n:2 * n]
        send_sems, recv_sems = refs[n_in], refs[n_in + 1]
        token = refs[-1]
        x, y, c = _place()
        to, _ = _gather_peers(x, y, c)
        me = 4 * x + 2 * y + c
        for a in range(n):
            for k, dev in enumerate(to):
                pltpu.make_async_remote_copy(
                    src_ref=srcs[a], dst_ref=lands[a].at[me], send_sem=send_sems.at[4 * a + k], recv_sem=recv_sems.at[4 * a + k],
                    device_id=dev, device_id_type=MESH).start()
        token[...] = jnp.zeros_like(token)

    zones = [lax.empty((N_DEV,) + a.shape, a.dtype) for a in arrs]
    args = [_in_hbm(a) for a in arrs] + [_in_hbm(z) for z in zones] + ([after] if after is not None else [])
    outs = _pcall(
        body, name=name,
        out_shape=(pltpu.SemaphoreType.DMA((4 * n,)), pltpu.SemaphoreType.DMA((4 * n,)),
                   *[pltpu.HBM(a.shape, a.dtype) for a in arrs], *[pltpu.HBM(z.shape, z.dtype) for z in zones],
                   jax.ShapeDtypeStruct((8, LANES), F32)),
        in_specs=[HBM] * (2 * n) + ([ANY] if after is not None else []),
        out_specs=(SEM, SEM, *[HBM] * (2 * n), pl.BlockSpec(memory_space=pltpu.VMEM)),
        input_output_aliases={i: 2 + i for i in range(2 * n)},
        compiler_params=pltpu.CompilerParams(has_side_effects=EFFECT),
    )(*args)
    return dict(send=outs[0], recv=outs[1], srcs=list(outs[2:2 + n]), lands=list(outs[2 + n:2 + 2 * n]), token=outs[-1])


def gather_wait(st, after, name):
    n = len(st["srcs"])

    def body(*refs):
        srcs, lands = refs[:n], refs[n:2 * n]
        send_sems, recv_sems = refs[2 * n], refs[2 * n + 1]
        x, y, c = _place()
        to, slots = _gather_peers(x, y, c)
        for a in range(n):
            for k, dev in enumerate(to):
                cp = pltpu.make_async_remote_copy(
                    src_ref=srcs[a], dst_ref=lands[a].at[slots[k]], send_sem=send_sems.at[4 * a + k],
                    recv_sem=recv_sems.at[4 * a + k], device_id=dev, device_id_type=MESH)
                cp.wait_send()
                cp.wait_recv()

    outs = _pcall(
        body, name=name,
        out_shape=(*[pltpu.HBM(a.shape, a.dtype) for a in st["srcs"]], *[pltpu.HBM(z.shape, z.dtype) for z in st["lands"]]),
        in_specs=[HBM] * (2 * n) + [SEM, SEM, ANY], out_specs=tuple([HBM] * (2 * n)),
        input_output_aliases={i: i for i in range(2 * n)},
        compiler_params=pltpu.CompilerParams(has_side_effects=EFFECT),
    )(*st["srcs"], *st["lands"], st["send"], st["recv"], after)
    return list(outs[n:])


def pass_start(zones, name):
    n = len(zones)

    def body(*refs):
        zs = refs[:n]
        send_sems, recv_sems = refs[n], refs[n + 1]
        token = refs[-1]
        x, y, c = _place()
        for a in range(n):
            for j, (px, py) in enumerate(_other_chips(x, y)):
                blk = zs[a].at[4 * px + 2 * py + c]
                pltpu.make_async_remote_copy(
                    src_ref=blk, dst_ref=blk, send_sem=send_sems.at[3 * a + j], recv_sem=recv_sems.at[3 * a + j],
                    device_id=(x, y, 1 - c), device_id_type=MESH).start()
        token[...] = jnp.zeros_like(token)

    outs = _pcall(
        body, name=name,
        out_shape=(pltpu.SemaphoreType.DMA((3 * n,)), pltpu.SemaphoreType.DMA((3 * n,)),
                   *[pltpu.HBM(z.shape, z.dtype) for z in zones], jax.ShapeDtypeStruct((8, LANES), F32)),
        in_specs=[HBM] * n, out_specs=(SEM, SEM, *[HBM] * n, pl.BlockSpec(memory_space=pltpu.VMEM)),
        input_output_aliases={i: 2 + i for i in range(n)},
        compiler_params=pltpu.CompilerParams(has_side_effects=EFFECT),
    )(*zones)
    return dict(send=outs[0], recv=outs[1], zones=list(outs[2:2 + n]), token=outs[-1])


def pass_wait(st, after, name):
    n = len(st["zones"])

    def body(*refs):
        zs = refs[:n]
        send_sems, recv_sems = refs[n], refs[n + 1]
        x, y, c = _place()
        for a in range(n):
            for j, (px, py) in enumerate(_other_chips(x, y)):
                cp = pltpu.make_async_remote_copy(
                    src_ref=zs[a].at[4 * px + 2 * py + c], dst_ref=zs[a].at[4 * px + 2 * py + 1 - c],
                    send_sem=send_sems.at[3 * a + j], recv_sem=recv_sems.at[3 * a + j],
                    device_id=(x, y, 1 - c), device_id_type=MESH)
                cp.wait_send()
                cp.wait_recv()

    outs = _pcall(
        body, name=name, out_shape=tuple(pltpu.HBM(z.shape, z.dtype) for z in st["zones"]),
        in_specs=[HBM] * n + [SEM, SEM, ANY], out_specs=tuple([HBM] * n),
        input_output_aliases={i: i for i in range(n)},
        compiler_params=pltpu.CompilerParams(has_side_effects=EFFECT),
    )(*st["zones"], st["send"], st["recv"], after)
    return list(outs)


def scatter_start(parts, name):
    n = len(parts)

    def body(*refs):
        srcs, lands = refs[:n], refs[n:2 * n]
        send_sems, recv_sems = refs[2 * n], refs[2 * n + 1]
        token = refs[-1]
        x, y, c = _place()
        for a in range(n):
            for j, (px, py) in enumerate(_other_chips(x, y)):
                pltpu.make_async_remote_copy(
                    src_ref=srcs[a].at[2 * px + py], dst_ref=lands[a].at[j], send_sem=send_sems.at[3 * a + j],
                    recv_sem=recv_sems.at[3 * a + j], device_id=(px, py, c), device_id_type=MESH).start()
        token[...] = jnp.zeros_like(token)

    zones = [lax.empty((3,) + p.shape[1:], p.dtype) for p in parts]
    outs = _pcall(
        body, name=name,
        out_shape=(pltpu.SemaphoreType.DMA((3 * n,)), pltpu.SemaphoreType.DMA((3 * n,)),
                   *[pltpu.HBM(p.shape, p.dtype) for p in parts], *[pltpu.HBM(z.shape, z.dtype) for z in zones],
                   jax.ShapeDtypeStruct((8, LANES), F32)),
        in_specs=[HBM] * (2 * n), out_specs=(SEM, SEM, *[HBM] * (2 * n), pl.BlockSpec(memory_space=pltpu.VMEM)),
        input_output_aliases={i: 2 + i for i in range(2 * n)},
        compiler_params=pltpu.CompilerParams(has_side_effects=EFFECT),
    )(*[_in_hbm(p) for p in parts], *[_in_hbm(z) for z in zones])
    return dict(send=outs[0], recv=outs[1], srcs=list(outs[2:2 + n]), lands=list(outs[2 + n:2 + 2 * n]), token=outs[-1])


def scatter_wait(st, after, name):
    n = len(st["srcs"])

    def body(*refs):
        srcs, lands = refs[:n], refs[n:2 * n]
        send_sems, recv_sems = refs[2 * n], refs[2 * n + 1]
        x, y, c = _place()
        for a in range(n):
            for j, (px, py) in enumerate(_other_chips(x, y)):
                cp = pltpu.make_async_remote_copy(
                    src_ref=srcs[a].at[2 * px + py], dst_ref=lands[a].at[j], send_sem=send_sems.at[3 * a + j],
                    recv_sem=recv_sems.at[3 * a + j], device_id=(px, py, c), device_id_type=MESH)
                cp.wait_send()
                cp.wait_recv()

    outs = _pcall(
        body, name=name,
        out_shape=(*[pltpu.HBM(a.shape, a.dtype) for a in st["srcs"]], *[pltpu.HBM(z.shape, z.dtype) for z in st["lands"]]),
        in_specs=[HBM] * (2 * n) + [SEM, SEM, ANY], out_specs=tuple([HBM] * (2 * n)),
        input_output_aliases={i: i for i in range(2 * n)},
        compiler_params=pltpu.CompilerParams(has_side_effects=EFFECT),
    )(*st["srcs"], *st["lands"], st["send"], st["recv"], after)
    return list(outs[:n]), list(outs[n:])


def _unshard(g, axis):
    nd = g.ndim - 1
    axis = axis % nd
    t = jnp.moveaxis(g, 0, axis)
    shp = list(g.shape[1:])
    shp[axis] *= N_DEV
    return t.reshape(shp)


def _to_shards(full, axis):
    axis = axis % full.ndim
    shp = list(full.shape)
    shp[axis:axis + 1] = [N_DEV, shp[axis] // N_DEV]
    return jnp.moveaxis(full.reshape(shp), axis, 0)


def _pack(arrs, rows):
    flat = jnp.concatenate([a.reshape(-1).astype(F32) for a in arrs])
    return jnp.pad(flat, (0, rows * LANES - flat.shape[0])).reshape(rows, LANES)


def _unpack(buf, shapes):
    flat = buf.reshape(-1)
    out, off = [], 0
    for s in shapes:
        n = math.prod(s)
        out.append(flat[off:off + n].reshape(s))
        off += n
    return out


def _rows_for(shapes):
    n = sum(math.prod(s) for s in shapes)
    return -(-n // (8 * LANES)) * 8


def _row(v, width=None):
    v = v.reshape(1, -1).astype(F32)
    if width is not None and v.shape[1] < width:
        v = jnp.pad(v, ((0, 0), (0, width - v.shape[1])))
    return v


def _after(order, width):
    if not order:
        return None
    t = order[0][0:1, 0:1]
    for o in order[1:]:
        t = t + o[0:1, 0:1]
    return jnp.broadcast_to(t, (1, width))


def _norm_after(norm, order):
    row = _after(order, norm.shape[1])
    return norm if row is None else norm + row


def ffn_layer_fwd(h, p, tag, order=()):
    u = rms_fwd(h, _norm_after(p["norm"], order), f"{tag}_rms")
    g = matmul(u, p["w_gate"], "nt", name=f"{tag}_gate")
    up = matmul(u, p["w_up"], "nt", name=f"{tag}_up")
    act = swiglu_fwd(g, up, f"{tag}_act")
    h2 = matmul(act, p["w_down"], "nn", residual=h, name=f"{tag}_down")
    return h2, (h, u, g, up, act)


def ffn_layer_bwd(dh, dh_b, saved, p, tag, order=()):
    h, u, g, up, act = saved
    dact = matmul(dh_b, p["w_down"], "nt", bias=_after(order, DFF), name=f"{tag}_dact")
    d_down = matmul(act, dh_b, "tn", out_dtype=BF16, name=f"{tag}_dwd")
    dg, dup = swiglu_bwd(dact, g, up, f"{tag}_dgu")
    du = matmul(dg, p["w_gate"], "nn", name=f"{tag}_dug")
    du = matmul(dup, p["w_up"], "nn", residual=du, name=f"{tag}_duu")
    d_gate = matmul(dg, u, "tn", out_dtype=BF16, name=f"{tag}_dwg")
    d_up = matmul(dup, u, "tn", out_dtype=BF16, name=f"{tag}_dwu")
    dh2, dh2_b, d_norm = rms_bwd(du, h, p["norm"], dh, f"{tag}_drms")
    return dh2, dh2_b, dict(norm=d_norm, w_gate=d_gate, w_up=d_up, w_down=d_down)


def conv_layer_fwd(h, p, tag, order=()):
    u = rms_fwd(h, _norm_after(p["norm"], order), f"{tag}_rms")
    hh = matmul(u, p["w_pw1"], "nn", bias=p["b_pw1"], name=f"{tag}_pw1")
    gl = glu_fwd(hh, f"{tag}_glu")
    c2 = dwconv_fwd(gl, 0, p["dw_w"], p["dw_b"], KCV, 128, False, f"{tag}_dw")[0]
    s = ln_silu_fwd(c2, p["ln_g"], p["ln_b"], f"{tag}_ln")
    h2 = matmul(s, p["w_pw2"], "nn", bias=p["b_pw2"], residual=h, name=f"{tag}_pw2")
    return h2, (h, u, hh, gl, c2, s)


def conv_layer_bwd(dh, dh_b, saved, p, tag, order=()):
    h, u, hh, gl, c2, s = saved
    ds = matmul(dh_b, p["w_pw2"], "nt", bias=_after(order, D), name=f"{tag}_ds")
    d_pw2 = matmul(s, dh_b, "tn", out_dtype=BF16, name=f"{tag}_dwpw2")
    dc2, d_lng, d_lnb, d_bpw2 = ln_silu_bwd(ds, c2, dh, p["ln_g"], p["ln_b"], f"{tag}_dln")
    dgl, d_dww, d_dwb = dwconv_bwd(dc2, None, gl, 0, p["dw_w"], KCV, 128, False, F32, f"{tag}_ddw")
    dhh, d_bpw1 = glu_bwd(dgl, hh, f"{tag}_dglu")
    du = matmul(dhh, p["w_pw1"], "nt", name=f"{tag}_du")
    d_pw1 = matmul(u, dhh, "tn", out_dtype=BF16, name=f"{tag}_dwpw1")
    dh2, dh2_b, d_norm = rms_bwd(du, h, p["norm"], dh, f"{tag}_drms")
    grads = dict(norm=d_norm, w_pw1=d_pw1, b_pw1=d_bpw1, dw_w=d_dww[:KCV], dw_b=d_dwb, ln_g=d_lng, ln_b=d_lnb,
                 w_pw2=d_pw2, b_pw2=d_bpw2)
    return dh2, dh2_b, grads


def ssm_layer_fwd(h, p, tag, order=()):
    u = rms_fwd(h, _norm_after(p["norm"], order), f"{tag}_rms")
    zx = matmul(u, p["w_in"], "nn", name=f"{tag}_in")
    cpre, xbc = dwconv_fwd(zx, DI // 512, p["conv_w"], p["conv_b"], KSSM, 512, True, f"{tag}_conv")
    dt, da, dtx, dax = dt_fwd(zx, p["dt_bias"], p["a_log"], f"{tag}_dt")
    daT = da[:, :NH].T
    y, states = ssd_fwd(xbc, dtx, dax, daT, p["d_full"], f"{tag}_ssd")
    yn = gatenorm_fwd(y, zx, p["gate_norm"], f"{tag}_gn")
    h2 = matmul(yn, p["w_out"], "nn", residual=h, name=f"{tag}_out")
    return h2, (h, u, zx, cpre, xbc, dt, dtx, dax, daT, y, states, yn)


def ssm_layer_bwd(dh, dh_b, saved, p, tag, order=()):
    h, u, zx, cpre, xbc, dt, dtx, dax, daT, y, states, yn = saved
    dyn = matmul(dh_b, p["w_out"], "nt", bias=_after(order, DI), name=f"{tag}_dyn")
    d_wout = matmul(yn, dh_b, "tn", out_dtype=BF16, name=f"{tag}_dwout")
    dy, dz, d_gn = gatenorm_bwd(dyn, y, zx, p["gate_norm"], f"{tag}_dgn")
    dxbc, ddt, dda, dD = ssd_bwd(dy, xbc, dtx, dax, daT, p["d_full"], states, f"{tag}_dssd")
    draw, d_dtb, d_alog = dt_bwd(ddt, dda, dt, zx, p["dt_bias"], p["a_log"], f"{tag}_ddt")
    dxpre, d_cw, d_cb = dwconv_bwd(dxbc, cpre, zx, DI // 512, p["conv_w"], KSSM, 512, True, BF16, f"{tag}_dconv")
    dzx = jnp.concatenate([dz, dxpre, draw, jnp.zeros((T, DINP_PAD - 2 * DI - 2 * NG * DS - LANES), BF16)], axis=1)
    du = matmul(dzx, p["w_in"], "nt", name=f"{tag}_du")
    d_win = matmul(u, dzx, "tn", out_dtype=BF16, name=f"{tag}_dwin")
    dh2, dh2_b, d_norm = rms_bwd(du, h, p["norm"], dh, f"{tag}_drms")
    d_d = headsum(dD.reshape(NH, HD), f"{tag}_dD").reshape(NH)
    grads = dict(norm=d_norm, w_in=d_win[:, :DINP], conv_w=d_cw[:KSSM], conv_b=d_cb, dt_bias=d_dtb[0, :NH],
                 a_log=d_alog[0, :NH], d=d_d, gate_norm=d_gn, w_out=d_wout)
    return dh2, dh2_b, grads


BIG = ["ssm_w_in", "ssm_w_out", "cv_w_pw1", "cv_w_pw2", "ffn_w_gate", "ffn_w_up", "ffn_w_down"]
TRANSPOSED = ("ffn_w_gate", "ffn_w_up")
LAYER_AXIS = {"ssm_w_in": -1, "ssm_w_out": 0, "cv_w_pw1": -1, "cv_w_pw2": 0, "ffn_w_gate": 0, "ffn_w_up": 0,
              "ffn_w_down": 0}
SMALL_SHARDED = ["ssm_conv_w", "cv_norm", "cv_b_pw1", "cv_dw_w", "cv_dw_b", "cv_ln_g", "cv_ln_b", "cv_b_pw2"]
SMALL_REPL = ["ssm_norm", "ssm_conv_b", "ssm_dt_bias", "ssm_a_log", "ssm_d", "ssm_gate_norm", "ffn_norm", "final_norm"]
WEIGHTS = ["ssm_norm", "ssm_w_in", "ssm_conv_w", "ssm_conv_b", "ssm_dt_bias", "ssm_a_log", "ssm_d", "ssm_gate_norm",
           "ssm_w_out", "cv_norm", "cv_w_pw1", "cv_b_pw1", "cv_dw_w", "cv_dw_b", "cv_ln_g", "cv_ln_b", "cv_w_pw2",
           "cv_b_pw2", "ffn_norm", "ffn_w_gate", "ffn_w_up", "ffn_w_down", "final_norm"]
SMALL = [n for n in WEIGHTS if n not in BIG]


N_STAGES = 8


def _stage_layer(s):
    i = s // 2
    if s % 2:
        return "ffn", i
    return ("ssm" if i % 2 == 0 else "cv"), i // 2


def _stage_group(s):
    fam, l = _stage_layer(s)
    names = {"ffn": ["ffn_w_gate", "ffn_w_up", "ffn_w_down"], "ssm": ["ssm_w_in", "ssm_w_out"],
             "cv": ["cv_w_pw1", "cv_w_pw2"]}[fam]
    return [(n, l) for n in names]


def _stage_params(s, big, small):
    fam, l = _stage_layer(s)
    if fam == "ffn":
        return dict(norm=_row(small["ffn_norm"][l]), w_gate=big["ffn_w_gate"], w_up=big["ffn_w_up"],
                    w_down=big["ffn_w_down"])
    if fam == "ssm":
        return dict(norm=_row(small["ssm_norm"][l]), w_in=jnp.pad(big["ssm_w_in"], ((0, 0), (0, DINP_PAD - DINP))),
                    conv_w=jnp.pad(small["ssm_conv_w"][l], ((0, 8 - KSSM), (0, 0))), conv_b=_row(small["ssm_conv_b"][l]),
                    dt_bias=_row(small["ssm_dt_bias"][l], LANES), a_log=_row(small["ssm_a_log"][l], LANES),
                    d_full=_row(jnp.repeat(small["ssm_d"][l], HD)), gate_norm=_row(small["ssm_gate_norm"][l]),
                    w_out=big["ssm_w_out"])
    return dict(norm=_row(small["cv_norm"][l]), w_pw1=big["cv_w_pw1"], b_pw1=_row(small["cv_b_pw1"][l]),
                dw_w=jnp.pad(small["cv_dw_w"][l], ((0, 32 - KCV), (0, 0))), dw_b=_row(small["cv_dw_b"][l]),
                ln_g=_row(small["cv_ln_g"][l]), ln_b=_row(small["cv_ln_b"][l]), w_pw2=big["cv_w_pw2"],
                b_pw2=_row(small["cv_b_pw2"][l]))


_STAGE_FWD = {"ffn": ffn_layer_fwd, "ssm": ssm_layer_fwd, "cv": conv_layer_fwd}
_STAGE_BWD = {"ffn": ffn_layer_bwd, "ssm": ssm_layer_bwd, "cv": conv_layer_bwd}


def _stage_fwd(s, h, p, order=()):
    fam, l = _stage_layer(s)
    return _STAGE_FWD[fam](h, p, f"{fam}{l}", order)


def _stage_bwd(s, dh, dh_b, p, saved, order=()):
    fam, l = _stage_layer(s)
    dh, dh_b, g = _STAGE_BWD[fam](dh, dh_b, saved, p, f"{fam}{l}", order)
    return dh, dh_b, {f"{fam}_{k}": val for k, val in g.items()}


def _local(x, tgt, full):
    h, tape = x, []
    for s in range(N_STAGES):
        big = {n: (full[n][l].T if n in TRANSPOSED else full[n][l]) for n, l in _stage_group(s)}
        p = _stage_params(s, big, full)
        h, saved = _stage_fwd(s, h, p)
        tape.append((p, saved))
    dh, dh_b, d_final, loss_row = loss_head(h, _row(full["final_norm"]), tgt, "loss_head")
    gl = {n: [None] * full[n].shape[0] for n in WEIGHTS if n != "final_norm"}
    for s in reversed(range(N_STAGES)):
        dh, dh_b, g = _stage_bwd(s, dh, dh_b, *tape[s])
        for n, val in g.items():
            val = val.T if n in TRANSPOSED else val
            gl[n][_stage_layer(s)[1]] = val.reshape(full[n].shape[1:])
    grads = {n: jnp.stack(vs) for n, vs in gl.items()}
    grads["final_norm"] = d_final.reshape(D)
    return loss_row, dh, grads


def _step(x, tgt, w, m, v):
    idx = 4 * lax.axis_index("x") + 2 * lax.axis_index("y") + lax.axis_index("c")
    small_shapes = [w[n].shape for n in SMALL_SHARDED]
    small_pack = _pack([w[n] for n in SMALL_SHARDED], _rows_for(small_shapes))

    def view(n, a):
        return jnp.swapaxes(a, 1, 2) if n in TRANSPOSED else a

    wv, mv, vv = ({n: view(n, t[n]) for n in BIG} for t in (w, m, v))

    def blocks(s):
        return [wv[n][l].astype(BF16) for n, l in _stage_group(s)] + ([small_pack] if s == 0 else [])

    arrs = [blocks(s) for s in range(N_STAGES)]
    first = gather_start(arrs[0], None, "gather0_start")
    passing = pass_start(gather_wait(first, first["token"], "gather0_wait"), "pass0_start")
    crossing = gather_start(arrs[1], passing["token"], "gather1_start")
    small = {n: w[n] for n in SMALL_REPL}
    h, tape, after = x, [], crossing["token"]
    for s in range(N_STAGES):
        zones = pass_wait(passing, after, f"pass{s}_wait")
        order = []
        if s + 1 < N_STAGES:
            passing = pass_start(gather_wait(crossing, zones[0], f"gather{s + 1}_wait"), f"pass{s + 1}_start")
            order.append(passing["token"])
        if s + 2 < N_STAGES:
            crossing = gather_start(arrs[s + 2], passing["token"], f"gather{s + 2}_start")
            order.append(crossing["token"])
        zones = [lax.dynamic_update_slice_in_dim(z, a[None], idx, 0) for z, a in zip(zones, arrs[s])]
        if s == 0:
            per_dev = [_unpack(zones[-1][k], small_shapes) for k in range(N_DEV)]
            for q, n in enumerate(SMALL_SHARDED):
                small[n] = _unshard(jnp.stack([per_dev[k][q] for k in range(N_DEV)]), -1)
        big = {n: _unshard(z, LAYER_AXIS[n]) for (n, _), z in zip(_stage_group(s), zones)}
        p = _stage_params(s, big, small)
        h, saved = _stage_fwd(s, h, p, order)
        tape.append((p, saved))
        after = h

    dh, dh_b, d_final, loss_row = loss_head(h, _row(w["final_norm"]), tgt, "loss_head")

    out = {}
    small_g = {n: [None] * w[n].shape[0] for n in SMALL if n != "final_norm"}

    def finish(s, st, after):
        by_chip, recv = scatter_wait(st, after, f"scatter{s}_wait")
        for (n, l), own, r in zip(_stage_group(s), by_chip, recv):
            out[n] = adamw_layer(r, own, wv[n], mv[n], vv[n], l, out.get(n), f"adamw_{n}{l}")

    pending, order = None, []
    for s in reversed(range(N_STAGES)):
        dh, dh_b, g = _stage_bwd(s, dh, dh_b, *tape[s], order)
        if pending is not None:
            finish(*pending, dh)
        gsend = [_to_shards(g[n], LAYER_AXIS[n]) for n, _ in _stage_group(s)]
        from_sibling = sibling_exchange(gsend, f"scatter{s}_sibling")
        by_chip = [pair_sum(a, r, f"pair_sum_{n}{l}") for (n, l), a, r in zip(_stage_group(s), gsend, from_sibling)]
        pending = (s, scatter_start(by_chip, f"scatter{s}_start"))
        order = [pending[1]["token"]]
        for n, val in g.items():
            if n not in BIG:
                small_g[n][_stage_layer(s)[1]] = val.reshape(small[n].shape[1:])
    finish(pending[0], pending[1], pending[1]["token"])
    for n in TRANSPOSED:
        out[n] = [jnp.swapaxes(a, 1, 2) for a in out[n]]

    grads = {n: jnp.stack(vs) for n, vs in small_g.items()}
    grads["final_norm"] = d_final.reshape(D)
    small_full_shapes = [grads[n].shape for n in SMALL] + [(1,)]
    packed = _pack([grads[n] for n in SMALL] + [loss_row[0, :1]], _rows_for(small_full_shapes))
    summed = sum_leading(all_gather([packed], "gather_small_grads")[0], "sum_small_grads")
    parts = _unpack(summed, small_full_shapes)
    loss = parts[-1][0]
    for n, g in zip(SMALL, parts[:-1]):
        if n in SMALL_SHARDED:
            s = w[n].shape[-1]
            g = lax.dynamic_slice_in_dim(g, idx * s, s, axis=g.ndim - 1)
        out[n] = adamw(g[None], w[n], m[n], v[n], f"adamw_{n}")
    return loss, dh, out


def kernel(x, ssm_norm, ssm_w_in, ssm_conv_w, ssm_conv_b, ssm_dt_bias, ssm_a_log, ssm_d, ssm_gate_norm, ssm_w_out, cv_norm, cv_w_pw1, cv_b_pw1, cv_dw_w, cv_dw_b, cv_ln_g, cv_ln_b, cv_w_pw2, cv_b_pw2, ffn_norm, ffn_w_gate, ffn_w_up, ffn_w_down, final_norm, loss_target, m_ssm_norm, m_ssm_w_in, m_ssm_conv_w, m_ssm_conv_b, m_ssm_dt_bias, m_ssm_a_log, m_ssm_d, m_ssm_gate_norm, m_ssm_w_out, m_cv_norm, m_cv_w_pw1, m_cv_b_pw1, m_cv_dw_w, m_cv_dw_b, m_cv_ln_g, m_cv_ln_b, m_cv_w_pw2, m_cv_b_pw2, m_ffn_norm, m_ffn_w_gate, m_ffn_w_up, m_ffn_w_down, m_final_norm, v_ssm_norm, v_ssm_w_in, v_ssm_conv_w, v_ssm_conv_b, v_ssm_dt_bias, v_ssm_a_log, v_ssm_d, v_ssm_gate_norm, v_ssm_w_out, v_cv_norm, v_cv_w_pw1, v_cv_b_pw1, v_cv_dw_w, v_cv_dw_b, v_cv_ln_g, v_cv_ln_b, v_cv_w_pw2, v_cv_b_pw2, v_ffn_norm, v_ffn_w_gate, v_ffn_w_up, v_ffn_w_down, v_final_norm):
    args = locals()
    w = {n: args[n] for n in WEIGHTS}
    m = {n: args["m_" + n] for n in WEIGHTS}
    v = {n: args["v_" + n] for n in WEIGHTS}
    loss, grad_x, out = _step(x[0], loss_target[0], w, m, v)
    res = [loss, grad_x[None]]
    for k in range(4):
        res += [out[n][k] for n in WEIGHTS]
    return tuple(res)
```

```python
import functools
import math

import jax
import jax.numpy as jnp
from jax import lax
from jax.experimental import pallas as pl
from jax.experimental.pallas import tpu as pltpu

F32 = jnp.float32
BF16 = jnp.bfloat16

N_DEV = 8
T = 2048
D = 1024
DI = 2048
NH = 32
HD = 64
NG = 4
GW = DI // NG
DS = 128
CONVD = DI + 2 * NG * DS
DINP = 2 * DI + 2 * NG * DS + NH
DINP_PAD = 5376
CH = 128
NCH = T // CH
DFF = 2816
KSSM = 4
KCV = 31
EPS = 1e-5
LANES = 128
VMEM_LIMIT = 56 * 1024 * 1024

ADAM_LR = 0.001
ADAM_B1 = 0.9
ADAM_B2 = 0.999
ADAM_EPS = 1e-08
ADAM_WD = 0.01
ADAM_STEP = 10

MESH = pl.DeviceIdType.MESH
ANY = pl.BlockSpec(memory_space=pl.ANY)


def _pcall(body, **kw):
    return pl.pallas_call(body, **kw)


def _cparams(sem):
    return pltpu.CompilerParams(dimension_semantics=sem, vmem_limit_bytes=VMEM_LIMIT)


def _pick(n, cands):
    for c in cands:
        if n % c == 0:
            return c
    raise ValueError(f"no tile for {n}")


def _sigmoid(x):
    return 1.0 / (1.0 + jnp.exp(-x))


def _silu(x):
    return x * _sigmoid(x)


def _dsilu(x):
    s = _sigmoid(x)
    return s * (1.0 + x * (1.0 - s))


_DIMS = {"nn": (((1,), (0,)), ((), ())), "nt": (((1,), (1,)), ((), ())), "tn": (((0,), (0,)), ((), ()))}


MM_VMEM_BUDGET = 40 * 1024 * 1024
MM_MAX_K = 3072


def _mm_tiles(M, N, K, out_bytes, has_res):
    tk = K if K <= MM_MAX_K else K // 2
    assert K % tk == 0 and tk % LANES == 0
    nk = K // tk
    best = None
    for tm in (2048, 1408, 1024, 512, 256, 128):
        if M % tm:
            continue
        for tn in (1408, 1024, 768, 512, 384, 256, 128):
            if N % tn:
                continue
            blocks = tm * tk * 2 + tk * tn * 2 + tm * tn * out_bytes + (tm * tn * 4 if has_res else 0)
            vmem = 2 * blocks + tm * tn * 4 * (2 if nk > 1 else 1)
            if vmem > MM_VMEM_BUDGET:
                continue
            traffic = (N // tn if nk > 1 else 1) * M * K + (M // tm) * N * K
            key = (-traffic, tm * tn)
            if best is None or key > best[0]:
                best = (key, tm, tn)
    assert best is not None, (M, N, K)
    return best[1], best[2], tk


def matmul(a, b, mode, *, name, bias=None, residual=None, out_dtype=F32):
    assert a.dtype == BF16 and b.dtype == BF16
    if mode == "nn":
        (M, K), (K2, N) = a.shape, b.shape
    elif mode == "nt":
        (M, K), (N, K2) = a.shape, b.shape
    else:
        (K, M), (K2, N) = a.shape, b.shape
    assert K == K2
    has_bias, has_res = bias is not None, residual is not None
    tm, tn, tk = _mm_tiles(M, N, K, jnp.dtype(out_dtype).itemsize, has_res)
    nk = K // tk
    dims = _DIMS[mode]

    def body(*refs):
        a_ref, b_ref = refs[0], refs[1]
        pos = 2
        bias_ref = res_ref = None
        if has_bias:
            bias_ref = refs[pos]
            pos += 1
        if has_res:
            res_ref = refs[pos]
            pos += 1
        o_ref = refs[pos]

        def finish(out):
            if has_bias:
                out = out + bias_ref[...]
            if has_res:
                out = out + res_ref[...]
            o_ref[...] = out.astype(o_ref.dtype)

        part = lax.dot_general(a_ref[...], b_ref[...], dims, preferred_element_type=F32)
        if nk == 1:
            finish(part)
            return
        acc = refs[pos + 1]
        k = pl.program_id(2)

        @pl.when(k == 0)
        def _():
            acc[...] = part

        @pl.when(jnp.logical_and(k > 0, k < nk - 1))
        def _():
            acc[...] += part

        @pl.when(k == nk - 1)
        def _():
            finish(acc[...] + part)

    if mode == "tn":
        a_spec = pl.BlockSpec((tk, tm), lambda i, j, k: (k, i))
    else:
        a_spec = pl.BlockSpec((tm, tk), lambda i, j, k: (i, k))
    if mode == "nt":
        b_spec = pl.BlockSpec((tn, tk), lambda i, j, k: (j, k))
    else:
        b_spec = pl.BlockSpec((tk, tn), lambda i, j, k: (k, j))
    in_specs, args = [a_spec, b_spec], [a, b]
    if has_bias:
        in_specs.append(pl.BlockSpec((1, tn), lambda i, j, k: (0, j)))
        args.append(bias.reshape(1, N).astype(F32))
    if has_res:
        in_specs.append(pl.BlockSpec((tm, tn), lambda i, j, k: (i, j)))
        args.append(residual)
    return _pcall(
        body, name=name, grid=(M // tm, N // tn, nk), in_specs=in_specs,
        out_specs=pl.BlockSpec((tm, tn), lambda i, j, k: (i, j)),
        out_shape=jax.ShapeDtypeStruct((M, N), out_dtype),
        scratch_shapes=[pltpu.VMEM((tm, tn), F32)] if nk > 1 else [],
        compiler_params=_cparams(("parallel", "parallel", "arbitrary")),
    )(*args)


def rowwise(fn, rows, bcasts, outs, accs=(), *, name, tm=256):
    n_rows, n_b, n_o, n_a = len(rows), len(bcasts), len(outs), len(accs)
    nt = T // tm

    def body(*refs):
        ins = [r[...] for r in refs[:n_rows + n_b]]
        res = fn(*ins)
        o_refs = refs[n_rows + n_b:n_rows + n_b + n_o]
        a_refs = refs[n_rows + n_b + n_o:]
        for r, v in zip(o_refs, res[:n_o]):
            r[...] = v.astype(r.dtype)
        if n_a:
            i = pl.program_id(0)

            @pl.when(i == 0)
            def _():
                for r in a_refs:
                    r[...] = jnp.zeros_like(r)

            for r, v in zip(a_refs, res[n_o:]):
                r[...] += v

    in_specs = [pl.BlockSpec((tm, w), functools.partial(lambda i, cb: (i, cb), cb=cb)) for (_, w, cb) in rows]
    in_specs += [pl.BlockSpec(b.shape, lambda i: (0, 0)) for b in bcasts]
    out_specs = [pl.BlockSpec((tm, w), lambda i: (i, 0)) for (w, _) in outs]
    out_specs += [pl.BlockSpec((1, w), lambda i: (0, 0)) for w in accs]
    out_shape = [jax.ShapeDtypeStruct((T, w), dt) for (w, dt) in outs]
    out_shape += [jax.ShapeDtypeStruct((1, w), F32) for w in accs]
    return _pcall(
        body, name=name, grid=(nt,), in_specs=in_specs, out_specs=out_specs, out_shape=out_shape,
        compiler_params=_cparams(("arbitrary",)),
    )(*[r[0] for r in rows], *bcasts)


def _full(a):
    return (a, a.shape[1], 0)


def _rsum(v):
    return jnp.sum(v, axis=0, keepdims=True)


def rms_fwd(h, g, name):
    def fn(x, g):
        r = lax.rsqrt(jnp.mean(x * x, axis=-1, keepdims=True) + EPS)
        return (x * r * g,)
    return rowwise(fn, [_full(h)], [g], [(D, BF16)], name=name)[0]


def rms_bwd(du, h, g, dres, name):
    def fn(du, x, dres, g):
        r = lax.rsqrt(jnp.mean(x * x, axis=-1, keepdims=True) + EPS)
        xh = x * r
        dxh = du * g
        dx = r * (dxh - xh * jnp.mean(dxh * xh, axis=-1, keepdims=True))
        dh = dres + dx
        return dh, dh, _rsum(du * xh)
    return rowwise(fn, [_full(du), _full(h), _full(dres)], [g], [(D, F32), (D, BF16)], [D], name=name)


def loss_head(h, g, tgt, name):
    def fn(x, tgt, g):
        r = lax.rsqrt(jnp.mean(x * x, axis=-1, keepdims=True) + EPS)
        xh = x * r
        err = xh * g - tgt
        lsum = jnp.sum(jnp.sum(err * err, axis=-1, keepdims=True), axis=0, keepdims=True) * (0.5 / D)
        dy = err * (1.0 / D)
        dxh = dy * g
        dx = r * (dxh - xh * jnp.mean(dxh * xh, axis=-1, keepdims=True))
        return dx, dx, _rsum(dy * xh), jnp.broadcast_to(lsum, (1, LANES))
    return rowwise(fn, [_full(h), _full(tgt)], [g], [(D, F32), (D, BF16)], [D, LANES], name=name)


def glu_fwd(hh, name):
    def fn(a, g):
        return (a * _sigmoid(g),)
    return rowwise(fn, [(hh, D, 0), (hh, D, 1)], [], [(D, F32)], name=name)[0]


def glu_bwd(dgl, hh, name):
    def fn(dgl, a, g):
        s = _sigmoid(g)
        dhh = jnp.concatenate([dgl * s, dgl * a * s * (1.0 - s)], axis=1)
        return dhh, _rsum(dhh)
    return rowwise(fn, [_full(dgl), (hh, D, 0), (hh, D, 1)], [], [(2 * D, BF16)], [2 * D], name=name)


def ln_silu_fwd(c2, g, b, name):
    def fn(x, g, b):
        mu = jnp.mean(x, axis=-1, keepdims=True)
        xc = x - mu
        r = lax.rsqrt(jnp.mean(xc * xc, axis=-1, keepdims=True) + EPS)
        return (_silu(xc * r * g + b),)
    return rowwise(fn, [_full(c2)], [g, b], [(D, BF16)], name=name)[0]


def ln_silu_bwd(ds, c2, dh, g, b, name):
    def fn(ds, x, dh, g, b):
        mu = jnp.mean(x, axis=-1, keepdims=True)
        xc = x - mu
        r = lax.rsqrt(jnp.mean(xc * xc, axis=-1, keepdims=True) + EPS)
        xh = xc * r
        dn = ds * _dsilu(xh * g + b)
        dxh = dn * g
        dx = r * (dxh - jnp.mean(dxh, axis=-1, keepdims=True) - xh * jnp.mean(dxh * xh, axis=-1, keepdims=True))
        return dx, _rsum(dn * xh), _rsum(dn), _rsum(dh)
    return rowwise(fn, [_full(ds), _full(c2), _full(dh)], [g, b], [(D, F32)], [D, D, D], name=name)


def gatenorm_fwd(y, zx, gn, name):
    def fn(y, z, gn):
        hg = y * _silu(z)
        parts = []
        for k in range(NG):
            hk = hg[:, k * GW:(k + 1) * GW]
            parts.append(hk * lax.rsqrt(jnp.mean(hk * hk, axis=-1, keepdims=True) + EPS))
        return (jnp.concatenate(parts, axis=1) * gn,)
    return rowwise(fn, [_full(y), (zx, DI, 0)], [gn], [(DI, BF16)], name=name)[0]


def gatenorm_bwd(dyn, y, zx, gn, name):
    def fn(dyn, y, z, gn):
        sz = _silu(z)
        hg = y * sz
        dxh = dyn * gn
        dhg, xhs = [], []
        for k in range(NG):
            sl = slice(k * GW, (k + 1) * GW)
            hk = hg[:, sl]
            r = lax.rsqrt(jnp.mean(hk * hk, axis=-1, keepdims=True) + EPS)
            xh = hk * r
            dk = dxh[:, sl]
            dhg.append(r * (dk - xh * jnp.mean(dk * xh, axis=-1, keepdims=True)))
            xhs.append(xh)
        dhg = jnp.concatenate(dhg, axis=1)
        xh = jnp.concatenate(xhs, axis=1)
        return dhg * sz, dhg * y * _dsilu(z), _rsum(dyn * xh)
    return rowwise(fn, [_full(dyn), _full(y), (zx, DI, 0)], [gn], [(DI, F32), (DI, BF16)], [DI], name=name)


def _softplus(x):
    return jnp.maximum(x, 0.0) + jnp.log(1.0 + jnp.exp(-jnp.abs(x)))


def _spread(v, e):
    hi = v.astype(BF16)
    r = v - hi.astype(F32)
    mid = r.astype(BF16)
    lo = (r - mid.astype(F32)).astype(BF16)
    return _dot(hi, e) + _dot(mid, e) + _dot(lo, e)


def dt_fwd(zx, dt_bias, a_log, name):
    heads = (jnp.arange(DI)[None, :] // HD == jnp.arange(LANES)[:, None]).astype(BF16)

    def fn(raw, bias, a_log, e):
        dt = _softplus(raw + bias)
        da = dt * (-jnp.exp(a_log))
        return dt, da, _spread(dt, e), _spread(da, e)

    return rowwise(fn, [(zx, LANES, (2 * DI + 2 * NG * DS) // LANES)], [dt_bias, a_log, heads],
                   [(LANES, F32), (LANES, F32), (DI, F32), (DI, F32)], name=name)


def dt_bwd(ddt, dda, dt, zx, dt_bias, a_log, name):
    def fn(ddt, dda, dt, raw, bias, a_log):
        a = -jnp.exp(a_log)
        draw = (ddt + dda * a) * _sigmoid(raw + bias)
        return draw, _rsum(draw), _rsum(dda * dt) * a
    return rowwise(fn, [_full(ddt), _full(dda), _full(dt), (zx, LANES, (2 * DI + 2 * NG * DS) // LANES)],
                   [dt_bias, a_log], [(LANES, BF16)], [LANES, LANES], name=name)


def headsum(v, name):
    def body(v_ref, o_ref):
        o_ref[...] = jnp.sum(v_ref[...], axis=1, keepdims=True)
    return _pcall(body, name=name, out_shape=jax.ShapeDtypeStruct((v.shape[0], 1), F32))(v)


CONV_ROWS = 256


def _shifted(win, o, rows):
    if o == 0:
        return win[0:rows]
    n = win.shape[0]
    return pltpu.roll(win, shift=n - o, axis=0)[0:rows]


def dwconv_fwd(x, x_cb0, w, b, K, ct, act, name):
    C = w.shape[1]
    pad = 8 if K <= 8 else 32
    KP = w.shape[0]
    n_out = 2 if act else 1

    def body(x_ref, w_ref, b_ref, *rest):
        o_refs, px = rest[:n_out], rest[n_out]
        px[0:pad, :] = jnp.zeros((pad, ct), F32)
        px[pad:pad + T, :] = x_ref[...]
        wv = w_ref[...]
        bv = b_ref[...]
        for r0 in range(0, T, CONV_ROWS):
            win = px[r0:r0 + CONV_ROWS + pad, :]
            acc = jnp.broadcast_to(bv, (CONV_ROWS, ct))
            for k in range(K):
                acc = acc + wv[k:k + 1, :] * _shifted(win, pad - (K - 1) + k, CONV_ROWS)
            o_refs[0][r0:r0 + CONV_ROWS, :] = acc
            if act:
                o_refs[1][r0:r0 + CONV_ROWS, :] = _silu(acc)

    return _pcall(
        body, name=name, grid=(C // ct,),
        in_specs=[pl.BlockSpec((T, ct), lambda j: (0, x_cb0 + j)), pl.BlockSpec((KP, ct), lambda j: (0, j)),
                  pl.BlockSpec((1, ct), lambda j: (0, j))],
        out_specs=[pl.BlockSpec((T, ct), lambda j: (0, j))] * n_out,
        out_shape=[jax.ShapeDtypeStruct((T, C), F32)] * n_out,
        scratch_shapes=[pltpu.VMEM((T + pad, ct), F32)],
        compiler_params=_cparams(("parallel",)),
    )(x, w, b)


def dwconv_bwd(dout, cpre, x, x_cb0, w, K, ct, act, out_dtype, name):
    C = w.shape[1]
    pad = 8 if K <= 8 else 32
    KP = w.shape[0]

    def body(*refs):
        if act:
            d_ref, c_ref, x_ref, w_ref, dx_ref, dw_ref, db_ref, px, pd = refs
        else:
            d_ref, x_ref, w_ref, dx_ref, dw_ref, db_ref, px, pd = refs
        px[0:pad, :] = jnp.zeros((pad, ct), F32)
        px[pad:pad + T, :] = x_ref[...]
        pd[T:T + pad, :] = jnp.zeros((pad, ct), F32)
        if act:
            pd[0:T, :] = d_ref[...] * _dsilu(c_ref[...])
        else:
            pd[0:T, :] = d_ref[...]
        wv = w_ref[...]
        dws = [jnp.zeros((1, ct), F32) for _ in range(K)]
        db = jnp.zeros((1, ct), F32)
        for r0 in range(0, T, CONV_ROWS):
            dwin = pd[r0:r0 + CONV_ROWS + pad, :]
            xwin = px[r0:r0 + CONV_ROWS + pad, :]
            dc = dwin[0:CONV_ROWS]
            db = db + _rsum(dc)
            acc = jnp.zeros((CONV_ROWS, ct), F32)
            for k in range(K):
                acc = acc + wv[k:k + 1, :] * _shifted(dwin, K - 1 - k, CONV_ROWS)
                dws[k] = dws[k] + _rsum(dc * _shifted(xwin, pad - (K - 1) + k, CONV_ROWS))
            dx_ref[r0:r0 + CONV_ROWS, :] = acc.astype(dx_ref.dtype)
        dw_ref[...] = jnp.zeros((KP, ct), F32)
        for k in range(K):
            dw_ref[k:k + 1, :] = dws[k]
        db_ref[...] = db

    col = pl.BlockSpec((T, ct), lambda j: (0, j))
    in_specs = [col] + ([col] if act else []) + [pl.BlockSpec((T, ct), lambda j: (0, x_cb0 + j)),
                                                 pl.BlockSpec((KP, ct), lambda j: (0, j))]
    args = [dout] + ([cpre] if act else []) + [x, w]
    return _pcall(
        body, name=name, grid=(C // ct,), in_specs=in_specs,
        out_specs=[col, pl.BlockSpec((KP, ct), lambda j: (0, j)), pl.BlockSpec((1, ct), lambda j: (0, j))],
        out_shape=[jax.ShapeDtypeStruct((T, C), out_dtype), jax.ShapeDtypeStruct((KP, C), F32),
                   jax.ShapeDtypeStruct((1, C), F32)],
        scratch_shapes=[pltpu.VMEM((T + pad, ct), F32), pltpu.VMEM((T + pad, ct), F32)],
        compiler_params=_cparams(("parallel",)),
    )(*args)


def _scan(a, axis, reverse=False):
    n = a.shape[axis]
    idx = lax.broadcasted_iota(jnp.int32, a.shape, axis)
    s = 1
    while s < n:
        if reverse:
            a = a + jnp.where(idx < n - s, pltpu.roll(a, shift=n - s, axis=axis), 0.0)
        else:
            a = a + jnp.where(idx >= s, pltpu.roll(a, shift=s, axis=axis), 0.0)
        s *= 2
    return a


_NT = _DIMS["nt"]
_TN = _DIMS["tn"]


def _dot(a, b, dims=_DIMS["nn"]):
    return lax.dot_general(a, b, dims, preferred_element_type=F32)


def ssd_fwd(xbc, dtx, dax, daT, dfull, name):
    def body(xbc_ref, dtx_ref, dax_ref, daT_ref, df_ref, y_ref, st_ref, S):
        ci = pl.program_id(0)

        @pl.when(ci == 0)
        def _():
            S[...] = jnp.zeros_like(S)

        row = lax.broadcasted_iota(jnp.int32, (CH, CH), 0)
        lane = lax.broadcasted_iota(jnp.int32, (CH, CH), 1)
        acsT = _scan(daT_ref[...], 1)
        for g in range(NG):
            c0 = g * GW
            xs = xbc_ref[:, c0:c0 + GW]
            acs = _scan(dax_ref[:, c0:c0 + GW], 0)
            Bm = xbc_ref[:, DI + g * DS:DI + (g + 1) * DS].astype(BF16)
            Cm = xbc_ref[:, DI + NG * DS + g * DS:DI + NG * DS + (g + 1) * DS].astype(BF16)
            xdt = xs * dtx_ref[:, c0:c0 + GW]
            atot = acs[CH - 1:CH, :]
            Sg = S[:, c0:c0 + GW]
            st_ref[:, c0:c0 + GW] = Sg
            CB = _dot(Cm, Bm, _NT)
            yg = jnp.exp(acs) * _dot(Cm, Sg.astype(BF16)) + xs * df_ref[:, c0:c0 + GW]
            xd = (xdt * jnp.exp(atot - acs)).astype(BF16)
            S[:, c0:c0 + GW] = jnp.exp(atot) * Sg + _dot(Bm, xd, _TN)
            xdt_b = xdt.astype(BF16)
            for r in range(NH // NG):
                h = g * (NH // NG) + r
                hs = slice(r * HD, (r + 1) * HD)
                seg = acs[:, r * HD:r * HD + 1] - acsT[h:h + 1, :]
                Lm = jnp.where(row >= lane, jnp.exp(jnp.minimum(seg, 0.0)), 0.0)
                yd = _dot((CB * Lm).astype(BF16), xdt_b[:, hs])
                y_ref[:, c0 + r * HD:c0 + (r + 1) * HD] = yg[:, hs] + yd

    return _pcall(
        body, name=name, grid=(NCH,),
        in_specs=[pl.BlockSpec((CH, CONVD), lambda i: (i, 0)), pl.BlockSpec((CH, DI), lambda i: (i, 0)),
                  pl.BlockSpec((CH, DI), lambda i: (i, 0)), pl.BlockSpec((NH, CH), lambda i: (0, i)),
                  pl.BlockSpec((1, DI), lambda i: (0, 0))],
        out_specs=[pl.BlockSpec((CH, DI), lambda i: (i, 0)), pl.BlockSpec((None, DS, DI), lambda i: (i, 0, 0))],
        out_shape=[jax.ShapeDtypeStruct((T, DI), F32), jax.ShapeDtypeStruct((NCH, DS, DI), F32)],
        scratch_shapes=[pltpu.VMEM((DS, DI), F32)],
        compiler_params=_cparams(("arbitrary",)),
    )(xbc, dtx, dax, daT, dfull)


def ssd_bwd(dy, xbc, dtx, dax, daT, dfull, states, name):
    def body(dy_ref, xbc_ref, dtx_ref, dax_ref, daT_ref, df_ref, st_ref, dxbc_ref, ddt_ref, dda_ref, dD_ref, dS):
        i = pl.program_id(0)

        @pl.when(i == 0)
        def _():
            dS[...] = jnp.zeros_like(dS)
            dD_ref[...] = jnp.zeros_like(dD_ref)

        row = lax.broadcasted_iota(jnp.int32, (CH, CH), 0)
        lane = lax.broadcasted_iota(jnp.int32, (CH, CH), 1)
        rowc = lax.broadcasted_iota(jnp.int32, (CH, 1), 0)
        acsT = _scan(daT_ref[...], 1)
        ddt_all = jnp.zeros((CH, LANES), F32)
        dacs_all = jnp.zeros((CH, LANES), F32)
        for g in range(NG):
            c0 = g * GW
            xs = xbc_ref[:, c0:c0 + GW]
            dtx = dtx_ref[:, c0:c0 + GW]
            acs = _scan(dax_ref[:, c0:c0 + GW], 0)
            Bm = xbc_ref[:, DI + g * DS:DI + (g + 1) * DS].astype(BF16)
            Cm = xbc_ref[:, DI + NG * DS + g * DS:DI + NG * DS + (g + 1) * DS].astype(BF16)
            xdt = xs * dtx
            atot = acs[CH - 1:CH, :]
            Sin = st_ref[:, c0:c0 + GW]
            dyg = dy_ref[:, c0:c0 + GW]
            dSo = dS[:, c0:c0 + GW]
            E = jnp.exp(acs)
            Etot = jnp.exp(atot)
            dec = jnp.exp(atot - acs)
            dD_ref[:, c0:c0 + GW] += _rsum(dyg * xs)
            dxs = dyg * df_ref[:, c0:c0 + GW]
            Sin_b = Sin.astype(BF16)
            dSo_b = dSo.astype(BF16)
            dY0 = dyg * E
            dY0_b = dY0.astype(BF16)
            dC = _dot(dY0_b, Sin_b, _NT)
            dS[:, c0:c0 + GW] = _dot(Cm, dY0_b, _TN) + Etot * dSo
            XD = xdt * dec
            dXD = _dot(Bm, dSo_b)
            dB = _dot(XD.astype(BF16), dSo_b, _NT)
            dxdt = dXD * dec
            Gq = dXD * XD
            dacs_x = dY0 * _dot(Cm, Sin_b) - Gq
            datot_x = _rsum(dSo * Sin) * Etot + _rsum(Gq)
            CB = _dot(Cm, Bm, _NT)
            CBT = _dot(Bm, Cm, _NT)
            dCB = jnp.zeros((CH, CH), F32)
            xdt_b = xdt.astype(BF16)
            dy_b = dyg.astype(BF16)
            for r in range(NH // NG):
                h = g * (NH // NG) + r
                hs = slice(r * HD, (r + 1) * HD)
                seg = acs[:, r * HD:r * HD + 1] - acsT[h:h + 1, :]
                Lm = jnp.where(row >= lane, jnp.exp(jnp.minimum(seg, 0.0)), 0.0)
                LmT = jnp.where(lane >= row, jnp.exp(jnp.minimum(-seg, 0.0)), 0.0)
                xr = xdt_b[:, hs]
                dyr = dy_b[:, hs]
                dM = _dot(dyr, xr, _NT)
                dMT = _dot(xr, dyr, _NT)
                dxdt_r = dxdt[:, hs] + _dot((CB * Lm).astype(BF16), dyr, _TN)
                dCB = dCB + dM * Lm
                dcol = (jnp.sum(dM * CB * Lm, axis=1, keepdims=True) - jnp.sum(dMT * CBT * LmT, axis=1, keepdims=True)
                        + jnp.sum(dacs_x[:, hs], axis=1, keepdims=True))
                dtot = jnp.sum(datot_x[:, hs], axis=1, keepdims=True)
                dcol = dcol + jnp.where(rowc == CH - 1, dtot, 0.0)
                ddt_col = jnp.sum(dxdt_r * xs[:, hs], axis=1, keepdims=True)
                ddt_all = ddt_all + jnp.where(lane == h, ddt_col, 0.0)
                dacs_all = dacs_all + jnp.where(lane == h, dcol, 0.0)
                dxbc_ref[:, c0 + r * HD:c0 + (r + 1) * HD] = dxs[:, hs] + dxdt_r * dtx[:, hs]
            dCB_b = dCB.astype(BF16)
            dxbc_ref[:, DI + g * DS:DI + (g + 1) * DS] = dB + _dot(dCB_b, Cm, _TN)
            dxbc_ref[:, DI + NG * DS + g * DS:DI + NG * DS + (g + 1) * DS] = dC + _dot(dCB_b, Bm)
        ddt_ref[...] = ddt_all
        dda_ref[...] = _scan(dacs_all, 0, reverse=True)

    last = NCH - 1
    return _pcall(
        body, name=name, grid=(NCH,),
        in_specs=[pl.BlockSpec((CH, DI), lambda i: (last - i, 0)), pl.BlockSpec((CH, CONVD), lambda i: (last - i, 0)),
                  pl.BlockSpec((CH, DI), lambda i: (last - i, 0)), pl.BlockSpec((CH, DI), lambda i: (last - i, 0)),
                  pl.BlockSpec((NH, CH), lambda i: (0, last - i)), pl.BlockSpec((1, DI), lambda i: (0, 0)),
                  pl.BlockSpec((None, DS, DI), lambda i: (last - i, 0, 0))],
        out_specs=[pl.BlockSpec((CH, CONVD), lambda i: (last - i, 0)), pl.BlockSpec((CH, LANES), lambda i: (last - i, 0)),
                   pl.BlockSpec((CH, LANES), lambda i: (last - i, 0)), pl.BlockSpec((1, DI), lambda i: (0, 0))],
        out_shape=[jax.ShapeDtypeStruct((T, CONVD), F32), jax.ShapeDtypeStruct((T, LANES), F32),
                   jax.ShapeDtypeStruct((T, LANES), F32), jax.ShapeDtypeStruct((1, DI), F32)],
        scratch_shapes=[pltpu.VMEM((DS, DI), F32)],
        compiler_params=_cparams(("arbitrary",)),
    )(dy, xbc, dtx, dax, daT, dfull, states)


def _as3d(shape):
    if len(shape) == 1:
        return (1, 1, shape[0])
    if len(shape) == 2:
        return (1, shape[0], shape[1])
    return (math.prod(shape[:-2]), shape[-2], shape[-1])


def _row_tile(R, C):
    if R * C <= 512 * 1024:
        return R
    return _pick(R, (512, 256, 128, 64, 32, 16, 8))


def adamw(parts, w, m, v, name):
    shape = w.shape
    L, R, C = _as3d(shape)
    P = parts.shape[0]
    tr = _row_tile(R, C)
    bc1 = 1.0 - ADAM_B1 ** ADAM_STEP
    bc2 = 1.0 - ADAM_B2 ** ADAM_STEP

    def body(p_ref, w_ref, m_ref, v_ref, g_out, d_out, m_out, v_out):
        g = p_ref[0].astype(F32)
        for k in range(1, P):
            g = g + p_ref[k].astype(F32)
        mn = ADAM_B1 * m_ref[...] + (1.0 - ADAM_B1) * g
        vn = ADAM_B2 * v_ref[...] + (1.0 - ADAM_B2) * (g * g)
        g_out[...] = g
        m_out[...] = mn
        v_out[...] = vn
        d_out[...] = -ADAM_LR * ((mn / bc1) / (jnp.sqrt(vn / bc2) + ADAM_EPS) + ADAM_WD * w_ref[...])

    blk = pl.BlockSpec((None, tr, C), lambda l, r: (l, r, 0))
    outs = _pcall(
        body, name=name, grid=(L, R // tr),
        in_specs=[pl.BlockSpec((P, None, tr, C), lambda l, r: (0, l, r, 0)), blk, blk, blk],
        out_specs=[blk] * 4, out_shape=[jax.ShapeDtypeStruct((L, R, C), F32)] * 4,
        compiler_params=_cparams(("parallel", "parallel")),
    )(parts.reshape(P, L, R, C), w.reshape(L, R, C), m.reshape(L, R, C), v.reshape(L, R, C))
    return [o.reshape(shape) for o in outs]


def adamw_layer(recv, own, w, m, v, layer, prev, name):
    L, R, C = w.shape
    P = recv.shape[0]
    tr = _row_tile(R, C)
    bc1 = 1.0 - ADAM_B1 ** ADAM_STEP
    bc2 = 1.0 - ADAM_B2 ** ADAM_STEP

    def body(r_ref, o_ref, w_ref, m_ref, v_ref, *rest):
        g_out, d_out, m_out, v_out = rest[-4:]
        g = o_ref[...].astype(F32)
        for k in range(P):
            g = g + r_ref[k].astype(F32)
        mn = ADAM_B1 * m_ref[...] + (1.0 - ADAM_B1) * g
        vn = ADAM_B2 * v_ref[...] + (1.0 - ADAM_B2) * (g * g)
        g_out[...] = g
        m_out[...] = mn
        v_out[...] = vn
        d_out[...] = -ADAM_LR * ((mn / bc1) / (jnp.sqrt(vn / bc2) + ADAM_EPS) + ADAM_WD * w_ref[...])

    slot = pl.BlockSpec((None, tr, C), lambda r: (layer, r, 0))
    own_spec = pl.BlockSpec((None, tr, C), lambda r: (2 * lax.axis_index("x") + lax.axis_index("y"), r, 0))
    in_specs = [pl.BlockSpec((P, tr, C), lambda r: (0, r, 0)), own_spec, slot, slot, slot]
    args = [recv, own, w, m, v]
    aliases = {}
    if prev is not None:
        in_specs += [ANY] * 4
        args += list(prev)
        aliases = {5 + k: k for k in range(4)}
    return _pcall(
        body, name=name, grid=(R // tr,), in_specs=in_specs, out_specs=[slot] * 4,
        out_shape=[jax.ShapeDtypeStruct((L, R, C), F32)] * 4, input_output_aliases=aliases,
        compiler_params=_cparams(("parallel",)),
    )(*args)


def sum_leading(parts, name):
    P, R, C = parts.shape

    def body(p_ref, o_ref):
        s = p_ref[0]
        for k in range(1, P):
            s = s + p_ref[k]
        o_ref[...] = s

    return _pcall(body, name=name, out_shape=jax.ShapeDtypeStruct((R, C), F32))(parts)


def pair_sum(gsend, recv, name):
    S = gsend.shape[1:]
    L, R, C = _as3d(S)
    tr = _row_tile(R, C)

    def body(g_ref, r_ref, o_ref):
        o_ref[...] = (g_ref[...].astype(F32) + r_ref[...].astype(F32)).astype(o_ref.dtype)

    blk = pl.BlockSpec((None, None, tr, C), lambda q, l, r: (q, l, r, 0))
    own = pl.BlockSpec((None, None, tr, C), lambda q, l, r: (2 * q + lax.axis_index("c"), l, r, 0))
    out = _pcall(
        body, name=name, grid=(4, L, R // tr), in_specs=[own, blk], out_specs=blk,
        out_shape=jax.ShapeDtypeStruct((4, L, R, C), BF16),
        compiler_params=_cparams(("parallel", "parallel", "parallel")),
    )(gsend.reshape(8, L, R, C), recv.reshape(4, L, R, C))
    return out.reshape((4,) + S)


def _place():
    return lax.axis_index("x"), lax.axis_index("y"), lax.axis_index("c")


def _other_chips(x, y):
    return [(1 - x, y), (x, 1 - y), (1 - x, 1 - y)]


def all_gather(arrs, name):
    n = len(arrs)

    def body(*refs):
        ins, outs = refs[:n], refs[n:2 * n]
        send_sems, recv_sems, local_sems = refs[2 * n:]
        x, y, c = _place()
        me, sibling = (x, y, c), (x, y, 1 - c)
        chips = _other_chips(x, y)

        def slot(a, px, py, pc):
            return outs[a].at[4 * px + 2 * py + pc]

        def copy(a, k, block, to, src=None):
            return pltpu.make_async_remote_copy(
                src_ref=slot(a, *block) if src is None else src, dst_ref=slot(a, *block),
                send_sem=send_sems.at[a, k], recv_sem=recv_sems.at[a, k], device_id=to, device_id_type=MESH)

        mine, first, passed = [], [], []
        for a in range(n):
            cp = pltpu.make_async_copy(ins[a], slot(a, *me), local_sems.at[a])
            cp.start()
            mine.append(cp)
            first.append(copy(a, 0, me, sibling, src=ins[a]))
            first += [copy(a, 1 + j, me, (*chip, c), src=ins[a]) for j, chip in enumerate(chips)]
        for cp in first:
            cp.start()
        for j, chip in enumerate(chips):
            for a in range(n):
                copy(a, 1 + j, (*chip, c), me).wait_recv()
                cp = copy(a, 4 + j, (*chip, c), sibling)
                cp.start()
                passed.append(cp)
        for a in range(n):
            copy(a, 0, sibling, me).wait_recv()
            for j, chip in enumerate(chips):
                copy(a, 4 + j, (*chip, 1 - c), me).wait_recv()
        for cp in first + passed:
            cp.wait_send()
        for cp in mine:
            cp.wait()

    return _pcall(
        body, name=name, in_specs=[ANY] * n, out_specs=[ANY] * n,
        out_shape=[jax.ShapeDtypeStruct((N_DEV,) + a.shape, a.dtype) for a in arrs],
        scratch_shapes=[pltpu.SemaphoreType.DMA((n, 7)), pltpu.SemaphoreType.DMA((n, 7)), pltpu.SemaphoreType.DMA((n,))],
    )(*arrs)


def sibling_exchange(gsends, name):
    n = len(gsends)

    def body(*refs):
        ins, outs = refs[:n], refs[n:2 * n]
        send_sems, recv_sems = refs[2 * n:]
        x, y, c = _place()
        copies = []
        for a in range(n):
            for q in range(4):
                cp = pltpu.make_async_remote_copy(
                    src_ref=ins[a].at[2 * q + 1 - c], dst_ref=outs[a].at[q],
                    send_sem=send_sems.at[a, q], recv_sem=recv_sems.at[a, q],
                    device_id=(x, y, 1 - c), device_id_type=MESH)
                cp.start()
                copies.append(cp)
        for cp in copies:
            cp.wait()

    return _pcall(
        body, name=name, in_specs=[ANY] * n, out_specs=[ANY] * n,
        out_shape=[jax.ShapeDtypeStruct((4,) + g.shape[1:], g.dtype) for g in gsends],
        scratch_shapes=[pltpu.SemaphoreType.DMA((n, 4)), pltpu.SemaphoreType.DMA((n, 4))],
    )(*gsends)


HBM =pl.BlockSpec(memory_space=pltpu.HBM)
SEM = pl.BlockSpec(memory_space=pltpu.SEMAPHORE)
EFFECT = pltpu.SideEffectType.DATAFLOW_SIDE_EFFECTING


def _in_hbm(a):
    return pltpu.with_memory_space_constraint(a, pltpu.HBM)


def _gather_peers(x, y, c):
    to = [(x, y, 1 - c)] + [(px, py, c) for px, py in _other_chips(x, y)]
    return to, [4 * px + 2 * py + pc for px, py, pc in to]


def gather_start(arrs, after, name):
    n = len(arrs)
    n_in = 2 * n + (1 if after is not None else 0)

    def body(*refs):
        srcs, lands = refs[:n], refs[n:2 * n]
        send_sems, recv_sems = refs[n_in], refs[n_in + 1]
        token = refs[-1]
        x, y, c = _place()
        to, _ = _gather_peers(x, y, c)
        me = 4 * x + 2 * y + c
        for a in range(n):
            for k, dev in enumerate(to):
                pltpu.make_async_remote_copy(
                    src_ref=srcs[a], dst_ref=lands[a].at[me], send_sem=send_sems.at[4 * a + k], recv_sem=recv_sems.at[4 * a + k],
                    device_id=dev, device_id_type=MESH).start()
        token[...] = jnp.zeros_like(token)

    zones = [lax.empty((N_DEV,) + a.shape, a.dtype) for a in arrs]
    args = [_in_hbm(a) for a in arrs] + [_in_hbm(z) for z in zones] + ([after] if after is not None else [])
    outs = _pcall(
        body, name=name,
        out_shape=(pltpu.SemaphoreType.DMA((4 * n,)), pltpu.SemaphoreType.DMA((4 * n,)),
                   *[pltpu.HBM(a.shape, a.dtype) for a in arrs], *[pltpu.HBM(z.shape, z.dtype) for z in zones],
                   jax.ShapeDtypeStruct((8, LANES), F32)),
        in_specs=[HBM] * (2 * n) + ([ANY] if after is not None else []),
        out_specs=(SEM, SEM, *[HBM] * (2 * n), pl.BlockSpec(memory_space=pltpu.VMEM)),
        input_output_aliases={i: 2 + i for i in range(2 * n)},
        compiler_params=pltpu.CompilerParams(has_side_effects=EFFECT),
    )(*args)
    return dict(send=outs[0], recv=outs[1], srcs=list(outs[2:2 + n]), lands=list(outs[2 + n:2 + 2 * n]), token=outs[-1])


def gather_wait(st, after, name):
    n = len(st["srcs"])

    def body(*refs):
        srcs, lands = refs[:n], refs[n:2 * n]
        send_sems, recv_sems = refs[2 * n], refs[2 * n + 1]
        x, y, c = _place()
        to, slots = _gather_peers(x, y, c)
        for a in range(n):
            for k, dev in enumerate(to):
                cp = pltpu.make_async_remote_copy(
                    src_ref=srcs[a], dst_ref=lands[a].at[slots[k]], send_sem=send_sems.at[4 * a + k],
                    recv_sem=recv_sems.at[4 * a + k], device_id=dev, device_id_type=MESH)
                cp.wait_send()
                cp.wait_recv()

    outs = _pcall(
        body, name=name,
        out_shape=(*[pltpu.HBM(a.shape, a.dtype) for a in st["srcs"]], *[pltpu.HBM(z.shape, z.dtype) for z in st["lands"]]),
        in_specs=[HBM] * (2 * n) + [SEM, SEM, ANY], out_specs=tuple([HBM] * (2 * n)),
        input_output_aliases={i: i for i in range(2 * n)},
        compiler_params=pltpu.CompilerParams(has_side_effects=EFFECT),
    )(*st["srcs"], *st["lands"], st["send"], st["recv"], after)
    return list(outs[n:])


def pass_start(zones, name):
    n = len(zones)

    def body(*refs):
        zs = refs[:n]
        send_sems, recv_sems = refs[n], refs[n + 1]
        token = refs[-1]
        x, y, c = _place()
        for a in range(n):
            for j, (px, py) in enumerate(_other_chips(x, y)):
                blk = zs[a].at[4 * px + 2 * py + c]
                pltpu.make_async_remote_copy(
                    src_ref=blk, dst_ref=blk, send_sem=send_sems.at[3 * a + j], recv_sem=recv_sems.at[3 * a + j],
                    device_id=(x, y, 1 - c), device_id_type=MESH).start()
        token[...] = jnp.zeros_like(token)

    outs = _pcall(
        body, name=name,
        out_shape=(pltpu.SemaphoreType.DMA((3 * n,)), pltpu.SemaphoreType.DMA((3 * n,)),
                   *[pltpu.HBM(z.shape, z.dtype) for z in zones], jax.ShapeDtypeStruct((8, LANES), F32)),
        in_specs=[HBM] * n, out_specs=(SEM, SEM, *[HBM] * n, pl.BlockSpec(memory_space=pltpu.VMEM)),
        input_output_aliases={i: 2 + i for i in range(n)},
        compiler_params=pltpu.CompilerParams(has_side_effects=EFFECT),
    )(*zones)
    return dict(send=outs[0], recv=outs[1], zones=list(outs[2:2 + n]), token=outs[-1])


def pass_wait(st, after, name):
    n = len(st["zones"])

    def body(*refs):
        zs = refs[:n]
        send_sems, recv_sems = refs[n], refs[n + 1]
        x, y, c = _place()
        for a in range(n):
            for j, (px, py) in enumerate(_other_chips(x, y)):
                cp = pltpu.make_async_remote_copy(
                    src_ref=zs[a].at[4 * px + 2 * py + c], dst_ref=zs[a].at[4 * px + 2 * py + 1 - c],
                    send_sem=send_sems.at[3 * a + j], recv_sem=recv_sems.at[3 * a + j],
                    device_id=(x, y, 1 - c), device_id_type=MESH)
                cp.wait_send()
                cp.wait_recv()

    outs = _pcall(
        body, name=name, out_shape=tuple(pltpu.HBM(z.shape, z.dtype) for z in st["zones"]),
        in_specs=[HBM] * n + [SEM, SEM, ANY], out_specs=tuple([HBM] * n),
        input_output_aliases={i: i for i in range(n)},
        compiler_params=pltpu.CompilerParams(has_side_effects=EFFECT),
    )(*st["zones"], st["send"], st["recv"], after)
    return list(outs)


def scatter_start(parts, name):
    n = len(parts)

    def body(*refs):
        srcs, lands = refs[:n], refs[n:2 * n]
        send_sems, recv_sems = refs[2 * n], refs[2 * n + 1]
        token = refs[-1]
        x, y, c = _place()
        for a in range(n):
            for j, (px, py) in enumerate(_other_chips(x, y)):
                pltpu.make_async_remote_copy(
                    src_ref=srcs[a].at[2 * px + py], dst_ref=lands[a].at[j], send_sem=send_sems.at[3 * a + j],
                    recv_sem=recv_sems.at[3 * a + j], device_id=(px, py, c), device_id_type=MESH).start()
        token[...] = jnp.zeros_like(token)

    zones = [lax.empty((3,) + p.shape[1:], p.dtype) for p in parts]
    outs = _pcall(
        body, name=name,
        out_shape=(pltpu.SemaphoreType.DMA((3 * n,)), pltpu.SemaphoreType.DMA((3 * n,)),
                   *[pltpu.HBM(p.shape, p.dtype) for p in parts], *[pltpu.HBM(z.shape, z.dtype) for z in zones],
                   jax.ShapeDtypeStruct((8, LANES), F32)),
        in_specs=[HBM] * (2 * n), out_specs=(SEM, SEM, *[HBM] * (2 * n), pl.BlockSpec(memory_space=pltpu.VMEM)),
        input_output_aliases={i: 2 + i for i in range(2 * n)},
        compiler_params=pltpu.CompilerParams(has_side_effects=EFFECT),
    )(*[_in_hbm(p) for p in parts], *[_in_hbm(z) for z in zones])
    return dict(send=outs[0], recv=outs[1], srcs=list(outs[2:2 + n]), lands=list(outs[2 + n:2 + 2 * n]), token=outs[-1])


def scatter_wait(st, after, name):
    n = len(st["srcs"])

    def body(*refs):
        srcs, lands = refs[:n], refs[n:2 * n]
        send_sems, recv_sems = refs[2 * n], refs[2 * n + 1]
        x, y, c = _place()
        for a in range(n):
            for j, (px, py) in enumerate(_other_chips(x, y)):
                cp = pltpu.make_async_remote_copy(
                    src_ref=srcs[a].at[2 * px + py], dst_ref=lands[a].at[j], send_sem=send_sems.at[3 * a + j],
                    recv_sem=recv_sems.at[3 * a + j], device_id=(px, py, c), device_id_type=MESH)
                cp.wait_send()
                cp.wait_recv()

    outs = _pcall(
        body, name=name,
        out_shape=(*[pltpu.HBM(a.shape, a.dtype) for a in st["srcs"]], *[pltpu.HBM(z.shape, z.dtype) for z in st["lands"]]),
        in_specs=[HBM] * (2 * n) + [SEM, SEM, ANY], out_specs=tuple([HBM] * (2 * n)),
        input_output_aliases={i: i for i in range(2 * n)},
        compiler_params=pltpu.CompilerParams(has_side_effects=EFFECT),
    )(*st["srcs"], *st["lands"], st["send"], st["recv"], after)
    return list(outs[:n]), list(outs[n:])


def _unshard(g, axis):
    nd = g.ndim - 1
    axis = axis % nd
    t = jnp.moveaxis(g, 0, axis)
    shp = list(g.shape[1:])
    shp[axis] *= N_DEV
    return t.reshape(shp)


def _to_shards(full, axis):
    axis = axis % full.ndim
    shp = list(full.shape)
    shp[axis:axis + 1] = [N_DEV, shp[axis] // N_DEV]
    return jnp.moveaxis(full.reshape(shp), axis, 0)


def _pack(arrs, rows):
    flat = jnp.concatenate([a.reshape(-1).astype(F32) for a in arrs])
    return jnp.pad(flat, (0, rows * LANES - flat.shape[0])).reshape(rows, LANES)


def _unpack(buf, shapes):
    flat = buf.reshape(-1)
    out, off = [], 0
    for s in shapes:
        n = math.prod(s)
        out.append(flat[off:off + n].reshape(s))
        off += n
    return out


def _rows_for(shapes):
    n = sum(math.prod(s) for s in shapes)
    return -(-n // (8 * LANES)) * 8


def _row(v, width=None):
    v = v.reshape(1, -1).astype(F32)
    if width is not None and v.shape[1] < width:
        v = jnp.pad(v, ((0, 0), (0, width - v.shape[1])))
    return v


def _after(order, width):
    if not order:
        return None
    t = order[0][0:1, 0:1]
    for o in order[1:]:
        t = t + o[0:1, 0:1]
    return jnp.broadcast_to(t, (1, width))


def _norm_after(norm, order):
    row = _after(order, norm.shape[1])
    return norm if row is None else norm + row


FFN_TN = 256


def ffn_in(h, norm, w_gate, w_up, name):
    def body(h_ref, n_ref, wg_ref, wu_ref, u_ref, g_ref, up_ref, act_ref, u_s):
        @pl.when(pl.program_id(0) == 0)
        def _():
            x = h_ref[...]
            r = lax.rsqrt(jnp.mean(x * x, axis=-1, keepdims=True) + EPS)
            u_s[...] = (x * r * n_ref[...]).astype(BF16)
            u_ref[...] = u_s[...]

        u = u_s[...]
        g = _dot(u, wg_ref[...], _NT)
        up = _dot(u, wu_ref[...], _NT)
        g_ref[...] = g.astype(BF16)
        up_ref[...] = up.astype(BF16)
        act_ref[...] = (_silu(g) * up).astype(BF16)

    whole = pl.BlockSpec((T, D), lambda j: (0, 0))
    wspec = pl.BlockSpec((FFN_TN, D), lambda j: (j, 0))
    col = pl.BlockSpec((T, FFN_TN), lambda j: (0, j))
    return _pcall(
        body, name=name, grid=(DFF // FFN_TN,), in_specs=[whole, pl.BlockSpec((1, D), lambda j: (0, 0)), wspec, wspec],
        out_specs=[whole, col, col, col],
        out_shape=[jax.ShapeDtypeStruct((T, D), BF16)] + [jax.ShapeDtypeStruct((T, DFF), BF16)] * 3,
        scratch_shapes=[pltpu.VMEM((T, D), BF16)], compiler_params=_cparams(("arbitrary",)),
    )(h, norm, w_gate, w_up)


def ffn_back(dh_b, w_down, g, up, after_row, name):
    has_row = after_row is not None

    def body(*refs):
        dh_ref, wd_ref, g_ref, up_ref = refs[:4]
        dg_ref, dup_ref = refs[-2:]
        da = _dot(dh_ref[...], wd_ref[...], _NT)
        if has_row:
            da = da + refs[4][...]
        g = g_ref[...].astype(F32)
        dg_ref[...] = (da * up_ref[...].astype(F32) * _dsilu(g)).astype(BF16)
        dup_ref[...] = (da * _silu(g)).astype(BF16)

    col = pl.BlockSpec((T, FFN_TN), lambda j: (0, j))
    in_specs = [pl.BlockSpec((T, D), lambda j: (0, 0)), pl.BlockSpec((FFN_TN, D), lambda j: (j, 0)), col, col]
    args = [dh_b, w_down, g, up]
    if has_row:
        in_specs.append(pl.BlockSpec((1, FFN_TN), lambda j: (0, j)))
        args.append(after_row)
    return _pcall(
        body, name=name, grid=(DFF // FFN_TN,), in_specs=in_specs, out_specs=[col, col],
        out_shape=[jax.ShapeDtypeStruct((T, DFF), BF16)] * 2, compiler_params=_cparams(("parallel",)),
    )(*args)


def ffn_layer_fwd(h, p, tag, order=()):
    u, g, up, act = ffn_in(h, _norm_after(p["norm"], order), p["w_gate"], p["w_up"], f"{tag}_in")
    h2 = matmul(act, p["w_down"], "nn", residual=h, name=f"{tag}_down")
    return h2, (h, u, g, up, act)


def ffn_layer_bwd(dh, dh_b, saved, p, tag, order=()):
    h, u, g, up, act = saved
    d_down = matmul(act, dh_b, "tn", out_dtype=BF16, name=f"{tag}_dwd")
    dg, dup = ffn_back(dh_b, p["w_down"], g, up, _after(order, DFF), f"{tag}_back")
    du = matmul(dg, p["w_gate"], "nn", name=f"{tag}_dug")
    du = matmul(dup, p["w_up"], "nn", residual=du, name=f"{tag}_duu")
    d_gate = matmul(dg, u, "tn", out_dtype=BF16, name=f"{tag}_dwg")
    d_up = matmul(dup, u, "tn", out_dtype=BF16, name=f"{tag}_dwu")
    dh2, dh2_b, d_norm = rms_bwd(du, h, p["norm"], dh, f"{tag}_drms")
    return dh2, dh2_b, dict(norm=d_norm, w_gate=d_gate, w_up=d_up, w_down=d_down)


def conv_layer_fwd(h, p, tag, order=()):
    u = rms_fwd(h, _norm_after(p["norm"], order), f"{tag}_rms")
    hh = matmul(u, p["w_pw1"], "nn", bias=p["b_pw1"], name=f"{tag}_pw1")
    gl = glu_fwd(hh, f"{tag}_glu")
    c2 = dwconv_fwd(gl, 0, p["dw_w"], p["dw_b"], KCV, 128, False, f"{tag}_dw")[0]
    s = ln_silu_fwd(c2, p["ln_g"], p["ln_b"], f"{tag}_ln")
    h2 = matmul(s, p["w_pw2"], "nn", bias=p["b_pw2"], residual=h, name=f"{tag}_pw2")
    return h2, (h, u, hh, gl, c2, s)


def conv_layer_bwd(dh, dh_b, saved, p, tag, order=()):
    h, u, hh, gl, c2, s = saved
    ds = matmul(dh_b, p["w_pw2"], "nt", bias=_after(order, D), name=f"{tag}_ds")
    d_pw2 = matmul(s, dh_b, "tn", out_dtype=BF16, name=f"{tag}_dwpw2")
    dc2, d_lng, d_lnb, d_bpw2 = ln_silu_bwd(ds, c2, dh, p["ln_g"], p["ln_b"], f"{tag}_dln")
    dgl, d_dww, d_dwb = dwconv_bwd(dc2, None, gl, 0, p["dw_w"], KCV, 128, False, F32, f"{tag}_ddw")
    dhh, d_bpw1 = glu_bwd(dgl, hh, f"{tag}_dglu")
    du = matmul(dhh, p["w_pw1"], "nt", name=f"{tag}_du")
    d_pw1 = matmul(u, dhh, "tn", out_dtype=BF16, name=f"{tag}_dwpw1")
    dh2, dh2_b, d_norm = rms_bwd(du, h, p["norm"], dh, f"{tag}_drms")
    grads = dict(norm=d_norm, w_pw1=d_pw1, b_pw1=d_bpw1, dw_w=d_dww[:KCV], dw_b=d_dwb, ln_g=d_lng, ln_b=d_lnb,
                 w_pw2=d_pw2, b_pw2=d_bpw2)
    return dh2, dh2_b, grads


def ssm_layer_fwd(h, p, tag, order=(), mid=None):
    u = rms_fwd(h, _norm_after(p["norm"], order), f"{tag}_rms")
    zx = matmul(u, p["w_in"], "nn", name=f"{tag}_in")
    cpre, xbc = dwconv_fwd(zx, DI // 512, p["conv_w"], p["conv_b"], KSSM, 512, True, f"{tag}_conv")
    dt, da, dtx, dax = dt_fwd(zx, p["dt_bias"], p["a_log"], f"{tag}_dt")
    daT = da[:, :NH].T
    y, states = ssd_fwd(xbc, dtx, dax, daT, p["d_full"], f"{tag}_ssd")
    gate_norm = p["gate_norm"] if mid is None else _norm_after(p["gate_norm"], mid(y))
    yn = gatenorm_fwd(y, zx, gate_norm, f"{tag}_gn")
    h2 = matmul(yn, p["w_out"], "nn", residual=h, name=f"{tag}_out")
    return h2, (h, u, zx, cpre, xbc, dt, dtx, dax, daT, y, states, yn)


def ssm_layer_bwd(dh, dh_b, saved, p, tag, order=()):
    h, u, zx, cpre, xbc, dt, dtx, dax, daT, y, states, yn = saved
    dyn = matmul(dh_b, p["w_out"], "nt", bias=_after(order, DI), name=f"{tag}_dyn")
    d_wout = matmul(yn, dh_b, "tn", out_dtype=BF16, name=f"{tag}_dwout")
    dy, dz, d_gn = gatenorm_bwd(dyn, y, zx, p["gate_norm"], f"{tag}_dgn")
    dxbc, ddt, dda, dD = ssd_bwd(dy, xbc, dtx, dax, daT, p["d_full"], states, f"{tag}_dssd")
    draw, d_dtb, d_alog = dt_bwd(ddt, dda, dt, zx, p["dt_bias"], p["a_log"], f"{tag}_ddt")
    dxpre, d_cw, d_cb = dwconv_bwd(dxbc, cpre, zx, DI // 512, p["conv_w"], KSSM, 512, True, BF16, f"{tag}_dconv")
    dzx = jnp.concatenate([dz, dxpre, draw, jnp.zeros((T, DINP_PAD - 2 * DI - 2 * NG * DS - LANES), BF16)], axis=1)
    du = matmul(dzx, p["w_in"], "nt", name=f"{tag}_du")
    d_win = matmul(u, dzx, "tn", out_dtype=BF16, name=f"{tag}_dwin")
    dh2, dh2_b, d_norm = rms_bwd(du, h, p["norm"], dh, f"{tag}_drms")
    d_d = headsum(dD.reshape(NH, HD), f"{tag}_dD").reshape(NH)
    grads = dict(norm=d_norm, w_in=d_win[:, :DINP], conv_w=d_cw[:KSSM], conv_b=d_cb, dt_bias=d_dtb[0, :NH],
                 a_log=d_alog[0, :NH], d=d_d, gate_norm=d_gn, w_out=d_wout)
    return dh2, dh2_b, grads


BIG = ["ssm_w_in", "ssm_w_out", "cv_w_pw1", "cv_w_pw2", "ffn_w_gate", "ffn_w_up", "ffn_w_down"]
TRANSPOSED = ("ffn_w_gate", "ffn_w_up")
LAYER_AXIS = {"ssm_w_in": -1, "ssm_w_out": 0, "cv_w_pw1": -1, "cv_w_pw2": 0, "ffn_w_gate": 0, "ffn_w_up": 0,
              "ffn_w_down": 0}
SMALL_SHARDED = ["ssm_conv_w", "cv_norm", "cv_b_pw1", "cv_dw_w", "cv_dw_b", "cv_ln_g", "cv_ln_b", "cv_b_pw2"]
SMALL_REPL = ["ssm_norm", "ssm_conv_b", "ssm_dt_bias", "ssm_a_log", "ssm_d", "ssm_gate_norm", "ffn_norm", "final_norm"]
WEIGHTS = ["ssm_norm", "ssm_w_in", "ssm_conv_w", "ssm_conv_b", "ssm_dt_bias", "ssm_a_log", "ssm_d", "ssm_gate_norm",
           "ssm_w_out", "cv_norm", "cv_w_pw1", "cv_b_pw1", "cv_dw_w", "cv_dw_b", "cv_ln_g", "cv_ln_b", "cv_w_pw2",
           "cv_b_pw2", "ffn_norm", "ffn_w_gate", "ffn_w_up", "ffn_w_down", "final_norm"]
SMALL = [n for n in WEIGHTS if n not in BIG]


N_STAGES = 8


def _stage_layer(s):
    i = s // 2
    if s % 2:
        return "ffn", i
    return ("ssm" if i % 2 == 0 else "cv"), i // 2


def _stage_group(s):
    fam, l = _stage_layer(s)
    names = {"ffn": ["ffn_w_gate", "ffn_w_up", "ffn_w_down"], "ssm": ["ssm_w_in", "ssm_w_out"],
             "cv": ["cv_w_pw1", "cv_w_pw2"]}[fam]
    return [(n, l) for n in names]


def _stage_params(s, big, small):
    fam, l = _stage_layer(s)
    if fam == "ffn":
        return dict(norm=_row(small["ffn_norm"][l]), w_gate=big["ffn_w_gate"], w_up=big["ffn_w_up"],
                    w_down=big["ffn_w_down"])
    if fam == "ssm":
        return dict(norm=_row(small["ssm_norm"][l]), w_in=jnp.pad(big["ssm_w_in"], ((0, 0), (0, DINP_PAD - DINP))),
                    conv_w=jnp.pad(small["ssm_conv_w"][l], ((0, 8 - KSSM), (0, 0))), conv_b=_row(small["ssm_conv_b"][l]),
                    dt_bias=_row(small["ssm_dt_bias"][l], LANES), a_log=_row(small["ssm_a_log"][l], LANES),
                    d_full=_row(jnp.repeat(small["ssm_d"][l], HD)), gate_norm=_row(small["ssm_gate_norm"][l]),
                    w_out=big["ssm_w_out"])
    return dict(norm=_row(small["cv_norm"][l]), w_pw1=big["cv_w_pw1"], b_pw1=_row(small["cv_b_pw1"][l]),
                dw_w=jnp.pad(small["cv_dw_w"][l], ((0, 32 - KCV), (0, 0))), dw_b=_row(small["cv_dw_b"][l]),
                ln_g=_row(small["cv_ln_g"][l]), ln_b=_row(small["cv_ln_b"][l]), w_pw2=big["cv_w_pw2"],
                b_pw2=_row(small["cv_b_pw2"][l]))


_STAGE_FWD = {"ffn": ffn_layer_fwd, "ssm": ssm_layer_fwd, "cv": conv_layer_fwd}
_STAGE_BWD = {"ffn": ffn_layer_bwd, "ssm": ssm_layer_bwd, "cv": conv_layer_bwd}


def _stage_fwd(s, h, p, order=(), mid=None):
    fam, l = _stage_layer(s)
    if mid is not None:
        return ssm_layer_fwd(h, p, f"{fam}{l}", order, mid)
    return _STAGE_FWD[fam](h, p, f"{fam}{l}", order)


def _stage_bwd(s, dh, dh_b, p, saved, order=()):
    fam, l = _stage_layer(s)
    dh, dh_b, g = _STAGE_BWD[fam](dh, dh_b, saved, p, f"{fam}{l}", order)
    return dh, dh_b, {f"{fam}_{k}": val for k, val in g.items()}


def _local(x, tgt, full):
    h, tape = x, []
    for s in range(N_STAGES):
        big = {n: (full[n][l].T if n in TRANSPOSED else full[n][l]) for n, l in _stage_group(s)}
        p = _stage_params(s, big, full)
        h, saved = _stage_fwd(s, h, p)
        tape.append((p, saved))
    dh, dh_b, d_final, loss_row = loss_head(h, _row(full["final_norm"]), tgt, "loss_head")
    gl = {n: [None] * full[n].shape[0] for n in WEIGHTS if n != "final_norm"}
    for s in reversed(range(N_STAGES)):
        dh, dh_b, g = _stage_bwd(s, dh, dh_b, *tape[s])
        for n, val in g.items():
            val = val.T if n in TRANSPOSED else val
            gl[n][_stage_layer(s)[1]] = val.reshape(full[n].shape[1:])
    grads = {n: jnp.stack(vs) for n, vs in gl.items()}
    grads["final_norm"] = d_final.reshape(D)
    return loss_row, dh, grads


def _step(x, tgt, w, m, v):
    idx = 4 * lax.axis_index("x") + 2 * lax.axis_index("y") + lax.axis_index("c")
    small_shapes = [w[n].shape for n in SMALL_SHARDED]
    small_pack = _pack([w[n] for n in SMALL_SHARDED], _rows_for(small_shapes))

    def view(n, a):
        return jnp.swapaxes(a, 1, 2) if n in TRANSPOSED else a

    wv, mv, vv = ({n: view(n, t[n]) for n in BIG} for t in (w, m, v))

    def blocks(s):
        return [wv[n][l].astype(BF16) for n, l in _stage_group(s)] + ([small_pack] if s == 0 else [])

    arrs = [blocks(s) for s in range(N_STAGES)]
    first = gather_start(arrs[0], None, "gather0_start")
    passing = pass_start(gather_wait(first, first["token"], "gather0_wait"), "pass0_start")
    crossing = gather_start(arrs[1], passing["token"], "gather1_start")
    small = {n: w[n] for n in SMALL_REPL}
    flight = dict(passing=passing, crossing=crossing)

    def advance(s, after):
        tokens = []
        if s + 1 < N_STAGES:
            landed = gather_wait(flight["crossing"], after, f"gather{s + 1}_wait")
            flight["passing"] = pass_start(landed, f"pass{s + 1}_start")
            tokens.append(flight["passing"]["token"])
        if s + 2 < N_STAGES:
            flight["crossing"] = gather_start(arrs[s + 2], flight["passing"]["token"], f"gather{s + 2}_start")
            tokens.append(flight["crossing"]["token"])
        return tokens

    h, tape, after = x, [], crossing["token"]
    for s in range(N_STAGES):
        zones = pass_wait(flight["passing"], after, f"pass{s}_wait")
        order = advance(s, zones[0]) if s else []
        mid = functools.partial(advance, 0) if s == 0 else None
        zones = [lax.dynamic_update_slice_in_dim(z, a[None], idx, 0) for z, a in zip(zones, arrs[s])]
        if s == 0:
            per_dev = [_unpack(zones[-1][k], small_shapes) for k in range(N_DEV)]
            for q, n in enumerate(SMALL_SHARDED):
                small[n] = _unshard(jnp.stack([per_dev[k][q] for k in range(N_DEV)]), -1)
        big = {n: _unshard(z, LAYER_AXIS[n]) for (n, _), z in zip(_stage_group(s), zones)}
        p = _stage_params(s, big, small)
        h, saved = _stage_fwd(s, h, p, order, mid)
        tape.append((p, saved))
        after = h

    dh, dh_b, d_final, loss_row = loss_head(h, _row(w["final_norm"]), tgt, "loss_head")

    out = {}
    small_g = {n: [None] * w[n].shape[0] for n in SMALL if n != "final_norm"}

    def finish(s, st, after):
        by_chip, recv = scatter_wait(st, after, f"scatter{s}_wait")
        for (n, l), own, r in zip(_stage_group(s), by_chip, recv):
            out[n] = adamw_layer(r, own, wv[n], mv[n], vv[n], l, out.get(n), f"adamw_{n}{l}")

    started, order = [], []
    for s in reversed(range(N_STAGES)):
        dh, dh_b, g = _stage_bwd(s, dh, dh_b, *tape[s], order)
        gsend = [_to_shards(g[n], LAYER_AXIS[n]) for n, _ in _stage_group(s)]
        from_sibling = sibling_exchange(gsend, f"scatter{s}_sibling")
        by_chip = [pair_sum(a, r, f"pair_sum_{n}{l}") for (n, l), a, r in zip(_stage_group(s), gsend, from_sibling)]
        started.append((s, scatter_start(by_chip, f"scatter{s}_start")))
        order = [started[-1][1]["token"]]
        for n, val in g.items():
            if n not in BIG:
                small_g[n][_stage_layer(s)[1]] = val.reshape(small[n].shape[1:])
    last = started[-1][1]["token"]
    for s, st in started[:-1]:
        finish(s, st, last)

    grads = {n: jnp.stack(vs) for n, vs in small_g.items()}
    grads["final_norm"] = d_final.reshape(D)
    small_full_shapes = [grads[n].shape for n in SMALL] + [(1,)]
    packed = _pack([grads[n] for n in SMALL] + [loss_row[0, :1]], _rows_for(small_full_shapes)) + last[0:1, 0:1]
    summed = sum_leading(all_gather([packed], "gather_small_grads")[0], "sum_small_grads")
    parts = _unpack(summed, small_full_shapes)
    loss = parts[-1][0]
    for n, g in zip(SMALL, parts[:-1]):
        if n in SMALL_SHARDED:
            s = w[n].shape[-1]
            g = lax.dynamic_slice_in_dim(g, idx * s, s, axis=g.ndim - 1)
        out[n] = adamw(g[None], w[n], m[n], v[n], f"adamw_{n}")
    finish(*started[-1], summed)
    for n in TRANSPOSED:
        out[n] = [jnp.swapaxes(a, 1, 2) for a in out[n]]
    return loss, dh, out


def kernel(x, ssm_norm, ssm_w_in, ssm_conv_w, ssm_conv_b, ssm_dt_bias, ssm_a_log, ssm_d, ssm_gate_norm, ssm_w_out, cv_norm, cv_w_pw1, cv_b_pw1, cv_dw_w, cv_dw_b, cv_ln_g, cv_ln_b, cv_w_pw2, cv_b_pw2, ffn_norm, ffn_w_gate, ffn_w_up, ffn_w_down, final_norm, loss_target, m_ssm_norm, m_ssm_w_in, m_ssm_conv_w, m_ssm_conv_b, m_ssm_dt_bias, m_ssm_a_log, m_ssm_d, m_ssm_gate_norm, m_ssm_w_out, m_cv_norm, m_cv_w_pw1, m_cv_b_pw1, m_cv_dw_w, m_cv_dw_b, m_cv_ln_g, m_cv_ln_b, m_cv_w_pw2, m_cv_b_pw2, m_ffn_norm, m_ffn_w_gate, m_ffn_w_up, m_ffn_w_down, m_final_norm, v_ssm_norm, v_ssm_w_in, v_ssm_conv_w, v_ssm_conv_b, v_ssm_dt_bias, v_ssm_a_log, v_ssm_d, v_ssm_gate_norm, v_ssm_w_out, v_cv_norm, v_cv_w_pw1, v_cv_b_pw1, v_cv_dw_w, v_cv_dw_b, v_cv_ln_g, v_cv_ln_b, v_cv_w_pw2, v_cv_b_pw2, v_ffn_norm, v_ffn_w_gate, v_ffn_w_up, v_ffn_w_down, v_final_norm):
    args = locals()
    w = {n: args[n] for n in WEIGHTS}
    m = {n: args["m_" + n] for n in WEIGHTS}
    v = {n: args["v_" + n] for n in WEIGHTS}
    loss, grad_x, out = _step(x[0], loss_target[0], w, m, v)
    res = [loss, grad_x[None]]
    for k in range(4):
        res += [out[n][k] for n in WEIGHTS]
    return tuple(res)
```

```python
import functools
import math

import jax
import jax.numpy as jnp
from jax import lax
from jax.experimental import pallas as pl
from jax.experimental.pallas import tpu as pltpu

F32 = jnp.float32
BF16 = jnp.bfloat16

N_DEV = 8
T = 2048
D = 1024
DI = 2048
NH = 32
HD = 64
NG = 4
GW = DI // NG
DS = 128
CONVD = DI + 2 * NG * DS
DINP = 2 * DI + 2 * NG * DS + NH
DINP_PAD = 5376
CH = 128
NCH = T // CH
DFF = 2816
KSSM = 4
KCV = 31
EPS = 1e-5
LANES = 128
VMEM_LIMIT = 56 * 1024 * 1024

ADAM_LR = 0.001
ADAM_B1 = 0.9
ADAM_B2 = 0.999
ADAM_EPS = 1e-08
ADAM_WD = 0.01
ADAM_STEP = 10

MESH = pl.DeviceIdType.MESH
ANY = pl.BlockSpec(memory_space=pl.ANY)


def _pcall(body, **kw):
    return pl.pallas_call(body, **kw)


def _cparams(sem):
    return pltpu.CompilerParams(dimension_semantics=sem, vmem_limit_bytes=VMEM_LIMIT)


def _pick(n, cands):
    for c in cands:
        if n % c == 0:
            return c
    raise ValueError(f"no tile for {n}")


def _sigmoid(x):
    return 1.0 / (1.0 + jnp.exp(-x))


def _silu(x):
    return x * _sigmoid(x)


def _dsilu(x):
    s = _sigmoid(x)
    return s * (1.0 + x * (1.0 - s))


_DIMS = {"nn": (((1,), (0,)), ((), ())), "nt": (((1,), (1,)), ((), ())), "tn": (((0,), (0,)), ((), ()))}


MM_VMEM_BUDGET = 40 * 1024 * 1024
MM_MAX_K = 3072


def _mm_tiles(M, N, K, out_bytes, has_res):
    tk = K if K <= MM_MAX_K else K // 2
    assert K % tk == 0 and tk % LANES == 0
    nk = K // tk
    best = None
    for tm in (2048, 1408, 1024, 512, 256, 128):
        if M % tm:
            continue
        for tn in (1408, 1024, 768, 512, 384, 256, 128):
            if N % tn:
                continue
            blocks = tm * tk * 2 + tk * tn * 2 + tm * tn * out_bytes + (tm * tn * 4 if has_res else 0)
            vmem = 2 * blocks + tm * tn * 4 * (2 if nk > 1 else 1)
            if vmem > MM_VMEM_BUDGET:
                continue
            traffic = (N // tn if nk > 1 else 1) * M * K + (M // tm) * N * K
            key = (-traffic, tm * tn)
            if best is None or key > best[0]:
                best = (key, tm, tn)
    assert best is not None, (M, N, K)
    return best[1], best[2], tk


def matmul(a, b, mode, *, name, bias=None, residual=None, out_dtype=F32):
    assert a.dtype == BF16 and b.dtype == BF16
    if mode == "nn":
        (M, K), (K2, N) = a.shape, b.shape
    elif mode == "nt":
        (M, K), (N, K2) = a.shape, b.shape
    else:
        (K, M), (K2, N) = a.shape, b.shape
    assert K == K2
    has_bias, has_res = bias is not None, residual is not None
    tm, tn, tk = _mm_tiles(M, N, K, jnp.dtype(out_dtype).itemsize, has_res)
    nk = K // tk
    dims = _DIMS[mode]

    def body(*refs):
        a_ref, b_ref = refs[0], refs[1]
        pos = 2
        bias_ref = res_ref = None
        if has_bias:
            bias_ref = refs[pos]
            pos += 1
        if has_res:
            res_ref = refs[pos]
            pos += 1
        o_ref = refs[pos]

        def finish(out):
            if has_bias:
                out = out + bias_ref[...]
            if has_res:
                out = out + res_ref[...]
            o_ref[...] = out.astype(o_ref.dtype)

        part = lax.dot_general(a_ref[...], b_ref[...], dims, preferred_element_type=F32)
        if nk == 1:
            finish(part)
            return
        acc = refs[pos + 1]
        k = pl.program_id(2)

        @pl.when(k == 0)
        def _():
            acc[...] = part

        @pl.when(jnp.logical_and(k > 0, k < nk - 1))
        def _():
            acc[...] += part

        @pl.when(k == nk - 1)
        def _():
            finish(acc[...] + part)

    if mode == "tn":
        a_spec = pl.BlockSpec((tk, tm), lambda i, j, k: (k, i))
    else:
        a_spec = pl.BlockSpec((tm, tk), lambda i, j, k: (i, k))
    if mode == "nt":
        b_spec = pl.BlockSpec((tn, tk), lambda i, j, k: (j, k))
    else:
        b_spec = pl.BlockSpec((tk, tn), lambda i, j, k: (k, j))
    in_specs, args = [a_spec, b_spec], [a, b]
    if has_bias:
        in_specs.append(pl.BlockSpec((1, tn), lambda i, j, k: (0, j)))
        args.append(bias.reshape(1, N).astype(F32))
    if has_res:
        in_specs.append(pl.BlockSpec((tm, tn), lambda i, j, k: (i, j)))
        args.append(residual)
    return _pcall(
        body, name=name, grid=(M // tm, N // tn, nk), in_specs=in_specs,
        out_specs=pl.BlockSpec((tm, tn), lambda i, j, k: (i, j)),
        out_shape=jax.ShapeDtypeStruct((M, N), out_dtype),
        scratch_shapes=[pltpu.VMEM((tm, tn), F32)] if nk > 1 else [],
        compiler_params=_cparams(("parallel", "parallel", "arbitrary")),
    )(*args)


def rowwise(fn, rows, bcasts, outs, accs=(), *, name, tm=256):
    n_rows, n_b, n_o, n_a = len(rows), len(bcasts), len(outs), len(accs)
    nt = T // tm

    def body(*refs):
        ins = [r[...] for r in refs[:n_rows + n_b]]
        res = fn(*ins)
        o_refs = refs[n_rows + n_b:n_rows + n_b + n_o]
        a_refs = refs[n_rows + n_b + n_o:]
        for r, v in zip(o_refs, res[:n_o]):
            r[...] = v.astype(r.dtype)
        if n_a:
            i = pl.program_id(0)

            @pl.when(i == 0)
            def _():
                for r in a_refs:
                    r[...] = jnp.zeros_like(r)

            for r, v in zip(a_refs, res[n_o:]):
                r[...] += v

    in_specs = [pl.BlockSpec((tm, w), functools.partial(lambda i, cb: (i, cb), cb=cb)) for (_, w, cb) in rows]
    in_specs += [pl.BlockSpec(b.shape, lambda i: (0, 0)) for b in bcasts]
    out_specs = [pl.BlockSpec((tm, w), lambda i: (i, 0)) for (w, _) in outs]
    out_specs += [pl.BlockSpec((1, w), lambda i: (0, 0)) for w in accs]
    out_shape = [jax.ShapeDtypeStruct((T, w), dt) for (w, dt) in outs]
    out_shape += [jax.ShapeDtypeStruct((1, w), F32) for w in accs]
    return _pcall(
        body, name=name, grid=(nt,), in_specs=in_specs, out_specs=out_specs, out_shape=out_shape,
        compiler_params=_cparams(("arbitrary",)),
    )(*[r[0] for r in rows], *bcasts)


def _full(a):
    return (a, a.shape[1], 0)


def _rsum(v):
    return jnp.sum(v, axis=0, keepdims=True)


def rms_fwd(h, g, name):
    def fn(x, g):
        r = lax.rsqrt(jnp.mean(x * x, axis=-1, keepdims=True) + EPS)
        return (x * r * g,)
    return rowwise(fn, [_full(h)], [g], [(D, BF16)], name=name)[0]


def rms_bwd(du, h, g, dres, name):
    def fn(du, x, dres, g):
        r = lax.rsqrt(jnp.mean(x * x, axis=-1, keepdims=True) + EPS)
        xh = x * r
        dxh = du * g
        dx = r * (dxh - xh * jnp.mean(dxh * xh, axis=-1, keepdims=True))
        dh = dres + dx
        return dh, dh, _rsum(du * xh)
    return rowwise(fn, [_full(du), _full(h), _full(dres)], [g], [(D, F32), (D, BF16)], [D], name=name)


def loss_head(h, g, tgt, name):
    def fn(x, tgt, g):
        r = lax.rsqrt(jnp.mean(x * x, axis=-1, keepdims=True) + EPS)
        xh = x * r
        err = xh * g - tgt
        lsum = jnp.sum(jnp.sum(err * err, axis=-1, keepdims=True), axis=0, keepdims=True) * (0.5 / D)
        dy = err * (1.0 / D)
        dxh = dy * g
        dx = r * (dxh - xh * jnp.mean(dxh * xh, axis=-1, keepdims=True))
        return dx, dx, _rsum(dy * xh), jnp.broadcast_to(lsum, (1, LANES))
    return rowwise(fn, [_full(h), _full(tgt)], [g], [(D, F32), (D, BF16)], [D, LANES], name=name)


def glu_fwd(hh, name):
    def fn(a, g):
        return (a * _sigmoid(g),)
    return rowwise(fn, [(hh, D, 0), (hh, D, 1)], [], [(D, F32)], name=name)[0]


def glu_bwd(dgl, hh, name):
    def fn(dgl, a, g):
        s = _sigmoid(g)
        dhh = jnp.concatenate([dgl * s, dgl * a * s * (1.0 - s)], axis=1)
        return dhh, _rsum(dhh)
    return rowwise(fn, [_full(dgl), (hh, D, 0), (hh, D, 1)], [], [(2 * D, BF16)], [2 * D], name=name)


def ln_silu_fwd(c2, g, b, name):
    def fn(x, g, b):
        mu = jnp.mean(x, axis=-1, keepdims=True)
        xc = x - mu
        r = lax.rsqrt(jnp.mean(xc * xc, axis=-1, keepdims=True) + EPS)
        return (_silu(xc * r * g + b),)
    return rowwise(fn, [_full(c2)], [g, b], [(D, BF16)], name=name)[0]


def ln_silu_bwd(ds, c2, dh, g, b, name):
    def fn(ds, x, dh, g, b):
        mu = jnp.mean(x, axis=-1, keepdims=True)
        xc = x - mu
        r = lax.rsqrt(jnp.mean(xc * xc, axis=-1, keepdims=True) + EPS)
        xh = xc * r
        dn = ds * _dsilu(xh * g + b)
        dxh = dn * g
        dx = r * (dxh - jnp.mean(dxh, axis=-1, keepdims=True) - xh * jnp.mean(dxh * xh, axis=-1, keepdims=True))
        return dx, _rsum(dn * xh), _rsum(dn), _rsum(dh)
    return rowwise(fn, [_full(ds), _full(c2), _full(dh)], [g, b], [(D, F32)], [D, D, D], name=name)


def gatenorm_fwd(y, zx, gn, name):
    def fn(y, z, gn):
        hg = y * _silu(z)
        parts = []
        for k in range(NG):
            hk = hg[:, k * GW:(k + 1) * GW]
            parts.append(hk * lax.rsqrt(jnp.mean(hk * hk, axis=-1, keepdims=True) + EPS))
        return (jnp.concatenate(parts, axis=1) * gn,)
    return rowwise(fn, [_full(y), (zx, DI, 0)], [gn], [(DI, BF16)], name=name)[0]


def gatenorm_bwd(dyn, y, zx, gn, name):
    def fn(dyn, y, z, gn):
        sz = _silu(z)
        hg = y * sz
        dxh = dyn * gn
        dhg, xhs = [], []
        for k in range(NG):
            sl = slice(k * GW, (k + 1) * GW)
            hk = hg[:, sl]
            r = lax.rsqrt(jnp.mean(hk * hk, axis=-1, keepdims=True) + EPS)
            xh = hk * r
            dk = dxh[:, sl]
            dhg.append(r * (dk - xh * jnp.mean(dk * xh, axis=-1, keepdims=True)))
            xhs.append(xh)
        dhg = jnp.concatenate(dhg, axis=1)
        xh = jnp.concatenate(xhs, axis=1)
        return dhg * sz, dhg * y * _dsilu(z), _rsum(dyn * xh)
    return rowwise(fn, [_full(dyn), _full(y), (zx, DI, 0)], [gn], [(DI, F32), (DI, BF16)], [DI], name=name)


def _softplus(x):
    return jnp.maximum(x, 0.0) + jnp.log(1.0 + jnp.exp(-jnp.abs(x)))


def _spread(v, e):
    hi = v.astype(BF16)
    r = v - hi.astype(F32)
    mid = r.astype(BF16)
    lo = (r - mid.astype(F32)).astype(BF16)
    return _dot(hi, e) + _dot(mid, e) + _dot(lo, e)


def _spread2(v, e):
    hi = v.astype(BF16)
    lo = (v - hi.astype(F32)).astype(BF16)
    return _dot(hi, e) + _dot(lo, e)


def dt_fwd(zx, dt_bias, a_log, name):
    heads = (jnp.arange(DI)[None, :] // HD == jnp.arange(LANES)[:, None]).astype(BF16)

    def fn(raw, bias, a_log, e):
        dt = _softplus(raw + bias)
        da = dt * (-jnp.exp(a_log))
        return dt, da, _spread(dt, e), _spread(da, e)

    return rowwise(fn, [(zx, LANES, (2 * DI + 2 * NG * DS) // LANES)], [dt_bias, a_log, heads],
                   [(LANES, F32), (LANES, F32), (DI, F32), (DI, F32)], name=name)


def dt_bwd(ddt, dda, dt, zx, dt_bias, a_log, name):
    def fn(ddt, dda, dt, raw, bias, a_log):
        a = -jnp.exp(a_log)
        draw = (ddt + dda * a) * _sigmoid(raw + bias)
        return draw, _rsum(draw), _rsum(dda * dt) * a
    return rowwise(fn, [_full(ddt), _full(dda), _full(dt), (zx, LANES, (2 * DI + 2 * NG * DS) // LANES)],
                   [dt_bias, a_log], [(LANES, BF16)], [LANES, LANES], name=name)


def headsum(v, name):
    def body(v_ref, o_ref):
        o_ref[...] = jnp.sum(v_ref[...], axis=1, keepdims=True)
    return _pcall(body, name=name, out_shape=jax.ShapeDtypeStruct((v.shape[0], 1), F32))(v)


CONV_ROWS = 256


def _shifted(win, o, rows):
    if o == 0:
        return win[0:rows]
    n = win.shape[0]
    return pltpu.roll(win, shift=n - o, axis=0)[0:rows]


def dwconv_fwd(x, x_cb0, w, b, K, ct, act, name):
    C = w.shape[1]
    pad = 8 if K <= 8 else 32
    KP = w.shape[0]
    n_out = 2 if act else 1

    def body(x_ref, w_ref, b_ref, *rest):
        o_refs, px = rest[:n_out], rest[n_out]
        px[0:pad, :] = jnp.zeros((pad, ct), F32)
        px[pad:pad + T, :] = x_ref[...]
        wv = w_ref[...]
        bv = b_ref[...]
        for r0 in range(0, T, CONV_ROWS):
            win = px[r0:r0 + CONV_ROWS + pad, :]
            acc = jnp.broadcast_to(bv, (CONV_ROWS, ct))
            for k in range(K):
                acc = acc + wv[k:k + 1, :] * _shifted(win, pad - (K - 1) + k, CONV_ROWS)
            o_refs[0][r0:r0 + CONV_ROWS, :] = acc
            if act:
                o_refs[1][r0:r0 + CONV_ROWS, :] = _silu(acc)

    return _pcall(
        body, name=name, grid=(C // ct,),
        in_specs=[pl.BlockSpec((T, ct), lambda j: (0, x_cb0 + j)), pl.BlockSpec((KP, ct), lambda j: (0, j)),
                  pl.BlockSpec((1, ct), lambda j: (0, j))],
        out_specs=[pl.BlockSpec((T, ct), lambda j: (0, j))] * n_out,
        out_shape=[jax.ShapeDtypeStruct((T, C), F32)] * n_out,
        scratch_shapes=[pltpu.VMEM((T + pad, ct), F32)],
        compiler_params=_cparams(("parallel",)),
    )(x, w, b)


def dwconv_bwd(dout, cpre, x, x_cb0, w, K, ct, act, out_dtype, name):
    C = w.shape[1]
    pad = 8 if K <= 8 else 32
    KP = w.shape[0]

    def body(*refs):
        if act:
            d_ref, c_ref, x_ref, w_ref, dx_ref, dw_ref, db_ref, px, pd = refs
        else:
            d_ref, x_ref, w_ref, dx_ref, dw_ref, db_ref, px, pd = refs
        px[0:pad, :] = jnp.zeros((pad, ct), F32)
        px[pad:pad + T, :] = x_ref[...]
        pd[T:T + pad, :] = jnp.zeros((pad, ct), F32)
        if act:
            pd[0:T, :] = d_ref[...] * _dsilu(c_ref[...])
        else:
            pd[0:T, :] = d_ref[...]
        wv = w_ref[...]
        dws = [jnp.zeros((1, ct), F32) for _ in range(K)]
        db = jnp.zeros((1, ct), F32)
        for r0 in range(0, T, CONV_ROWS):
            dwin = pd[r0:r0 + CONV_ROWS + pad, :]
            xwin = px[r0:r0 + CONV_ROWS + pad, :]
            dc = dwin[0:CONV_ROWS]
            db = db + _rsum(dc)
            acc = jnp.zeros((CONV_ROWS, ct), F32)
            for k in range(K):
                acc = acc + wv[k:k + 1, :] * _shifted(dwin, K - 1 - k, CONV_ROWS)
                dws[k] = dws[k] + _rsum(dc * _shifted(xwin, pad - (K - 1) + k, CONV_ROWS))
            dx_ref[r0:r0 + CONV_ROWS, :] = acc.astype(dx_ref.dtype)
        dw_ref[...] = jnp.zeros((KP, ct), F32)
        for k in range(K):
            dw_ref[k:k + 1, :] = dws[k]
        db_ref[...] = db

    col = pl.BlockSpec((T, ct), lambda j: (0, j))
    in_specs = [col] + ([col] if act else []) + [pl.BlockSpec((T, ct), lambda j: (0, x_cb0 + j)),
                                                 pl.BlockSpec((KP, ct), lambda j: (0, j))]
    args = [dout] + ([cpre] if act else []) + [x, w]
    return _pcall(
        body, name=name, grid=(C // ct,), in_specs=in_specs,
        out_specs=[col, pl.BlockSpec((KP, ct), lambda j: (0, j)), pl.BlockSpec((1, ct), lambda j: (0, j))],
        out_shape=[jax.ShapeDtypeStruct((T, C), out_dtype), jax.ShapeDtypeStruct((KP, C), F32),
                   jax.ShapeDtypeStruct((1, C), F32)],
        scratch_shapes=[pltpu.VMEM((T + pad, ct), F32), pltpu.VMEM((T + pad, ct), F32)],
        compiler_params=_cparams(("parallel",)),
    )(*args)


def _scan(a, axis, reverse=False):
    n = a.shape[axis]
    idx = lax.broadcasted_iota(jnp.int32, a.shape, axis)
    s = 1
    while s < n:
        if reverse:
            a = a + jnp.where(idx < n - s, pltpu.roll(a, shift=n - s, axis=axis), 0.0)
        else:
            a = a + jnp.where(idx >= s, pltpu.roll(a, shift=s, axis=axis), 0.0)
        s *= 2
    return a


_NT = _DIMS["nt"]
_TN = _DIMS["tn"]


def _dot(a, b, dims=_DIMS["nn"]):
    return lax.dot_general(a, b, dims, preferred_element_type=F32)


def ssd_fwd(xbc, dtx, dax, daT, dfull, name):
    def body(xbc_ref, dtx_ref, dax_ref, daT_ref, df_ref, y_ref, st_ref, S):
        ci = pl.program_id(0)

        @pl.when(ci == 0)
        def _():
            S[...] = jnp.zeros_like(S)

        row = lax.broadcasted_iota(jnp.int32, (CH, CH), 0)
        lane = lax.broadcasted_iota(jnp.int32, (CH, CH), 1)
        acsT = _scan(daT_ref[...], 1)
        for g in range(NG):
            c0 = g * GW
            xs = xbc_ref[:, c0:c0 + GW]
            acs = _scan(dax_ref[:, c0:c0 + GW], 0)
            Bm = xbc_ref[:, DI + g * DS:DI + (g + 1) * DS].astype(BF16)
            Cm = xbc_ref[:, DI + NG * DS + g * DS:DI + NG * DS + (g + 1) * DS].astype(BF16)
            xdt = xs * dtx_ref[:, c0:c0 + GW]
            atot = acs[CH - 1:CH, :]
            Sg = S[:, c0:c0 + GW]
            st_ref[:, c0:c0 + GW] = Sg
            CB = _dot(Cm, Bm, _NT)
            yg = jnp.exp(acs) * _dot(Cm, Sg.astype(BF16)) + xs * df_ref[:, c0:c0 + GW]
            xd = (xdt * jnp.exp(atot - acs)).astype(BF16)
            S[:, c0:c0 + GW] = jnp.exp(atot) * Sg + _dot(Bm, xd, _TN)
            xdt_b = xdt.astype(BF16)
            for r in range(NH // NG):
                h = g * (NH // NG) + r
                hs = slice(r * HD, (r + 1) * HD)
                seg = acs[:, r * HD:r * HD + 1] - acsT[h:h + 1, :]
                Lm = jnp.where(row >= lane, jnp.exp(jnp.minimum(seg, 0.0)), 0.0)
                yd = _dot((CB * Lm).astype(BF16), xdt_b[:, hs])
                y_ref[:, c0 + r * HD:c0 + (r + 1) * HD] = yg[:, hs] + yd

    return _pcall(
        body, name=name, grid=(NCH,),
        in_specs=[pl.BlockSpec((CH, CONVD), lambda i: (i, 0)), pl.BlockSpec((CH, DI), lambda i: (i, 0)),
                  pl.BlockSpec((CH, DI), lambda i: (i, 0)), pl.BlockSpec((NH, CH), lambda i: (0, i)),
                  pl.BlockSpec((1, DI), lambda i: (0, 0))],
        out_specs=[pl.BlockSpec((CH, DI), lambda i: (i, 0)), pl.BlockSpec((None, DS, DI), lambda i: (i, 0, 0))],
        out_shape=[jax.ShapeDtypeStruct((T, DI), F32), jax.ShapeDtypeStruct((NCH, DS, DI), F32)],
        scratch_shapes=[pltpu.VMEM((DS, DI), F32)],
        compiler_params=_cparams(("arbitrary",)),
    )(xbc, dtx, dax, daT, dfull)


def ssd_bwd(dy, xbc, dtx, dax, daT, dfull, states, name):
    hsum = (jnp.arange(DI)[:, None] // HD == jnp.arange(LANES)[None, :]).astype(BF16).reshape(NG, GW, LANES)

    def body(dy_ref, xbc_ref, dtx_ref, dax_ref, daT_ref, df_ref, st_ref, hsum_ref, dxbc_ref, ddt_ref, dda_ref, dD_ref, dS):
        i = pl.program_id(0)

        @pl.when(i == 0)
        def _():
            dS[...] = jnp.zeros_like(dS)
            dD_ref[...] = jnp.zeros_like(dD_ref)

        row = lax.broadcasted_iota(jnp.int32, (CH, CH), 0)
        lane = lax.broadcasted_iota(jnp.int32, (CH, CH), 1)
        acsT = _scan(daT_ref[...], 1)
        ddt_all = jnp.zeros((CH, LANES), F32)
        dacs_all = jnp.zeros((CH, LANES), F32)
        colacc = jnp.zeros((CH, CH), F32)
        for g in range(NG):
            c0 = g * GW
            xs = xbc_ref[:, c0:c0 + GW]
            dtx = dtx_ref[:, c0:c0 + GW]
            acs = _scan(dax_ref[:, c0:c0 + GW], 0)
            Bm = xbc_ref[:, DI + g * DS:DI + (g + 1) * DS].astype(BF16)
            Cm = xbc_ref[:, DI + NG * DS + g * DS:DI + NG * DS + (g + 1) * DS].astype(BF16)
            xdt = xs * dtx
            atot = acs[CH - 1:CH, :]
            Sin = st_ref[:, c0:c0 + GW]
            dyg = dy_ref[:, c0:c0 + GW]
            dSo = dS[:, c0:c0 + GW]
            E = jnp.exp(acs)
            Etot = jnp.exp(atot)
            dec = jnp.exp(atot - acs)
            dD_ref[:, c0:c0 + GW] += _rsum(dyg * xs)
            dxs = dyg * df_ref[:, c0:c0 + GW]
            Sin_b = Sin.astype(BF16)
            dSo_b = dSo.astype(BF16)
            dY0 = dyg * E
            dY0_b = dY0.astype(BF16)
            dC = _dot(dY0_b, Sin_b, _NT)
            dS[:, c0:c0 + GW] = _dot(Cm, dY0_b, _TN) + Etot * dSo
            XD = xdt * dec
            dXD = _dot(Bm, dSo_b)
            dB = _dot(XD.astype(BF16), dSo_b, _NT)
            dxdt = dXD * dec
            Gq = dXD * XD
            dacs_x = dY0 * _dot(Cm, Sin_b) - Gq
            datot_x = _rsum(dSo * Sin) * Etot + _rsum(Gq)
            dacs_all = dacs_all + _spread2(dacs_x, hsum_ref[g])
            dtot8 = _spread2(jnp.broadcast_to(datot_x, (8, GW)), hsum_ref[g])
            dacs_all = dacs_all + jnp.where(row == CH - 1, jnp.broadcast_to(dtot8[0:1, :], (CH, LANES)), 0.0)
            CB = _dot(Cm, Bm, _NT)
            dCB = jnp.zeros((CH, CH), F32)
            xdt_b = xdt.astype(BF16)
            dy_b = dyg.astype(BF16)
            for r in range(NH // NG):
                h = g * (NH // NG) + r
                hs = slice(r * HD, (r + 1) * HD)
                seg = acs[:, r * HD:r * HD + 1] - acsT[h:h + 1, :]
                Lm = jnp.where(row >= lane, jnp.exp(jnp.minimum(seg, 0.0)), 0.0)
                dyr = dy_b[:, hs]
                dML = _dot(dyr, xdt_b[:, hs], _NT) * Lm
                dxbc_ref[:, c0 + r * HD:c0 + (r + 1) * HD] = _dot((CB * Lm).astype(BF16), dyr, _TN)
                dCB = dCB + dML
                dseg = dML * CB
                dacs_all = dacs_all + _spread2(dseg, (lane == h).astype(BF16))
                colacc = colacc + jnp.where(row == h, jnp.sum(dseg, axis=0, keepdims=True), 0.0)
            dxdt = dxdt + dxbc_ref[:, c0:c0 + GW]
            ddt_all = ddt_all + _spread2(dxdt * xs, hsum_ref[g])
            dxbc_ref[:, c0:c0 + GW] = dxs + dxdt * dtx
            dCB_b = dCB.astype(BF16)
            dxbc_ref[:, DI + g * DS:DI + (g + 1) * DS] = dB + _dot(dCB_b, Cm, _TN)
            dxbc_ref[:, DI + NG * DS + g * DS:DI + NG * DS + (g + 1) * DS] = dC + _dot(dCB_b, Bm)
        ddt_ref[...] = ddt_all
        dda_ref[...] = _scan(dacs_all - colacc.T, 0, reverse=True)

    last = NCH - 1
    return _pcall(
        body, name=name, grid=(NCH,),
        in_specs=[pl.BlockSpec((CH, DI), lambda i: (last - i, 0)), pl.BlockSpec((CH, CONVD), lambda i: (last - i, 0)),
                  pl.BlockSpec((CH, DI), lambda i: (last - i, 0)), pl.BlockSpec((CH, DI), lambda i: (last - i, 0)),
                  pl.BlockSpec((NH, CH), lambda i: (0, last - i)), pl.BlockSpec((1, DI), lambda i: (0, 0)),
                  pl.BlockSpec((None, DS, DI), lambda i: (last - i, 0, 0)),
                  pl.BlockSpec((NG, GW, LANES), lambda i: (0, 0, 0))],
        out_specs=[pl.BlockSpec((CH, CONVD), lambda i: (last - i, 0)), pl.BlockSpec((CH, LANES), lambda i: (last - i, 0)),
                   pl.BlockSpec((CH, LANES), lambda i: (last - i, 0)), pl.BlockSpec((1, DI), lambda i: (0, 0))],
        out_shape=[jax.ShapeDtypeStruct((T, CONVD), F32), jax.ShapeDtypeStruct((T, LANES), F32),
                   jax.ShapeDtypeStruct((T, LANES), F32), jax.ShapeDtypeStruct((1, DI), F32)],
        scratch_shapes=[pltpu.VMEM((DS, DI), F32)],
        compiler_params=_cparams(("arbitrary",)),
    )(dy, xbc, dtx, dax, daT, dfull, states, hsum)


def _as3d(shape):
    if len(shape) == 1:
        return (1, 1, shape[0])
    if len(shape) == 2:
        return (1, shape[0], shape[1])
    return (math.prod(shape[:-2]), shape[-2], shape[-1])


def _row_tile(R, C):
    if R * C <= 512 * 1024:
        return R
    return _pick(R, (512, 256, 128, 64, 32, 16, 8))


def adamw(parts, w, m, v, name):
    shape = w.shape
    L, R, C = _as3d(shape)
    P = parts.shape[0]
    tr = _row_tile(R, C)
    bc1 = 1.0 - ADAM_B1 ** ADAM_STEP
    bc2 = 1.0 - ADAM_B2 ** ADAM_STEP

    def body(p_ref, w_ref, m_ref, v_ref, g_out, d_out, m_out, v_out):
        g = p_ref[0].astype(F32)
        for k in range(1, P):
            g = g + p_ref[k].astype(F32)
        mn = ADAM_B1 * m_ref[...] + (1.0 - ADAM_B1) * g
        vn = ADAM_B2 * v_ref[...] + (1.0 - ADAM_B2) * (g * g)
        g_out[...] = g
        m_out[...] = mn
        v_out[...] = vn
        d_out[...] = -ADAM_LR * ((mn / bc1) / (jnp.sqrt(vn / bc2) + ADAM_EPS) + ADAM_WD * w_ref[...])

    blk = pl.BlockSpec((None, tr, C), lambda l, r: (l, r, 0))
    outs = _pcall(
        body, name=name, grid=(L, R // tr),
        in_specs=[pl.BlockSpec((P, None, tr, C), lambda l, r: (0, l, r, 0)), blk, blk, blk],
        out_specs=[blk] * 4, out_shape=[jax.ShapeDtypeStruct((L, R, C), F32)] * 4,
        compiler_params=_cparams(("parallel", "parallel")),
    )(parts.reshape(P, L, R, C), w.reshape(L, R, C), m.reshape(L, R, C), v.reshape(L, R, C))
    return [o.reshape(shape) for o in outs]


def adamw_layer(recv, own, w, m, v, layer, prev, name):
    L, R, C = w.shape
    P = recv.shape[0]
    tr = _row_tile(R, C)
    bc1 = 1.0 - ADAM_B1 ** ADAM_STEP
    bc2 = 1.0 - ADAM_B2 ** ADAM_STEP

    def body(r_ref, o_ref, w_ref, m_ref, v_ref, *rest):
        g_out, d_out, m_out, v_out = rest[-4:]
        g = o_ref[...].astype(F32)
        for k in range(P):
            g = g + r_ref[k].astype(F32)
        mn = ADAM_B1 * m_ref[...] + (1.0 - ADAM_B1) * g
        vn = ADAM_B2 * v_ref[...] + (1.0 - ADAM_B2) * (g * g)
        g_out[...] = g
        m_out[...] = mn
        v_out[...] = vn
        d_out[...] = -ADAM_LR * ((mn / bc1) / (jnp.sqrt(vn / bc2) + ADAM_EPS) + ADAM_WD * w_ref[...])

    slot = pl.BlockSpec((None, tr, C), lambda r: (layer, r, 0))
    own_spec = pl.BlockSpec((None, tr, C), lambda r: (2 * lax.axis_index("x") + lax.axis_index("y"), r, 0))
    in_specs = [pl.BlockSpec((P, tr, C), lambda r: (0, r, 0)), own_spec, slot, slot, slot]
    args = [recv, own, w, m, v]
    aliases = {}
    if prev is not None:
        in_specs += [ANY] * 4
        args += list(prev)
        aliases = {5 + k: k for k in range(4)}
    return _pcall(
        body, name=name, grid=(R // tr,), in_specs=in_specs, out_specs=[slot] * 4,
        out_shape=[jax.ShapeDtypeStruct((L, R, C), F32)] * 4, input_output_aliases=aliases,
        compiler_params=_cparams(("parallel",)),
    )(*args)


def sum_leading(parts, name):
    P, R, C = parts.shape

    def body(p_ref, o_ref):
        s = p_ref[0]
        for k in range(1, P):
            s = s + p_ref[k]
        o_ref[...] = s

    return _pcall(body, name=name, out_shape=jax.ShapeDtypeStruct((R, C), F32))(parts)


def pair_sum(gsend, recv, name):
    S = gsend.shape[1:]
    L, R, C = _as3d(S)
    tr = _row_tile(R, C)

    def body(g_ref, r_ref, o_ref):
        o_ref[...] = (g_ref[...].astype(F32) + r_ref[...].astype(F32)).astype(o_ref.dtype)

    blk = pl.BlockSpec((None, None, tr, C), lambda q, l, r: (q, l, r, 0))
    own = pl.BlockSpec((None, None, tr, C), lambda q, l, r: (2 * q + lax.axis_index("c"), l, r, 0))
    out = _pcall(
        body, name=name, grid=(4, L, R // tr), in_specs=[own, blk], out_specs=blk,
        out_shape=jax.ShapeDtypeStruct((4, L, R, C), BF16),
        compiler_params=_cparams(("parallel", "parallel", "parallel")),
    )(gsend.reshape(8, L, R, C), recv.reshape(4, L, R, C))
    return out.reshape((4,) + S)


def _place():
    return lax.axis_index("x"), lax.axis_index("y"), lax.axis_index("c")


def _other_chips(x, y):
    return [(1 - x, y), (x, 1 - y), (1 - x, 1 - y)]


def all_gather(arrs, name):
    n = len(arrs)

    def body(*refs):
        ins, outs = refs[:n], refs[n:2 * n]
        send_sems, recv_sems, local_sems = refs[2 * n:]
        x, y, c = _place()
        me, sibling = (x, y, c), (x, y, 1 - c)
        chips = _other_chips(x, y)

        def slot(a, px, py, pc):
            return outs[a].at[4 * px + 2 * py + pc]

        def copy(a, k, block, to, src=None):
            return pltpu.make_async_remote_copy(
                src_ref=slot(a, *block) if src is None else src, dst_ref=slot(a, *block),
                send_sem=send_sems.at[a, k], recv_sem=recv_sems.at[a, k], device_id=to, device_id_type=MESH)

        mine, first, passed = [], [], []
        for a in range(n):
            cp = pltpu.make_async_copy(ins[a], slot(a, *me), local_sems.at[a])
            cp.start()
            mine.append(cp)
            first.append(copy(a, 0, me, sibling, src=ins[a]))
            first += [copy(a, 1 + j, me, (*chip, c), src=ins[a]) for j, chip in enumerate(chips)]
        for cp in first:
            cp.start()
        for j, chip in enumerate(chips):
            for a in range(n):
                copy(a, 1 + j, (*chip, c), me).wait_recv()
                cp = copy(a, 4 + j, (*chip, c), sibling)
                cp.start()
                passed.append(cp)
        for a in range(n):
            copy(a, 0, sibling, me).wait_recv()
            for j, chip in enumerate(chips):
                copy(a, 4 + j, (*chip, 1 - c), me).wait_recv()
        for cp in first + passed:
            cp.wait_send()
        for cp in mine:
            cp.wait()

    return _pcall(
        body, name=name, in_specs=[ANY] * n, out_specs=[ANY] * n,
        out_shape=[jax.ShapeDtypeStruct((N_DEV,) + a.shape, a.dtype) for a in arrs],
        scratch_shapes=[pltpu.SemaphoreType.DMA((n, 7)), pltpu.SemaphoreType.DMA((n, 7)), pltpu.SemaphoreType.DMA((n,))],
    )(*arrs)


def sibling_exchange(gsends, name, after=None):
    n = len(gsends)
    n_in = n + (1 if after is not None else 0)

    def body(*refs):
        ins, outs = refs[:n], refs[n_in:n_in + n]
        send_sems, recv_sems = refs[n_in + n:]
        x, y, c = _place()
        copies = []
        for a in range(n):
            for q in range(4):
                cp = pltpu.make_async_remote_copy(
                    src_ref=ins[a].at[2 * q + 1 - c], dst_ref=outs[a].at[q],
                    send_sem=send_sems.at[a, q], recv_sem=recv_sems.at[a, q],
                    device_id=(x, y, 1 - c), device_id_type=MESH)
                cp.start()
                copies.append(cp)
        for cp in copies:
            cp.wait()

    return _pcall(
        body, name=name, in_specs=[ANY] * n_in, out_specs=[ANY] * n,
        out_shape=[jax.ShapeDtypeStruct((4,) + g.shape[1:], g.dtype) for g in gsends],
        scratch_shapes=[pltpu.SemaphoreType.DMA((n, 4)), pltpu.SemaphoreType.DMA((n, 4))],
    )(*gsends, *([after] if after is not None else []))


HBM =pl.BlockSpec(memory_space=pltpu.HBM)
SEM = pl.BlockSpec(memory_space=pltpu.SEMAPHORE)
EFFECT = pltpu.SideEffectType.DATAFLOW_SIDE_EFFECTING


def _in_hbm(a):
    return pltpu.with_memory_space_constraint(a, pltpu.HBM)


def _gather_peers(x, y, c):
    to = [(x, y, 1 - c)] + [(px, py, c) for px, py in _other_chips(x, y)]
    return to, [4 * px + 2 * py + pc for px, py, pc in to]


def gather_start(arrs, after, name):
    n = len(arrs)
    n_in = 2 * n + (1 if after is not None else 0)

    def body(*refs):
        srcs, lands = refs[:n], refs[n:2 * n]
        send_sems, recv_sems = refs[n_in], refs[n_in + 1]
        token = refs[-1]
        x, y, c = _place()
        to, _ = _gather_peers(x, y, c)
        me = 4 * x + 2 * y + c
        for a in range(n):
            for k, dev in enumerate(to):
                pltpu.make_async_remote_copy(
                    src_ref=srcs[a], dst_ref=lands[a].at[me], send_sem=send_sems.at[4 * a + k], recv_sem=recv_sems.at[4 * a + k],
                    device_id=dev, device_id_type=MESH).start()
        token[...] = jnp.zeros_like(token)

    zones = [lax.empty((N_DEV,) + a.shape, a.dtype) for a in arrs]
    args = [_in_hbm(a) for a in arrs] + [_in_hbm(z) for z in zones] + ([after] if after is not None else [])
    outs = _pcall(
        body, name=name,
        out_shape=(pltpu.SemaphoreType.DMA((4 * n,)), pltpu.SemaphoreType.DMA((4 * n,)),
                   *[pltpu.HBM(a.shape, a.dtype) for a in arrs], *[pltpu.HBM(z.shape, z.dtype) for z in zones],
                   jax.ShapeDtypeStruct((8, LANES), F32)),
        in_specs=[HBM] * (2 * n) + ([ANY] if after is not None else []),
        out_specs=(SEM, SEM, *[HBM] * (2 * n), pl.BlockSpec(memory_space=pltpu.VMEM)),
        input_output_aliases={i: 2 + i for i in range(2 * n)},
        compiler_params=pltpu.CompilerParams(has_side_effects=EFFECT),
    )(*args)
    return dict(send=outs[0], recv=outs[1], srcs=list(outs[2:2 + n]), lands=list(outs[2 + n:2 + 2 * n]), token=outs[-1])


def gather_wait(st, after, name):
    n = len(st["srcs"])

    def body(*refs):
        srcs, lands = refs[:n], refs[n:2 * n]
        send_sems, recv_sems = refs[2 * n], refs[2 * n + 1]
        x, y, c = _place()
        to, slots = _gather_peers(x, y, c)
        for a in range(n):
            for k, dev in enumerate(to):
                cp = pltpu.make_async_remote_copy(
                    src_ref=srcs[a], dst_ref=lands[a].at[slots[k]], send_sem=send_sems.at[4 * a + k],
                    recv_sem=recv_sems.at[4 * a + k], device_id=dev, device_id_type=MESH)
                cp.wait_send()
                cp.wait_recv()

    outs = _pcall(
        body, name=name,
        out_shape=(*[pltpu.HBM(a.shape, a.dtype) for a in st["srcs"]], *[pltpu.HBM(z.shape, z.dtype) for z in st["lands"]]),
        in_specs=[HBM] * (2 * n) + [SEM, SEM, ANY], out_specs=tuple([HBM] * (2 * n)),
        input_output_aliases={i: i for i in range(2 * n)},
        compiler_params=pltpu.CompilerParams(has_side_effects=EFFECT),
    )(*st["srcs"], *st["lands"], st["send"], st["recv"], after)
    return list(outs[n:])


def pass_start(zones, name):
    n = len(zones)

    def body(*refs):
        zs = refs[:n]
        send_sems, recv_sems = refs[n], refs[n + 1]
        token = refs[-1]
        x, y, c = _place()
        for a in range(n):
            for j, (px, py) in enumerate(_other_chips(x, y)):
                blk = zs[a].at[4 * px + 2 * py + c]
                pltpu.make_async_remote_copy(
                    src_ref=blk, dst_ref=blk, send_sem=send_sems.at[3 * a + j], recv_sem=recv_sems.at[3 * a + j],
                    device_id=(x, y, 1 - c), device_id_type=MESH).start()
        token[...] = jnp.zeros_like(token)

    outs = _pcall(
        body, name=name,
        out_shape=(pltpu.SemaphoreType.DMA((3 * n,)), pltpu.SemaphoreType.DMA((3 * n,)),
                   *[pltpu.HBM(z.shape, z.dtype) for z in zones], jax.ShapeDtypeStruct((8, LANES), F32)),
        in_specs=[HBM] * n, out_specs=(SEM, SEM, *[HBM] * n, pl.BlockSpec(memory_space=pltpu.VMEM)),
        input_output_aliases={i: 2 + i for i in range(n)},
        compiler_params=pltpu.CompilerParams(has_side_effects=EFFECT),
    )(*zones)
    return dict(send=outs[0], recv=outs[1], zones=list(outs[2:2 + n]), token=outs[-1])


def pass_wait(st, after, name):
    n = len(st["zones"])

    def body(*refs):
        zs = refs[:n]
        send_sems, recv_sems = refs[n], refs[n + 1]
        x, y, c = _place()
        for a in range(n):
            for j, (px, py) in enumerate(_other_chips(x, y)):
                cp = pltpu.make_async_remote_copy(
                    src_ref=zs[a].at[4 * px + 2 * py + c], dst_ref=zs[a].at[4 * px + 2 * py + 1 - c],
                    send_sem=send_sems.at[3 * a + j], recv_sem=recv_sems.at[3 * a + j],
                    device_id=(x, y, 1 - c), device_id_type=MESH)
                cp.wait_send()
                cp.wait_recv()

    outs = _pcall(
        body, name=name, out_shape=tuple(pltpu.HBM(z.shape, z.dtype) for z in st["zones"]),
        in_specs=[HBM] * n + [SEM, SEM, ANY], out_specs=tuple([HBM] * n),
        input_output_aliases={i: i for i in range(n)},
        compiler_params=pltpu.CompilerParams(has_side_effects=EFFECT),
    )(*st["zones"], st["send"], st["recv"], after)
    return list(outs)


def scatter_start(parts, name):
    n = len(parts)

    def body(*refs):
        srcs, lands = refs[:n], refs[n:2 * n]
        send_sems, recv_sems = refs[2 * n], refs[2 * n + 1]
        token = refs[-1]
        x, y, c = _place()
        for a in range(n):
            for j, (px, py) in enumerate(_other_chips(x, y)):
                pltpu.make_async_remote_copy(
                    src_ref=srcs[a].at[2 * px + py], dst_ref=lands[a].at[j], send_sem=send_sems.at[3 * a + j],
                    recv_sem=recv_sems.at[3 * a + j], device_id=(px, py, c), device_id_type=MESH).start()
        token[...] = jnp.zeros_like(token)

    zones = [lax.empty((3,) + p.shape[1:], p.dtype) for p in parts]
    outs = _pcall(
        body, name=name,
        out_shape=(pltpu.SemaphoreType.DMA((3 * n,)), pltpu.SemaphoreType.DMA((3 * n,)),
                   *[pltpu.HBM(p.shape, p.dtype) for p in parts], *[pltpu.HBM(z.shape, z.dtype) for z in zones],
                   jax.ShapeDtypeStruct((8, LANES), F32)),
        in_specs=[HBM] * (2 * n), out_specs=(SEM, SEM, *[HBM] * (2 * n), pl.BlockSpec(memory_space=pltpu.VMEM)),
        input_output_aliases={i: 2 + i for i in range(2 * n)},
        compiler_params=pltpu.CompilerParams(has_side_effects=EFFECT),
    )(*[_in_hbm(p) for p in parts], *[_in_hbm(z) for z in zones])
    return dict(send=outs[0], recv=outs[1], srcs=list(outs[2:2 + n]), lands=list(outs[2 + n:2 + 2 * n]), token=outs[-1])


def scatter_wait(st, after, name):
    n = len(st["srcs"])

    def body(*refs):
        srcs, lands = refs[:n], refs[n:2 * n]
        send_sems, recv_sems = refs[2 * n], refs[2 * n + 1]
        x, y, c = _place()
        for a in range(n):
            for j, (px, py) in enumerate(_other_chips(x, y)):
                cp = pltpu.make_async_remote_copy(
                    src_ref=srcs[a].at[2 * px + py], dst_ref=lands[a].at[j], send_sem=send_sems.at[3 * a + j],
                    recv_sem=recv_sems.at[3 * a + j], device_id=(px, py, c), device_id_type=MESH)
                cp.wait_send()
                cp.wait_recv()

    outs = _pcall(
        body, name=name,
        out_shape=(*[pltpu.HBM(a.shape, a.dtype) for a in st["srcs"]], *[pltpu.HBM(z.shape, z.dtype) for z in st["lands"]]),
        in_specs=[HBM] * (2 * n) + [SEM, SEM, ANY], out_specs=tuple([HBM] * (2 * n)),
        input_output_aliases={i: i for i in range(2 * n)},
        compiler_params=pltpu.CompilerParams(has_side_effects=EFFECT),
    )(*st["srcs"], *st["lands"], st["send"], st["recv"], after)
    return list(outs[:n]), list(outs[n:])


def _unshard(g, axis):
    nd = g.ndim - 1
    axis = axis % nd
    t = jnp.moveaxis(g, 0, axis)
    shp = list(g.shape[1:])
    shp[axis] *= N_DEV
    return t.reshape(shp)


def _to_shards(full, axis):
    axis = axis % full.ndim
    shp = list(full.shape)
    shp[axis:axis + 1] = [N_DEV, shp[axis] // N_DEV]
    return jnp.moveaxis(full.reshape(shp), axis, 0)


def _pack(arrs, rows):
    flat = jnp.concatenate([a.reshape(-1).astype(F32) for a in arrs])
    return jnp.pad(flat, (0, rows * LANES - flat.shape[0])).reshape(rows, LANES)


def _unpack(buf, shapes):
    flat = buf.reshape(-1)
    out, off = [], 0
    for s in shapes:
        n = math.prod(s)
        out.append(flat[off:off + n].reshape(s))
        off += n
    return out


def _rows_for(shapes):
    n = sum(math.prod(s) for s in shapes)
    return -(-n // (8 * LANES)) * 8


def _row(v, width=None):
    v = v.reshape(1, -1).astype(F32)
    if width is not None and v.shape[1] < width:
        v = jnp.pad(v, ((0, 0), (0, width - v.shape[1])))
    return v


def _after(order, width):
    if not order:
        return None
    t = order[0][0:1, 0:1]
    for o in order[1:]:
        t = t + o[0:1, 0:1]
    return jnp.broadcast_to(t, (1, width))


def _norm_after(norm, order):
    row = _after(order, norm.shape[1])
    return norm if row is None else norm + row


FFN_TN = 256


def ffn_in(h, norm, w_gate, w_up, name):
    def body(h_ref, n_ref, wg_ref, wu_ref, u_ref, g_ref, up_ref, act_ref, u_s):
        @pl.when(pl.program_id(0) == 0)
        def _():
            x = h_ref[...]
            r = lax.rsqrt(jnp.mean(x * x, axis=-1, keepdims=True) + EPS)
            u_s[...] = (x * r * n_ref[...]).astype(BF16)
            u_ref[...] = u_s[...]

        u = u_s[...]
        g = _dot(u, wg_ref[...], _NT)
        up = _dot(u, wu_ref[...], _NT)
        g_ref[...] = g.astype(BF16)
        up_ref[...] = up.astype(BF16)
        act_ref[...] = (_silu(g) * up).astype(BF16)

    whole = pl.BlockSpec((T, D), lambda j: (0, 0))
    wspec = pl.BlockSpec((FFN_TN, D), lambda j: (j, 0))
    col = pl.BlockSpec((T, FFN_TN), lambda j: (0, j))
    return _pcall(
        body, name=name, grid=(DFF // FFN_TN,), in_specs=[whole, pl.BlockSpec((1, D), lambda j: (0, 0)), wspec, wspec],
        out_specs=[whole, col, col, col],
        out_shape=[jax.ShapeDtypeStruct((T, D), BF16)] + [jax.ShapeDtypeStruct((T, DFF), BF16)] * 3,
        scratch_shapes=[pltpu.VMEM((T, D), BF16)], compiler_params=_cparams(("arbitrary",)),
    )(h, norm, w_gate, w_up)


def ffn_back(dh_b, w_down, g, up, after_row, name):
    has_row = after_row is not None

    def body(*refs):
        dh_ref, wd_ref, g_ref, up_ref = refs[:4]
        dg_ref, dup_ref = refs[-2:]
        da = _dot(dh_ref[...], wd_ref[...], _NT)
        if has_row:
            da = da + refs[4][...]
        g = g_ref[...].astype(F32)
        dg_ref[...] = (da * up_ref[...].astype(F32) * _dsilu(g)).astype(BF16)
        dup_ref[...] = (da * _silu(g)).astype(BF16)

    col = pl.BlockSpec((T, FFN_TN), lambda j: (0, j))
    in_specs = [pl.BlockSpec((T, D), lambda j: (0, 0)), pl.BlockSpec((FFN_TN, D), lambda j: (j, 0)), col, col]
    args = [dh_b, w_down, g, up]
    if has_row:
        in_specs.append(pl.BlockSpec((1, FFN_TN), lambda j: (0, j)))
        args.append(after_row)
    return _pcall(
        body, name=name, grid=(DFF // FFN_TN,), in_specs=in_specs, out_specs=[col, col],
        out_shape=[jax.ShapeDtypeStruct((T, DFF), BF16)] * 2, compiler_params=_cparams(("parallel",)),
    )(*args)


def ffn_layer_fwd(h, p, tag, order=()):
    u, g, up, act = ffn_in(h, _norm_after(p["norm"], order), p["w_gate"], p["w_up"], f"{tag}_in")
    h2 = matmul(act, p["w_down"], "nn", residual=h, name=f"{tag}_down")
    return h2, (h, u, g, up, act)


def ffn_layer_bwd(dh, dh_b, saved, p, tag, order=()):
    h, u, g, up, act = saved
    d_down = matmul(act, dh_b, "tn", out_dtype=BF16, name=f"{tag}_dwd")
    dg, dup = ffn_back(dh_b, p["w_down"], g, up, _after(order, DFF), f"{tag}_back")
    du = matmul(dg, p["w_gate"], "nn", name=f"{tag}_dug")
    du = matmul(dup, p["w_up"], "nn", residual=du, name=f"{tag}_duu")
    d_gate = matmul(dg, u, "tn", out_dtype=BF16, name=f"{tag}_dwg")
    d_up = matmul(dup, u, "tn", out_dtype=BF16, name=f"{tag}_dwu")
    dh2, dh2_b, d_norm = rms_bwd(du, h, p["norm"], dh, f"{tag}_drms")
    return dh2, dh2_b, dict(norm=d_norm, w_gate=d_gate, w_up=d_up, w_down=d_down)


def conv_layer_fwd(h, p, tag, order=()):
    u = rms_fwd(h, _norm_after(p["norm"], order), f"{tag}_rms")
    hh = matmul(u, p["w_pw1"], "nn", bias=p["b_pw1"], name=f"{tag}_pw1")
    gl = glu_fwd(hh, f"{tag}_glu")
    c2 = dwconv_fwd(gl, 0, p["dw_w"], p["dw_b"], KCV, 128, False, f"{tag}_dw")[0]
    s = ln_silu_fwd(c2, p["ln_g"], p["ln_b"], f"{tag}_ln")
    h2 = matmul(s, p["w_pw2"], "nn", bias=p["b_pw2"], residual=h, name=f"{tag}_pw2")
    return h2, (h, u, hh, gl, c2, s)


def conv_layer_bwd(dh, dh_b, saved, p, tag, order=()):
    h, u, hh, gl, c2, s = saved
    ds = matmul(dh_b, p["w_pw2"], "nt", bias=_after(order, D), name=f"{tag}_ds")
    d_pw2 = matmul(s, dh_b, "tn", out_dtype=BF16, name=f"{tag}_dwpw2")
    dc2, d_lng, d_lnb, d_bpw2 = ln_silu_bwd(ds, c2, dh, p["ln_g"], p["ln_b"], f"{tag}_dln")
    dgl, d_dww, d_dwb = dwconv_bwd(dc2, None, gl, 0, p["dw_w"], KCV, 128, False, F32, f"{tag}_ddw")
    dhh, d_bpw1 = glu_bwd(dgl, hh, f"{tag}_dglu")
    du = matmul(dhh, p["w_pw1"], "nt", name=f"{tag}_du")
    d_pw1 = matmul(u, dhh, "tn", out_dtype=BF16, name=f"{tag}_dwpw1")
    dh2, dh2_b, d_norm = rms_bwd(du, h, p["norm"], dh, f"{tag}_drms")
    grads = dict(norm=d_norm, w_pw1=d_pw1, b_pw1=d_bpw1, dw_w=d_dww[:KCV], dw_b=d_dwb, ln_g=d_lng, ln_b=d_lnb,
                 w_pw2=d_pw2, b_pw2=d_bpw2)
    return dh2, dh2_b, grads


def ssm_layer_fwd(h, p, tag, order=(), mid=None):
    u = rms_fwd(h, _norm_after(p["norm"], order), f"{tag}_rms")
    zx = matmul(u, p["w_in"], "nn", name=f"{tag}_in")
    cpre, xbc = dwconv_fwd(zx, DI // 512, p["conv_w"], p["conv_b"], KSSM, 512, True, f"{tag}_conv")
    dt, da, dtx, dax = dt_fwd(zx, p["dt_bias"], p["a_log"], f"{tag}_dt")
    daT = da[:, :NH].T
    y, states = ssd_fwd(xbc, dtx, dax, daT, p["d_full"], f"{tag}_ssd")
    gate_norm = p["gate_norm"] if mid is None else _norm_after(p["gate_norm"], mid(y))
    yn = gatenorm_fwd(y, zx, gate_norm, f"{tag}_gn")
    h2 = matmul(yn, p["w_out"], "nn", residual=h, name=f"{tag}_out")
    return h2, (h, u, zx, cpre, xbc, dt, dtx, dax, daT, y, states, yn)


def ssm_layer_bwd(dh, dh_b, saved, p, tag, order=()):
    h, u, zx, cpre, xbc, dt, dtx, dax, daT, y, states, yn = saved
    dyn = matmul(dh_b, p["w_out"], "nt", bias=_after(order, DI), name=f"{tag}_dyn")
    d_wout = matmul(yn, dh_b, "tn", out_dtype=BF16, name=f"{tag}_dwout")
    dy, dz, d_gn = gatenorm_bwd(dyn, y, zx, p["gate_norm"], f"{tag}_dgn")
    dxbc, ddt, dda, dD = ssd_bwd(dy, xbc, dtx, dax, daT, p["d_full"], states, f"{tag}_dssd")
    draw, d_dtb, d_alog = dt_bwd(ddt, dda, dt, zx, p["dt_bias"], p["a_log"], f"{tag}_ddt")
    dxpre, d_cw, d_cb = dwconv_bwd(dxbc, cpre, zx, DI // 512, p["conv_w"], KSSM, 512, True, BF16, f"{tag}_dconv")
    dzx = jnp.concatenate([dz, dxpre, draw, jnp.zeros((T, DINP_PAD - 2 * DI - 2 * NG * DS - LANES), BF16)], axis=1)
    du = matmul(dzx, p["w_in"], "nt", name=f"{tag}_du")
    d_win = matmul(u, dzx, "tn", out_dtype=BF16, name=f"{tag}_dwin")
    dh2, dh2_b, d_norm = rms_bwd(du, h, p["norm"], dh, f"{tag}_drms")
    d_d = headsum(dD.reshape(NH, HD), f"{tag}_dD").reshape(NH)
    grads = dict(norm=d_norm, w_in=d_win[:, :DINP], conv_w=d_cw[:KSSM], conv_b=d_cb, dt_bias=d_dtb[0, :NH],
                 a_log=d_alog[0, :NH], d=d_d, gate_norm=d_gn, w_out=d_wout)
    return dh2, dh2_b, grads


BIG = ["ssm_w_in", "ssm_w_out", "cv_w_pw1", "cv_w_pw2", "ffn_w_gate", "ffn_w_up", "ffn_w_down"]
TRANSPOSED = ("ffn_w_gate", "ffn_w_up")
LAYER_AXIS = {"ssm_w_in": -1, "ssm_w_out": 0, "cv_w_pw1": -1, "cv_w_pw2": 0, "ffn_w_gate": 0, "ffn_w_up": 0,
              "ffn_w_down": 0}
SMALL_SHARDED = ["ssm_conv_w", "cv_norm", "cv_b_pw1", "cv_dw_w", "cv_dw_b", "cv_ln_g", "cv_ln_b", "cv_b_pw2"]
SMALL_REPL = ["ssm_norm", "ssm_conv_b", "ssm_dt_bias", "ssm_a_log", "ssm_d", "ssm_gate_norm", "ffn_norm", "final_norm"]
WEIGHTS = ["ssm_norm", "ssm_w_in", "ssm_conv_w", "ssm_conv_b", "ssm_dt_bias", "ssm_a_log", "ssm_d", "ssm_gate_norm",
           "ssm_w_out", "cv_norm", "cv_w_pw1", "cv_b_pw1", "cv_dw_w", "cv_dw_b", "cv_ln_g", "cv_ln_b", "cv_w_pw2",
           "cv_b_pw2", "ffn_norm", "ffn_w_gate", "ffn_w_up", "ffn_w_down", "final_norm"]
SMALL = [n for n in WEIGHTS if n not in BIG]


N_STAGES = 8


def _stage_layer(s):
    i = s // 2
    if s % 2:
        return "ffn", i
    return ("ssm" if i % 2 == 0 else "cv"), i // 2


def _stage_group(s):
    fam, l = _stage_layer(s)
    names = {"ffn": ["ffn_w_gate", "ffn_w_up", "ffn_w_down"], "ssm": ["ssm_w_in", "ssm_w_out"],
             "cv": ["cv_w_pw1", "cv_w_pw2"]}[fam]
    return [(n, l) for n in names]


def _stage_params(s, big, small):
    fam, l = _stage_layer(s)
    if fam == "ffn":
        return dict(norm=_row(small["ffn_norm"][l]), w_gate=big["ffn_w_gate"], w_up=big["ffn_w_up"],
                    w_down=big["ffn_w_down"])
    if fam == "ssm":
        return dict(norm=_row(small["ssm_norm"][l]), w_in=jnp.pad(big["ssm_w_in"], ((0, 0), (0, DINP_PAD - DINP))),
                    conv_w=jnp.pad(small["ssm_conv_w"][l], ((0, 8 - KSSM), (0, 0))), conv_b=_row(small["ssm_conv_b"][l]),
                    dt_bias=_row(small["ssm_dt_bias"][l], LANES), a_log=_row(small["ssm_a_log"][l], LANES),
                    d_full=_row(jnp.repeat(small["ssm_d"][l], HD)), gate_norm=_row(small["ssm_gate_norm"][l]),
                    w_out=big["ssm_w_out"])
    return dict(norm=_row(small["cv_norm"][l]), w_pw1=big["cv_w_pw1"], b_pw1=_row(small["cv_b_pw1"][l]),
                dw_w=jnp.pad(small["cv_dw_w"][l], ((0, 32 - KCV), (0, 0))), dw_b=_row(small["cv_dw_b"][l]),
                ln_g=_row(small["cv_ln_g"][l]), ln_b=_row(small["cv_ln_b"][l]), w_pw2=big["cv_w_pw2"],
                b_pw2=_row(small["cv_b_pw2"][l]))


_STAGE_FWD = {"ffn": ffn_layer_fwd, "ssm": ssm_layer_fwd, "cv": conv_layer_fwd}
_STAGE_BWD = {"ffn": ffn_layer_bwd, "ssm": ssm_layer_bwd, "cv": conv_layer_bwd}


def _stage_fwd(s, h, p, order=(), mid=None):
    fam, l = _stage_layer(s)
    if mid is not None:
        return ssm_layer_fwd(h, p, f"{fam}{l}", order, mid)
    return _STAGE_FWD[fam](h, p, f"{fam}{l}", order)


def _stage_bwd(s, dh, dh_b, p, saved, order=()):
    fam, l = _stage_layer(s)
    dh, dh_b, g = _STAGE_BWD[fam](dh, dh_b, saved, p, f"{fam}{l}", order)
    return dh, dh_b, {f"{fam}_{k}": val for k, val in g.items()}


def _local(x, tgt, full):
    h, tape = x, []
    for s in range(N_STAGES):
        big = {n: (full[n][l].T if n in TRANSPOSED else full[n][l]) for n, l in _stage_group(s)}
        p = _stage_params(s, big, full)
        h, saved = _stage_fwd(s, h, p)
        tape.append((p, saved))
    dh, dh_b, d_final, loss_row = loss_head(h, _row(full["final_norm"]), tgt, "loss_head")
    gl = {n: [None] * full[n].shape[0] for n in WEIGHTS if n != "final_norm"}
    for s in reversed(range(N_STAGES)):
        dh, dh_b, g = _stage_bwd(s, dh, dh_b, *tape[s])
        for n, val in g.items():
            val = val.T if n in TRANSPOSED else val
            gl[n][_stage_layer(s)[1]] = val.reshape(full[n].shape[1:])
    grads = {n: jnp.stack(vs) for n, vs in gl.items()}
    grads["final_norm"] = d_final.reshape(D)
    return loss_row, dh, grads


def _step(x, tgt, w, m, v):
    idx = 4 * lax.axis_index("x") + 2 * lax.axis_index("y") + lax.axis_index("c")
    small_shapes = [w[n].shape for n in SMALL_SHARDED]
    small_pack = _pack([w[n] for n in SMALL_SHARDED], _rows_for(small_shapes))

    def view(n, a):
        return jnp.swapaxes(a, 1, 2) if n in TRANSPOSED else a

    wv, mv, vv = ({n: view(n, t[n]) for n in BIG} for t in (w, m, v))

    def blocks(s):
        return [wv[n][l].astype(BF16) for n, l in _stage_group(s)] + ([small_pack] if s == 0 else [])

    arrs = [blocks(s) for s in range(N_STAGES)]
    first = gather_start(arrs[0], None, "gather0_start")
    passing = pass_start(gather_wait(first, first["token"], "gather0_wait"), "pass0_start")
    crossing = gather_start(arrs[1], passing["token"], "gather1_start")
    small = {n: w[n] for n in SMALL_REPL}
    flight = dict(passing=passing, crossing=crossing)

    def advance(s, after):
        tokens = []
        if s + 1 < N_STAGES:
            landed = gather_wait(flight["crossing"], after, f"gather{s + 1}_wait")
            flight["passing"] = pass_start(landed, f"pass{s + 1}_start")
            tokens.append(flight["passing"]["token"])
        if s + 2 < N_STAGES:
            flight["crossing"] = gather_start(arrs[s + 2], flight["passing"]["token"], f"gather{s + 2}_start")
            tokens.append(flight["crossing"]["token"])
        return tokens

    h, tape, after = x, [], crossing["token"]
    for s in range(N_STAGES):
        zones = pass_wait(flight["passing"], after, f"pass{s}_wait")
        order = advance(s, zones[0]) if s else []
        mid = functools.partial(advance, 0) if s == 0 else None
        zones = [lax.dynamic_update_slice_in_dim(z, a[None], idx, 0) for z, a in zip(zones, arrs[s])]
        if s == 0:
            per_dev = [_unpack(zones[-1][k], small_shapes) for k in range(N_DEV)]
            for q, n in enumerate(SMALL_SHARDED):
                small[n] = _unshard(jnp.stack([per_dev[k][q] for k in range(N_DEV)]), -1)
        big = {n: _unshard(z, LAYER_AXIS[n]) for (n, _), z in zip(_stage_group(s), zones)}
        p = _stage_params(s, big, small)
        h, saved = _stage_fwd(s, h, p, order, mid)
        tape.append((p, saved))
        after = h

    dh, dh_b, d_final, loss_row = loss_head(h, _row(w["final_norm"]), tgt, "loss_head")

    out = {}
    small_g = {n: [None] * w[n].shape[0] for n in SMALL if n != "final_norm"}

    def finish(s, st, after):
        by_chip, recv = scatter_wait(st, after, f"scatter{s}_wait")
        for (n, l), own, r in zip(_stage_group(s), by_chip, recv):
            out[n] = adamw_layer(r, own, wv[n], mv[n], vv[n], l, out.get(n), f"adamw_{n}{l}")

    started, order, summed = [], [], None
    for s in reversed(range(N_STAGES)):
        dh, dh_b, g = _stage_bwd(s, dh, dh_b, *tape[s], order)
        for n, val in g.items():
            if n not in BIG:
                small_g[n][_stage_layer(s)[1]] = val.reshape(small[n].shape[1:])
        if s == 0:
            grads = {n: jnp.stack(vs) for n, vs in small_g.items()}
            grads["final_norm"] = d_final.reshape(D)
            small_full_shapes = [grads[n].shape for n in SMALL] + [(1,)]
            packed = _pack([grads[n] for n in SMALL] + [loss_row[0, :1]], _rows_for(small_full_shapes))
            summed = sum_leading(all_gather([packed], "gather_small_grads")[0], "sum_small_grads")
        gsend = [_to_shards(g[n], LAYER_AXIS[n]) for n, _ in _stage_group(s)]
        from_sibling = sibling_exchange(gsend, f"scatter{s}_sibling", summed)
        by_chip = [pair_sum(a, r, f"pair_sum_{n}{l}") for (n, l), a, r in zip(_stage_group(s), gsend, from_sibling)]
        started.append((s, scatter_start(by_chip, f"scatter{s}_start")))
        order = [started[-1][1]["token"]]
    last = started[-1][1]["token"]
    for s, st in started[:-1]:
        finish(s, st, last)
    parts = _unpack(summed + last[0:1, 0:1], small_full_shapes)
    loss = parts[-1][0]
    for n, g in zip(SMALL, parts[:-1]):
        if n in SMALL_SHARDED:
            s = w[n].shape[-1]
            g = lax.dynamic_slice_in_dim(g, idx * s, s, axis=g.ndim - 1)
        out[n] = adamw(g[None], w[n], m[n], v[n], f"adamw_{n}")
    finish(*started[-1], summed)
    for n in TRANSPOSED:
        out[n] = [jnp.swapaxes(a, 1, 2) for a in out[n]]
    return loss, dh, out


def kernel(x, ssm_norm, ssm_w_in, ssm_conv_w, ssm_conv_b, ssm_dt_bias, ssm_a_log, ssm_d, ssm_gate_norm, ssm_w_out, cv_norm, cv_w_pw1, cv_b_pw1, cv_dw_w, cv_dw_b, cv_ln_g, cv_ln_b, cv_w_pw2, cv_b_pw2, ffn_norm, ffn_w_gate, ffn_w_up, ffn_w_down, final_norm, loss_target, m_ssm_norm, m_ssm_w_in, m_ssm_conv_w, m_ssm_conv_b, m_ssm_dt_bias, m_ssm_a_log, m_ssm_d, m_ssm_gate_norm, m_ssm_w_out, m_cv_norm, m_cv_w_pw1, m_cv_b_pw1, m_cv_dw_w, m_cv_dw_b, m_cv_ln_g, m_cv_ln_b, m_cv_w_pw2, m_cv_b_pw2, m_ffn_norm, m_ffn_w_gate, m_ffn_w_up, m_ffn_w_down, m_final_norm, v_ssm_norm, v_ssm_w_in, v_ssm_conv_w, v_ssm_conv_b, v_ssm_dt_bias, v_ssm_a_log, v_ssm_d, v_ssm_gate_norm, v_ssm_w_out, v_cv_norm, v_cv_w_pw1, v_cv_b_pw1, v_cv_dw_w, v_cv_dw_b, v_cv_ln_g, v_cv_ln_b, v_cv_w_pw2, v_cv_b_pw2, v_ffn_norm, v_ffn_w_gate, v_ffn_w_up, v_ffn_w_down, v_final_norm):
    args = locals()
    w = {n: args[n] for n in WEIGHTS}
    m = {n: args["m_" + n] for n in WEIGHTS}
    v = {n: args["v_" + n] for n in WEIGHTS}
    loss, grad_x, out = _step(x[0], loss_target[0], w, m, v)
    res = [loss, grad_x[None]]
    for k in range(4):
        res += [out[n][k] for n in WEIGHTS]
    return tuple(res)
```

```python
import functools
import math

import jax
import jax.numpy as jnp
from jax import lax
from jax.experimental import pallas as pl
from jax.experimental.pallas import tpu as pltpu

F32 = jnp.float32
BF16 = jnp.bfloat16

N_DEV = 8
T = 2048
D = 1024
DI = 2048
NH = 32
HD = 64
NG = 4
GW = DI // NG
DS = 128
CONVD = DI + 2 * NG * DS
DINP = 2 * DI + 2 * NG * DS + NH
DINP_PAD = 5376
CH = 128
NCH = T // CH
DFF = 2816
KSSM = 4
KCV = 31
EPS = 1e-5
LANES = 128
VMEM_LIMIT = 56 * 1024 * 1024

ADAM_LR = 0.001
ADAM_B1 = 0.9
ADAM_B2 = 0.999
ADAM_EPS = 1e-08
ADAM_WD = 0.01
ADAM_STEP = 10

MESH = pl.DeviceIdType.MESH
ANY = pl.BlockSpec(memory_space=pl.ANY)


def _pcall(body, **kw):
    return pl.pallas_call(body, **kw)


def _cparams(sem):
    return pltpu.CompilerParams(dimension_semantics=sem, vmem_limit_bytes=VMEM_LIMIT)


def _pick(n, cands):
    for c in cands:
        if n % c == 0:
            return c
    raise ValueError(f"no tile for {n}")


def _sigmoid(x):
    return 1.0 / (1.0 + jnp.exp(-x))


def _silu(x):
    return x * _sigmoid(x)


def _dsilu(x):
    s = _sigmoid(x)
    return s * (1.0 + x * (1.0 - s))


_DIMS = {"nn": (((1,), (0,)), ((), ())), "nt": (((1,), (1,)), ((), ())), "tn": (((0,), (0,)), ((), ()))}


MM_VMEM_BUDGET = 40 * 1024 * 1024
MM_MAX_K = 3072


def _mm_tiles(M, N, K, out_bytes, has_res):
    tk = K if K <= MM_MAX_K else K // 2
    assert K % tk == 0 and tk % LANES == 0
    nk = K // tk
    best = None
    for tm in (2048, 1792, 1408, 1024, 768, 512, 256, 128):
        if M % tm:
            continue
        for tn in (1408, 1024, 768, 512, 384, 256, 128):
            if N % tn:
                continue
            blocks = tm * tk * 2 + tk * tn * 2 + tm * tn * out_bytes + (tm * tn * 4 if has_res else 0)
            vmem = 2 * blocks + tm * tn * 4 * (2 if nk > 1 else 1)
            if vmem > MM_VMEM_BUDGET:
                continue
            traffic = (N // tn if nk > 1 else 1) * M * K + (M // tm) * N * K
            key = (-traffic, tm * tn)
            if best is None or key > best[0]:
                best = (key, tm, tn)
    assert best is not None, (M, N, K)
    return best[1], best[2], tk


def matmul(a, b, mode, *, name, bias=None, residual=None, out_dtype=F32):
    assert a.dtype == BF16 and b.dtype == BF16
    if mode == "nn":
        (M, K), (K2, N) = a.shape, b.shape
    elif mode == "nt":
        (M, K), (N, K2) = a.shape, b.shape
    else:
        (K, M), (K2, N) = a.shape, b.shape
    assert K == K2
    has_bias, has_res = bias is not None, residual is not None
    tm, tn, tk = _mm_tiles(M, N, K, jnp.dtype(out_dtype).itemsize, has_res)
    nk = K // tk
    dims = _DIMS[mode]

    def body(*refs):
        a_ref, b_ref = refs[0], refs[1]
        pos = 2
        bias_ref = res_ref = None
        if has_bias:
            bias_ref = refs[pos]
            pos += 1
        if has_res:
            res_ref = refs[pos]
            pos += 1
        o_ref = refs[pos]

        def finish(out):
            if has_bias:
                out = out + bias_ref[...]
            if has_res:
                out = out + res_ref[...]
            o_ref[...] = out.astype(o_ref.dtype)

        part = lax.dot_general(a_ref[...], b_ref[...], dims, preferred_element_type=F32)
        if nk == 1:
            finish(part)
            return
        acc = refs[pos + 1]
        k = pl.program_id(2)

        @pl.when(k == 0)
        def _():
            acc[...] = part

        @pl.when(jnp.logical_and(k > 0, k < nk - 1))
        def _():
            acc[...] += part

        @pl.when(k == nk - 1)
        def _():
            finish(acc[...] + part)

    if mode == "tn":
        a_spec = pl.BlockSpec((tk, tm), lambda i, j, k: (k, i))
    else:
        a_spec = pl.BlockSpec((tm, tk), lambda i, j, k: (i, k))
    if mode == "nt":
        b_spec = pl.BlockSpec((tn, tk), lambda i, j, k: (j, k))
    else:
        b_spec = pl.BlockSpec((tk, tn), lambda i, j, k: (k, j))
    in_specs, args = [a_spec, b_spec], [a, b]
    if has_bias:
        in_specs.append(pl.BlockSpec((1, tn), lambda i, j, k: (0, j)))
        args.append(bias.reshape(1, N).astype(F32))
    if has_res:
        in_specs.append(pl.BlockSpec((tm, tn), lambda i, j, k: (i, j)))
        args.append(residual)
    return _pcall(
        body, name=name, grid=(M // tm, N // tn, nk), in_specs=in_specs,
        out_specs=pl.BlockSpec((tm, tn), lambda i, j, k: (i, j)),
        out_shape=jax.ShapeDtypeStruct((M, N), out_dtype),
        scratch_shapes=[pltpu.VMEM((tm, tn), F32)] if nk > 1 else [],
        compiler_params=_cparams(("parallel", "parallel", "arbitrary")),
    )(*args)


def rowwise(fn, rows, bcasts, outs, accs=(), *, name, tm=256):
    n_rows, n_b, n_o, n_a = len(rows), len(bcasts), len(outs), len(accs)
    nt = T // tm

    def body(*refs):
        ins = [r[...] for r in refs[:n_rows + n_b]]
        res = fn(*ins)
        o_refs = refs[n_rows + n_b:n_rows + n_b + n_o]
        a_refs = refs[n_rows + n_b + n_o:]
        for r, v in zip(o_refs, res[:n_o]):
            r[...] = v.astype(r.dtype)
        if n_a:
            i = pl.program_id(0)

            @pl.when(i == 0)
            def _():
                for r in a_refs:
                    r[...] = jnp.zeros_like(r)

            for r, v in zip(a_refs, res[n_o:]):
                r[...] += v

    in_specs = [pl.BlockSpec((tm, w), functools.partial(lambda i, cb: (i, cb), cb=cb)) for (_, w, cb) in rows]
    in_specs += [pl.BlockSpec(b.shape, lambda i: (0, 0)) for b in bcasts]
    out_specs = [pl.BlockSpec((tm, w), lambda i: (i, 0)) for (w, _) in outs]
    out_specs += [pl.BlockSpec((1, w), lambda i: (0, 0)) for w in accs]
    out_shape = [jax.ShapeDtypeStruct((T, w), dt) for (w, dt) in outs]
    out_shape += [jax.ShapeDtypeStruct((1, w), F32) for w in accs]
    return _pcall(
        body, name=name, grid=(nt,), in_specs=in_specs, out_specs=out_specs, out_shape=out_shape,
        compiler_params=_cparams(("arbitrary",)),
    )(*[r[0] for r in rows], *bcasts)


def _full(a):
    return (a, a.shape[1], 0)


def _rsum(v):
    return jnp.sum(v, axis=0, keepdims=True)


def rms_fwd(h, g, name):
    def fn(x, g):
        r = lax.rsqrt(jnp.mean(x * x, axis=-1, keepdims=True) + EPS)
        return (x * r * g,)
    return rowwise(fn, [_full(h)], [g], [(D, BF16)], name=name)[0]


def rms_bwd(du, h, g, dres, name):
    def fn(du, x, dres, g):
        r = lax.rsqrt(jnp.mean(x * x, axis=-1, keepdims=True) + EPS)
        xh = x * r
        dxh = du * g
        dx = r * (dxh - xh * jnp.mean(dxh * xh, axis=-1, keepdims=True))
        dh = dres + dx
        return dh, dh, _rsum(du * xh)
    return rowwise(fn, [_full(du), _full(h), _full(dres)], [g], [(D, F32), (D, BF16)], [D], name=name)


def loss_head(h, g, tgt, name):
    def fn(x, tgt, g):
        r = lax.rsqrt(jnp.mean(x * x, axis=-1, keepdims=True) + EPS)
        xh = x * r
        err = xh * g - tgt
        lsum = jnp.sum(jnp.sum(err * err, axis=-1, keepdims=True), axis=0, keepdims=True) * (0.5 / D)
        dy = err * (1.0 / D)
        dxh = dy * g
        dx = r * (dxh - xh * jnp.mean(dxh * xh, axis=-1, keepdims=True))
        return dx, dx, _rsum(dy * xh), jnp.broadcast_to(lsum, (1, LANES))
    return rowwise(fn, [_full(h), _full(tgt)], [g], [(D, F32), (D, BF16)], [D, LANES], name=name)


def glu_fwd(hh, name):
    def fn(a, g):
        return (a * _sigmoid(g),)
    return rowwise(fn, [(hh, D, 0), (hh, D, 1)], [], [(D, F32)], name=name)[0]


def glu_bwd(dgl, hh, name):
    def fn(dgl, a, g):
        s = _sigmoid(g)
        dhh = jnp.concatenate([dgl * s, dgl * a * s * (1.0 - s)], axis=1)
        return dhh, _rsum(dhh)
    return rowwise(fn, [_full(dgl), (hh, D, 0), (hh, D, 1)], [], [(2 * D, BF16)], [2 * D], name=name)


def ln_silu_fwd(c2, g, b, name):
    def fn(x, g, b):
        mu = jnp.mean(x, axis=-1, keepdims=True)
        xc = x - mu
        r = lax.rsqrt(jnp.mean(xc * xc, axis=-1, keepdims=True) + EPS)
        return (_silu(xc * r * g + b),)
    return rowwise(fn, [_full(c2)], [g, b], [(D, BF16)], name=name)[0]


def ln_silu_bwd(ds, c2, dh, g, b, name):
    def fn(ds, x, dh, g, b):
        mu = jnp.mean(x, axis=-1, keepdims=True)
        xc = x - mu
        r = lax.rsqrt(jnp.mean(xc * xc, axis=-1, keepdims=True) + EPS)
        xh = xc * r
        dn = ds * _dsilu(xh * g + b)
        dxh = dn * g
        dx = r * (dxh - jnp.mean(dxh, axis=-1, keepdims=True) - xh * jnp.mean(dxh * xh, axis=-1, keepdims=True))
        return dx, _rsum(dn * xh), _rsum(dn), _rsum(dh)
    return rowwise(fn, [_full(ds), _full(c2), _full(dh)], [g, b], [(D, F32)], [D, D, D], name=name)


def gatenorm_fwd(y, zx, gn, name):
    def fn(y, z, gn):
        hg = y * _silu(z)
        parts = []
        for k in range(NG):
            hk = hg[:, k * GW:(k + 1) * GW]
            parts.append(hk * lax.rsqrt(jnp.mean(hk * hk, axis=-1, keepdims=True) + EPS))
        return (jnp.concatenate(parts, axis=1) * gn,)
    return rowwise(fn, [_full(y), (zx, DI, 0)], [gn], [(DI, BF16)], name=name)[0]


def gatenorm_bwd(dyn, y, zx, gn, name):
    def fn(dyn, y, z, gn):
        sz = _silu(z)
        hg = y * sz
        dxh = dyn * gn
        dhg, xhs = [], []
        for k in range(NG):
            sl = slice(k * GW, (k + 1) * GW)
            hk = hg[:, sl]
            r = lax.rsqrt(jnp.mean(hk * hk, axis=-1, keepdims=True) + EPS)
            xh = hk * r
            dk = dxh[:, sl]
            dhg.append(r * (dk - xh * jnp.mean(dk * xh, axis=-1, keepdims=True)))
            xhs.append(xh)
        dhg = jnp.concatenate(dhg, axis=1)
        xh = jnp.concatenate(xhs, axis=1)
        return dhg * sz, dhg * y * _dsilu(z), _rsum(dyn * xh)
    return rowwise(fn, [_full(dyn), _full(y), (zx, DI, 0)], [gn], [(DI, F32), (DI, BF16)], [DI], name=name)


def _softplus(x):
    return jnp.maximum(x, 0.0) + jnp.log(1.0 + jnp.exp(-jnp.abs(x)))


def _spread(v, e):
    hi = v.astype(BF16)
    r = v - hi.astype(F32)
    mid = r.astype(BF16)
    lo = (r - mid.astype(F32)).astype(BF16)
    return _dot(hi, e) + _dot(mid, e) + _dot(lo, e)


def _spread2(v, e):
    hi = v.astype(BF16)
    lo = (v - hi.astype(F32)).astype(BF16)
    return _dot(hi, e) + _dot(lo, e)


def dt_fwd(zx, dt_bias, a_log, name):
    heads = (jnp.arange(DI)[None, :] // HD == jnp.arange(LANES)[:, None]).astype(BF16)

    def fn(raw, bias, a_log, e):
        dt = _softplus(raw + bias)
        da = dt * (-jnp.exp(a_log))
        return dt, da, _spread(dt, e), _spread(da, e)

    return rowwise(fn, [(zx, LANES, (2 * DI + 2 * NG * DS) // LANES)], [dt_bias, a_log, heads],
                   [(LANES, F32), (LANES, F32), (DI, F32), (DI, F32)], name=name)


def dt_bwd(ddt, dda, dt, zx, dt_bias, a_log, name):
    def fn(ddt, dda, dt, raw, bias, a_log):
        a = -jnp.exp(a_log)
        draw = (ddt + dda * a) * _sigmoid(raw + bias)
        return draw, _rsum(draw), _rsum(dda * dt) * a
    return rowwise(fn, [_full(ddt), _full(dda), _full(dt), (zx, LANES, (2 * DI + 2 * NG * DS) // LANES)],
                   [dt_bias, a_log], [(LANES, BF16)], [LANES, LANES], name=name)


def headsum(v, name):
    def body(v_ref, o_ref):
        o_ref[...] = jnp.sum(v_ref[...], axis=1, keepdims=True)
    return _pcall(body, name=name, out_shape=jax.ShapeDtypeStruct((v.shape[0], 1), F32))(v)


CONV_ROWS = 256


def _shifted(win, o, rows):
    if o == 0:
        return win[0:rows]
    n = win.shape[0]
    return pltpu.roll(win, shift=n - o, axis=0)[0:rows]


def dwconv_fwd(x, x_cb0, w, b, K, ct, act, name):
    C = w.shape[1]
    pad = 8 if K <= 8 else 32
    KP = w.shape[0]
    n_out = 2 if act else 1

    def body(x_ref, w_ref, b_ref, *rest):
        o_refs, px = rest[:n_out], rest[n_out]
        px[0:pad, :] = jnp.zeros((pad, ct), F32)
        px[pad:pad + T, :] = x_ref[...]
        wv = w_ref[...]
        bv = b_ref[...]
        for r0 in range(0, T, CONV_ROWS):
            win = px[r0:r0 + CONV_ROWS + pad, :]
            acc = jnp.broadcast_to(bv, (CONV_ROWS, ct))
            for k in range(K):
                acc = acc + wv[k:k + 1, :] * _shifted(win, pad - (K - 1) + k, CONV_ROWS)
            o_refs[0][r0:r0 + CONV_ROWS, :] = acc
            if act:
                o_refs[1][r0:r0 + CONV_ROWS, :] = _silu(acc)

    return _pcall(
        body, name=name, grid=(C // ct,),
        in_specs=[pl.BlockSpec((T, ct), lambda j: (0, x_cb0 + j)), pl.BlockSpec((KP, ct), lambda j: (0, j)),
                  pl.BlockSpec((1, ct), lambda j: (0, j))],
        out_specs=[pl.BlockSpec((T, ct), lambda j: (0, j))] * n_out,
        out_shape=[jax.ShapeDtypeStruct((T, C), F32)] * n_out,
        scratch_shapes=[pltpu.VMEM((T + pad, ct), F32)],
        compiler_params=_cparams(("parallel",)),
    )(x, w, b)


def dwconv_bwd(dout, cpre, x, x_cb0, w, K, ct, act, out_dtype, name):
    C = w.shape[1]
    pad = 8 if K <= 8 else 32
    KP = w.shape[0]

    def body(*refs):
        if act:
            d_ref, c_ref, x_ref, w_ref, dx_ref, dw_ref, db_ref, px, pd = refs
        else:
            d_ref, x_ref, w_ref, dx_ref, dw_ref, db_ref, px, pd = refs
        px[0:pad, :] = jnp.zeros((pad, ct), F32)
        px[pad:pad + T, :] = x_ref[...]
        pd[T:T + pad, :] = jnp.zeros((pad, ct), F32)
        if act:
            pd[0:T, :] = d_ref[...] * _dsilu(c_ref[...])
        else:
            pd[0:T, :] = d_ref[...]
        wv = w_ref[...]
        dws = [jnp.zeros((1, ct), F32) for _ in range(K)]
        db = jnp.zeros((1, ct), F32)
        for r0 in range(0, T, CONV_ROWS):
            dwin = pd[r0:r0 + CONV_ROWS + pad, :]
            xwin = px[r0:r0 + CONV_ROWS + pad, :]
            dc = dwin[0:CONV_ROWS]
            db = db + _rsum(dc)
            acc = jnp.zeros((CONV_ROWS, ct), F32)
            for k in range(K):
                acc = acc + wv[k:k + 1, :] * _shifted(dwin, K - 1 - k, CONV_ROWS)
                dws[k] = dws[k] + _rsum(dc * _shifted(xwin, pad - (K - 1) + k, CONV_ROWS))
            dx_ref[r0:r0 + CONV_ROWS, :] = acc.astype(dx_ref.dtype)
        dw_ref[...] = jnp.zeros((KP, ct), F32)
        for k in range(K):
            dw_ref[k:k + 1, :] = dws[k]
        db_ref[...] = db

    col = pl.BlockSpec((T, ct), lambda j: (0, j))
    in_specs = [col] + ([col] if act else []) + [pl.BlockSpec((T, ct), lambda j: (0, x_cb0 + j)),
                                                 pl.BlockSpec((KP, ct), lambda j: (0, j))]
    args = [dout] + ([cpre] if act else []) + [x, w]
    return _pcall(
        body, name=name, grid=(C // ct,), in_specs=in_specs,
        out_specs=[col, pl.BlockSpec((KP, ct), lambda j: (0, j)), pl.BlockSpec((1, ct), lambda j: (0, j))],
        out_shape=[jax.ShapeDtypeStruct((T, C), out_dtype), jax.ShapeDtypeStruct((KP, C), F32),
                   jax.ShapeDtypeStruct((1, C), F32)],
        scratch_shapes=[pltpu.VMEM((T + pad, ct), F32), pltpu.VMEM((T + pad, ct), F32)],
        compiler_params=_cparams(("parallel",)),
    )(*args)


def _scan(a, axis, reverse=False):
    n = a.shape[axis]
    idx = lax.broadcasted_iota(jnp.int32, a.shape, axis)
    s = 1
    while s < n:
        if reverse:
            a = a + jnp.where(idx < n - s, pltpu.roll(a, shift=n - s, axis=axis), 0.0)
        else:
            a = a + jnp.where(idx >= s, pltpu.roll(a, shift=s, axis=axis), 0.0)
        s *= 2
    return a


_NT = _DIMS["nt"]
_TN = _DIMS["tn"]


def _dot(a, b, dims=_DIMS["nn"]):
    return lax.dot_general(a, b, dims, preferred_element_type=F32)


def ssd_fwd(xbc, dtx, dax, daT, dfull, name):
    def body(xbc_ref, dtx_ref, dax_ref, daT_ref, df_ref, y_ref, st_ref, S):
        ci = pl.program_id(0)

        @pl.when(ci == 0)
        def _():
            S[...] = jnp.zeros_like(S)

        row = lax.broadcasted_iota(jnp.int32, (CH, CH), 0)
        lane = lax.broadcasted_iota(jnp.int32, (CH, CH), 1)
        acsT = _scan(daT_ref[...], 1)
        for g in range(NG):
            c0 = g * GW
            xs = xbc_ref[:, c0:c0 + GW]
            acs = _scan(dax_ref[:, c0:c0 + GW], 0)
            Bm = xbc_ref[:, DI + g * DS:DI + (g + 1) * DS].astype(BF16)
            Cm = xbc_ref[:, DI + NG * DS + g * DS:DI + NG * DS + (g + 1) * DS].astype(BF16)
            xdt = xs * dtx_ref[:, c0:c0 + GW]
            atot = acs[CH - 1:CH, :]
            Sg = S[:, c0:c0 + GW]
            st_ref[:, c0:c0 + GW] = Sg
            CB = _dot(Cm, Bm, _NT)
            yg = jnp.exp(acs) * _dot(Cm, Sg.astype(BF16)) + xs * df_ref[:, c0:c0 + GW]
            xd = (xdt * jnp.exp(atot - acs)).astype(BF16)
            S[:, c0:c0 + GW] = jnp.exp(atot) * Sg + _dot(Bm, xd, _TN)
            xdt_b = xdt.astype(BF16)
            for r in range(NH // NG):
                h = g * (NH // NG) + r
                hs = slice(r * HD, (r + 1) * HD)
                seg = acs[:, r * HD:r * HD + 1] - acsT[h:h + 1, :]
                Lm = jnp.where(row >= lane, jnp.exp(jnp.minimum(seg, 0.0)), 0.0)
                yd = _dot((CB * Lm).astype(BF16), xdt_b[:, hs])
                y_ref[:, c0 + r * HD:c0 + (r + 1) * HD] = yg[:, hs] + yd

    return _pcall(
        body, name=name, grid=(NCH,),
        in_specs=[pl.BlockSpec((CH, CONVD), lambda i: (i, 0)), pl.BlockSpec((CH, DI), lambda i: (i, 0)),
                  pl.BlockSpec((CH, DI), lambda i: (i, 0)), pl.BlockSpec((NH, CH), lambda i: (0, i)),
                  pl.BlockSpec((1, DI), lambda i: (0, 0))],
        out_specs=[pl.BlockSpec((CH, DI), lambda i: (i, 0)), pl.BlockSpec((None, DS, DI), lambda i: (i, 0, 0))],
        out_shape=[jax.ShapeDtypeStruct((T, DI), F32), jax.ShapeDtypeStruct((NCH, DS, DI), F32)],
        scratch_shapes=[pltpu.VMEM((DS, DI), F32)],
        compiler_params=_cparams(("arbitrary",)),
    )(xbc, dtx, dax, daT, dfull)


def ssd_bwd(dy, xbc, dtx, dax, daT, dfull, states, name):
    hsum = (jnp.arange(DI)[:, None] // HD == jnp.arange(LANES)[None, :]).astype(BF16).reshape(NG, GW, LANES)

    def body(dy_ref, xbc_ref, dtx_ref, dax_ref, daT_ref, df_ref, st_ref, hsum_ref, dxbc_ref, ddt_ref, dda_ref, dD_ref, dS):
        i = pl.program_id(0)

        @pl.when(i == 0)
        def _():
            dS[...] = jnp.zeros_like(dS)
            dD_ref[...] = jnp.zeros_like(dD_ref)

        row = lax.broadcasted_iota(jnp.int32, (CH, CH), 0)
        lane = lax.broadcasted_iota(jnp.int32, (CH, CH), 1)
        acsT = _scan(daT_ref[...], 1)
        ddt_all = jnp.zeros((CH, LANES), F32)
        dacs_all = jnp.zeros((CH, LANES), F32)
        colacc = jnp.zeros((CH, CH), F32)
        for g in range(NG):
            c0 = g * GW
            xs = xbc_ref[:, c0:c0 + GW]
            dtx = dtx_ref[:, c0:c0 + GW]
            acs = _scan(dax_ref[:, c0:c0 + GW], 0)
            Bm = xbc_ref[:, DI + g * DS:DI + (g + 1) * DS].astype(BF16)
            Cm = xbc_ref[:, DI + NG * DS + g * DS:DI + NG * DS + (g + 1) * DS].astype(BF16)
            xdt = xs * dtx
            atot = acs[CH - 1:CH, :]
            Sin = st_ref[:, c0:c0 + GW]
            dyg = dy_ref[:, c0:c0 + GW]
            dSo = dS[:, c0:c0 + GW]
            E = jnp.exp(acs)
            Etot = jnp.exp(atot)
            dec = jnp.exp(atot - acs)
            dD_ref[:, c0:c0 + GW] += _rsum(dyg * xs)
            dxs = dyg * df_ref[:, c0:c0 + GW]
            Sin_b = Sin.astype(BF16)
            dSo_b = dSo.astype(BF16)
            dY0 = dyg * E
            dY0_b = dY0.astype(BF16)
            dC = _dot(dY0_b, Sin_b, _NT)
            dS[:, c0:c0 + GW] = _dot(Cm, dY0_b, _TN) + Etot * dSo
            XD = xdt * dec
            dXD = _dot(Bm, dSo_b)
            dB = _dot(XD.astype(BF16), dSo_b, _NT)
            dxdt = dXD * dec
            Gq = dXD * XD
            dacs_x = dY0 * _dot(Cm, Sin_b) - Gq
            datot_x = _rsum(dSo * Sin) * Etot + _rsum(Gq)
            dacs_all = dacs_all + _spread2(dacs_x, hsum_ref[g])
            dtot8 = _spread2(jnp.broadcast_to(datot_x, (8, GW)), hsum_ref[g])
            dacs_all = dacs_all + jnp.where(row == CH - 1, jnp.broadcast_to(dtot8[0:1, :], (CH, LANES)), 0.0)
            CB = _dot(Cm, Bm, _NT)
            dCB = jnp.zeros((CH, CH), F32)
            xdt_b = xdt.astype(BF16)
            dy_b = dyg.astype(BF16)
            for r in range(NH // NG):
                h = g * (NH // NG) + r
                hs = slice(r * HD, (r + 1) * HD)
                seg = acs[:, r * HD:r * HD + 1] - acsT[h:h + 1, :]
                Lm = jnp.where(row >= lane, jnp.exp(jnp.minimum(seg, 0.0)), 0.0)
                dyr = dy_b[:, hs]
                dML = _dot(dyr, xdt_b[:, hs], _NT) * Lm
                dxbc_ref[:, c0 + r * HD:c0 + (r + 1) * HD] = _dot((CB * Lm).astype(BF16), dyr, _TN)
                dCB = dCB + dML
                dseg = dML * CB
                dacs_all = dacs_all + _spread2(dseg, (lane == h).astype(BF16))
                colacc = colacc + jnp.where(row == h, jnp.sum(dseg, axis=0, keepdims=True), 0.0)
            dxdt = dxdt + dxbc_ref[:, c0:c0 + GW]
            ddt_all = ddt_all + _spread2(dxdt * xs, hsum_ref[g])
            dxbc_ref[:, c0:c0 + GW] = dxs + dxdt * dtx
            dCB_b = dCB.astype(BF16)
            dxbc_ref[:, DI + g * DS:DI + (g + 1) * DS] = dB + _dot(dCB_b, Cm, _TN)
            dxbc_ref[:, DI + NG * DS + g * DS:DI + NG * DS + (g + 1) * DS] = dC + _dot(dCB_b, Bm)
        ddt_ref[...] = ddt_all
        dda_ref[...] = _scan(dacs_all - colacc.T, 0, reverse=True)

    last = NCH - 1
    return _pcall(
        body, name=name, grid=(NCH,),
        in_specs=[pl.BlockSpec((CH, DI), lambda i: (last - i, 0)), pl.BlockSpec((CH, CONVD), lambda i: (last - i, 0)),
                  pl.BlockSpec((CH, DI), lambda i: (last - i, 0)), pl.BlockSpec((CH, DI), lambda i: (last - i, 0)),
                  pl.BlockSpec((NH, CH), lambda i: (0, last - i)), pl.BlockSpec((1, DI), lambda i: (0, 0)),
                  pl.BlockSpec((None, DS, DI), lambda i: (last - i, 0, 0)),
                  pl.BlockSpec((NG, GW, LANES), lambda i: (0, 0, 0))],
        out_specs=[pl.BlockSpec((CH, CONVD), lambda i: (last - i, 0)), pl.BlockSpec((CH, LANES), lambda i: (last - i, 0)),
                   pl.BlockSpec((CH, LANES), lambda i: (last - i, 0)), pl.BlockSpec((1, DI), lambda i: (0, 0))],
        out_shape=[jax.ShapeDtypeStruct((T, CONVD), F32), jax.ShapeDtypeStruct((T, LANES), F32),
                   jax.ShapeDtypeStruct((T, LANES), F32), jax.ShapeDtypeStruct((1, DI), F32)],
        scratch_shapes=[pltpu.VMEM((DS, DI), F32)],
        compiler_params=_cparams(("arbitrary",)),
    )(dy, xbc, dtx, dax, daT, dfull, states, hsum)


def _as3d(shape):
    if len(shape) == 1:
        return (1, 1, shape[0])
    if len(shape) == 2:
        return (1, shape[0], shape[1])
    return (math.prod(shape[:-2]), shape[-2], shape[-1])


def _row_tile(R, C):
    if R * C <= 512 * 1024:
        return R
    return next((t for t in (512, 256, 128, 64, 32, 16) if R % t == 0), R)


def adamw(parts, w, m, v, name):
    shape = w.shape
    L, R, C = _as3d(shape)
    P = parts.shape[0]
    tr = _row_tile(R, C)
    bc1 = 1.0 - ADAM_B1 ** ADAM_STEP
    bc2 = 1.0 - ADAM_B2 ** ADAM_STEP

    def body(p_ref, w_ref, m_ref, v_ref, g_out, d_out, m_out, v_out):
        g = p_ref[0].astype(F32)
        for k in range(1, P):
            g = g + p_ref[k].astype(F32)
        mn = ADAM_B1 * m_ref[...] + (1.0 - ADAM_B1) * g
        vn = ADAM_B2 * v_ref[...] + (1.0 - ADAM_B2) * (g * g)
        g_out[...] = g
        m_out[...] = mn
        v_out[...] = vn
        d_out[...] = -ADAM_LR * ((mn / bc1) / (jnp.sqrt(vn / bc2) + ADAM_EPS) + ADAM_WD * w_ref[...])

    blk = pl.BlockSpec((None, tr, C), lambda l, r: (l, r, 0))
    outs = _pcall(
        body, name=name, grid=(L, R // tr),
        in_specs=[pl.BlockSpec((P, None, tr, C), lambda l, r: (0, l, r, 0)), blk, blk, blk],
        out_specs=[blk] * 4, out_shape=[jax.ShapeDtypeStruct((L, R, C), F32)] * 4,
        compiler_params=_cparams(("parallel", "parallel")),
    )(parts.reshape(P, L, R, C), w.reshape(L, R, C), m.reshape(L, R, C), v.reshape(L, R, C))
    return [o.reshape(shape) for o in outs]


def adamw_layer(recv, own, w, m, v, layer, prev, name, tr=None):
    R, tail = w.shape[1], tuple(w.shape[2:])
    zero = (0,) * len(tail)
    P = recv.shape[0]
    tr = tr or _row_tile(R, math.prod(tail))
    bc1 = 1.0 - ADAM_B1 ** ADAM_STEP
    bc2 = 1.0 - ADAM_B2 ** ADAM_STEP

    def body(r_ref, o_ref, w_ref, m_ref, v_ref, *rest):
        g_out, d_out, m_out, v_out = rest[-4:]
        g = o_ref[...].astype(F32)
        for k in range(P):
            g = g + r_ref[k].astype(F32)
        mn = ADAM_B1 * m_ref[...] + (1.0 - ADAM_B1) * g
        vn = ADAM_B2 * v_ref[...] + (1.0 - ADAM_B2) * (g * g)
        g_out[...] = g
        m_out[...] = mn
        v_out[...] = vn
        d_out[...] = -ADAM_LR * ((mn / bc1) / (jnp.sqrt(vn / bc2) + ADAM_EPS) + ADAM_WD * w_ref[...])

    slot = pl.BlockSpec((None, tr) + tail, lambda r: (layer, r) + zero)
    own_spec = pl.BlockSpec((None, tr) + tail, lambda r: (2 * lax.axis_index("x") + lax.axis_index("y"), r) + zero)
    in_specs = [pl.BlockSpec((P, tr) + tail, lambda r: (0, r) + zero), own_spec, slot, slot, slot]
    args = [recv, own, w, m, v]
    aliases = {}
    if prev is not None:
        in_specs += [ANY] * 4
        args += list(prev)
        aliases = {5 + k: k for k in range(4)}
    return _pcall(
        body, name=name, grid=(R // tr,), in_specs=in_specs, out_specs=[slot] * 4,
        out_shape=[jax.ShapeDtypeStruct(w.shape, F32)] * 4, input_output_aliases=aliases,
        compiler_params=_cparams(("parallel",)),
    )(*args)


def sum_leading(parts, name):
    P, R, C = parts.shape

    def body(p_ref, o_ref):
        s = p_ref[0]
        for k in range(1, P):
            s = s + p_ref[k]
        o_ref[...] = s

    return _pcall(body, name=name, out_shape=jax.ShapeDtypeStruct((R, C), F32))(parts)


def pair_sum(gsend, recv, name):
    S = gsend.shape[1:]
    L, R, C = _as3d(S)
    tr = _row_tile(R, C)

    def body(g_ref, r_ref, o_ref):
        o_ref[...] = (g_ref[...].astype(F32) + r_ref[...].astype(F32)).astype(o_ref.dtype)

    blk = pl.BlockSpec((None, None, tr, C), lambda q, l, r: (q, l, r, 0))
    own = pl.BlockSpec((None, None, tr, C), lambda q, l, r: (2 * q + lax.axis_index("c"), l, r, 0))
    out = _pcall(
        body, name=name, grid=(4, L, R // tr), in_specs=[own, blk], out_specs=blk,
        out_shape=jax.ShapeDtypeStruct((4, L, R, C), BF16),
        compiler_params=_cparams(("parallel", "parallel", "parallel")),
    )(gsend.reshape(8, L, R, C), recv.reshape(4, L, R, C))
    return out.reshape((4,) + S)


def _place():
    return lax.axis_index("x"), lax.axis_index("y"), lax.axis_index("c")


def _other_chips(x, y):
    return [(1 - x, y), (x, 1 - y), (1 - x, 1 - y)]


def all_gather(arrs, name):
    n = len(arrs)

    def body(*refs):
        ins, outs = refs[:n], refs[n:2 * n]
        send_sems, recv_sems, local_sems = refs[2 * n:]
        x, y, c = _place()
        me, sibling = (x, y, c), (x, y, 1 - c)
        chips = _other_chips(x, y)

        def slot(a, px, py, pc):
            return outs[a].at[4 * px + 2 * py + pc]

        def copy(a, k, block, to, src=None):
            return pltpu.make_async_remote_copy(
                src_ref=slot(a, *block) if src is None else src, dst_ref=slot(a, *block),
                send_sem=send_sems.at[a, k], recv_sem=recv_sems.at[a, k], device_id=to, device_id_type=MESH)

        mine, first, passed = [], [], []
        for a in range(n):
            cp = pltpu.make_async_copy(ins[a], slot(a, *me), local_sems.at[a])
            cp.start()
            mine.append(cp)
            first.append(copy(a, 0, me, sibling, src=ins[a]))
            first += [copy(a, 1 + j, me, (*chip, c), src=ins[a]) for j, chip in enumerate(chips)]
        for cp in first:
            cp.start()
        for j, chip in enumerate(chips):
            for a in range(n):
                copy(a, 1 + j, (*chip, c), me).wait_recv()
                cp = copy(a, 4 + j, (*chip, c), sibling)
                cp.start()
                passed.append(cp)
        for a in range(n):
            copy(a, 0, sibling, me).wait_recv()
            for j, chip in enumerate(chips):
                copy(a, 4 + j, (*chip, 1 - c), me).wait_recv()
        for cp in first + passed:
            cp.wait_send()
        for cp in mine:
            cp.wait()

    return _pcall(
        body, name=name, in_specs=[ANY] * n, out_specs=[ANY] * n,
        out_shape=[jax.ShapeDtypeStruct((N_DEV,) + a.shape, a.dtype) for a in arrs],
        scratch_shapes=[pltpu.SemaphoreType.DMA((n, 7)), pltpu.SemaphoreType.DMA((n, 7)), pltpu.SemaphoreType.DMA((n,))],
    )(*arrs)


def sibling_exchange(gsends, name, after=None):
    n = len(gsends)
    n_in = n + (1 if after is not None else 0)

    def body(*refs):
        ins, outs = refs[:n], refs[n_in:n_in + n]
        send_sems, recv_sems = refs[n_in + n:]
        x, y, c = _place()
        copies = []
        for a in range(n):
            for q in range(4):
                cp = pltpu.make_async_remote_copy(
                    src_ref=ins[a].at[2 * q + 1 - c], dst_ref=outs[a].at[q],
                    send_sem=send_sems.at[a, q], recv_sem=recv_sems.at[a, q],
                    device_id=(x, y, 1 - c), device_id_type=MESH)
                cp.start()
                copies.append(cp)
        for cp in copies:
            cp.wait()

    return _pcall(
        body, name=name, in_specs=[ANY] * n_in, out_specs=[ANY] * n,
        out_shape=[jax.ShapeDtypeStruct((4,) + g.shape[1:], g.dtype) for g in gsends],
        scratch_shapes=[pltpu.SemaphoreType.DMA((n, 4)), pltpu.SemaphoreType.DMA((n, 4))],
    )(*gsends, *([after] if after is not None else []))


HBM =pl.BlockSpec(memory_space=pltpu.HBM)
SEM = pl.BlockSpec(memory_space=pltpu.SEMAPHORE)
EFFECT = pltpu.SideEffectType.DATAFLOW_SIDE_EFFECTING


def _in_hbm(a):
    return pltpu.with_memory_space_constraint(a, pltpu.HBM)


def _gather_peers(x, y, c):
    to = [(x, y, 1 - c)] + [(px, py, c) for px, py in _other_chips(x, y)]
    return to, [4 * px + 2 * py + pc for px, py, pc in to]


def gather_start(arrs, after, name):
    n = len(arrs)
    n_in = 2 * n + (1 if after is not None else 0)

    def body(*refs):
        srcs, lands = refs[:n], refs[n:2 * n]
        send_sems, recv_sems = refs[n_in], refs[n_in + 1]
        token = refs[-1]
        x, y, c = _place()
        to, _ = _gather_peers(x, y, c)
        me = 4 * x + 2 * y + c
        for a in range(n):
            for k, dev in enumerate(to):
                pltpu.make_async_remote_copy(
                    src_ref=srcs[a], dst_ref=lands[a].at[me], send_sem=send_sems.at[4 * a + k], recv_sem=recv_sems.at[4 * a + k],
                    device_id=dev, device_id_type=MESH).start()
        token[...] = jnp.zeros_like(token)

    zones = [lax.empty((N_DEV,) + a.shape, a.dtype) for a in arrs]
    args = [_in_hbm(a) for a in arrs] + [_in_hbm(z) for z in zones] + ([after] if after is not None else [])
    outs = _pcall(
        body, name=name,
        out_shape=(pltpu.SemaphoreType.DMA((4 * n,)), pltpu.SemaphoreType.DMA((4 * n,)),
                   *[pltpu.HBM(a.shape, a.dtype) for a in arrs], *[pltpu.HBM(z.shape, z.dtype) for z in zones],
                   jax.ShapeDtypeStruct((8, LANES), F32)),
        in_specs=[HBM] * (2 * n) + ([ANY] if after is not None else []),
        out_specs=(SEM, SEM, *[HBM] * (2 * n), pl.BlockSpec(memory_space=pltpu.VMEM)),
        input_output_aliases={i: 2 + i for i in range(2 * n)},
        compiler_params=pltpu.CompilerParams(has_side_effects=EFFECT),
    )(*args)
    return dict(send=outs[0], recv=outs[1], srcs=list(outs[2:2 + n]), lands=list(outs[2 + n:2 + 2 * n]), token=outs[-1])


def gather_wait(st, after, name):
    n = len(st["srcs"])

    def body(*refs):
        srcs, lands = refs[:n], refs[n:2 * n]
        send_sems, recv_sems = refs[2 * n], refs[2 * n + 1]
        x, y, c = _place()
        to, slots = _gather_peers(x, y, c)
        for a in range(n):
            for k, dev in enumerate(to):
                cp = pltpu.make_async_remote_copy(
                    src_ref=srcs[a], dst_ref=lands[a].at[slots[k]], send_sem=send_sems.at[4 * a + k],
                    recv_sem=recv_sems.at[4 * a + k], device_id=dev, device_id_type=MESH)
                cp.wait_send()
                cp.wait_recv()

    outs = _pcall(
        body, name=name,
        out_shape=(*[pltpu.HBM(a.shape, a.dtype) for a in st["srcs"]], *[pltpu.HBM(z.shape, z.dtype) for z in st["lands"]]),
        in_specs=[HBM] * (2 * n) + [SEM, SEM, ANY], out_specs=tuple([HBM] * (2 * n)),
        input_output_aliases={i: i for i in range(2 * n)},
        compiler_params=pltpu.CompilerParams(has_side_effects=EFFECT),
    )(*st["srcs"], *st["lands"], st["send"], st["recv"], after)
    return list(outs[n:])


def pass_start(zones, name):
    n = len(zones)

    def body(*refs):
        zs = refs[:n]
        send_sems, recv_sems = refs[n], refs[n + 1]
        token = refs[-1]
        x, y, c = _place()
        for a in range(n):
            for j, (px, py) in enumerate(_other_chips(x, y)):
                blk = zs[a].at[4 * px + 2 * py + c]
                pltpu.make_async_remote_copy(
                    src_ref=blk, dst_ref=blk, send_sem=send_sems.at[3 * a + j], recv_sem=recv_sems.at[3 * a + j],
                    device_id=(x, y, 1 - c), device_id_type=MESH).start()
        token[...] = jnp.zeros_like(token)

    outs = _pcall(
        body, name=name,
        out_shape=(pltpu.SemaphoreType.DMA((3 * n,)), pltpu.SemaphoreType.DMA((3 * n,)),
                   *[pltpu.HBM(z.shape, z.dtype) for z in zones], jax.ShapeDtypeStruct((8, LANES), F32)),
        in_specs=[HBM] * n, out_specs=(SEM, SEM, *[HBM] * n, pl.BlockSpec(memory_space=pltpu.VMEM)),
        input_output_aliases={i: 2 + i for i in range(n)},
        compiler_params=pltpu.CompilerParams(has_side_effects=EFFECT),
    )(*zones)
    return dict(send=outs[0], recv=outs[1], zones=list(outs[2:2 + n]), token=outs[-1])


def pass_wait(st, after, name):
    n = len(st["zones"])

    def body(*refs):
        zs = refs[:n]
        send_sems, recv_sems = refs[n], refs[n + 1]
        x, y, c = _place()
        for a in range(n):
            for j, (px, py) in enumerate(_other_chips(x, y)):
                cp = pltpu.make_async_remote_copy(
                    src_ref=zs[a].at[4 * px + 2 * py + c], dst_ref=zs[a].at[4 * px + 2 * py + 1 - c],
                    send_sem=send_sems.at[3 * a + j], recv_sem=recv_sems.at[3 * a + j],
                    device_id=(x, y, 1 - c), device_id_type=MESH)
                cp.wait_send()
                cp.wait_recv()

    outs = _pcall(
        body, name=name, out_shape=tuple(pltpu.HBM(z.shape, z.dtype) for z in st["zones"]),
        in_specs=[HBM] * n + [SEM, SEM, ANY], out_specs=tuple([HBM] * n),
        input_output_aliases={i: i for i in range(n)},
        compiler_params=pltpu.CompilerParams(has_side_effects=EFFECT),
    )(*st["zones"], st["send"], st["recv"], after)
    return list(outs)


def scatter_start(parts, name):
    n = len(parts)

    def body(*refs):
        srcs, lands = refs[:n], refs[n:2 * n]
        send_sems, recv_sems = refs[2 * n], refs[2 * n + 1]
        token = refs[-1]
        x, y, c = _place()
        for a in range(n):
            for j, (px, py) in enumerate(_other_chips(x, y)):
                pltpu.make_async_remote_copy(
                    src_ref=srcs[a].at[2 * px + py], dst_ref=lands[a].at[j], send_sem=send_sems.at[3 * a + j],
                    recv_sem=recv_sems.at[3 * a + j], device_id=(px, py, c), device_id_type=MESH).start()
        token[...] = jnp.zeros_like(token)

    zones = [lax.empty((3,) + p.shape[1:], p.dtype) for p in parts]
    outs = _pcall(
        body, name=name,
        out_shape=(pltpu.SemaphoreType.DMA((3 * n,)), pltpu.SemaphoreType.DMA((3 * n,)),
                   *[pltpu.HBM(p.shape, p.dtype) for p in parts], *[pltpu.HBM(z.shape, z.dtype) for z in zones],
                   jax.ShapeDtypeStruct((8, LANES), F32)),
        in_specs=[HBM] * (2 * n), out_specs=(SEM, SEM, *[HBM] * (2 * n), pl.BlockSpec(memory_space=pltpu.VMEM)),
        input_output_aliases={i: 2 + i for i in range(2 * n)},
        compiler_params=pltpu.CompilerParams(has_side_effects=EFFECT),
    )(*[_in_hbm(p) for p in parts], *[_in_hbm(z) for z in zones])
    return dict(send=outs[0], recv=outs[1], srcs=list(outs[2:2 + n]), lands=list(outs[2 + n:2 + 2 * n]), token=outs[-1])


def scatter_wait(st, after, name):
    n = len(st["srcs"])

    def body(*refs):
        srcs, lands = refs[:n], refs[n:2 * n]
        send_sems, recv_sems = refs[2 * n], refs[2 * n + 1]
        x, y, c = _place()
        for a in range(n):
            for j, (px, py) in enumerate(_other_chips(x, y)):
                cp = pltpu.make_async_remote_copy(
                    src_ref=srcs[a].at[2 * px + py], dst_ref=lands[a].at[j], send_sem=send_sems.at[3 * a + j],
                    recv_sem=recv_sems.at[3 * a + j], device_id=(px, py, c), device_id_type=MESH)
                cp.wait_send()
                cp.wait_recv()

    outs = _pcall(
        body, name=name,
        out_shape=(*[pltpu.HBM(a.shape, a.dtype) for a in st["srcs"]], *[pltpu.HBM(z.shape, z.dtype) for z in st["lands"]]),
        in_specs=[HBM] * (2 * n) + [SEM, SEM, ANY], out_specs=tuple([HBM] * (2 * n)),
        input_output_aliases={i: i for i in range(2 * n)},
        compiler_params=pltpu.CompilerParams(has_side_effects=EFFECT),
    )(*st["srcs"], *st["lands"], st["send"], st["recv"], after)
    return list(outs[:n]), list(outs[n:])


def _unshard(g, axis):
    nd = g.ndim - 1
    axis = axis % nd
    t = jnp.moveaxis(g, 0, axis)
    shp = list(g.shape[1:])
    shp[axis] *= N_DEV
    return t.reshape(shp)


def _to_shards(full, axis):
    axis = axis % full.ndim
    shp = list(full.shape)
    shp[axis:axis + 1] = [N_DEV, shp[axis] // N_DEV]
    return jnp.moveaxis(full.reshape(shp), axis, 0)


def _pack(arrs, rows):
    flat = jnp.concatenate([a.reshape(-1).astype(F32) for a in arrs])
    return jnp.pad(flat, (0, rows * LANES - flat.shape[0])).reshape(rows, LANES)


def _unpack(buf, shapes):
    flat = buf.reshape(-1)
    out, off = [], 0
    for s in shapes:
        n = math.prod(s)
        out.append(flat[off:off + n].reshape(s))
        off += n
    return out


def _rows_for(shapes):
    n = sum(math.prod(s) for s in shapes)
    return -(-n // (8 * LANES)) * 8


def _row(v, width=None):
    v = v.reshape(1, -1).astype(F32)
    if width is not None and v.shape[1] < width:
        v = jnp.pad(v, ((0, 0), (0, width - v.shape[1])))
    return v


def _after(order, width):
    if not order:
        return None
    t = order[0][0:1, 0:1]
    for o in order[1:]:
        t = t + o[0:1, 0:1]
    return jnp.broadcast_to(t, (1, width))


def _norm_after(norm, order):
    row = _after(order, norm.shape[1])
    return norm if row is None else norm + row


FFN_TN = 256


def ffn_in(h, norm, w_gate, w_up, name):
    def body(h_ref, n_ref, wg_ref, wu_ref, u_ref, g_ref, up_ref, act_ref, u_s):
        @pl.when(pl.program_id(0) == 0)
        def _():
            x = h_ref[...]
            r = lax.rsqrt(jnp.mean(x * x, axis=-1, keepdims=True) + EPS)
            u_s[...] = (x * r * n_ref[...]).astype(BF16)
            u_ref[...] = u_s[...]

        u = u_s[...]
        g = _dot(u, wg_ref[...], _NT)
        up = _dot(u, wu_ref[...], _NT)
        g_ref[...] = g.astype(BF16)
        up_ref[...] = up.astype(BF16)
        act_ref[...] = (_silu(g) * up).astype(BF16)

    whole = pl.BlockSpec((T, D), lambda j: (0, 0))
    wspec = pl.BlockSpec((FFN_TN, D), lambda j: (j, 0))
    col = pl.BlockSpec((T, FFN_TN), lambda j: (0, j))
    return _pcall(
        body, name=name, grid=(DFF // FFN_TN,), in_specs=[whole, pl.BlockSpec((1, D), lambda j: (0, 0)), wspec, wspec],
        out_specs=[whole, col, col, col],
        out_shape=[jax.ShapeDtypeStruct((T, D), BF16)] + [jax.ShapeDtypeStruct((T, DFF), BF16)] * 3,
        scratch_shapes=[pltpu.VMEM((T, D), BF16)], compiler_params=_cparams(("arbitrary",)),
    )(h, norm, w_gate, w_up)


def ffn_back(dh_b, w_down, g, up, after_row, name):
    has_row = after_row is not None

    def body(*refs):
        dh_ref, wd_ref, g_ref, up_ref = refs[:4]
        dg_ref, dup_ref = refs[-2:]
        da = _dot(dh_ref[...], wd_ref[...], _NT)
        if has_row:
            da = da + refs[4][...]
        g = g_ref[...].astype(F32)
        dg_ref[...] = (da * up_ref[...].astype(F32) * _dsilu(g)).astype(BF16)
        dup_ref[...] = (da * _silu(g)).astype(BF16)

    col = pl.BlockSpec((T, FFN_TN), lambda j: (0, j))
    in_specs = [pl.BlockSpec((T, D), lambda j: (0, 0)), pl.BlockSpec((FFN_TN, D), lambda j: (j, 0)), col, col]
    args = [dh_b, w_down, g, up]
    if has_row:
        in_specs.append(pl.BlockSpec((1, FFN_TN), lambda j: (0, j)))
        args.append(after_row)
    return _pcall(
        body, name=name, grid=(DFF // FFN_TN,), in_specs=in_specs, out_specs=[col, col],
        out_shape=[jax.ShapeDtypeStruct((T, DFF), BF16)] * 2, compiler_params=_cparams(("parallel",)),
    )(*args)


def ffn_layer_fwd(h, p, tag, order=()):
    u, g, up, act = ffn_in(h, _norm_after(p["norm"], order), p["w_gate"], p["w_up"], f"{tag}_in")
    h2 = matmul(act, p["w_down"], "nn", residual=h, name=f"{tag}_down")
    return h2, (h, u, g, up, act)


def ffn_layer_bwd(dh, dh_b, saved, p, tag, order=()):
    h, u, g, up, act = saved
    d_down = matmul(act, dh_b, "tn", out_dtype=BF16, name=f"{tag}_dwd")
    dg, dup = ffn_back(dh_b, p["w_down"], g, up, _after(order, DFF), f"{tag}_back")
    du = matmul(dg, p["w_gate"], "nn", name=f"{tag}_dug")
    du = matmul(dup, p["w_up"], "nn", residual=du, name=f"{tag}_duu")
    d_gate = matmul(dg, u, "tn", out_dtype=BF16, name=f"{tag}_dwg")
    d_up = matmul(dup, u, "tn", out_dtype=BF16, name=f"{tag}_dwu")
    dh2, dh2_b, d_norm = rms_bwd(du, h, p["norm"], dh, f"{tag}_drms")
    return dh2, dh2_b, dict(norm=d_norm, w_gate=d_gate, w_up=d_up, w_down=d_down)


def conv_layer_fwd(h, p, tag, order=()):
    u = rms_fwd(h, _norm_after(p["norm"], order), f"{tag}_rms")
    hh = matmul(u, p["w_pw1"], "nn", bias=p["b_pw1"], name=f"{tag}_pw1")
    gl = glu_fwd(hh, f"{tag}_glu")
    c2 = dwconv_fwd(gl, 0, p["dw_w"], p["dw_b"], KCV, 128, False, f"{tag}_dw")[0]
    s = ln_silu_fwd(c2, p["ln_g"], p["ln_b"], f"{tag}_ln")
    h2 = matmul(s, p["w_pw2"], "nn", bias=p["b_pw2"], residual=h, name=f"{tag}_pw2")
    return h2, (h, u, hh, gl, c2, s)


def conv_layer_bwd(dh, dh_b, saved, p, tag, order=()):
    h, u, hh, gl, c2, s = saved
    ds = matmul(dh_b, p["w_pw2"], "nt", bias=_after(order, D), name=f"{tag}_ds")
    d_pw2 = matmul(s, dh_b, "tn", out_dtype=BF16, name=f"{tag}_dwpw2")
    dc2, d_lng, d_lnb, d_bpw2 = ln_silu_bwd(ds, c2, dh, p["ln_g"], p["ln_b"], f"{tag}_dln")
    dgl, d_dww, d_dwb = dwconv_bwd(dc2, None, gl, 0, p["dw_w"], KCV, 128, False, F32, f"{tag}_ddw")
    dhh, d_bpw1 = glu_bwd(dgl, hh, f"{tag}_dglu")
    du = matmul(dhh, p["w_pw1"], "nt", name=f"{tag}_du")
    d_pw1 = matmul(u, dhh, "tn", out_dtype=BF16, name=f"{tag}_dwpw1")
    dh2, dh2_b, d_norm = rms_bwd(du, h, p["norm"], dh, f"{tag}_drms")
    grads = dict(norm=d_norm, w_pw1=d_pw1, b_pw1=d_bpw1, dw_w=d_dww[:KCV], dw_b=d_dwb, ln_g=d_lng, ln_b=d_lnb,
                 w_pw2=d_pw2, b_pw2=d_bpw2)
    return dh2, dh2_b, grads


def ssm_layer_fwd(h, p, tag, order=(), mid=None):
    u = rms_fwd(h, _norm_after(p["norm"], order), f"{tag}_rms")
    zx = matmul(u, p["w_in"], "nt", name=f"{tag}_in")
    cpre, xbc = dwconv_fwd(zx, DI // 512, p["conv_w"], p["conv_b"], KSSM, 512, True, f"{tag}_conv")
    dt, da, dtx, dax = dt_fwd(zx, p["dt_bias"], p["a_log"], f"{tag}_dt")
    daT = da[:, :NH].T
    y, states = ssd_fwd(xbc, dtx, dax, daT, p["d_full"], f"{tag}_ssd")
    gate_norm = p["gate_norm"] if mid is None else _norm_after(p["gate_norm"], mid(y))
    yn = gatenorm_fwd(y, zx, gate_norm, f"{tag}_gn")
    h2 = matmul(yn, p["w_out"], "nn", residual=h, name=f"{tag}_out")
    return h2, (h, u, zx, cpre, xbc, dt, dtx, dax, daT, y, states, yn)


def ssm_layer_bwd(dh, dh_b, saved, p, tag, order=()):
    h, u, zx, cpre, xbc, dt, dtx, dax, daT, y, states, yn = saved
    dyn = matmul(dh_b, p["w_out"], "nt", bias=_after(order, DI), name=f"{tag}_dyn")
    d_wout = matmul(yn, dh_b, "tn", out_dtype=BF16, name=f"{tag}_dwout")
    dy, dz, d_gn = gatenorm_bwd(dyn, y, zx, p["gate_norm"], f"{tag}_dgn")
    dxbc, ddt, dda, dD = ssd_bwd(dy, xbc, dtx, dax, daT, p["d_full"], states, f"{tag}_dssd")
    draw, d_dtb, d_alog = dt_bwd(ddt, dda, dt, zx, p["dt_bias"], p["a_log"], f"{tag}_ddt")
    dxpre, d_cw, d_cb = dwconv_bwd(dxbc, cpre, zx, DI // 512, p["conv_w"], KSSM, 512, True, BF16, f"{tag}_dconv")
    dzx = jnp.concatenate([dz, dxpre, draw, jnp.zeros((T, DINP_PAD - 2 * DI - 2 * NG * DS - LANES), BF16)], axis=1)
    du = matmul(dzx, p["w_in"], "nn", name=f"{tag}_du")
    d_win = matmul(dzx, u, "tn", out_dtype=BF16, name=f"{tag}_dwin")
    dh2, dh2_b, d_norm = rms_bwd(du, h, p["norm"], dh, f"{tag}_drms")
    d_d = headsum(dD.reshape(NH, HD), f"{tag}_dD").reshape(NH)
    grads = dict(norm=d_norm, w_in=d_win[:DINP], conv_w=d_cw[:KSSM], conv_b=d_cb, dt_bias=d_dtb[0, :NH],
                 a_log=d_alog[0, :NH], d=d_d, gate_norm=d_gn, w_out=d_wout)
    return dh2, dh2_b, grads


BIG = ["ssm_w_in", "ssm_w_out", "cv_w_pw1", "cv_w_pw2", "ffn_w_gate", "ffn_w_up", "ffn_w_down"]
TRANSPOSED = ("ffn_w_gate", "ffn_w_up", "ssm_w_in")
LAYER_AXIS = {"ssm_w_in": 0, "ssm_w_out": 0, "cv_w_pw1": -1, "cv_w_pw2": 0, "ffn_w_gate": 0, "ffn_w_up": 0,
              "ffn_w_down": 0}
W_IN_ROWS = 28
SMALL_SHARDED = ["ssm_conv_w", "cv_norm", "cv_b_pw1", "cv_dw_w", "cv_dw_b", "cv_ln_g", "cv_ln_b", "cv_b_pw2"]
SMALL_REPL = ["ssm_norm", "ssm_conv_b", "ssm_dt_bias", "ssm_a_log", "ssm_d", "ssm_gate_norm", "ffn_norm", "final_norm"]
WEIGHTS = ["ssm_norm", "ssm_w_in", "ssm_conv_w", "ssm_conv_b", "ssm_dt_bias", "ssm_a_log", "ssm_d", "ssm_gate_norm",
           "ssm_w_out", "cv_norm", "cv_w_pw1", "cv_b_pw1", "cv_dw_w", "cv_dw_b", "cv_ln_g", "cv_ln_b", "cv_w_pw2",
           "cv_b_pw2", "ffn_norm", "ffn_w_gate", "ffn_w_up", "ffn_w_down", "final_norm"]
SMALL = [n for n in WEIGHTS if n not in BIG]


N_STAGES = 8


def _stage_layer(s):
    i = s // 2
    if s % 2:
        return "ffn", i
    return ("ssm" if i % 2 == 0 else "cv"), i // 2


def _stage_group(s):
    fam, l = _stage_layer(s)
    names = {"ffn": ["ffn_w_gate", "ffn_w_up", "ffn_w_down"], "ssm": ["ssm_w_in", "ssm_w_out"],
             "cv": ["cv_w_pw1", "cv_w_pw2"]}[fam]
    return [(n, l) for n in names]


def _stage_params(s, big, small):
    fam, l = _stage_layer(s)
    if fam == "ffn":
        return dict(norm=_row(small["ffn_norm"][l]), w_gate=big["ffn_w_gate"], w_up=big["ffn_w_up"],
                    w_down=big["ffn_w_down"])
    if fam == "ssm":
        return dict(norm=_row(small["ssm_norm"][l]), w_in=jnp.pad(big["ssm_w_in"], ((0, DINP_PAD - DINP), (0, 0))),
                    conv_w=jnp.pad(small["ssm_conv_w"][l], ((0, 8 - KSSM), (0, 0))), conv_b=_row(small["ssm_conv_b"][l]),
                    dt_bias=_row(small["ssm_dt_bias"][l], LANES), a_log=_row(small["ssm_a_log"][l], LANES),
                    d_full=_row(jnp.repeat(small["ssm_d"][l], HD)), gate_norm=_row(small["ssm_gate_norm"][l]),
                    w_out=big["ssm_w_out"])
    return dict(norm=_row(small["cv_norm"][l]), w_pw1=big["cv_w_pw1"], b_pw1=_row(small["cv_b_pw1"][l]),
                dw_w=jnp.pad(small["cv_dw_w"][l], ((0, 32 - KCV), (0, 0))), dw_b=_row(small["cv_dw_b"][l]),
                ln_g=_row(small["cv_ln_g"][l]), ln_b=_row(small["cv_ln_b"][l]), w_pw2=big["cv_w_pw2"],
                b_pw2=_row(small["cv_b_pw2"][l]))


_STAGE_FWD = {"ffn": ffn_layer_fwd, "ssm": ssm_layer_fwd, "cv": conv_layer_fwd}
_STAGE_BWD = {"ffn": ffn_layer_bwd, "ssm": ssm_layer_bwd, "cv": conv_layer_bwd}


def _stage_fwd(s, h, p, order=(), mid=None):
    fam, l = _stage_layer(s)
    if mid is not None:
        return ssm_layer_fwd(h, p, f"{fam}{l}", order, mid)
    return _STAGE_FWD[fam](h, p, f"{fam}{l}", order)


def _stage_bwd(s, dh, dh_b, p, saved, order=()):
    fam, l = _stage_layer(s)
    dh, dh_b, g = _STAGE_BWD[fam](dh, dh_b, saved, p, f"{fam}{l}", order)
    return dh, dh_b, {f"{fam}_{k}": val for k, val in g.items()}


def _local(x, tgt, full):
    h, tape = x, []
    for s in range(N_STAGES):
        big = {n: (full[n][l].T if n in TRANSPOSED else full[n][l]) for n, l in _stage_group(s)}
        p = _stage_params(s, big, full)
        h, saved = _stage_fwd(s, h, p)
        tape.append((p, saved))
    dh, dh_b, d_final, loss_row = loss_head(h, _row(full["final_norm"]), tgt, "loss_head")
    gl = {n: [None] * full[n].shape[0] for n in WEIGHTS if n != "final_norm"}
    for s in reversed(range(N_STAGES)):
        dh, dh_b, g = _stage_bwd(s, dh, dh_b, *tape[s])
        for n, val in g.items():
            val = val.T if n in TRANSPOSED else val
            gl[n][_stage_layer(s)[1]] = val.reshape(full[n].shape[1:])
    grads = {n: jnp.stack(vs) for n, vs in gl.items()}
    grads["final_norm"] = d_final.reshape(D)
    return loss_row, dh, grads


def _step(x, tgt, w, m, v):
    idx = 4 * lax.axis_index("x") + 2 * lax.axis_index("y") + lax.axis_index("c")
    small_shapes = [w[n].shape for n in SMALL_SHARDED]
    small_pack = _pack([w[n] for n in SMALL_SHARDED], _rows_for(small_shapes))

    def view(n, a):
        if n == "ssm_w_in":
            return jnp.transpose(a, (2, 0, 1))
        return jnp.swapaxes(a, 1, 2) if n in TRANSPOSED else a

    def unview(n, a):
        if n == "ssm_w_in":
            return jnp.transpose(a, (1, 2, 0))
        return jnp.swapaxes(a, 1, 2) if n in TRANSPOSED else a

    wv, mv, vv = ({n: view(n, t[n]) for n in BIG} for t in (w, m, v))

    def blocks(s):
        own = [(wv[n][:, l] if n == "ssm_w_in" else wv[n][l]).astype(BF16) for n, l in _stage_group(s)]
        return own + ([small_pack] if s == 0 else [])

    arrs = [blocks(s) for s in range(N_STAGES)]
    first = gather_start(arrs[0], None, "gather0_start")
    passing = pass_start(gather_wait(first, first["token"], "gather0_wait"), "pass0_start")
    crossing = gather_start(arrs[1], passing["token"], "gather1_start")
    small = {n: w[n] for n in SMALL_REPL}
    flight = dict(passing=passing, crossing=crossing)

    def advance(s, after):
        tokens = []
        if s + 1 < N_STAGES:
            landed = gather_wait(flight["crossing"], after, f"gather{s + 1}_wait")
            flight["passing"] = pass_start(landed, f"pass{s + 1}_start")
            tokens.append(flight["passing"]["token"])
        if s + 2 < N_STAGES:
            flight["crossing"] = gather_start(arrs[s + 2], flight["passing"]["token"], f"gather{s + 2}_start")
            tokens.append(flight["crossing"]["token"])
        return tokens

    h, tape, after = x, [], crossing["token"]
    for s in range(N_STAGES):
        zones = pass_wait(flight["passing"], after, f"pass{s}_wait")
        order = advance(s, zones[0]) if s else []
        mid = functools.partial(advance, 0) if s == 0 else None
        zones = [lax.dynamic_update_slice_in_dim(z, a[None], idx, 0) for z, a in zip(zones, arrs[s])]
        if s == 0:
            per_dev = [_unpack(zones[-1][k], small_shapes) for k in range(N_DEV)]
            for q, n in enumerate(SMALL_SHARDED):
                small[n] = _unshard(jnp.stack([per_dev[k][q] for k in range(N_DEV)]), -1)
        big = {n: _unshard(z, LAYER_AXIS[n]) for (n, _), z in zip(_stage_group(s), zones)}
        p = _stage_params(s, big, small)
        h, saved = _stage_fwd(s, h, p, order, mid)
        tape.append((p, saved))
        after = h

    dh, dh_b, d_final, loss_row = loss_head(h, _row(w["final_norm"]), tgt, "loss_head")

    out = {}
    small_g = {n: [None] * w[n].shape[0] for n in SMALL if n != "final_norm"}

    w_in_parts = {}

    def finish(s, st, after):
        by_chip, recv = scatter_wait(st, after, f"scatter{s}_wait")
        for (n, l), own, r in zip(_stage_group(s), by_chip, recv):
            if n != "ssm_w_in":
                out[n] = adamw_layer(r, own, wv[n], mv[n], vv[n], l, out.get(n), f"adamw_{n}{l}")
                continue
            w_in_parts[l] = (r, own)
            if len(w_in_parts) == 2:
                r2, own2 = (jnp.stack([w_in_parts[0][k], w_in_parts[1][k]], axis=2) for k in (0, 1))
                res = adamw_layer(r2, own2, wv[n][None], mv[n][None], vv[n][None], 0, None, f"adamw_{n}", W_IN_ROWS)
                out[n] = [a[0] for a in res]

    started, order, summed = [], [], None
    for s in reversed(range(N_STAGES)):
        dh, dh_b, g = _stage_bwd(s, dh, dh_b, *tape[s], order)
        for n, val in g.items():
            if n not in BIG:
                small_g[n][_stage_layer(s)[1]] = val.reshape(small[n].shape[1:])
        if s == 0:
            grads = {n: jnp.stack(vs) for n, vs in small_g.items()}
            grads["final_norm"] = d_final.reshape(D)
            small_full_shapes = [grads[n].shape for n in SMALL] + [(1,)]
            packed = _pack([grads[n] for n in SMALL] + [loss_row[0, :1]], _rows_for(small_full_shapes))
            summed = sum_leading(all_gather([packed], "gather_small_grads")[0], "sum_small_grads")
        gsend = [_to_shards(g[n], LAYER_AXIS[n]) for n, _ in _stage_group(s)]
        from_sibling = sibling_exchange(gsend, f"scatter{s}_sibling", summed)
        by_chip = [pair_sum(a, r, f"pair_sum_{n}{l}") for (n, l), a, r in zip(_stage_group(s), gsend, from_sibling)]
        started.append((s, scatter_start(by_chip, f"scatter{s}_start")))
        order = [started[-1][1]["token"]]
    last = started[-1][1]["token"]
    for s, st in started[:-1]:
        finish(s, st, last)
    parts = _unpack(summed + last[0:1, 0:1], small_full_shapes)
    loss = parts[-1][0]
    for n, g in zip(SMALL, parts[:-1]):
        if n in SMALL_SHARDED:
            s = w[n].shape[-1]
            g = lax.dynamic_slice_in_dim(g, idx * s, s, axis=g.ndim - 1)
        out[n] = adamw(g[None], w[n], m[n], v[n], f"adamw_{n}")
    done = [out[n][0].reshape(-1)[:1] for n in out]
    finish(*started[-1], functools.reduce(jnp.add, done))
    for n in TRANSPOSED:
        out[n] = [unview(n, a) for a in out[n]]
    return loss, dh, out


def kernel(x, ssm_norm, ssm_w_in, ssm_conv_w, ssm_conv_b, ssm_dt_bias, ssm_a_log, ssm_d, ssm_gate_norm, ssm_w_out, cv_norm, cv_w_pw1, cv_b_pw1, cv_dw_w, cv_dw_b, cv_ln_g, cv_ln_b, cv_w_pw2, cv_b_pw2, ffn_norm, ffn_w_gate, ffn_w_up, ffn_w_down, final_norm, loss_target, m_ssm_norm, m_ssm_w_in, m_ssm_conv_w, m_ssm_conv_b, m_ssm_dt_bias, m_ssm_a_log, m_ssm_d, m_ssm_gate_norm, m_ssm_w_out, m_cv_norm, m_cv_w_pw1, m_cv_b_pw1, m_cv_dw_w, m_cv_dw_b, m_cv_ln_g, m_cv_ln_b, m_cv_w_pw2, m_cv_b_pw2, m_ffn_norm, m_ffn_w_gate, m_ffn_w_up, m_ffn_w_down, m_final_norm, v_ssm_norm, v_ssm_w_in, v_ssm_conv_w, v_ssm_conv_b, v_ssm_dt_bias, v_ssm_a_log, v_ssm_d, v_ssm_gate_norm, v_ssm_w_out, v_cv_norm, v_cv_w_pw1, v_cv_b_pw1, v_cv_dw_w, v_cv_dw_b, v_cv_ln_g, v_cv_ln_b, v_cv_w_pw2, v_cv_b_pw2, v_ffn_norm, v_ffn_w_gate, v_ffn_w_up, v_ffn_w_down, v_final_norm):
    args = locals()
    w = {n: args[n] for n in WEIGHTS}
    m = {n: args["m_" + n] for n in WEIGHTS}
    v = {n: args["v_" + n] for n in WEIGHTS}
    loss, grad_x, out = _step(x[0], loss_target[0], w, m, v)
    res = [loss, grad_x[None]]
    for k in range(4):
        res += [out[n][k] for n in WEIGHTS]
    return tuple(res)
```

```python
import functools
import math

import jax
import jax.numpy as jnp
from jax import lax
from jax.experimental import pallas as pl
from jax.experimental.pallas import tpu as pltpu

F32 = jnp.float32
BF16 = jnp.bfloat16

N_DEV = 8
T = 2048
D = 1024
DI = 2048
NH = 32
HD = 64
NG = 4
GW = DI // NG
DS = 128
CONVD = DI + 2 * NG * DS
DINP = 2 * DI + 2 * NG * DS + NH
DINP_PAD = 5376
CH = 128
NCH = T // CH
DFF = 2816
KSSM = 4
KCV = 31
EPS = 1e-5
LANES = 128
VMEM_LIMIT = 56 * 1024 * 1024

ADAM_LR = 0.001
ADAM_B1 = 0.9
ADAM_B2 = 0.999
ADAM_EPS = 1e-08
ADAM_WD = 0.01
ADAM_STEP = 10

MESH = pl.DeviceIdType.MESH
ANY = pl.BlockSpec(memory_space=pl.ANY)


def _pcall(body, **kw):
    return pl.pallas_call(body, **kw)


def _cparams(sem):
    return pltpu.CompilerParams(dimension_semantics=sem, vmem_limit_bytes=VMEM_LIMIT)


def _pick(n, cands):
    for c in cands:
        if n % c == 0:
            return c
    raise ValueError(f"no tile for {n}")


def _sigmoid(x):
    return 1.0 / (1.0 + jnp.exp(-x))


def _silu(x):
    return x * _sigmoid(x)


def _dsilu(x):
    s = _sigmoid(x)
    return s * (1.0 + x * (1.0 - s))


_DIMS = {"nn": (((1,), (0,)), ((), ())), "nt": (((1,), (1,)), ((), ())), "tn": (((0,), (0,)), ((), ()))}


MM_VMEM_BUDGET = 40 * 1024 * 1024
MM_MAX_K = 3072


def _mm_tiles(M, N, K, out_bytes, has_res):
    tk = K if K <= MM_MAX_K else K // 2
    assert K % tk == 0 and tk % LANES == 0
    nk = K // tk
    best = None
    for tm in (2048, 1792, 1408, 1024, 768, 512, 256, 128):
        if M % tm:
            continue
        for tn in (1408, 1024, 768, 512, 384, 256, 128):
            if N % tn:
                continue
            blocks = tm * tk * 2 + tk * tn * 2 + tm * tn * out_bytes + (tm * tn * 4 if has_res else 0)
            vmem = 2 * blocks + tm * tn * 4 * (2 if nk > 1 else 1)
            if vmem > MM_VMEM_BUDGET:
                continue
            traffic = (N // tn if nk > 1 else 1) * M * K + (M // tm) * N * K
            key = (-traffic, tm * tn)
            if best is None or key > best[0]:
                best = (key, tm, tn)
    assert best is not None, (M, N, K)
    return best[1], best[2], tk


def matmul(a, b, mode, *, name, bias=None, residual=None, out_dtype=F32):
    assert a.dtype == BF16 and b.dtype == BF16
    if mode == "nn":
        (M, K), (K2, N) = a.shape, b.shape
    elif mode == "nt":
        (M, K), (N, K2) = a.shape, b.shape
    else:
        (K, M), (K2, N) = a.shape, b.shape
    assert K == K2
    has_bias, has_res = bias is not None, residual is not None
    tm, tn, tk = _mm_tiles(M, N, K, jnp.dtype(out_dtype).itemsize, has_res)
    nk = K // tk
    dims = _DIMS[mode]

    def body(*refs):
        a_ref, b_ref = refs[0], refs[1]
        pos = 2
        bias_ref = res_ref = None
        if has_bias:
            bias_ref = refs[pos]
            pos += 1
        if has_res:
            res_ref = refs[pos]
            pos += 1
        o_ref = refs[pos]

        def finish(out):
            if has_bias:
                out = out + bias_ref[...]
            if has_res:
                out = out + res_ref[...]
            o_ref[...] = out.astype(o_ref.dtype)

        part = lax.dot_general(a_ref[...], b_ref[...], dims, preferred_element_type=F32)
        if nk == 1:
            finish(part)
            return
        acc = refs[pos + 1]
        k = pl.program_id(2)

        @pl.when(k == 0)
        def _():
            acc[...] = part

        @pl.when(jnp.logical_and(k > 0, k < nk - 1))
        def _():
            acc[...] += part

        @pl.when(k == nk - 1)
        def _():
            finish(acc[...] + part)

    if mode == "tn":
        a_spec = pl.BlockSpec((tk, tm), lambda i, j, k: (k, i))
    else:
        a_spec = pl.BlockSpec((tm, tk), lambda i, j, k: (i, k))
    if mode == "nt":
        b_spec = pl.BlockSpec((tn, tk), lambda i, j, k: (j, k))
    else:
        b_spec = pl.BlockSpec((tk, tn), lambda i, j, k: (k, j))
    in_specs, args = [a_spec, b_spec], [a, b]
    if has_bias:
        in_specs.append(pl.BlockSpec((1, tn), lambda i, j, k: (0, j)))
        args.append(bias.reshape(1, N).astype(F32))
    if has_res:
        in_specs.append(pl.BlockSpec((tm, tn), lambda i, j, k: (i, j)))
        args.append(residual)
    return _pcall(
        body, name=name, grid=(M // tm, N // tn, nk), in_specs=in_specs,
        out_specs=pl.BlockSpec((tm, tn), lambda i, j, k: (i, j)),
        out_shape=jax.ShapeDtypeStruct((M, N), out_dtype),
        scratch_shapes=[pltpu.VMEM((tm, tn), F32)] if nk > 1 else [],
        compiler_params=_cparams(("parallel", "parallel", "arbitrary")),
    )(*args)


def rowwise(fn, rows, bcasts, outs, accs=(), *, name, tm=256):
    n_rows, n_b, n_o, n_a = len(rows), len(bcasts), len(outs), len(accs)
    nt = T // tm

    def body(*refs):
        ins = [r[...] for r in refs[:n_rows + n_b]]
        res = fn(*ins)
        o_refs = refs[n_rows + n_b:n_rows + n_b + n_o]
        a_refs = refs[n_rows + n_b + n_o:]
        for r, v in zip(o_refs, res[:n_o]):
            r[...] = v.astype(r.dtype)
        if n_a:
            i = pl.program_id(0)

            @pl.when(i == 0)
            def _():
                for r in a_refs:
                    r[...] = jnp.zeros_like(r)

            for r, v in zip(a_refs, res[n_o:]):
                r[...] += v

    in_specs = [pl.BlockSpec((tm, w), functools.partial(lambda i, cb: (i, cb), cb=cb)) for (_, w, cb) in rows]
    in_specs += [pl.BlockSpec(b.shape, lambda i: (0, 0)) for b in bcasts]
    out_specs = [pl.BlockSpec((tm, w), lambda i: (i, 0)) for (w, _) in outs]
    out_specs += [pl.BlockSpec((1, w), lambda i: (0, 0)) for w in accs]
    out_shape = [jax.ShapeDtypeStruct((T, w), dt) for (w, dt) in outs]
    out_shape += [jax.ShapeDtypeStruct((1, w), F32) for w in accs]
    return _pcall(
        body, name=name, grid=(nt,), in_specs=in_specs, out_specs=out_specs, out_shape=out_shape,
        compiler_params=_cparams(("arbitrary",)),
    )(*[r[0] for r in rows], *bcasts)


def _full(a):
    return (a, a.shape[1], 0)


def _rsum(v):
    return jnp.sum(v, axis=0, keepdims=True)


def rms_fwd(h, g, name):
    def fn(x, g):
        r = lax.rsqrt(jnp.mean(x * x, axis=-1, keepdims=True) + EPS)
        return (x * r * g,)
    return rowwise(fn, [_full(h)], [g], [(D, BF16)], name=name)[0]


def rms_bwd(du, h, g, dres, name):
    def fn(du, x, dres, g):
        r = lax.rsqrt(jnp.mean(x * x, axis=-1, keepdims=True) + EPS)
        xh = x * r
        dxh = du * g
        dx = r * (dxh - xh * jnp.mean(dxh * xh, axis=-1, keepdims=True))
        dh = dres + dx
        return dh, dh, _rsum(du * xh)
    return rowwise(fn, [_full(du), _full(h), _full(dres)], [g], [(D, F32), (D, BF16)], [D], name=name)


def loss_head(h, g, tgt, name):
    def fn(x, tgt, g):
        r = lax.rsqrt(jnp.mean(x * x, axis=-1, keepdims=True) + EPS)
        xh = x * r
        err = xh * g - tgt
        lsum = jnp.sum(jnp.sum(err * err, axis=-1, keepdims=True), axis=0, keepdims=True) * (0.5 / D)
        dy = err * (1.0 / D)
        dxh = dy * g
        dx = r * (dxh - xh * jnp.mean(dxh * xh, axis=-1, keepdims=True))
        return dx, dx, _rsum(dy * xh), jnp.broadcast_to(lsum, (1, LANES))
    return rowwise(fn, [_full(h), _full(tgt)], [g], [(D, F32), (D, BF16)], [D, LANES], name=name)


def glu_fwd(hh, name):
    def fn(a, g):
        return (a * _sigmoid(g),)
    return rowwise(fn, [(hh, D, 0), (hh, D, 1)], [], [(D, F32)], name=name)[0]


def glu_bwd(dgl, hh, name):
    def fn(dgl, a, g):
        s = _sigmoid(g)
        dhh = jnp.concatenate([dgl * s, dgl * a * s * (1.0 - s)], axis=1)
        return dhh, _rsum(dhh)
    return rowwise(fn, [_full(dgl), (hh, D, 0), (hh, D, 1)], [], [(2 * D, BF16)], [2 * D], name=name)


def ln_silu_fwd(c2, g, b, name):
    def fn(x, g, b):
        mu = jnp.mean(x, axis=-1, keepdims=True)
        xc = x - mu
        r = lax.rsqrt(jnp.mean(xc * xc, axis=-1, keepdims=True) + EPS)
        return (_silu(xc * r * g + b),)
    return rowwise(fn, [_full(c2)], [g, b], [(D, BF16)], name=name)[0]


def ln_silu_bwd(ds, c2, dh, g, b, name):
    def fn(ds, x, dh, g, b):
        mu = jnp.mean(x, axis=-1, keepdims=True)
        xc = x - mu
        r = lax.rsqrt(jnp.mean(xc * xc, axis=-1, keepdims=True) + EPS)
        xh = xc * r
        dn = ds * _dsilu(xh * g + b)
        dxh = dn * g
        dx = r * (dxh - jnp.mean(dxh, axis=-1, keepdims=True) - xh * jnp.mean(dxh * xh, axis=-1, keepdims=True))
        return dx, _rsum(dn * xh), _rsum(dn), _rsum(dh)
    return rowwise(fn, [_full(ds), _full(c2), _full(dh)], [g, b], [(D, F32)], [D, D, D], name=name)


def gatenorm_fwd(y, zx, gn, name):
    def fn(y, z, gn):
        hg = y * _silu(z)
        parts = []
        for k in range(NG):
            hk = hg[:, k * GW:(k + 1) * GW]
            parts.append(hk * lax.rsqrt(jnp.mean(hk * hk, axis=-1, keepdims=True) + EPS))
        return (jnp.concatenate(parts, axis=1) * gn,)
    return rowwise(fn, [_full(y), (zx, DI, 0)], [gn], [(DI, BF16)], name=name)[0]


def gatenorm_bwd(dyn, y, zx, gn, name):
    def fn(dyn, y, z, gn):
        sz = _silu(z)
        hg = y * sz
        dxh = dyn * gn
        dhg, xhs = [], []
        for k in range(NG):
            sl = slice(k * GW, (k + 1) * GW)
            hk = hg[:, sl]
            r = lax.rsqrt(jnp.mean(hk * hk, axis=-1, keepdims=True) + EPS)
            xh = hk * r
            dk = dxh[:, sl]
            dhg.append(r * (dk - xh * jnp.mean(dk * xh, axis=-1, keepdims=True)))
            xhs.append(xh)
        dhg = jnp.concatenate(dhg, axis=1)
        xh = jnp.concatenate(xhs, axis=1)
        return dhg * sz, dhg * y * _dsilu(z), _rsum(dyn * xh)
    return rowwise(fn, [_full(dyn), _full(y), (zx, DI, 0)], [gn], [(DI, F32), (DI, BF16)], [DI], name=name)


def _softplus(x):
    return jnp.maximum(x, 0.0) + jnp.log(1.0 + jnp.exp(-jnp.abs(x)))


def _spread(v, e):
    hi = v.astype(BF16)
    r = v - hi.astype(F32)
    mid = r.astype(BF16)
    lo = (r - mid.astype(F32)).astype(BF16)
    return _dot(hi, e) + _dot(mid, e) + _dot(lo, e)


def _spread2(v, e):
    hi = v.astype(BF16)
    lo = (v - hi.astype(F32)).astype(BF16)
    return _dot(hi, e) + _dot(lo, e)


def dt_fwd(zx, dt_bias, a_log, name):
    heads = (jnp.arange(DI)[None, :] // HD == jnp.arange(LANES)[:, None]).astype(BF16)

    def fn(raw, bias, a_log, e):
        dt = _softplus(raw + bias)
        da = dt * (-jnp.exp(a_log))
        return dt, da, _spread(dt, e), _spread(da, e)

    return rowwise(fn, [(zx, LANES, (2 * DI + 2 * NG * DS) // LANES)], [dt_bias, a_log, heads],
                   [(LANES, F32), (LANES, F32), (DI, F32), (DI, F32)], name=name)


def dt_bwd(ddt, dda, dt, zx, dt_bias, a_log, name):
    def fn(ddt, dda, dt, raw, bias, a_log):
        a = -jnp.exp(a_log)
        draw = (ddt + dda * a) * _sigmoid(raw + bias)
        return draw, _rsum(draw), _rsum(dda * dt) * a
    return rowwise(fn, [_full(ddt), _full(dda), _full(dt), (zx, LANES, (2 * DI + 2 * NG * DS) // LANES)],
                   [dt_bias, a_log], [(LANES, BF16)], [LANES, LANES], name=name)


def headsum(v, name):
    def body(v_ref, o_ref):
        o_ref[...] = jnp.sum(v_ref[...], axis=1, keepdims=True)
    return _pcall(body, name=name, out_shape=jax.ShapeDtypeStruct((v.shape[0], 1), F32))(v)


CONV_ROWS = 256


def _shifted(win, o, rows):
    if o == 0:
        return win[0:rows]
    n = win.shape[0]
    return pltpu.roll(win, shift=n - o, axis=0)[0:rows]


def dwconv_fwd(x, x_cb0, w, b, K, ct, act, name):
    C = w.shape[1]
    pad = 8 if K <= 8 else 32
    KP = w.shape[0]
    n_out = 2 if act else 1

    def body(x_ref, w_ref, b_ref, *rest):
        o_refs, px = rest[:n_out], rest[n_out]
        px[0:pad, :] = jnp.zeros((pad, ct), F32)
        px[pad:pad + T, :] = x_ref[...]
        wv = w_ref[...]
        bv = b_ref[...]
        for r0 in range(0, T, CONV_ROWS):
            win = px[r0:r0 + CONV_ROWS + pad, :]
            acc = jnp.broadcast_to(bv, (CONV_ROWS, ct))
            for k in range(K):
                acc = acc + wv[k:k + 1, :] * _shifted(win, pad - (K - 1) + k, CONV_ROWS)
            o_refs[0][r0:r0 + CONV_ROWS, :] = acc
            if act:
                o_refs[1][r0:r0 + CONV_ROWS, :] = _silu(acc)

    return _pcall(
        body, name=name, grid=(C // ct,),
        in_specs=[pl.BlockSpec((T, ct), lambda j: (0, x_cb0 + j)), pl.BlockSpec((KP, ct), lambda j: (0, j)),
                  pl.BlockSpec((1, ct), lambda j: (0, j))],
        out_specs=[pl.BlockSpec((T, ct), lambda j: (0, j))] * n_out,
        out_shape=[jax.ShapeDtypeStruct((T, C), F32)] * n_out,
        scratch_shapes=[pltpu.VMEM((T + pad, ct), F32)],
        compiler_params=_cparams(("parallel",)),
    )(x, w, b)


def dwconv_bwd(dout, cpre, x, x_cb0, w, K, ct, act, out_dtype, name):
    C = w.shape[1]
    pad = 8 if K <= 8 else 32
    KP = w.shape[0]

    def body(*refs):
        if act:
            d_ref, c_ref, x_ref, w_ref, dx_ref, dw_ref, db_ref, px, pd = refs
        else:
            d_ref, x_ref, w_ref, dx_ref, dw_ref, db_ref, px, pd = refs
        px[0:pad, :] = jnp.zeros((pad, ct), F32)
        px[pad:pad + T, :] = x_ref[...]
        pd[T:T + pad, :] = jnp.zeros((pad, ct), F32)
        if act:
            pd[0:T, :] = d_ref[...] * _dsilu(c_ref[...])
        else:
            pd[0:T, :] = d_ref[...]
        wv = w_ref[...]
        dws = [jnp.zeros((1, ct), F32) for _ in range(K)]
        db = jnp.zeros((1, ct), F32)
        for r0 in range(0, T, CONV_ROWS):
            dwin = pd[r0:r0 + CONV_ROWS + pad, :]
            xwin = px[r0:r0 + CONV_ROWS + pad, :]
            dc = dwin[0:CONV_ROWS]
            db = db + _rsum(dc)
            acc = jnp.zeros((CONV_ROWS, ct), F32)
            for k in range(K):
                acc = acc + wv[k:k + 1, :] * _shifted(dwin, K - 1 - k, CONV_ROWS)
                dws[k] = dws[k] + _rsum(dc * _shifted(xwin, pad - (K - 1) + k, CONV_ROWS))
            dx_ref[r0:r0 + CONV_ROWS, :] = acc.astype(dx_ref.dtype)
        dw_ref[...] = jnp.zeros((KP, ct), F32)
        for k in range(K):
            dw_ref[k:k + 1, :] = dws[k]
        db_ref[...] = db

    col = pl.BlockSpec((T, ct), lambda j: (0, j))
    in_specs = [col] + ([col] if act else []) + [pl.BlockSpec((T, ct), lambda j: (0, x_cb0 + j)),
                                                 pl.BlockSpec((KP, ct), lambda j: (0, j))]
    args = [dout] + ([cpre] if act else []) + [x, w]
    return _pcall(
        body, name=name, grid=(C // ct,), in_specs=in_specs,
        out_specs=[col, pl.BlockSpec((KP, ct), lambda j: (0, j)), pl.BlockSpec((1, ct), lambda j: (0, j))],
        out_shape=[jax.ShapeDtypeStruct((T, C), out_dtype), jax.ShapeDtypeStruct((KP, C), F32),
                   jax.ShapeDtypeStruct((1, C), F32)],
        scratch_shapes=[pltpu.VMEM((T + pad, ct), F32), pltpu.VMEM((T + pad, ct), F32)],
        compiler_params=_cparams(("parallel",)),
    )(*args)


def _scan(a, axis, reverse=False):
    n = a.shape[axis]
    idx = lax.broadcasted_iota(jnp.int32, a.shape, axis)
    s = 1
    while s < n:
        if reverse:
            a = a + jnp.where(idx < n - s, pltpu.roll(a, shift=n - s, axis=axis), 0.0)
        else:
            a = a + jnp.where(idx >= s, pltpu.roll(a, shift=s, axis=axis), 0.0)
        s *= 2
    return a


_NT = _DIMS["nt"]
_TN = _DIMS["tn"]


def _dot(a, b, dims=_DIMS["nn"]):
    return lax.dot_general(a, b, dims, preferred_element_type=F32)


def ssd_fwd(xbc, dtx, dax, daT, dfull, name):
    def body(xbc_ref, dtx_ref, dax_ref, daT_ref, df_ref, y_ref, st_ref, S):
        ci = pl.program_id(0)

        @pl.when(ci == 0)
        def _():
            S[...] = jnp.zeros_like(S)

        row = lax.broadcasted_iota(jnp.int32, (CH, CH), 0)
        lane = lax.broadcasted_iota(jnp.int32, (CH, CH), 1)
        acsT = _scan(daT_ref[...], 1)
        for g in range(NG):
            c0 = g * GW
            xs = xbc_ref[:, c0:c0 + GW]
            acs = _scan(dax_ref[:, c0:c0 + GW], 0)
            Bm = xbc_ref[:, DI + g * DS:DI + (g + 1) * DS].astype(BF16)
            Cm = xbc_ref[:, DI + NG * DS + g * DS:DI + NG * DS + (g + 1) * DS].astype(BF16)
            xdt = xs * dtx_ref[:, c0:c0 + GW]
            atot = acs[CH - 1:CH, :]
            Sg = S[:, c0:c0 + GW]
            st_ref[:, c0:c0 + GW] = Sg
            CB = _dot(Cm, Bm, _NT)
            yg = jnp.exp(acs) * _dot(Cm, Sg.astype(BF16)) + xs * df_ref[:, c0:c0 + GW]
            xd = (xdt * jnp.exp(atot - acs)).astype(BF16)
            S[:, c0:c0 + GW] = jnp.exp(atot) * Sg + _dot(Bm, xd, _TN)
            xdt_b = xdt.astype(BF16)
            for r in range(NH // NG):
                h = g * (NH // NG) + r
                hs = slice(r * HD, (r + 1) * HD)
                seg = acs[:, r * HD:r * HD + 1] - acsT[h:h + 1, :]
                Lm = jnp.where(row >= lane, jnp.exp(jnp.minimum(seg, 0.0)), 0.0)
                yd = _dot((CB * Lm).astype(BF16), xdt_b[:, hs])
                y_ref[:, c0 + r * HD:c0 + (r + 1) * HD] = yg[:, hs] + yd

    return _pcall(
        body, name=name, grid=(NCH,),
        in_specs=[pl.BlockSpec((CH, CONVD), lambda i: (i, 0)), pl.BlockSpec((CH, DI), lambda i: (i, 0)),
                  pl.BlockSpec((CH, DI), lambda i: (i, 0)), pl.BlockSpec((NH, CH), lambda i: (0, i)),
                  pl.BlockSpec((1, DI), lambda i: (0, 0))],
        out_specs=[pl.BlockSpec((CH, DI), lambda i: (i, 0)), pl.BlockSpec((None, DS, DI), lambda i: (i, 0, 0))],
        out_shape=[jax.ShapeDtypeStruct((T, DI), F32), jax.ShapeDtypeStruct((NCH, DS, DI), F32)],
        scratch_shapes=[pltpu.VMEM((DS, DI), F32)],
        compiler_params=_cparams(("arbitrary",)),
    )(xbc, dtx, dax, daT, dfull)


def ssd_bwd(dy, xbc, dtx, dax, daT, dfull, states, name):
    hsum = (jnp.arange(DI)[:, None] // HD == jnp.arange(LANES)[None, :]).astype(BF16).reshape(NG, GW, LANES)

    def body(dy_ref, xbc_ref, dtx_ref, dax_ref, daT_ref, df_ref, st_ref, hsum_ref, dxbc_ref, ddt_ref, dda_ref, dD_ref, dS):
        i = pl.program_id(0)

        @pl.when(i == 0)
        def _():
            dS[...] = jnp.zeros_like(dS)
            dD_ref[...] = jnp.zeros_like(dD_ref)

        row = lax.broadcasted_iota(jnp.int32, (CH, CH), 0)
        lane = lax.broadcasted_iota(jnp.int32, (CH, CH), 1)
        acsT = _scan(daT_ref[...], 1)
        ddt_all = jnp.zeros((CH, LANES), F32)
        dacs_all = jnp.zeros((CH, LANES), F32)
        colacc = jnp.zeros((CH, CH), F32)
        for g in range(NG):
            c0 = g * GW
            xs = xbc_ref[:, c0:c0 + GW]
            dtx = dtx_ref[:, c0:c0 + GW]
            acs = _scan(dax_ref[:, c0:c0 + GW], 0)
            Bm = xbc_ref[:, DI + g * DS:DI + (g + 1) * DS].astype(BF16)
            Cm = xbc_ref[:, DI + NG * DS + g * DS:DI + NG * DS + (g + 1) * DS].astype(BF16)
            xdt = xs * dtx
            atot = acs[CH - 1:CH, :]
            Sin = st_ref[:, c0:c0 + GW]
            dyg = dy_ref[:, c0:c0 + GW]
            dSo = dS[:, c0:c0 + GW]
            E = jnp.exp(acs)
            Etot = jnp.exp(atot)
            dec = jnp.exp(atot - acs)
            dD_ref[:, c0:c0 + GW] += _rsum(dyg * xs)
            dxs = dyg * df_ref[:, c0:c0 + GW]
            Sin_b = Sin.astype(BF16)
            dSo_b = dSo.astype(BF16)
            dY0 = dyg * E
            dY0_b = dY0.astype(BF16)
            dC = _dot(dY0_b, Sin_b, _NT)
            dS[:, c0:c0 + GW] = _dot(Cm, dY0_b, _TN) + Etot * dSo
            XD = xdt * dec
            dXD = _dot(Bm, dSo_b)
            dB = _dot(XD.astype(BF16), dSo_b, _NT)
            dxdt = dXD * dec
            Gq = dXD * XD
            dacs_x = dY0 * _dot(Cm, Sin_b) - Gq
            datot_x = _rsum(dSo * Sin) * Etot + _rsum(Gq)
            dacs_all = dacs_all + _spread2(dacs_x, hsum_ref[g])
            dtot8 = _spread2(jnp.broadcast_to(datot_x, (8, GW)), hsum_ref[g])
            dacs_all = dacs_all + jnp.where(row == CH - 1, jnp.broadcast_to(dtot8[0:1, :], (CH, LANES)), 0.0)
            CB = _dot(Cm, Bm, _NT)
            dCB = jnp.zeros((CH, CH), F32)
            xdt_b = xdt.astype(BF16)
            dy_b = dyg.astype(BF16)
            for r in range(NH // NG):
                h = g * (NH // NG) + r
                hs = slice(r * HD, (r + 1) * HD)
                seg = acs[:, r * HD:r * HD + 1] - acsT[h:h + 1, :]
                Lm = jnp.where(row >= lane, jnp.exp(jnp.minimum(seg, 0.0)), 0.0)
                dyr = dy_b[:, hs]
                dML = _dot(dyr, xdt_b[:, hs], _NT) * Lm
                dxbc_ref[:, c0 + r * HD:c0 + (r + 1) * HD] = _dot((CB * Lm).astype(BF16), dyr, _TN)
                dCB = dCB + dML
                dseg = dML * CB
                dacs_all = dacs_all + _spread2(dseg, (lane == h).astype(BF16))
                colacc = colacc + jnp.where(row == h, jnp.sum(dseg, axis=0, keepdims=True), 0.0)
            dxdt = dxdt + dxbc_ref[:, c0:c0 + GW]
            ddt_all = ddt_all + _spread2(dxdt * xs, hsum_ref[g])
            dxbc_ref[:, c0:c0 + GW] = dxs + dxdt * dtx
            dCB_b = dCB.astype(BF16)
            dxbc_ref[:, DI + g * DS:DI + (g + 1) * DS] = dB + _dot(dCB_b, Cm, _TN)
            dxbc_ref[:, DI + NG * DS + g * DS:DI + NG * DS + (g + 1) * DS] = dC + _dot(dCB_b, Bm)
        ddt_ref[...] = ddt_all
        dda_ref[...] = _scan(dacs_all - colacc.T, 0, reverse=True)

    last = NCH - 1
    return _pcall(
        body, name=name, grid=(NCH,),
        in_specs=[pl.BlockSpec((CH, DI), lambda i: (last - i, 0)), pl.BlockSpec((CH, CONVD), lambda i: (last - i, 0)),
                  pl.BlockSpec((CH, DI), lambda i: (last - i, 0)), pl.BlockSpec((CH, DI), lambda i: (last - i, 0)),
                  pl.BlockSpec((NH, CH), lambda i: (0, last - i)), pl.BlockSpec((1, DI), lambda i: (0, 0)),
                  pl.BlockSpec((None, DS, DI), lambda i: (last - i, 0, 0)),
                  pl.BlockSpec((NG, GW, LANES), lambda i: (0, 0, 0))],
        out_specs=[pl.BlockSpec((CH, CONVD), lambda i: (last - i, 0)), pl.BlockSpec((CH, LANES), lambda i: (last - i, 0)),
                   pl.BlockSpec((CH, LANES), lambda i: (last - i, 0)), pl.BlockSpec((1, DI), lambda i: (0, 0))],
        out_shape=[jax.ShapeDtypeStruct((T, CONVD), F32), jax.ShapeDtypeStruct((T, LANES), F32),
                   jax.ShapeDtypeStruct((T, LANES), F32), jax.ShapeDtypeStruct((1, DI), F32)],
        scratch_shapes=[pltpu.VMEM((DS, DI), F32)],
        compiler_params=_cparams(("arbitrary",)),
    )(dy, xbc, dtx, dax, daT, dfull, states, hsum)


def _as3d(shape):
    if len(shape) == 1:
        return (1, 1, shape[0])
    if len(shape) == 2:
        return (1, shape[0], shape[1])
    return (math.prod(shape[:-2]), shape[-2], shape[-1])


def _row_tile(R, C):
    if R * C <= 512 * 1024:
        return R
    return next((t for t in (512, 256, 128, 64, 32, 16) if R % t == 0), R)


def adamw(parts, w, m, v, name):
    shape = w.shape
    L, R, C = _as3d(shape)
    P = parts.shape[0]
    tr = _row_tile(R, C)
    bc1 = 1.0 - ADAM_B1 ** ADAM_STEP
    bc2 = 1.0 - ADAM_B2 ** ADAM_STEP

    def body(p_ref, w_ref, m_ref, v_ref, g_out, d_out, m_out, v_out):
        g = p_ref[0].astype(F32)
        for k in range(1, P):
            g = g + p_ref[k].astype(F32)
        mn = ADAM_B1 * m_ref[...] + (1.0 - ADAM_B1) * g
        vn = ADAM_B2 * v_ref[...] + (1.0 - ADAM_B2) * (g * g)
        g_out[...] = g
        m_out[...] = mn
        v_out[...] = vn
        d_out[...] = -ADAM_LR * ((mn / bc1) / (jnp.sqrt(vn / bc2) + ADAM_EPS) + ADAM_WD * w_ref[...])

    blk = pl.BlockSpec((None, tr, C), lambda l, r: (l, r, 0))
    outs = _pcall(
        body, name=name, grid=(L, R // tr),
        in_specs=[pl.BlockSpec((P, None, tr, C), lambda l, r: (0, l, r, 0)), blk, blk, blk],
        out_specs=[blk] * 4, out_shape=[jax.ShapeDtypeStruct((L, R, C), F32)] * 4,
        compiler_params=_cparams(("parallel", "parallel")),
    )(parts.reshape(P, L, R, C), w.reshape(L, R, C), m.reshape(L, R, C), v.reshape(L, R, C))
    return [o.reshape(shape) for o in outs]


def adamw_layer(recv, own, w, m, v, layer, prev, name, tr=None):
    R, tail = w.shape[1], tuple(w.shape[2:])
    zero = (0,) * len(tail)
    P = recv.shape[0]
    tr = tr or _row_tile(R, math.prod(tail))
    bc1 = 1.0 - ADAM_B1 ** ADAM_STEP
    bc2 = 1.0 - ADAM_B2 ** ADAM_STEP

    def body(r_ref, o_ref, w_ref, m_ref, v_ref, *rest):
        g_out, d_out, m_out, v_out = rest[-4:]
        g = o_ref[...].astype(F32)
        for k in range(P):
            g = g + r_ref[k].astype(F32)
        mn = ADAM_B1 * m_ref[...] + (1.0 - ADAM_B1) * g
        vn = ADAM_B2 * v_ref[...] + (1.0 - ADAM_B2) * (g * g)
        g_out[...] = g
        m_out[...] = mn
        v_out[...] = vn
        d_out[...] = -ADAM_LR * ((mn / bc1) / (jnp.sqrt(vn / bc2) + ADAM_EPS) + ADAM_WD * w_ref[...])

    slot = pl.BlockSpec((None, tr) + tail, lambda r: (layer, r) + zero)
    own_spec = pl.BlockSpec((None, tr) + tail, lambda r: (2 * lax.axis_index("x") + lax.axis_index("y"), r) + zero)
    in_specs = [pl.BlockSpec((P, tr) + tail, lambda r: (0, r) + zero), own_spec, slot, slot, slot]
    args = [recv, own, w, m, v]
    aliases = {}
    if prev is not None:
        in_specs += [ANY] * 4
        args += list(prev)
        aliases = {5 + k: k for k in range(4)}
    return _pcall(
        body, name=name, grid=(R // tr,), in_specs=in_specs, out_specs=[slot] * 4,
        out_shape=[jax.ShapeDtypeStruct(w.shape, F32)] * 4, input_output_aliases=aliases,
        compiler_params=_cparams(("parallel",)),
    )(*args)


def sum_leading(parts, name):
    P, R, C = parts.shape

    def body(p_ref, o_ref):
        s = p_ref[0]
        for k in range(1, P):
            s = s + p_ref[k]
        o_ref[...] = s

    return _pcall(body, name=name, out_shape=jax.ShapeDtypeStruct((R, C), F32))(parts)


def pair_sum(gsend, recv, name):
    S = gsend.shape[1:]
    L, R, C = _as3d(S)
    tr = _row_tile(R, C)

    def body(g_ref, r_ref, o_ref):
        o_ref[...] = (g_ref[...].astype(F32) + r_ref[...].astype(F32)).astype(o_ref.dtype)

    blk = pl.BlockSpec((None, None, tr, C), lambda q, l, r: (q, l, r, 0))
    own = pl.BlockSpec((None, None, tr, C), lambda q, l, r: (2 * q + lax.axis_index("c"), l, r, 0))
    out = _pcall(
        body, name=name, grid=(4, L, R // tr), in_specs=[own, blk], out_specs=blk,
        out_shape=jax.ShapeDtypeStruct((4, L, R, C), BF16),
        compiler_params=_cparams(("parallel", "parallel", "parallel")),
    )(gsend.reshape(8, L, R, C), recv.reshape(4, L, R, C))
    return out.reshape((4,) + S)


def _place():
    return lax.axis_index("x"), lax.axis_index("y"), lax.axis_index("c")


def _other_chips(x, y):
    return [(1 - x, y), (x, 1 - y), (1 - x, 1 - y)]


def all_gather(arrs, name):
    n = len(arrs)

    def body(*refs):
        ins, outs = refs[:n], refs[n:2 * n]
        send_sems, recv_sems, local_sems = refs[2 * n:]
        x, y, c = _place()
        me, sibling = (x, y, c), (x, y, 1 - c)
        chips = _other_chips(x, y)

        def slot(a, px, py, pc):
            return outs[a].at[4 * px + 2 * py + pc]

        def copy(a, k, block, to, src=None):
            return pltpu.make_async_remote_copy(
                src_ref=slot(a, *block) if src is None else src, dst_ref=slot(a, *block),
                send_sem=send_sems.at[a, k], recv_sem=recv_sems.at[a, k], device_id=to, device_id_type=MESH)

        mine, first, passed = [], [], []
        for a in range(n):
            cp = pltpu.make_async_copy(ins[a], slot(a, *me), local_sems.at[a])
            cp.start()
            mine.append(cp)
            first.append(copy(a, 0, me, sibling, src=ins[a]))
            first += [copy(a, 1 + j, me, (*chip, c), src=ins[a]) for j, chip in enumerate(chips)]
        for cp in first:
            cp.start()
        for j, chip in enumerate(chips):
            for a in range(n):
                copy(a, 1 + j, (*chip, c), me).wait_recv()
                cp = copy(a, 4 + j, (*chip, c), sibling)
                cp.start()
                passed.append(cp)
        for a in range(n):
            copy(a, 0, sibling, me).wait_recv()
            for j, chip in enumerate(chips):
                copy(a, 4 + j, (*chip, 1 - c), me).wait_recv()
        for cp in first + passed:
            cp.wait_send()
        for cp in mine:
            cp.wait()

    return _pcall(
        body, name=name, in_specs=[ANY] * n, out_specs=[ANY] * n,
        out_shape=[jax.ShapeDtypeStruct((N_DEV,) + a.shape, a.dtype) for a in arrs],
        scratch_shapes=[pltpu.SemaphoreType.DMA((n, 7)), pltpu.SemaphoreType.DMA((n, 7)), pltpu.SemaphoreType.DMA((n,))],
    )(*arrs)


def sibling_exchange(gsends, name, after=None):
    n = len(gsends)
    n_in = n + (1 if after is not None else 0)

    def body(*refs):
        ins, outs = refs[:n], refs[n_in:n_in + n]
        send_sems, recv_sems = refs[n_in + n:]
        x, y, c = _place()
        copies = []
        for a in range(n):
            for q in range(4):
                cp = pltpu.make_async_remote_copy(
                    src_ref=ins[a].at[2 * q + 1 - c], dst_ref=outs[a].at[q],
                    send_sem=send_sems.at[a, q], recv_sem=recv_sems.at[a, q],
                    device_id=(x, y, 1 - c), device_id_type=MESH)
                cp.start()
                copies.append(cp)
        for cp in copies:
            cp.wait()

    return _pcall(
        body, name=name, in_specs=[ANY] * n_in, out_specs=[ANY] * n,
        out_shape=[jax.ShapeDtypeStruct((4,) + g.shape[1:], g.dtype) for g in gsends],
        scratch_shapes=[pltpu.SemaphoreType.DMA((n, 4)), pltpu.SemaphoreType.DMA((n, 4))],
    )(*gsends, *([after] if after is not None else []))


HBM =pl.BlockSpec(memory_space=pltpu.HBM)
SEM = pl.BlockSpec(memory_space=pltpu.SEMAPHORE)
EFFECT = pltpu.SideEffectType.DATAFLOW_SIDE_EFFECTING


def _in_hbm(a):
    return pltpu.with_memory_space_constraint(a, pltpu.HBM)


def _gather_peers(x, y, c):
    to = [(x, y, 1 - c)] + [(px, py, c) for px, py in _other_chips(x, y)]
    return to, [4 * px + 2 * py + pc for px, py, pc in to]


def gather_start(arrs, after, name):
    n = len(arrs)
    n_in = 2 * n + (1 if after is not None else 0)

    def body(*refs):
        srcs, lands = refs[:n], refs[n:2 * n]
        send_sems, recv_sems = refs[n_in], refs[n_in + 1]
        token = refs[-1]
        x, y, c = _place()
        to, _ = _gather_peers(x, y, c)
        me = 4 * x + 2 * y + c
        for a in range(n):
            for k, dev in enumerate(to):
                pltpu.make_async_remote_copy(
                    src_ref=srcs[a], dst_ref=lands[a].at[me], send_sem=send_sems.at[4 * a + k], recv_sem=recv_sems.at[4 * a + k],
                    device_id=dev, device_id_type=MESH).start()
        token[...] = jnp.zeros_like(token)

    zones = [lax.empty((N_DEV,) + a.shape, a.dtype) for a in arrs]
    args = [_in_hbm(a) for a in arrs] + [_in_hbm(z) for z in zones] + ([after] if after is not None else [])
    outs = _pcall(
        body, name=name,
        out_shape=(pltpu.SemaphoreType.DMA((4 * n,)), pltpu.SemaphoreType.DMA((4 * n,)),
                   *[pltpu.HBM(a.shape, a.dtype) for a in arrs], *[pltpu.HBM(z.shape, z.dtype) for z in zones],
                   jax.ShapeDtypeStruct((8, LANES), F32)),
        in_specs=[HBM] * (2 * n) + ([ANY] if after is not None else []),
        out_specs=(SEM, SEM, *[HBM] * (2 * n), pl.BlockSpec(memory_space=pltpu.VMEM)),
        input_output_aliases={i: 2 + i for i in range(2 * n)},
        compiler_params=pltpu.CompilerParams(has_side_effects=EFFECT),
    )(*args)
    return dict(send=outs[0], recv=outs[1], srcs=list(outs[2:2 + n]), lands=list(outs[2 + n:2 + 2 * n]), token=outs[-1])


def gather_wait(st, after, name):
    n = len(st["srcs"])

    def body(*refs):
        srcs, lands = refs[:n], refs[n:2 * n]
        send_sems, recv_sems = refs[2 * n], refs[2 * n + 1]
        x, y, c = _place()
        to, slots = _gather_peers(x, y, c)
        for a in range(n):
            for k, dev in enumerate(to):
                cp = pltpu.make_async_remote_copy(
                    src_ref=srcs[a], dst_ref=lands[a].at[slots[k]], send_sem=send_sems.at[4 * a + k],
                    recv_sem=recv_sems.at[4 * a + k], device_id=dev, device_id_type=MESH)
                cp.wait_send()
                cp.wait_recv()

    outs = _pcall(
        body, name=name,
        out_shape=(*[pltpu.HBM(a.shape, a.dtype) for a in st["srcs"]], *[pltpu.HBM(z.shape, z.dtype) for z in st["lands"]]),
        in_specs=[HBM] * (2 * n) + [SEM, SEM, ANY], out_specs=tuple([HBM] * (2 * n)),
        input_output_aliases={i: i for i in range(2 * n)},
        compiler_params=pltpu.CompilerParams(has_side_effects=EFFECT),
    )(*st["srcs"], *st["lands"], st["send"], st["recv"], after)
    return list(outs[n:])


def pass_start(zones, name):
    n = len(zones)

    def body(*refs):
        zs = refs[:n]
        send_sems, recv_sems = refs[n], refs[n + 1]
        token = refs[-1]
        x, y, c = _place()
        for a in range(n):
            for j, (px, py) in enumerate(_other_chips(x, y)):
                blk = zs[a].at[4 * px + 2 * py + c]
                pltpu.make_async_remote_copy(
                    src_ref=blk, dst_ref=blk, send_sem=send_sems.at[3 * a + j], recv_sem=recv_sems.at[3 * a + j],
                    device_id=(x, y, 1 - c), device_id_type=MESH).start()
        token[...] = jnp.zeros_like(token)

    outs = _pcall(
        body, name=name,
        out_shape=(pltpu.SemaphoreType.DMA((3 * n,)), pltpu.SemaphoreType.DMA((3 * n,)),
                   *[pltpu.HBM(z.shape, z.dtype) for z in zones], jax.ShapeDtypeStruct((8, LANES), F32)),
        in_specs=[HBM] * n, out_specs=(SEM, SEM, *[HBM] * n, pl.BlockSpec(memory_space=pltpu.VMEM)),
        input_output_aliases={i: 2 + i for i in range(n)},
        compiler_params=pltpu.CompilerParams(has_side_effects=EFFECT),
    )(*zones)
    return dict(send=outs[0], recv=outs[1], zones=list(outs[2:2 + n]), token=outs[-1])


def pass_wait(st, after, name):
    n = len(st["zones"])

    def body(*refs):
        zs = refs[:n]
        send_sems, recv_sems = refs[n], refs[n + 1]
        x, y, c = _place()
        for a in range(n):
            for j, (px, py) in enumerate(_other_chips(x, y)):
                cp = pltpu.make_async_remote_copy(
                    src_ref=zs[a].at[4 * px + 2 * py + c], dst_ref=zs[a].at[4 * px + 2 * py + 1 - c],
                    send_sem=send_sems.at[3 * a + j], recv_sem=recv_sems.at[3 * a + j],
                    device_id=(x, y, 1 - c), device_id_type=MESH)
                cp.wait_send()
                cp.wait_recv()

    outs = _pcall(
        body, name=name, out_shape=tuple(pltpu.HBM(z.shape, z.dtype) for z in st["zones"]),
        in_specs=[HBM] * n + [SEM, SEM, ANY], out_specs=tuple([HBM] * n),
        input_output_aliases={i: i for i in range(n)},
        compiler_params=pltpu.CompilerParams(has_side_effects=EFFECT),
    )(*st["zones"], st["send"], st["recv"], after)
    return list(outs)


def scatter_start(parts, name):
    n = len(parts)

    def body(*refs):
        srcs, lands = refs[:n], refs[n:2 * n]
        send_sems, recv_sems = refs[2 * n], refs[2 * n + 1]
        token = refs[-1]
        x, y, c = _place()
        for a in range(n):
            for j, (px, py) in enumerate(_other_chips(x, y)):
                pltpu.make_async_remote_copy(
                    src_ref=srcs[a].at[2 * px + py], dst_ref=lands[a].at[j], send_sem=send_sems.at[3 * a + j],
                    recv_sem=recv_sems.at[3 * a + j], device_id=(px, py, c), device_id_type=MESH).start()
        token[...] = jnp.zeros_like(token)

    zones = [lax.empty((3,) + p.shape[1:], p.dtype) for p in parts]
    outs = _pcall(
        body, name=name,
        out_shape=(pltpu.SemaphoreType.DMA((3 * n,)), pltpu.SemaphoreType.DMA((3 * n,)),
                   *[pltpu.HBM(p.shape, p.dtype) for p in parts], *[pltpu.HBM(z.shape, z.dtype) for z in zones],
                   jax.ShapeDtypeStruct((8, LANES), F32)),
        in_specs=[HBM] * (2 * n), out_specs=(SEM, SEM, *[HBM] * (2 * n), pl.BlockSpec(memory_space=pltpu.VMEM)),
        input_output_aliases={i: 2 + i for i in range(2 * n)},
        compiler_params=pltpu.CompilerParams(has_side_effects=EFFECT),
    )(*[_in_hbm(p) for p in parts], *[_in_hbm(z) for z in zones])
    return dict(send=outs[0], recv=outs[1], srcs=list(outs[2:2 + n]), lands=list(outs[2 + n:2 + 2 * n]), token=outs[-1])


def scatter_wait(st, after, name):
    n = len(st["srcs"])

    def body(*refs):
        srcs, lands = refs[:n], refs[n:2 * n]
        send_sems, recv_sems = refs[2 * n], refs[2 * n + 1]
        x, y, c = _place()
        for a in range(n):
            for j, (px, py) in enumerate(_other_chips(x, y)):
                cp = pltpu.make_async_remote_copy(
                    src_ref=srcs[a].at[2 * px + py], dst_ref=lands[a].at[j], send_sem=send_sems.at[3 * a + j],
                    recv_sem=recv_sems.at[3 * a + j], device_id=(px, py, c), device_id_type=MESH)
                cp.wait_send()
                cp.wait_recv()

    outs = _pcall(
        body, name=name,
        out_shape=(*[pltpu.HBM(a.shape, a.dtype) for a in st["srcs"]], *[pltpu.HBM(z.shape, z.dtype) for z in st["lands"]]),
        in_specs=[HBM] * (2 * n) + [SEM, SEM, ANY], out_specs=tuple([HBM] * (2 * n)),
        input_output_aliases={i: i for i in range(2 * n)},
        compiler_params=pltpu.CompilerParams(has_side_effects=EFFECT),
    )(*st["srcs"], *st["lands"], st["send"], st["recv"], after)
    return list(outs[:n]), list(outs[n:])


def _unshard(g, axis):
    nd = g.ndim - 1
    axis = axis % nd
    t = jnp.moveaxis(g, 0, axis)
    shp = list(g.shape[1:])
    shp[axis] *= N_DEV
    return t.reshape(shp)


def _to_shards(full, axis):
    axis = axis % full.ndim
    shp = list(full.shape)
    shp[axis:axis + 1] = [N_DEV, shp[axis] // N_DEV]
    return jnp.moveaxis(full.reshape(shp), axis, 0)


def _pack(arrs, rows):
    flat = jnp.concatenate([a.reshape(-1).astype(F32) for a in arrs])
    return jnp.pad(flat, (0, rows * LANES - flat.shape[0])).reshape(rows, LANES)


def _unpack(buf, shapes):
    flat = buf.reshape(-1)
    out, off = [], 0
    for s in shapes:
        n = math.prod(s)
        out.append(flat[off:off + n].reshape(s))
        off += n
    return out


def _rows_for(shapes):
    n = sum(math.prod(s) for s in shapes)
    return -(-n // (8 * LANES)) * 8


def _row(v, width=None):
    v = v.reshape(1, -1).astype(F32)
    if width is not None and v.shape[1] < width:
        v = jnp.pad(v, ((0, 0), (0, width - v.shape[1])))
    return v


def _after(order, width):
    if not order:
        return None
    t = order[0][0:1, 0:1]
    for o in order[1:]:
        t = t + o[0:1, 0:1]
    return jnp.broadcast_to(t, (1, width))


def _norm_after(norm, order):
    row = _after(order, norm.shape[1])
    return norm if row is None else norm + row


FFN_TN = 256


def ffn_in(h, norm, w_gate, w_up, name):
    def body(h_ref, n_ref, wg_ref, wu_ref, u_ref, g_ref, up_ref, act_ref, u_s):
        @pl.when(pl.program_id(0) == 0)
        def _():
            x = h_ref[...]
            r = lax.rsqrt(jnp.mean(x * x, axis=-1, keepdims=True) + EPS)
            u_s[...] = (x * r * n_ref[...]).astype(BF16)
            u_ref[...] = u_s[...]

        u = u_s[...]
        g = _dot(u, wg_ref[...], _NT)
        up = _dot(u, wu_ref[...], _NT)
        g_ref[...] = g.astype(BF16)
        up_ref[...] = up.astype(BF16)
        act_ref[...] = (_silu(g) * up).astype(BF16)

    whole = pl.BlockSpec((T, D), lambda j: (0, 0))
    wspec = pl.BlockSpec((FFN_TN, D), lambda j: (j, 0))
    col = pl.BlockSpec((T, FFN_TN), lambda j: (0, j))
    return _pcall(
        body, name=name, grid=(DFF // FFN_TN,), in_specs=[whole, pl.BlockSpec((1, D), lambda j: (0, 0)), wspec, wspec],
        out_specs=[whole, col, col, col],
        out_shape=[jax.ShapeDtypeStruct((T, D), BF16)] + [jax.ShapeDtypeStruct((T, DFF), BF16)] * 3,
        scratch_shapes=[pltpu.VMEM((T, D), BF16)], compiler_params=_cparams(("arbitrary",)),
    )(h, norm, w_gate, w_up)


def ffn_back(dh_b, w_down, g, up, after_row, name):
    has_row = after_row is not None

    def body(*refs):
        dh_ref, wd_ref, g_ref, up_ref = refs[:4]
        dg_ref, dup_ref = refs[-2:]
        da = _dot(dh_ref[...], wd_ref[...], _NT)
        if has_row:
            da = da + refs[4][...]
        g = g_ref[...].astype(F32)
        dg_ref[...] = (da * up_ref[...].astype(F32) * _dsilu(g)).astype(BF16)
        dup_ref[...] = (da * _silu(g)).astype(BF16)

    col = pl.BlockSpec((T, FFN_TN), lambda j: (0, j))
    in_specs = [pl.BlockSpec((T, D), lambda j: (0, 0)), pl.BlockSpec((FFN_TN, D), lambda j: (j, 0)), col, col]
    args = [dh_b, w_down, g, up]
    if has_row:
        in_specs.append(pl.BlockSpec((1, FFN_TN), lambda j: (0, j)))
        args.append(after_row)
    return _pcall(
        body, name=name, grid=(DFF // FFN_TN,), in_specs=in_specs, out_specs=[col, col],
        out_shape=[jax.ShapeDtypeStruct((T, DFF), BF16)] * 2, compiler_params=_cparams(("parallel",)),
    )(*args)


def ffn_layer_fwd(h, p, tag, order=()):
    u, g, up, act = ffn_in(h, _norm_after(p["norm"], order), p["w_gate"], p["w_up"], f"{tag}_in")
    h2 = matmul(act, p["w_down"], "nn", residual=h, name=f"{tag}_down")
    return h2, (h, u, g, up, act)


def ffn_layer_bwd(dh, dh_b, saved, p, tag, order=()):
    h, u, g, up, act = saved
    d_down = matmul(act, dh_b, "tn", out_dtype=BF16, name=f"{tag}_dwd")
    dg, dup = ffn_back(dh_b, p["w_down"], g, up, _after(order, DFF), f"{tag}_back")
    du = matmul(dg, p["w_gate"], "nn", name=f"{tag}_dug")
    du = matmul(dup, p["w_up"], "nn", residual=du, name=f"{tag}_duu")
    d_gate = matmul(dg, u, "tn", out_dtype=BF16, name=f"{tag}_dwg")
    d_up = matmul(dup, u, "tn", out_dtype=BF16, name=f"{tag}_dwu")
    dh2, dh2_b, d_norm = rms_bwd(du, h, p["norm"], dh, f"{tag}_drms")
    return dh2, dh2_b, dict(norm=d_norm, w_gate=d_gate, w_up=d_up, w_down=d_down)


def conv_layer_fwd(h, p, tag, order=()):
    u = rms_fwd(h, _norm_after(p["norm"], order), f"{tag}_rms")
    hh = matmul(u, p["w_pw1"], "nn", bias=p["b_pw1"], name=f"{tag}_pw1")
    gl = glu_fwd(hh, f"{tag}_glu")
    c2 = dwconv_fwd(gl, 0, p["dw_w"], p["dw_b"], KCV, 128, False, f"{tag}_dw")[0]
    s = ln_silu_fwd(c2, p["ln_g"], p["ln_b"], f"{tag}_ln")
    h2 = matmul(s, p["w_pw2"], "nn", bias=p["b_pw2"], residual=h, name=f"{tag}_pw2")
    return h2, (h, u, hh, gl, c2, s)


def conv_layer_bwd(dh, dh_b, saved, p, tag, order=()):
    h, u, hh, gl, c2, s = saved
    ds = matmul(dh_b, p["w_pw2"], "nt", bias=_after(order, D), name=f"{tag}_ds")
    d_pw2 = matmul(s, dh_b, "tn", out_dtype=BF16, name=f"{tag}_dwpw2")
    dc2, d_lng, d_lnb, d_bpw2 = ln_silu_bwd(ds, c2, dh, p["ln_g"], p["ln_b"], f"{tag}_dln")
    dgl, d_dww, d_dwb = dwconv_bwd(dc2, None, gl, 0, p["dw_w"], KCV, 128, False, F32, f"{tag}_ddw")
    dhh, d_bpw1 = glu_bwd(dgl, hh, f"{tag}_dglu")
    du = matmul(dhh, p["w_pw1"], "nt", name=f"{tag}_du")
    d_pw1 = matmul(u, dhh, "tn", out_dtype=BF16, name=f"{tag}_dwpw1")
    dh2, dh2_b, d_norm = rms_bwd(du, h, p["norm"], dh, f"{tag}_drms")
    grads = dict(norm=d_norm, w_pw1=d_pw1, b_pw1=d_bpw1, dw_w=d_dww[:KCV], dw_b=d_dwb, ln_g=d_lng, ln_b=d_lnb,
                 w_pw2=d_pw2, b_pw2=d_bpw2)
    return dh2, dh2_b, grads


def ssm_layer_fwd(h, p, tag, order=(), mid=None):
    u = rms_fwd(h, _norm_after(p["norm"], order), f"{tag}_rms")
    zx = matmul(u, p["w_in"], "nn", name=f"{tag}_in")
    cpre, xbc = dwconv_fwd(zx, DI // 512, p["conv_w"], p["conv_b"], KSSM, 512, True, f"{tag}_conv")
    dt, da, dtx, dax = dt_fwd(zx, p["dt_bias"], p["a_log"], f"{tag}_dt")
    daT = da[:, :NH].T
    y, states = ssd_fwd(xbc, dtx, dax, daT, p["d_full"], f"{tag}_ssd")
    gate_norm = p["gate_norm"] if mid is None else _norm_after(p["gate_norm"], mid(y))
    yn = gatenorm_fwd(y, zx, gate_norm, f"{tag}_gn")
    h2 = matmul(yn, p["w_out"], "nn", residual=h, name=f"{tag}_out")
    return h2, (h, u, zx, cpre, xbc, dt, dtx, dax, daT, y, states, yn)


def ssm_layer_bwd(dh, dh_b, saved, p, tag, order=()):
    h, u, zx, cpre, xbc, dt, dtx, dax, daT, y, states, yn = saved
    dyn = matmul(dh_b, p["w_out"], "nt", bias=_after(order, DI), name=f"{tag}_dyn")
    d_wout = matmul(yn, dh_b, "tn", out_dtype=BF16, name=f"{tag}_dwout")
    dy, dz, d_gn = gatenorm_bwd(dyn, y, zx, p["gate_norm"], f"{tag}_dgn")
    dxbc, ddt, dda, dD = ssd_bwd(dy, xbc, dtx, dax, daT, p["d_full"], states, f"{tag}_dssd")
    draw, d_dtb, d_alog = dt_bwd(ddt, dda, dt, zx, p["dt_bias"], p["a_log"], f"{tag}_ddt")
    dxpre, d_cw, d_cb = dwconv_bwd(dxbc, cpre, zx, DI // 512, p["conv_w"], KSSM, 512, True, BF16, f"{tag}_dconv")
    dzx = jnp.concatenate([dz, dxpre, draw, jnp.zeros((T, DINP_PAD - 2 * DI - 2 * NG * DS - LANES), BF16)], axis=1)
    du = matmul(dzx, p["w_in"], "nt", name=f"{tag}_du")
    d_win = matmul(u, dzx, "tn", out_dtype=BF16, name=f"{tag}_dwin")
    dh2, dh2_b, d_norm = rms_bwd(du, h, p["norm"], dh, f"{tag}_drms")
    d_d = headsum(dD.reshape(NH, HD), f"{tag}_dD").reshape(NH)
    grads = dict(norm=d_norm, w_in=d_win[:, :DINP], conv_w=d_cw[:KSSM], conv_b=d_cb, dt_bias=d_dtb[0, :NH],
                 a_log=d_alog[0, :NH], d=d_d, gate_norm=d_gn, w_out=d_wout)
    return dh2, dh2_b, grads


BIG = ["ssm_w_in", "ssm_w_out", "cv_w_pw1", "cv_w_pw2", "ffn_w_gate", "ffn_w_up", "ffn_w_down"]
TRANSPOSED = ("ffn_w_gate", "ffn_w_up")
LAYER_AXIS = {"ssm_w_in": -1, "ssm_w_out": 0, "cv_w_pw1": -1, "cv_w_pw2": 0, "ffn_w_gate": 0, "ffn_w_up": 0,
              "ffn_w_down": 0}
SMALL_SHARDED = ["ssm_conv_w", "cv_norm", "cv_b_pw1", "cv_dw_w", "cv_dw_b", "cv_ln_g", "cv_ln_b", "cv_b_pw2"]
SMALL_REPL = ["ssm_norm", "ssm_conv_b", "ssm_dt_bias", "ssm_a_log", "ssm_d", "ssm_gate_norm", "ffn_norm", "final_norm"]
WEIGHTS = ["ssm_norm", "ssm_w_in", "ssm_conv_w", "ssm_conv_b", "ssm_dt_bias", "ssm_a_log", "ssm_d", "ssm_gate_norm",
           "ssm_w_out", "cv_norm", "cv_w_pw1", "cv_b_pw1", "cv_dw_w", "cv_dw_b", "cv_ln_g", "cv_ln_b", "cv_w_pw2",
           "cv_b_pw2", "ffn_norm", "ffn_w_gate", "ffn_w_up", "ffn_w_down", "final_norm"]
SMALL = [n for n in WEIGHTS if n not in BIG]


N_STAGES = 8


def _stage_layer(s):
    i = s // 2
    if s % 2:
        return "ffn", i
    return ("ssm" if i % 2 == 0 else "cv"), i // 2


def _stage_group(s):
    fam, l = _stage_layer(s)
    names = {"ffn": ["ffn_w_gate", "ffn_w_up", "ffn_w_down"], "ssm": ["ssm_w_in", "ssm_w_out"],
             "cv": ["cv_w_pw1", "cv_w_pw2"]}[fam]
    return [(n, l) for n in names]


def _stage_params(s, big, small):
    fam, l = _stage_layer(s)
    if fam == "ffn":
        return dict(norm=_row(small["ffn_norm"][l]), w_gate=big["ffn_w_gate"], w_up=big["ffn_w_up"],
                    w_down=big["ffn_w_down"])
    if fam == "ssm":
        return dict(norm=_row(small["ssm_norm"][l]), w_in=jnp.pad(big["ssm_w_in"], ((0, 0), (0, DINP_PAD - DINP))),
                    conv_w=jnp.pad(small["ssm_conv_w"][l], ((0, 8 - KSSM), (0, 0))), conv_b=_row(small["ssm_conv_b"][l]),
                    dt_bias=_row(small["ssm_dt_bias"][l], LANES), a_log=_row(small["ssm_a_log"][l], LANES),
                    d_full=_row(jnp.repeat(small["ssm_d"][l], HD)), gate_norm=_row(small["ssm_gate_norm"][l]),
                    w_out=big["ssm_w_out"])
    return dict(norm=_row(small["cv_norm"][l]), w_pw1=big["cv_w_pw1"], b_pw1=_row(small["cv_b_pw1"][l]),
                dw_w=jnp.pad(small["cv_dw_w"][l], ((0, 32 - KCV), (0, 0))), dw_b=_row(small["cv_dw_b"][l]),
                ln_g=_row(small["cv_ln_g"][l]), ln_b=_row(small["cv_ln_b"][l]), w_pw2=big["cv_w_pw2"],
                b_pw2=_row(small["cv_b_pw2"][l]))


_STAGE_FWD = {"ffn": ffn_layer_fwd, "ssm": ssm_layer_fwd, "cv": conv_layer_fwd}
_STAGE_BWD = {"ffn": ffn_layer_bwd, "ssm": ssm_layer_bwd, "cv": conv_layer_bwd}


def _stage_fwd(s, h, p, order=(), mid=None):
    fam, l = _stage_layer(s)
    if mid is not None:
        return ssm_layer_fwd(h, p, f"{fam}{l}", order, mid)
    return _STAGE_FWD[fam](h, p, f"{fam}{l}", order)


def _stage_bwd(s, dh, dh_b, p, saved, order=()):
    fam, l = _stage_layer(s)
    dh, dh_b, g = _STAGE_BWD[fam](dh, dh_b, saved, p, f"{fam}{l}", order)
    return dh, dh_b, {f"{fam}_{k}": val for k, val in g.items()}


def _local(x, tgt, full):
    h, tape = x, []
    for s in range(N_STAGES):
        big = {n: (full[n][l].T if n in TRANSPOSED else full[n][l]) for n, l in _stage_group(s)}
        p = _stage_params(s, big, full)
        h, saved = _stage_fwd(s, h, p)
        tape.append((p, saved))
    dh, dh_b, d_final, loss_row = loss_head(h, _row(full["final_norm"]), tgt, "loss_head")
    gl = {n: [None] * full[n].shape[0] for n in WEIGHTS if n != "final_norm"}
    for s in reversed(range(N_STAGES)):
        dh, dh_b, g = _stage_bwd(s, dh, dh_b, *tape[s])
        for n, val in g.items():
            val = val.T if n in TRANSPOSED else val
            gl[n][_stage_layer(s)[1]] = val.reshape(full[n].shape[1:])
    grads = {n: jnp.stack(vs) for n, vs in gl.items()}
    grads["final_norm"] = d_final.reshape(D)
    return loss_row, dh, grads


def _step(x, tgt, w, m, v):
    idx = 4 * lax.axis_index("x") + 2 * lax.axis_index("y") + lax.axis_index("c")
    small_shapes = [w[n].shape for n in SMALL_SHARDED]
    small_pack = _pack([w[n] for n in SMALL_SHARDED], _rows_for(small_shapes))

    def view(n, a):
        return jnp.swapaxes(a, 1, 2) if n in TRANSPOSED else a

    wv, mv, vv = ({n: view(n, t[n]) for n in BIG} for t in (w, m, v))

    def blocks(s):
        return [wv[n][l].astype(BF16) for n, l in _stage_group(s)] + ([small_pack] if s == 0 else [])

    arrs = [blocks(s) for s in range(N_STAGES)]
    first = gather_start(arrs[0], None, "gather0_start")
    passing = pass_start(gather_wait(first, first["token"], "gather0_wait"), "pass0_start")
    crossing = {1: gather_start(arrs[1], passing["token"], "gather1_start")}
    crossing[2] = gather_start(arrs[2], crossing[1]["token"], "gather2_start")
    small = {n: w[n] for n in SMALL_REPL}
    flight = dict(passing=passing)

    def advance(s, after):
        tokens = []
        if s + 1 < N_STAGES:
            landed = gather_wait(crossing.pop(s + 1), after, f"gather{s + 1}_wait")
            flight["passing"] = pass_start(landed, f"pass{s + 1}_start")
            tokens.append(flight["passing"]["token"])
        if s + 3 < N_STAGES:
            crossing[s + 3] = gather_start(arrs[s + 3], flight["passing"]["token"], f"gather{s + 3}_start")
            tokens.append(crossing[s + 3]["token"])
        return tokens

    h, tape, after = x, [], crossing[2]["token"]
    for s in range(N_STAGES):
        zones = pass_wait(flight["passing"], after, f"pass{s}_wait")
        order = advance(s, zones[0]) if s else []
        mid = functools.partial(advance, 0) if s == 0 else None
        zones = [lax.dynamic_update_slice_in_dim(z, a[None], idx, 0) for z, a in zip(zones, arrs[s])]
        if s == 0:
            per_dev = [_unpack(zones[-1][k], small_shapes) for k in range(N_DEV)]
            for q, n in enumerate(SMALL_SHARDED):
                small[n] = _unshard(jnp.stack([per_dev[k][q] for k in range(N_DEV)]), -1)
        big = {n: _unshard(z, LAYER_AXIS[n]) for (n, _), z in zip(_stage_group(s), zones)}
        p = _stage_params(s, big, small)
        h, saved = _stage_fwd(s, h, p, order, mid)
        tape.append((p, saved))
        after = h

    dh, dh_b, d_final, loss_row = loss_head(h, _row(w["final_norm"]), tgt, "loss_head")

    out = {}
    small_g = {n: [None] * w[n].shape[0] for n in SMALL if n != "final_norm"}

    def finish(s, st, after):
        by_chip, recv = scatter_wait(st, after, f"scatter{s}_wait")
        for (n, l), own, r in zip(_stage_group(s), by_chip, recv):
            out[n] = adamw_layer(r, own, wv[n], mv[n], vv[n], l, out.get(n), f"adamw_{n}{l}")

    started, order, summed = [], [], None
    for s in reversed(range(N_STAGES)):
        dh, dh_b, g = _stage_bwd(s, dh, dh_b, *tape[s], order)
        for n, val in g.items():
            if n not in BIG:
                small_g[n][_stage_layer(s)[1]] = val.reshape(small[n].shape[1:])
        if s == 0:
            grads = {n: jnp.stack(vs) for n, vs in small_g.items()}
            grads["final_norm"] = d_final.reshape(D)
            small_full_shapes = [grads[n].shape for n in SMALL] + [(1,)]
            packed = _pack([grads[n] for n in SMALL] + [loss_row[0, :1]], _rows_for(small_full_shapes))
            summed = sum_leading(all_gather([packed], "gather_small_grads")[0], "sum_small_grads")
        gsend = [_to_shards(g[n], LAYER_AXIS[n]) for n, _ in _stage_group(s)]
        from_sibling = sibling_exchange(gsend, f"scatter{s}_sibling", summed)
        by_chip = [pair_sum(a, r, f"pair_sum_{n}{l}") for (n, l), a, r in zip(_stage_group(s), gsend, from_sibling)]
        started.append((s, scatter_start(by_chip, f"scatter{s}_start")))
        order = [started[-1][1]["token"]]
    last = started[-1][1]["token"]
    for s, st in started[:-1]:
        finish(s, st, last)
    parts = _unpack(summed + last[0:1, 0:1], small_full_shapes)
    loss = parts[-1][0]
    for n, g in zip(SMALL, parts[:-1]):
        if n in SMALL_SHARDED:
            s = w[n].shape[-1]
            g = lax.dynamic_slice_in_dim(g, idx * s, s, axis=g.ndim - 1)
        out[n] = adamw(g[None], w[n], m[n], v[n], f"adamw_{n}")
    done = [out[n][0].reshape(-1)[:1] for n in out]
    finish(*started[-1], functools.reduce(jnp.add, done))
    for n in TRANSPOSED:
        out[n] = [view(n, a) for a in out[n]]
    return loss, dh, out


def kernel(x, ssm_norm, ssm_w_in, ssm_conv_w, ssm_conv_b, ssm_dt_bias, ssm_a_log, ssm_d, ssm_gate_norm, ssm_w_out, cv_norm, cv_w_pw1, cv_b_pw1, cv_dw_w, cv_dw_b, cv_ln_g, cv_ln_b, cv_w_pw2, cv_b_pw2, ffn_norm, ffn_w_gate, ffn_w_up, ffn_w_down, final_norm, loss_target, m_ssm_norm, m_ssm_w_in, m_ssm_conv_w, m_ssm_conv_b, m_ssm_dt_bias, m_ssm_a_log, m_ssm_d, m_ssm_gate_norm, m_ssm_w_out, m_cv_norm, m_cv_w_pw1, m_cv_b_pw1, m_cv_dw_w, m_cv_dw_b, m_cv_ln_g, m_cv_ln_b, m_cv_w_pw2, m_cv_b_pw2, m_ffn_norm, m_ffn_w_gate, m_ffn_w_up, m_ffn_w_down, m_final_norm, v_ssm_norm, v_ssm_w_in, v_ssm_conv_w, v_ssm_conv_b, v_ssm_dt_bias, v_ssm_a_log, v_ssm_d, v_ssm_gate_norm, v_ssm_w_out, v_cv_norm, v_cv_w_pw1, v_cv_b_pw1, v_cv_dw_w, v_cv_dw_b, v_cv_ln_g, v_cv_ln_b, v_cv_w_pw2, v_cv_b_pw2, v_ffn_norm, v_ffn_w_gate, v_ffn_w_up, v_ffn_w_down, v_final_norm):
    args = locals()
    w = {n: args[n] for n in WEIGHTS}
    m = {n: args["m_" + n] for n in WEIGHTS}
    v = {n: args["v_" + n] for n in WEIGHTS}
    loss, grad_x, out = _step(x[0], loss_target[0], w, m, v)
    res = [loss, grad_x[None]]
    for k in range(4):
        res += [out[n][k] for n in WEIGHTS]
    return tuple(res)
```

```python
import functools
import math

import jax
import jax.numpy as jnp
from jax import lax
from jax.experimental import pallas as pl
from jax.experimental.pallas import tpu as pltpu

F32 = jnp.float32
BF16 = jnp.bfloat16

N_DEV = 8
T = 2048
D = 1024
DI = 2048
NH = 32
HD = 64
NG = 4
GW = DI // NG
DS = 128
CONVD = DI + 2 * NG * DS
DINP = 2 * DI + 2 * NG * DS + NH
DINP_PAD = 5376
CH = 128
NCH = T // CH
DFF = 2816
KSSM = 4
KCV = 31
EPS = 1e-5
LANES = 128
VMEM_LIMIT = 56 * 1024 * 1024

ADAM_LR = 0.001
ADAM_B1 = 0.9
ADAM_B2 = 0.999
ADAM_EPS = 1e-08
ADAM_WD = 0.01
ADAM_STEP = 10

MESH = pl.DeviceIdType.MESH
ANY = pl.BlockSpec(memory_space=pl.ANY)


def _pcall(body, **kw):
    return pl.pallas_call(body, **kw)


def _cparams(sem):
    return pltpu.CompilerParams(dimension_semantics=sem, vmem_limit_bytes=VMEM_LIMIT)


def _pick(n, cands):
    for c in cands:
        if n % c == 0:
            return c
    raise ValueError(f"no tile for {n}")


def _sigmoid(x):
    return 1.0 / (1.0 + jnp.exp(-x))


def _silu(x):
    return x * _sigmoid(x)


def _dsilu(x):
    s = _sigmoid(x)
    return s * (1.0 + x * (1.0 - s))


_DIMS = {"nn": (((1,), (0,)), ((), ())), "nt": (((1,), (1,)), ((), ())), "tn": (((0,), (0,)), ((), ()))}


MM_VMEM_BUDGET = 40 * 1024 * 1024
MM_MAX_K = 3072


def _mm_tiles(M, N, K, out_bytes, has_res):
    tk = K if K <= MM_MAX_K else K // 2
    assert K % tk == 0 and tk % LANES == 0
    nk = K // tk
    best = None
    for tm in (2048, 1792, 1408, 1024, 768, 512, 256, 128):
        if M % tm:
            continue
        for tn in (1408, 1024, 768, 512, 384, 256, 128):
            if N % tn:
                continue
            blocks = tm * tk * 2 + tk * tn * 2 + tm * tn * out_bytes + (tm * tn * 4 if has_res else 0)
            vmem = 2 * blocks + tm * tn * 4 * (2 if nk > 1 else 1)
            if vmem > MM_VMEM_BUDGET:
                continue
            traffic = (N // tn if nk > 1 else 1) * M * K + (M // tm) * N * K
            key = (-traffic, tm * tn)
            if best is None or key > best[0]:
                best = (key, tm, tn)
    assert best is not None, (M, N, K)
    return best[1], best[2], tk


def matmul(a, b, mode, *, name, bias=None, residual=None, out_dtype=F32):
    assert a.dtype == BF16 and b.dtype == BF16
    if mode == "nn":
        (M, K), (K2, N) = a.shape, b.shape
    elif mode == "nt":
        (M, K), (N, K2) = a.shape, b.shape
    else:
        (K, M), (K2, N) = a.shape, b.shape
    assert K == K2
    has_bias, has_res = bias is not None, residual is not None
    tm, tn, tk = _mm_tiles(M, N, K, jnp.dtype(out_dtype).itemsize, has_res)
    nk = K // tk
    dims = _DIMS[mode]

    def body(*refs):
        a_ref, b_ref = refs[0], refs[1]
        pos = 2
        bias_ref = res_ref = None
        if has_bias:
            bias_ref = refs[pos]
            pos += 1
        if has_res:
            res_ref = refs[pos]
            pos += 1
        o_ref = refs[pos]

        def finish(out):
            if has_bias:
                out = out + bias_ref[...]
            if has_res:
                out = out + res_ref[...]
            o_ref[...] = out.astype(o_ref.dtype)

        part = lax.dot_general(a_ref[...], b_ref[...], dims, preferred_element_type=F32)
        if nk == 1:
            finish(part)
            return
        acc = refs[pos + 1]
        k = pl.program_id(2)

        @pl.when(k == 0)
        def _():
            acc[...] = part

        @pl.when(jnp.logical_and(k > 0, k < nk - 1))
        def _():
            acc[...] += part

        @pl.when(k == nk - 1)
        def _():
            finish(acc[...] + part)

    if mode == "tn":
        a_spec = pl.BlockSpec((tk, tm), lambda i, j, k: (k, i))
    else:
        a_spec = pl.BlockSpec((tm, tk), lambda i, j, k: (i, k))
    if mode == "nt":
        b_spec = pl.BlockSpec((tn, tk), lambda i, j, k: (j, k))
    else:
        b_spec = pl.BlockSpec((tk, tn), lambda i, j, k: (k, j))
    in_specs, args = [a_spec, b_spec], [a, b]
    if has_bias:
        in_specs.append(pl.BlockSpec((1, tn), lambda i, j, k: (0, j)))
        args.append(bias.reshape(1, N).astype(F32))
    if has_res:
        in_specs.append(pl.BlockSpec((tm, tn), lambda i, j, k: (i, j)))
        args.append(residual)
    return _pcall(
        body, name=name, grid=(M // tm, N // tn, nk), in_specs=in_specs,
        out_specs=pl.BlockSpec((tm, tn), lambda i, j, k: (i, j)),
        out_shape=jax.ShapeDtypeStruct((M, N), out_dtype),
        scratch_shapes=[pltpu.VMEM((tm, tn), F32)] if nk > 1 else [],
        compiler_params=_cparams(("parallel", "parallel", "arbitrary")),
    )(*args)


def rowwise(fn, rows, bcasts, outs, accs=(), *, name, tm=256):
    n_rows, n_b, n_o, n_a = len(rows), len(bcasts), len(outs), len(accs)
    nt = T // tm

    def body(*refs):
        ins = [r[...] for r in refs[:n_rows + n_b]]
        res = fn(*ins)
        o_refs = refs[n_rows + n_b:n_rows + n_b + n_o]
        a_refs = refs[n_rows + n_b + n_o:]
        for r, v in zip(o_refs, res[:n_o]):
            r[...] = v.astype(r.dtype)
        if n_a:
            i = pl.program_id(0)

            @pl.when(i == 0)
            def _():
                for r in a_refs:
                    r[...] = jnp.zeros_like(r)

            for r, v in zip(a_refs, res[n_o:]):
                r[...] += v

    in_specs = [pl.BlockSpec((tm, w), functools.partial(lambda i, cb: (i, cb), cb=cb)) for (_, w, cb) in rows]
    in_specs += [pl.BlockSpec(b.shape, lambda i: (0, 0)) for b in bcasts]
    out_specs = [pl.BlockSpec((tm, w), lambda i: (i, 0)) for (w, _) in outs]
    out_specs += [pl.BlockSpec((1, w), lambda i: (0, 0)) for w in accs]
    out_shape = [jax.ShapeDtypeStruct((T, w), dt) for (w, dt) in outs]
    out_shape += [jax.ShapeDtypeStruct((1, w), F32) for w in accs]
    return _pcall(
        body, name=name, grid=(nt,), in_specs=in_specs, out_specs=out_specs, out_shape=out_shape,
        compiler_params=_cparams(("arbitrary",)),
    )(*[r[0] for r in rows], *bcasts)


def _full(a):
    return (a, a.shape[1], 0)


def _rsum(v):
    return jnp.sum(v, axis=0, keepdims=True)


def rms_fwd(h, g, name):
    def fn(x, g):
        r = lax.rsqrt(jnp.mean(x * x, axis=-1, keepdims=True) + EPS)
        return (x * r * g,)
    return rowwise(fn, [_full(h)], [g], [(D, BF16)], name=name)[0]


def rms_bwd(du, h, g, dres, name):
    def fn(du, x, dres, g):
        r = lax.rsqrt(jnp.mean(x * x, axis=-1, keepdims=True) + EPS)
        xh = x * r
        dxh = du * g
        dx = r * (dxh - xh * jnp.mean(dxh * xh, axis=-1, keepdims=True))
        dh = dres + dx
        return dh, dh, _rsum(du * xh)
    return rowwise(fn, [_full(du), _full(h), _full(dres)], [g], [(D, F32), (D, BF16)], [D], name=name)


def loss_head(h, g, tgt, name):
    def fn(x, tgt, g):
        r = lax.rsqrt(jnp.mean(x * x, axis=-1, keepdims=True) + EPS)
        xh = x * r
        err = xh * g - tgt
        lsum = jnp.sum(jnp.sum(err * err, axis=-1, keepdims=True), axis=0, keepdims=True) * (0.5 / D)
        dy = err * (1.0 / D)
        dxh = dy * g
        dx = r * (dxh - xh * jnp.mean(dxh * xh, axis=-1, keepdims=True))
        return dx, dx, _rsum(dy * xh), jnp.broadcast_to(lsum, (1, LANES))
    return rowwise(fn, [_full(h), _full(tgt)], [g], [(D, F32), (D, BF16)], [D, LANES], name=name)


def glu_fwd(hh, name):
    def fn(a, g):
        return (a * _sigmoid(g),)
    return rowwise(fn, [(hh, D, 0), (hh, D, 1)], [], [(D, F32)], name=name)[0]


def glu_bwd(dgl, hh, name):
    def fn(dgl, a, g):
        s = _sigmoid(g)
        dhh = jnp.concatenate([dgl * s, dgl * a * s * (1.0 - s)], axis=1)
        return dhh, _rsum(dhh)
    return rowwise(fn, [_full(dgl), (hh, D, 0), (hh, D, 1)], [], [(2 * D, BF16)], [2 * D], name=name)


def ln_silu_fwd(c2, g, b, name):
    def fn(x, g, b):
        mu = jnp.mean(x, axis=-1, keepdims=True)
        xc = x - mu
        r = lax.rsqrt(jnp.mean(xc * xc, axis=-1, keepdims=True) + EPS)
        return (_silu(xc * r * g + b),)
    return rowwise(fn, [_full(c2)], [g, b], [(D, BF16)], name=name)[0]


def ln_silu_bwd(ds, c2, dh, g, b, name):
    def fn(ds, x, dh, g, b):
        mu = jnp.mean(x, axis=-1, keepdims=True)
        xc = x - mu
        r = lax.rsqrt(jnp.mean(xc * xc, axis=-1, keepdims=True) + EPS)
        xh = xc * r
        dn = ds * _dsilu(xh * g + b)
        dxh = dn * g
        dx = r * (dxh - jnp.mean(dxh, axis=-1, keepdims=True) - xh * jnp.mean(dxh * xh, axis=-1, keepdims=True))
        return dx, _rsum(dn * xh), _rsum(dn), _rsum(dh)
    return rowwise(fn, [_full(ds), _full(c2), _full(dh)], [g, b], [(D, F32)], [D, D, D], name=name)


def gatenorm_fwd(y, zx, gn, name):
    def fn(y, z, gn):
        hg = y * _silu(z)
        parts = []
        for k in range(NG):
            hk = hg[:, k * GW:(k + 1) * GW]
            parts.append(hk * lax.rsqrt(jnp.mean(hk * hk, axis=-1, keepdims=True) + EPS))
        return (jnp.concatenate(parts, axis=1) * gn,)
    return rowwise(fn, [_full(y), (zx, DI, 0)], [gn], [(DI, BF16)], name=name)[0]


def gatenorm_bwd(dyn, y, zx, gn, name):
    def fn(dyn, y, z, gn):
        sz = _silu(z)
        hg = y * sz
        dxh = dyn * gn
        dhg, xhs = [], []
        for k in range(NG):
            sl = slice(k * GW, (k + 1) * GW)
            hk = hg[:, sl]
            r = lax.rsqrt(jnp.mean(hk * hk, axis=-1, keepdims=True) + EPS)
            xh = hk * r
            dk = dxh[:, sl]
            dhg.append(r * (dk - xh * jnp.mean(dk * xh, axis=-1, keepdims=True)))
            xhs.append(xh)
        dhg = jnp.concatenate(dhg, axis=1)
        xh = jnp.concatenate(xhs, axis=1)
        return dhg * sz, dhg * y * _dsilu(z), _rsum(dyn * xh)
    return rowwise(fn, [_full(dyn), _full(y), (zx, DI, 0)], [gn], [(DI, F32), (DI, BF16)], [DI], name=name)


def _softplus(x):
    return jnp.maximum(x, 0.0) + jnp.log(1.0 + jnp.exp(-jnp.abs(x)))


def _spread(v, e):
    hi = v.astype(BF16)
    r = v - hi.astype(F32)
    mid = r.astype(BF16)
    lo = (r - mid.astype(F32)).astype(BF16)
    return _dot(hi, e) + _dot(mid, e) + _dot(lo, e)


def _spread2(v, e):
    hi = v.astype(BF16)
    lo = (v - hi.astype(F32)).astype(BF16)
    return _dot(hi, e) + _dot(lo, e)


def dt_fwd(zx, dt_bias, a_log, name):
    heads = (jnp.arange(DI)[None, :] // HD == jnp.arange(LANES)[:, None]).astype(BF16)

    def fn(raw, bias, a_log, e):
        dt = _softplus(raw + bias)
        da = dt * (-jnp.exp(a_log))
        return dt, da, _spread(dt, e), _spread(da, e)

    return rowwise(fn, [(zx, LANES, (2 * DI + 2 * NG * DS) // LANES)], [dt_bias, a_log, heads],
                   [(LANES, F32), (LANES, F32), (DI, F32), (DI, F32)], name=name)


def dt_bwd(ddt, dda, dt, zx, dt_bias, a_log, name):
    def fn(ddt, dda, dt, raw, bias, a_log):
        a = -jnp.exp(a_log)
        draw = (ddt + dda * a) * _sigmoid(raw + bias)
        return draw, _rsum(draw), _rsum(dda * dt) * a
    return rowwise(fn, [_full(ddt), _full(dda), _full(dt), (zx, LANES, (2 * DI + 2 * NG * DS) // LANES)],
                   [dt_bias, a_log], [(LANES, BF16)], [LANES, LANES], name=name)


def headsum(v, name):
    def body(v_ref, o_ref):
        o_ref[...] = jnp.sum(v_ref[...], axis=1, keepdims=True)
    return _pcall(body, name=name, out_shape=jax.ShapeDtypeStruct((v.shape[0], 1), F32))(v)


CONV_ROWS = 256


def _shifted(win, o, rows):
    if o == 0:
        return win[0:rows]
    n = win.shape[0]
    return pltpu.roll(win, shift=n - o, axis=0)[0:rows]


def dwconv_fwd(x, x_cb0, w, b, K, ct, act, name):
    C = w.shape[1]
    pad = 8 if K <= 8 else 32
    KP = w.shape[0]
    n_out = 2 if act else 1

    def body(x_ref, w_ref, b_ref, *rest):
        o_refs, px = rest[:n_out], rest[n_out]
        px[0:pad, :] = jnp.zeros((pad, ct), F32)
        px[pad:pad + T, :] = x_ref[...]
        wv = w_ref[...]
        bv = b_ref[...]
        for r0 in range(0, T, CONV_ROWS):
            win = px[r0:r0 + CONV_ROWS + pad, :]
            acc = jnp.broadcast_to(bv, (CONV_ROWS, ct))
            for k in range(K):
                acc = acc + wv[k:k + 1, :] * _shifted(win, pad - (K - 1) + k, CONV_ROWS)
            o_refs[0][r0:r0 + CONV_ROWS, :] = acc
            if act:
                o_refs[1][r0:r0 + CONV_ROWS, :] = _silu(acc)

    return _pcall(
        body, name=name, grid=(C // ct,),
        in_specs=[pl.BlockSpec((T, ct), lambda j: (0, x_cb0 + j)), pl.BlockSpec((KP, ct), lambda j: (0, j)),
                  pl.BlockSpec((1, ct), lambda j: (0, j))],
        out_specs=[pl.BlockSpec((T, ct), lambda j: (0, j))] * n_out,
        out_shape=[jax.ShapeDtypeStruct((T, C), F32)] * n_out,
        scratch_shapes=[pltpu.VMEM((T + pad, ct), F32)],
        compiler_params=_cparams(("parallel",)),
    )(x, w, b)


def dwconv_bwd(dout, cpre, x, x_cb0, w, K, ct, act, out_dtype, name):
    C = w.shape[1]
    pad = 8 if K <= 8 else 32
    KP = w.shape[0]

    def body(*refs):
        if act:
            d_ref, c_ref, x_ref, w_ref, dx_ref, dw_ref, db_ref, px, pd = refs
        else:
            d_ref, x_ref, w_ref, dx_ref, dw_ref, db_ref, px, pd = refs
        px[0:pad, :] = jnp.zeros((pad, ct), F32)
        px[pad:pad + T, :] = x_ref[...]
        pd[T:T + pad, :] = jnp.zeros((pad, ct), F32)
        if act:
            pd[0:T, :] = d_ref[...] * _dsilu(c_ref[...])
        else:
            pd[0:T, :] = d_ref[...]
        wv = w_ref[...]
        dws = [jnp.zeros((1, ct), F32) for _ in range(K)]
        db = jnp.zeros((1, ct), F32)
        for r0 in range(0, T, CONV_ROWS):
            dwin = pd[r0:r0 + CONV_ROWS + pad, :]
            xwin = px[r0:r0 + CONV_ROWS + pad, :]
            dc = dwin[0:CONV_ROWS]
            db = db + _rsum(dc)
            acc = jnp.zeros((CONV_ROWS, ct), F32)
            for k in range(K):
                acc = acc + wv[k:k + 1, :] * _shifted(dwin, K - 1 - k, CONV_ROWS)
                dws[k] = dws[k] + _rsum(dc * _shifted(xwin, pad - (K - 1) + k, CONV_ROWS))
            dx_ref[r0:r0 + CONV_ROWS, :] = acc.astype(dx_ref.dtype)
        dw_ref[...] = jnp.zeros((KP, ct), F32)
        for k in range(K):
            dw_ref[k:k + 1, :] = dws[k]
        db_ref[...] = db

    col = pl.BlockSpec((T, ct), lambda j: (0, j))
    in_specs = [col] + ([col] if act else []) + [pl.BlockSpec((T, ct), lambda j: (0, x_cb0 + j)),
                                                 pl.BlockSpec((KP, ct), lambda j: (0, j))]
    args = [dout] + ([cpre] if act else []) + [x, w]
    return _pcall(
        body, name=name, grid=(C // ct,), in_specs=in_specs,
        out_specs=[col, pl.BlockSpec((KP, ct), lambda j: (0, j)), pl.BlockSpec((1, ct), lambda j: (0, j))],
        out_shape=[jax.ShapeDtypeStruct((T, C), out_dtype), jax.ShapeDtypeStruct((KP, C), F32),
                   jax.ShapeDtypeStruct((1, C), F32)],
        scratch_shapes=[pltpu.VMEM((T + pad, ct), F32), pltpu.VMEM((T + pad, ct), F32)],
        compiler_params=_cparams(("parallel",)),
    )(*args)


def _scan(a, axis, reverse=False):
    n = a.shape[axis]
    idx = lax.broadcasted_iota(jnp.int32, a.shape, axis)
    s = 1
    while s < n:
        if reverse:
            a = a + jnp.where(idx < n - s, pltpu.roll(a, shift=n - s, axis=axis), 0.0)
        else:
            a = a + jnp.where(idx >= s, pltpu.roll(a, shift=s, axis=axis), 0.0)
        s *= 2
    return a


_NT = _DIMS["nt"]
_TN = _DIMS["tn"]


def _dot(a, b, dims=_DIMS["nn"]):
    return lax.dot_general(a, b, dims, preferred_element_type=F32)


def ssd_fwd(xbc, dtx, dax, daT, dfull, name):
    def body(xbc_ref, dtx_ref, dax_ref, daT_ref, df_ref, y_ref, st_ref, S):
        ci = pl.program_id(0)

        @pl.when(ci == 0)
        def _():
            S[...] = jnp.zeros_like(S)

        row = lax.broadcasted_iota(jnp.int32, (CH, CH), 0)
        lane = lax.broadcasted_iota(jnp.int32, (CH, CH), 1)
        acsT = _scan(daT_ref[...], 1)
        for g in range(NG):
            c0 = g * GW
            xs = xbc_ref[:, c0:c0 + GW]
            acs = _scan(dax_ref[:, c0:c0 + GW], 0)
            Bm = xbc_ref[:, DI + g * DS:DI + (g + 1) * DS].astype(BF16)
            Cm = xbc_ref[:, DI + NG * DS + g * DS:DI + NG * DS + (g + 1) * DS].astype(BF16)
            xdt = xs * dtx_ref[:, c0:c0 + GW]
            atot = acs[CH - 1:CH, :]
            Sg = S[:, c0:c0 + GW]
            st_ref[:, c0:c0 + GW] = Sg
            CB = _dot(Cm, Bm, _NT)
            yg = jnp.exp(acs) * _dot(Cm, Sg.astype(BF16)) + xs * df_ref[:, c0:c0 + GW]
            xd = (xdt * jnp.exp(atot - acs)).astype(BF16)
            S[:, c0:c0 + GW] = jnp.exp(atot) * Sg + _dot(Bm, xd, _TN)
            xdt_b = xdt.astype(BF16)
            for r in range(NH // NG):
                h = g * (NH // NG) + r
                hs = slice(r * HD, (r + 1) * HD)
                seg = acs[:, r * HD:r * HD + 1] - acsT[h:h + 1, :]
                Lm = jnp.where(row >= lane, jnp.exp(jnp.minimum(seg, 0.0)), 0.0)
                yd = _dot((CB * Lm).astype(BF16), xdt_b[:, hs])
                y_ref[:, c0 + r * HD:c0 + (r + 1) * HD] = yg[:, hs] + yd

    return _pcall(
        body, name=name, grid=(NCH,),
        in_specs=[pl.BlockSpec((CH, CONVD), lambda i: (i, 0)), pl.BlockSpec((CH, DI), lambda i: (i, 0)),
                  pl.BlockSpec((CH, DI), lambda i: (i, 0)), pl.BlockSpec((NH, CH), lambda i: (0, i)),
                  pl.BlockSpec((1, DI), lambda i: (0, 0))],
        out_specs=[pl.BlockSpec((CH, DI), lambda i: (i, 0)), pl.BlockSpec((None, DS, DI), lambda i: (i, 0, 0))],
        out_shape=[jax.ShapeDtypeStruct((T, DI), F32), jax.ShapeDtypeStruct((NCH, DS, DI), F32)],
        scratch_shapes=[pltpu.VMEM((DS, DI), F32)],
        compiler_params=_cparams(("arbitrary",)),
    )(xbc, dtx, dax, daT, dfull)


def ssd_bwd(dy, xbc, dtx, dax, daT, dfull, states, name):
    hsum = (jnp.arange(DI)[:, None] // HD == jnp.arange(LANES)[None, :]).astype(BF16).reshape(NG, GW, LANES)

    def body(dy_ref, xbc_ref, dtx_ref, dax_ref, daT_ref, df_ref, st_ref, hsum_ref, dxbc_ref, ddt_ref, dda_ref, dD_ref, dS):
        i = pl.program_id(0)

        @pl.when(i == 0)
        def _():
            dS[...] = jnp.zeros_like(dS)
            dD_ref[...] = jnp.zeros_like(dD_ref)

        row = lax.broadcasted_iota(jnp.int32, (CH, CH), 0)
        lane = lax.broadcasted_iota(jnp.int32, (CH, CH), 1)
        acsT = _scan(daT_ref[...], 1)
        ddt_all = jnp.zeros((CH, LANES), F32)
        dacs_all = jnp.zeros((CH, LANES), F32)
        colacc = jnp.zeros((CH, CH), F32)
        for g in range(NG):
            c0 = g * GW
            xs = xbc_ref[:, c0:c0 + GW]
            dtx = dtx_ref[:, c0:c0 + GW]
            acs = _scan(dax_ref[:, c0:c0 + GW], 0)
            Bm = xbc_ref[:, DI + g * DS:DI + (g + 1) * DS].astype(BF16)
            Cm = xbc_ref[:, DI + NG * DS + g * DS:DI + NG * DS + (g + 1) * DS].astype(BF16)
            xdt = xs * dtx
            atot = acs[CH - 1:CH, :]
            Sin = st_ref[:, c0:c0 + GW]
            dyg = dy_ref[:, c0:c0 + GW]
            dSo = dS[:, c0:c0 + GW]
            E = jnp.exp(acs)
            Etot = jnp.exp(atot)
            dec = jnp.exp(atot - acs)
            dD_ref[:, c0:c0 + GW] += _rsum(dyg * xs)
            dxs = dyg * df_ref[:, c0:c0 + GW]
            Sin_b = Sin.astype(BF16)
            dSo_b = dSo.astype(BF16)
            dY0 = dyg * E
            dY0_b = dY0.astype(BF16)
            dC = _dot(dY0_b, Sin_b, _NT)
            dS[:, c0:c0 + GW] = _dot(Cm, dY0_b, _TN) + Etot * dSo
            XD = xdt * dec
            dXD = _dot(Bm, dSo_b)
            dB = _dot(XD.astype(BF16), dSo_b, _NT)
            dxdt = dXD * dec
            Gq = dXD * XD
            dacs_x = dY0 * _dot(Cm, Sin_b) - Gq
            datot_x = _rsum(dSo * Sin) * Etot + _rsum(Gq)
            dacs_all = dacs_all + _spread2(dacs_x, hsum_ref[g])
            dtot8 = _spread2(jnp.broadcast_to(datot_x, (8, GW)), hsum_ref[g])
            dacs_all = dacs_all + jnp.where(row == CH - 1, jnp.broadcast_to(dtot8[0:1, :], (CH, LANES)), 0.0)
            CB = _dot(Cm, Bm, _NT)
            dCB = jnp.zeros((CH, CH), F32)
            xdt_b = xdt.astype(BF16)
            dy_b = dyg.astype(BF16)
            for r in range(NH // NG):
                h = g * (NH // NG) + r
                hs = slice(r * HD, (r + 1) * HD)
                seg = acs[:, r * HD:r * HD + 1] - acsT[h:h + 1, :]
                Lm = jnp.where(row >= lane, jnp.exp(jnp.minimum(seg, 0.0)), 0.0)
                dyr = dy_b[:, hs]
                dML = _dot(dyr, xdt_b[:, hs], _NT) * Lm
                dxbc_ref[:, c0 + r * HD:c0 + (r + 1) * HD] = _dot((CB * Lm).astype(BF16), dyr, _TN)
                dCB = dCB + dML
                dseg = dML * CB
                dacs_all = dacs_all + _spread2(dseg, (lane == h).astype(BF16))
                colacc = colacc + jnp.where(row == h, jnp.sum(dseg, axis=0, keepdims=True), 0.0)
            dxdt = dxdt + dxbc_ref[:, c0:c0 + GW]
            ddt_all = ddt_all + _spread2(dxdt * xs, hsum_ref[g])
            dxbc_ref[:, c0:c0 + GW] = dxs + dxdt * dtx
            dCB_b = dCB.astype(BF16)
            dxbc_ref[:, DI + g * DS:DI + (g + 1) * DS] = dB + _dot(dCB_b, Cm, _TN)
            dxbc_ref[:, DI + NG * DS + g * DS:DI + NG * DS + (g + 1) * DS] = dC + _dot(dCB_b, Bm)
        ddt_ref[...] = ddt_all
        dda_ref[...] = _scan(dacs_all - colacc.T, 0, reverse=True)

    last = NCH - 1
    return _pcall(
        body, name=name, grid=(NCH,),
        in_specs=[pl.BlockSpec((CH, DI), lambda i: (last - i, 0)), pl.BlockSpec((CH, CONVD), lambda i: (last - i, 0)),
                  pl.BlockSpec((CH, DI), lambda i: (last - i, 0)), pl.BlockSpec((CH, DI), lambda i: (last - i, 0)),
                  pl.BlockSpec((NH, CH), lambda i: (0, last - i)), pl.BlockSpec((1, DI), lambda i: (0, 0)),
                  pl.BlockSpec((None, DS, DI), lambda i: (last - i, 0, 0)),
                  pl.BlockSpec((NG, GW, LANES), lambda i: (0, 0, 0))],
        out_specs=[pl.BlockSpec((CH, CONVD), lambda i: (last - i, 0)), pl.BlockSpec((CH, LANES), lambda i: (last - i, 0)),
                   pl.BlockSpec((CH, LANES), lambda i: (last - i, 0)), pl.BlockSpec((1, DI), lambda i: (0, 0))],
        out_shape=[jax.ShapeDtypeStruct((T, CONVD), F32), jax.ShapeDtypeStruct((T, LANES), F32),
                   jax.ShapeDtypeStruct((T, LANES), F32), jax.ShapeDtypeStruct((1, DI), F32)],
        scratch_shapes=[pltpu.VMEM((DS, DI), F32)],
        compiler_params=_cparams(("arbitrary",)),
    )(dy, xbc, dtx, dax, daT, dfull, states, hsum)


def _as3d(shape):
    if len(shape) == 1:
        return (1, 1, shape[0])
    if len(shape) == 2:
        return (1, shape[0], shape[1])
    return (math.prod(shape[:-2]), shape[-2], shape[-1])


def _row_tile(R, C):
    if R * C <= 512 * 1024:
        return R
    return next((t for t in (512, 256, 128, 64, 32, 16) if R % t == 0), R)


def adamw(parts, w, m, v, name):
    shape = w.shape
    L, R, C = _as3d(shape)
    P = parts.shape[0]
    tr = _row_tile(R, C)
    bc1 = 1.0 - ADAM_B1 ** ADAM_STEP
    bc2 = 1.0 - ADAM_B2 ** ADAM_STEP

    def body(p_ref, w_ref, m_ref, v_ref, g_out, d_out, m_out, v_out):
        g = p_ref[0].astype(F32)
        for k in range(1, P):
            g = g + p_ref[k].astype(F32)
        mn = ADAM_B1 * m_ref[...] + (1.0 - ADAM_B1) * g
        vn = ADAM_B2 * v_ref[...] + (1.0 - ADAM_B2) * (g * g)
        g_out[...] = g
        m_out[...] = mn
        v_out[...] = vn
        d_out[...] = -ADAM_LR * ((mn / bc1) / (jnp.sqrt(vn / bc2) + ADAM_EPS) + ADAM_WD * w_ref[...])

    blk = pl.BlockSpec((None, tr, C), lambda l, r: (l, r, 0))
    outs = _pcall(
        body, name=name, grid=(L, R // tr),
        in_specs=[pl.BlockSpec((P, None, tr, C), lambda l, r: (0, l, r, 0)), blk, blk, blk],
        out_specs=[blk] * 4, out_shape=[jax.ShapeDtypeStruct((L, R, C), F32)] * 4,
        compiler_params=_cparams(("parallel", "parallel")),
    )(parts.reshape(P, L, R, C), w.reshape(L, R, C), m.reshape(L, R, C), v.reshape(L, R, C))
    return [o.reshape(shape) for o in outs]


ADAMW_STEPS = 4


def adamw_stage(recvs, owns, ws, ms, vs, layers, prevs, name):
    n = len(ws)
    chained = prevs[0] is not None
    assert all((p is not None) == chained for p in prevs)
    bc1 = 1.0 - ADAM_B1 ** ADAM_STEP
    bc2 = 1.0 - ADAM_B2 ** ADAM_STEP
    n_in = (9 if chained else 5) * n

    def body(*refs):
        for a in range(n):
            r_ref, o_ref, w_ref, m_ref, v_ref = refs[5 * a:5 * a + 5]
            g_out, d_out, m_out, v_out = refs[n_in + 4 * a:n_in + 4 * a + 4]
            g = o_ref[...].astype(F32)
            for k in range(r_ref.shape[0]):
                g = g + r_ref[k].astype(F32)
            mn = ADAM_B1 * m_ref[...] + (1.0 - ADAM_B1) * g
            vn = ADAM_B2 * v_ref[...] + (1.0 - ADAM_B2) * (g * g)
            g_out[...] = g
            m_out[...] = mn
            v_out[...] = vn
            d_out[...] = -ADAM_LR * ((mn / bc1) / (jnp.sqrt(vn / bc2) + ADAM_EPS) + ADAM_WD * w_ref[...])

    in_specs, args, out_specs, out_shape = [], [], [], []
    for a in range(n):
        _, R, C = ws[a].shape
        tr = R // ADAMW_STEPS
        assert tr * ADAMW_STEPS == R and tr % 8 == 0
        slot = pl.BlockSpec((None, tr, C), functools.partial(lambda r, l: (l, r, 0), l=layers[a]))
        own = pl.BlockSpec((None, tr, C), lambda r: (2 * lax.axis_index("x") + lax.axis_index("y"), r, 0))
        in_specs += [pl.BlockSpec((recvs[a].shape[0], tr, C), lambda r: (0, r, 0)), own, slot, slot, slot]
        args += [recvs[a], owns[a], ws[a], ms[a], vs[a]]
        out_specs += [slot] * 4
        out_shape += [jax.ShapeDtypeStruct(ws[a].shape, F32)] * 4
    aliases = {}
    if chained:
        for a in range(n):
            in_specs += [ANY] * 4
            args += list(prevs[a])
            aliases.update({5 * n + 4 * a + k: 4 * a + k for k in range(4)})
    outs = _pcall(
        body, name=name, grid=(ADAMW_STEPS,), in_specs=in_specs, out_specs=out_specs, out_shape=out_shape,
        input_output_aliases=aliases, compiler_params=_cparams(("parallel",)),
    )(*args)
    return [list(outs[4 * a:4 * a + 4]) for a in range(n)]


def sum_leading(parts, name):
    P, R, C = parts.shape

    def body(p_ref, o_ref):
        s = p_ref[0]
        for k in range(1, P):
            s = s + p_ref[k]
        o_ref[...] = s

    return _pcall(body, name=name, out_shape=jax.ShapeDtypeStruct((R, C), F32))(parts)


def pair_sum(gsends, recvs, name):
    n = len(gsends)

    def body(*refs):
        for g_ref, r_ref, o_ref in zip(refs[:n], refs[n:2 * n], refs[2 * n:]):
            o_ref[...] = (g_ref[...].astype(F32) + r_ref[...].astype(F32)).astype(o_ref.dtype)

    def slot(a):
        return pl.BlockSpec((None,) + a.shape[1:], lambda q: (q, 0, 0))

    def own(a):
        return pl.BlockSpec((None,) + a.shape[1:], lambda q: (2 * q + lax.axis_index("c"), 0, 0))

    return _pcall(
        body, name=name, grid=(4,), in_specs=[own(g) for g in gsends] + [slot(r) for r in recvs],
        out_specs=[slot(r) for r in recvs], out_shape=[jax.ShapeDtypeStruct(r.shape, BF16) for r in recvs],
        compiler_params=_cparams(("parallel",)),
    )(*gsends, *recvs)


def _place():
    return lax.axis_index("x"), lax.axis_index("y"), lax.axis_index("c")


def _other_chips(x, y):
    return [(1 - x, y), (x, 1 - y), (1 - x, 1 - y)]


def all_gather(arrs, name):
    n = len(arrs)

    def body(*refs):
        ins, outs = refs[:n], refs[n:2 * n]
        send_sems, recv_sems, local_sems = refs[2 * n:]
        x, y, c = _place()
        me, sibling = (x, y, c), (x, y, 1 - c)
        chips = _other_chips(x, y)

        def slot(a, px, py, pc):
            return outs[a].at[4 * px + 2 * py + pc]

        def copy(a, k, block, to, src=None):
            return pltpu.make_async_remote_copy(
                src_ref=slot(a, *block) if src is None else src, dst_ref=slot(a, *block),
                send_sem=send_sems.at[a, k], recv_sem=recv_sems.at[a, k], device_id=to, device_id_type=MESH)

        mine, first, passed = [], [], []
        for a in range(n):
            cp = pltpu.make_async_copy(ins[a], slot(a, *me), local_sems.at[a])
            cp.start()
            mine.append(cp)
            first.append(copy(a, 0, me, sibling, src=ins[a]))
            first += [copy(a, 1 + j, me, (*chip, c), src=ins[a]) for j, chip in enumerate(chips)]
        for cp in first:
            cp.start()
        for j, chip in enumerate(chips):
            for a in range(n):
                copy(a, 1 + j, (*chip, c), me).wait_recv()
                cp = copy(a, 4 + j, (*chip, c), sibling)
                cp.start()
                passed.append(cp)
        for a in range(n):
            copy(a, 0, sibling, me).wait_recv()
            for j, chip in enumerate(chips):
                copy(a, 4 + j, (*chip, 1 - c), me).wait_recv()
        for cp in first + passed:
            cp.wait_send()
        for cp in mine:
            cp.wait()

    return _pcall(
        body, name=name, in_specs=[ANY] * n, out_specs=[ANY] * n,
        out_shape=[jax.ShapeDtypeStruct((N_DEV,) + a.shape, a.dtype) for a in arrs],
        scratch_shapes=[pltpu.SemaphoreType.DMA((n, 7)), pltpu.SemaphoreType.DMA((n, 7)), pltpu.SemaphoreType.DMA((n,))],
    )(*arrs)


def sibling_exchange(gsends, name, after=None):
    n = len(gsends)
    n_in = n + (1 if after is not None else 0)

    def body(*refs):
        ins, outs = refs[:n], refs[n_in:n_in + n]
        send_sems, recv_sems = refs[n_in + n:]
        x, y, c = _place()
        copies = []
        for a in range(n):
            for q in range(4):
                cp = pltpu.make_async_remote_copy(
                    src_ref=ins[a].at[2 * q + 1 - c], dst_ref=outs[a].at[q],
                    send_sem=send_sems.at[a, q], recv_sem=recv_sems.at[a, q],
                    device_id=(x, y, 1 - c), device_id_type=MESH)
                cp.start()
                copies.append(cp)
        for cp in copies:
            cp.wait()

    return _pcall(
        body, name=name, in_specs=[ANY] * n_in, out_specs=[ANY] * n,
        out_shape=[jax.ShapeDtypeStruct((4,) + g.shape[1:], g.dtype) for g in gsends],
        scratch_shapes=[pltpu.SemaphoreType.DMA((n, 4)), pltpu.SemaphoreType.DMA((n, 4))],
    )(*gsends, *([after] if after is not None else []))


HBM =pl.BlockSpec(memory_space=pltpu.HBM)
SEM = pl.BlockSpec(memory_space=pltpu.SEMAPHORE)
EFFECT = pltpu.SideEffectType.DATAFLOW_SIDE_EFFECTING


def _in_hbm(a):
    return pltpu.with_memory_space_constraint(a, pltpu.HBM)


def _gather_peers(x, y, c):
    to = [(x, y, 1 - c)] + [(px, py, c) for px, py in _other_chips(x, y)]
    return to, [4 * px + 2 * py + pc for px, py, pc in to]


def gather_start(arrs, after, name):
    n = len(arrs)
    n_in = 2 * n + (1 if after is not None else 0)

    def body(*refs):
        srcs, lands = refs[:n], refs[n:2 * n]
        send_sems, recv_sems = refs[n_in], refs[n_in + 1]
        token = refs[-1]
        x, y, c = _place()
        to, _ = _gather_peers(x, y, c)
        me = 4 * x + 2 * y + c
        for a in range(n):
            for k, dev in enumerate(to):
                pltpu.make_async_remote_copy(
                    src_ref=srcs[a], dst_ref=lands[a].at[me], send_sem=send_sems.at[4 * a + k], recv_sem=recv_sems.at[4 * a + k],
                    device_id=dev, device_id_type=MESH).start()
        token[...] = jnp.zeros_like(token)

    zones = [lax.empty((N_DEV,) + a.shape, a.dtype) for a in arrs]
    args = [_in_hbm(a) for a in arrs] + [_in_hbm(z) for z in zones] + ([after] if after is not None else [])
    outs = _pcall(
        body, name=name,
        out_shape=(pltpu.SemaphoreType.DMA((4 * n,)), pltpu.SemaphoreType.DMA((4 * n,)),
                   *[pltpu.HBM(a.shape, a.dtype) for a in arrs], *[pltpu.HBM(z.shape, z.dtype) for z in zones],
                   jax.ShapeDtypeStruct((8, LANES), F32)),
        in_specs=[HBM] * (2 * n) + ([ANY] if after is not None else []),
        out_specs=(SEM, SEM, *[HBM] * (2 * n), pl.BlockSpec(memory_space=pltpu.VMEM)),
        input_output_aliases={i: 2 + i for i in range(2 * n)},
        compiler_params=pltpu.CompilerParams(has_side_effects=EFFECT),
    )(*args)
    return dict(send=outs[0], recv=outs[1], srcs=list(outs[2:2 + n]), lands=list(outs[2 + n:2 + 2 * n]), token=outs[-1])


def gather_wait(st, after, name):
    n = len(st["srcs"])

    def body(*refs):
        srcs, lands = refs[:n], refs[n:2 * n]
        send_sems, recv_sems = refs[2 * n], refs[2 * n + 1]
        x, y, c = _place()
        to, slots = _gather_peers(x, y, c)
        for a in range(n):
            for k, dev in enumerate(to):
                cp = pltpu.make_async_remote_copy(
                    src_ref=srcs[a], dst_ref=lands[a].at[slots[k]], send_sem=send_sems.at[4 * a + k],
                    recv_sem=recv_sems.at[4 * a + k], device_id=dev, device_id_type=MESH)
                cp.wait_send()
                cp.wait_recv()

    outs = _pcall(
        body, name=name,
        out_shape=(*[pltpu.HBM(a.shape, a.dtype) for a in st["srcs"]], *[pltpu.HBM(z.shape, z.dtype) for z in st["lands"]]),
        in_specs=[HBM] * (2 * n) + [SEM, SEM, ANY], out_specs=tuple([HBM] * (2 * n)),
        input_output_aliases={i: i for i in range(2 * n)},
        compiler_params=pltpu.CompilerParams(has_side_effects=EFFECT),
    )(*st["srcs"], *st["lands"], st["send"], st["recv"], after)
    return list(outs[n:])


def pass_start(zones, name):
    n = len(zones)

    def body(*refs):
        zs = refs[:n]
        send_sems, recv_sems = refs[n], refs[n + 1]
        token = refs[-1]
        x, y, c = _place()
        for a in range(n):
            for j, (px, py) in enumerate(_other_chips(x, y)):
                blk = zs[a].at[4 * px + 2 * py + c]
                pltpu.make_async_remote_copy(
                    src_ref=blk, dst_ref=blk, send_sem=send_sems.at[3 * a + j], recv_sem=recv_sems.at[3 * a + j],
                    device_id=(x, y, 1 - c), device_id_type=MESH).start()
        token[...] = jnp.zeros_like(token)

    outs = _pcall(
        body, name=name,
        out_shape=(pltpu.SemaphoreType.DMA((3 * n,)), pltpu.SemaphoreType.DMA((3 * n,)),
                   *[pltpu.HBM(z.shape, z.dtype) for z in zones], jax.ShapeDtypeStruct((8, LANES), F32)),
        in_specs=[HBM] * n, out_specs=(SEM, SEM, *[HBM] * n, pl.BlockSpec(memory_space=pltpu.VMEM)),
        input_output_aliases={i: 2 + i for i in range(n)},
        compiler_params=pltpu.CompilerParams(has_side_effects=EFFECT),
    )(*zones)
    return dict(send=outs[0], recv=outs[1], zones=list(outs[2:2 + n]), token=outs[-1])


def pass_wait(st, after, name):
    n = len(st["zones"])

    def body(*refs):
        zs = refs[:n]
        send_sems, recv_sems = refs[n], refs[n + 1]
        x, y, c = _place()
        for a in range(n):
            for j, (px, py) in enumerate(_other_chips(x, y)):
                cp = pltpu.make_async_remote_copy(
                    src_ref=zs[a].at[4 * px + 2 * py + c], dst_ref=zs[a].at[4 * px + 2 * py + 1 - c],
                    send_sem=send_sems.at[3 * a + j], recv_sem=recv_sems.at[3 * a + j],
                    device_id=(x, y, 1 - c), device_id_type=MESH)
                cp.wait_send()
                cp.wait_recv()

    outs = _pcall(
        body, name=name, out_shape=tuple(pltpu.HBM(z.shape, z.dtype) for z in st["zones"]),
        in_specs=[HBM] * n + [SEM, SEM, ANY], out_specs=tuple([HBM] * n),
        input_output_aliases={i: i for i in range(n)},
        compiler_params=pltpu.CompilerParams(has_side_effects=EFFECT),
    )(*st["zones"], st["send"], st["recv"], after)
    return list(outs)


def scatter_start(parts, name):
    n = len(parts)

    def body(*refs):
        srcs, lands = refs[:n], refs[n:2 * n]
        send_sems, recv_sems = refs[2 * n], refs[2 * n + 1]
        token = refs[-1]
        x, y, c = _place()
        for a in range(n):
            for j, (px, py) in enumerate(_other_chips(x, y)):
                pltpu.make_async_remote_copy(
                    src_ref=srcs[a].at[2 * px + py], dst_ref=lands[a].at[j], send_sem=send_sems.at[3 * a + j],
                    recv_sem=recv_sems.at[3 * a + j], device_id=(px, py, c), device_id_type=MESH).start()
        token[...] = jnp.zeros_like(token)

    zones = [lax.empty((3,) + p.shape[1:], p.dtype) for p in parts]
    outs = _pcall(
        body, name=name,
        out_shape=(pltpu.SemaphoreType.DMA((3 * n,)), pltpu.SemaphoreType.DMA((3 * n,)),
                   *[pltpu.HBM(p.shape, p.dtype) for p in parts], *[pltpu.HBM(z.shape, z.dtype) for z in zones],
                   jax.ShapeDtypeStruct((8, LANES), F32)),
        in_specs=[HBM] * (2 * n), out_specs=(SEM, SEM, *[HBM] * (2 * n), pl.BlockSpec(memory_space=pltpu.VMEM)),
        input_output_aliases={i: 2 + i for i in range(2 * n)},
        compiler_params=pltpu.CompilerParams(has_side_effects=EFFECT),
    )(*[_in_hbm(p) for p in parts], *[_in_hbm(z) for z in zones])
    return dict(send=outs[0], recv=outs[1], srcs=list(outs[2:2 + n]), lands=list(outs[2 + n:2 + 2 * n]), token=outs[-1])


def scatter_wait(st, after, name):
    n = len(st["srcs"])

    def body(*refs):
        srcs, lands = refs[:n], refs[n:2 * n]
        send_sems, recv_sems = refs[2 * n], refs[2 * n + 1]
        x, y, c = _place()
        for a in range(n):
            for j, (px, py) in enumerate(_other_chips(x, y)):
                cp = pltpu.make_async_remote_copy(
                    src_ref=srcs[a].at[2 * px + py], dst_ref=lands[a].at[j], send_sem=send_sems.at[3 * a + j],
                    recv_sem=recv_sems.at[3 * a + j], device_id=(px, py, c), device_id_type=MESH)
                cp.wait_send()
                cp.wait_recv()

    outs = _pcall(
        body, name=name,
        out_shape=(*[pltpu.HBM(a.shape, a.dtype) for a in st["srcs"]], *[pltpu.HBM(z.shape, z.dtype) for z in st["lands"]]),
        in_specs=[HBM] * (2 * n) + [SEM, SEM, ANY], out_specs=tuple([HBM] * (2 * n)),
        input_output_aliases={i: i for i in range(2 * n)},
        compiler_params=pltpu.CompilerParams(has_side_effects=EFFECT),
    )(*st["srcs"], *st["lands"], st["send"], st["recv"], after)
    return list(outs[:n]), list(outs[n:])


def _unshard(g, axis):
    nd = g.ndim - 1
    axis = axis % nd
    t = jnp.moveaxis(g, 0, axis)
    shp = list(g.shape[1:])
    shp[axis] *= N_DEV
    return t.reshape(shp)


def _to_shards(full, axis):
    axis = axis % full.ndim
    shp = list(full.shape)
    shp[axis:axis + 1] = [N_DEV, shp[axis] // N_DEV]
    return jnp.moveaxis(full.reshape(shp), axis, 0)


def _pack(arrs, rows):
    flat = jnp.concatenate([a.reshape(-1).astype(F32) for a in arrs])
    return jnp.pad(flat, (0, rows * LANES - flat.shape[0])).reshape(rows, LANES)


def _unpack(buf, shapes):
    flat = buf.reshape(-1)
    out, off = [], 0
    for s in shapes:
        n = math.prod(s)
        out.append(flat[off:off + n].reshape(s))
        off += n
    return out


def _rows_for(shapes):
    n = sum(math.prod(s) for s in shapes)
    return -(-n // (8 * LANES)) * 8


def _row(v, width=None):
    v = v.reshape(1, -1).astype(F32)
    if width is not None and v.shape[1] < width:
        v = jnp.pad(v, ((0, 0), (0, width - v.shape[1])))
    return v


def _after(order, width):
    if not order:
        return None
    t = order[0][0:1, 0:1]
    for o in order[1:]:
        t = t + o[0:1, 0:1]
    return jnp.broadcast_to(t, (1, width))


def _norm_after(norm, order):
    row = _after(order, norm.shape[1])
    return norm if row is None else norm + row


FFN_TN = 256


def ffn_in(h, norm, w_gate, w_up, name):
    def body(h_ref, n_ref, wg_ref, wu_ref, u_ref, g_ref, up_ref, act_ref, u_s):
        @pl.when(pl.program_id(0) == 0)
        def _():
            x = h_ref[...]
            r = lax.rsqrt(jnp.mean(x * x, axis=-1, keepdims=True) + EPS)
            u_s[...] = (x * r * n_ref[...]).astype(BF16)
            u_ref[...] = u_s[...]

        u = u_s[...]
        g = _dot(u, wg_ref[...], _NT)
        up = _dot(u, wu_ref[...], _NT)
        g_ref[...] = g.astype(BF16)
        up_ref[...] = up.astype(BF16)
        act_ref[...] = (_silu(g) * up).astype(BF16)

    whole = pl.BlockSpec((T, D), lambda j: (0, 0))
    wspec = pl.BlockSpec((FFN_TN, D), lambda j: (j, 0))
    col = pl.BlockSpec((T, FFN_TN), lambda j: (0, j))
    return _pcall(
        body, name=name, grid=(DFF // FFN_TN,), in_specs=[whole, pl.BlockSpec((1, D), lambda j: (0, 0)), wspec, wspec],
        out_specs=[whole, col, col, col],
        out_shape=[jax.ShapeDtypeStruct((T, D), BF16)] + [jax.ShapeDtypeStruct((T, DFF), BF16)] * 3,
        scratch_shapes=[pltpu.VMEM((T, D), BF16)], compiler_params=_cparams(("arbitrary",)),
    )(h, norm, w_gate, w_up)


def ffn_back(dh_b, w_down, g, up, after_row, name):
    has_row = after_row is not None

    def body(*refs):
        dh_ref, wd_ref, g_ref, up_ref = refs[:4]
        dg_ref, dup_ref = refs[-2:]
        da = _dot(dh_ref[...], wd_ref[...], _NT)
        if has_row:
            da = da + refs[4][...]
        g = g_ref[...].astype(F32)
        dg_ref[...] = (da * up_ref[...].astype(F32) * _dsilu(g)).astype(BF16)
        dup_ref[...] = (da * _silu(g)).astype(BF16)

    col = pl.BlockSpec((T, FFN_TN), lambda j: (0, j))
    in_specs = [pl.BlockSpec((T, D), lambda j: (0, 0)), pl.BlockSpec((FFN_TN, D), lambda j: (j, 0)), col, col]
    args = [dh_b, w_down, g, up]
    if has_row:
        in_specs.append(pl.BlockSpec((1, FFN_TN), lambda j: (0, j)))
        args.append(after_row)
    return _pcall(
        body, name=name, grid=(DFF // FFN_TN,), in_specs=in_specs, out_specs=[col, col],
        out_shape=[jax.ShapeDtypeStruct((T, DFF), BF16)] * 2, compiler_params=_cparams(("parallel",)),
    )(*args)


def ffn_layer_fwd(h, p, tag, order=()):
    u, g, up, act = ffn_in(h, _norm_after(p["norm"], order), p["w_gate"], p["w_up"], f"{tag}_in")
    h2 = matmul(act, p["w_down"], "nn", residual=h, name=f"{tag}_down")
    return h2, (h, u, g, up, act)


def ffn_layer_bwd(dh, dh_b, saved, p, tag, order=()):
    h, u, g, up, act = saved
    d_down = matmul(act, dh_b, "tn", out_dtype=BF16, name=f"{tag}_dwd")
    dg, dup = ffn_back(dh_b, p["w_down"], g, up, _after(order, DFF), f"{tag}_back")
    du = matmul(dg, p["w_gate"], "nn", name=f"{tag}_dug")
    du = matmul(dup, p["w_up"], "nn", residual=du, name=f"{tag}_duu")
    d_gate = matmul(dg, u, "tn", out_dtype=BF16, name=f"{tag}_dwg")
    d_up = matmul(dup, u, "tn", out_dtype=BF16, name=f"{tag}_dwu")
    dh2, dh2_b, d_norm = rms_bwd(du, h, p["norm"], dh, f"{tag}_drms")
    return dh2, dh2_b, dict(norm=d_norm, w_gate=d_gate, w_up=d_up, w_down=d_down)


def conv_layer_fwd(h, p, tag, order=()):
    u = rms_fwd(h, _norm_after(p["norm"], order), f"{tag}_rms")
    hh = matmul(u, p["w_pw1"], "nn", bias=p["b_pw1"], name=f"{tag}_pw1")
    gl = glu_fwd(hh, f"{tag}_glu")
    c2 = dwconv_fwd(gl, 0, p["dw_w"], p["dw_b"], KCV, 128, False, f"{tag}_dw")[0]
    s = ln_silu_fwd(c2, p["ln_g"], p["ln_b"], f"{tag}_ln")
    h2 = matmul(s, p["w_pw2"], "nn", bias=p["b_pw2"], residual=h, name=f"{tag}_pw2")
    return h2, (h, u, hh, gl, c2, s)


def conv_layer_bwd(dh, dh_b, saved, p, tag, order=()):
    h, u, hh, gl, c2, s = saved
    ds = matmul(dh_b, p["w_pw2"], "nt", bias=_after(order, D), name=f"{tag}_ds")
    d_pw2 = matmul(s, dh_b, "tn", out_dtype=BF16, name=f"{tag}_dwpw2")
    dc2, d_lng, d_lnb, d_bpw2 = ln_silu_bwd(ds, c2, dh, p["ln_g"], p["ln_b"], f"{tag}_dln")
    dgl, d_dww, d_dwb = dwconv_bwd(dc2, None, gl, 0, p["dw_w"], KCV, 128, False, F32, f"{tag}_ddw")
    dhh, d_bpw1 = glu_bwd(dgl, hh, f"{tag}_dglu")
    du = matmul(dhh, p["w_pw1"], "nt", name=f"{tag}_du")
    d_pw1 = matmul(u, dhh, "tn", out_dtype=BF16, name=f"{tag}_dwpw1")
    dh2, dh2_b, d_norm = rms_bwd(du, h, p["norm"], dh, f"{tag}_drms")
    grads = dict(norm=d_norm, w_pw1=d_pw1, b_pw1=d_bpw1, dw_w=d_dww[:KCV], dw_b=d_dwb, ln_g=d_lng, ln_b=d_lnb,
                 w_pw2=d_pw2, b_pw2=d_bpw2)
    return dh2, dh2_b, grads


def ssm_layer_fwd(h, p, tag, order=(), mid=None):
    u = rms_fwd(h, _norm_after(p["norm"], order), f"{tag}_rms")
    zx = matmul(u, p["w_in"], "nn", name=f"{tag}_in")
    cpre, xbc = dwconv_fwd(zx, DI // 512, p["conv_w"], p["conv_b"], KSSM, 512, True, f"{tag}_conv")
    dt, da, dtx, dax = dt_fwd(zx, p["dt_bias"], p["a_log"], f"{tag}_dt")
    daT = da[:, :NH].T
    y, states = ssd_fwd(xbc, dtx, dax, daT, p["d_full"], f"{tag}_ssd")
    gate_norm = p["gate_norm"] if mid is None else _norm_after(p["gate_norm"], mid(y))
    yn = gatenorm_fwd(y, zx, gate_norm, f"{tag}_gn")
    h2 = matmul(yn, p["w_out"], "nn", residual=h, name=f"{tag}_out")
    return h2, (h, u, zx, cpre, xbc, dt, dtx, dax, daT, y, states, yn)


def ssm_layer_bwd(dh, dh_b, saved, p, tag, order=()):
    h, u, zx, cpre, xbc, dt, dtx, dax, daT, y, states, yn = saved
    dyn = matmul(dh_b, p["w_out"], "nt", bias=_after(order, DI), name=f"{tag}_dyn")
    d_wout = matmul(yn, dh_b, "tn", out_dtype=BF16, name=f"{tag}_dwout")
    dy, dz, d_gn = gatenorm_bwd(dyn, y, zx, p["gate_norm"], f"{tag}_dgn")
    dxbc, ddt, dda, dD = ssd_bwd(dy, xbc, dtx, dax, daT, p["d_full"], states, f"{tag}_dssd")
    draw, d_dtb, d_alog = dt_bwd(ddt, dda, dt, zx, p["dt_bias"], p["a_log"], f"{tag}_ddt")
    dxpre, d_cw, d_cb = dwconv_bwd(dxbc, cpre, zx, DI // 512, p["conv_w"], KSSM, 512, True, BF16, f"{tag}_dconv")
    dzx = jnp.concatenate([dz, dxpre, draw, jnp.zeros((T, DINP_PAD - 2 * DI - 2 * NG * DS - LANES), BF16)], axis=1)
    du = matmul(dzx, p["w_in"], "nt", name=f"{tag}_du")
    d_win = matmul(u, dzx, "tn", out_dtype=BF16, name=f"{tag}_dwin")
    dh2, dh2_b, d_norm = rms_bwd(du, h, p["norm"], dh, f"{tag}_drms")
    d_d = headsum(dD.reshape(NH, HD), f"{tag}_dD").reshape(NH)
    grads = dict(norm=d_norm, w_in=d_win[:, :DINP], conv_w=d_cw[:KSSM], conv_b=d_cb, dt_bias=d_dtb[0, :NH],
                 a_log=d_alog[0, :NH], d=d_d, gate_norm=d_gn, w_out=d_wout)
    return dh2, dh2_b, grads


BIG = ["ssm_w_in", "ssm_w_out", "cv_w_pw1", "cv_w_pw2", "ffn_w_gate", "ffn_w_up", "ffn_w_down"]
TRANSPOSED = ("ffn_w_gate", "ffn_w_up")
LAYER_AXIS = {"ssm_w_in": -1, "ssm_w_out": 0, "cv_w_pw1": -1, "cv_w_pw2": 0, "ffn_w_gate": 0, "ffn_w_up": 0,
              "ffn_w_down": 0}
SMALL_SHARDED = ["ssm_conv_w", "cv_norm", "cv_b_pw1", "cv_dw_w", "cv_dw_b", "cv_ln_g", "cv_ln_b", "cv_b_pw2"]
SMALL_REPL = ["ssm_norm", "ssm_conv_b", "ssm_dt_bias", "ssm_a_log", "ssm_d", "ssm_gate_norm", "ffn_norm", "final_norm"]
WEIGHTS = ["ssm_norm", "ssm_w_in", "ssm_conv_w", "ssm_conv_b", "ssm_dt_bias", "ssm_a_log", "ssm_d", "ssm_gate_norm",
           "ssm_w_out", "cv_norm", "cv_w_pw1", "cv_b_pw1", "cv_dw_w", "cv_dw_b", "cv_ln_g", "cv_ln_b", "cv_w_pw2",
           "cv_b_pw2", "ffn_norm", "ffn_w_gate", "ffn_w_up", "ffn_w_down", "final_norm"]
SMALL = [n for n in WEIGHTS if n not in BIG]


N_STAGES = 8


def _stage_layer(s):
    i = s // 2
    if s % 2:
        return "ffn", i
    return ("ssm" if i % 2 == 0 else "cv"), i // 2


def _stage_group(s):
    fam, l = _stage_layer(s)
    names = {"ffn": ["ffn_w_gate", "ffn_w_up", "ffn_w_down"], "ssm": ["ssm_w_in", "ssm_w_out"],
             "cv": ["cv_w_pw1", "cv_w_pw2"]}[fam]
    return [(n, l) for n in names]


def _stage_params(s, big, small):
    fam, l = _stage_layer(s)
    if fam == "ffn":
        return dict(norm=_row(small["ffn_norm"][l]), w_gate=big["ffn_w_gate"], w_up=big["ffn_w_up"],
                    w_down=big["ffn_w_down"])
    if fam == "ssm":
        return dict(norm=_row(small["ssm_norm"][l]), w_in=jnp.pad(big["ssm_w_in"], ((0, 0), (0, DINP_PAD - DINP))),
                    conv_w=jnp.pad(small["ssm_conv_w"][l], ((0, 8 - KSSM), (0, 0))), conv_b=_row(small["ssm_conv_b"][l]),
                    dt_bias=_row(small["ssm_dt_bias"][l], LANES), a_log=_row(small["ssm_a_log"][l], LANES),
                    d_full=_row(jnp.repeat(small["ssm_d"][l], HD)), gate_norm=_row(small["ssm_gate_norm"][l]),
                    w_out=big["ssm_w_out"])
    return dict(norm=_row(small["cv_norm"][l]), w_pw1=big["cv_w_pw1"], b_pw1=_row(small["cv_b_pw1"][l]),
                dw_w=jnp.pad(small["cv_dw_w"][l], ((0, 32 - KCV), (0, 0))), dw_b=_row(small["cv_dw_b"][l]),
                ln_g=_row(small["cv_ln_g"][l]), ln_b=_row(small["cv_ln_b"][l]), w_pw2=big["cv_w_pw2"],
                b_pw2=_row(small["cv_b_pw2"][l]))


_STAGE_FWD = {"ffn": ffn_layer_fwd, "ssm": ssm_layer_fwd, "cv": conv_layer_fwd}
_STAGE_BWD = {"ffn": ffn_layer_bwd, "ssm": ssm_layer_bwd, "cv": conv_layer_bwd}


def _stage_fwd(s, h, p, order=(), mid=None):
    fam, l = _stage_layer(s)
    if mid is not None:
        return ssm_layer_fwd(h, p, f"{fam}{l}", order, mid)
    return _STAGE_FWD[fam](h, p, f"{fam}{l}", order)


def _stage_bwd(s, dh, dh_b, p, saved, order=()):
    fam, l = _stage_layer(s)
    dh, dh_b, g = _STAGE_BWD[fam](dh, dh_b, saved, p, f"{fam}{l}", order)
    return dh, dh_b, {f"{fam}_{k}": val for k, val in g.items()}


def _local(x, tgt, full):
    h, tape = x, []
    for s in range(N_STAGES):
        big = {n: (full[n][l].T if n in TRANSPOSED else full[n][l]) for n, l in _stage_group(s)}
        p = _stage_params(s, big, full)
        h, saved = _stage_fwd(s, h, p)
        tape.append((p, saved))
    dh, dh_b, d_final, loss_row = loss_head(h, _row(full["final_norm"]), tgt, "loss_head")
    gl = {n: [None] * full[n].shape[0] for n in WEIGHTS if n != "final_norm"}
    for s in reversed(range(N_STAGES)):
        dh, dh_b, g = _stage_bwd(s, dh, dh_b, *tape[s])
        for n, val in g.items():
            val = val.T if n in TRANSPOSED else val
            gl[n][_stage_layer(s)[1]] = val.reshape(full[n].shape[1:])
    grads = {n: jnp.stack(vs) for n, vs in gl.items()}
    grads["final_norm"] = d_final.reshape(D)
    return loss_row, dh, grads


def _step(x, tgt, w, m, v):
    idx = 4 * lax.axis_index("x") + 2 * lax.axis_index("y") + lax.axis_index("c")
    small_shapes = [w[n].shape for n in SMALL_SHARDED]
    small_pack = _pack([w[n] for n in SMALL_SHARDED], _rows_for(small_shapes))

    def view(n, a):
        return jnp.swapaxes(a, 1, 2) if n in TRANSPOSED else a

    wv, mv, vv = ({n: view(n, t[n]) for n in BIG} for t in (w, m, v))

    def blocks(s):
        return [wv[n][l].astype(BF16) for n, l in _stage_group(s)] + ([small_pack] if s == 0 else [])

    arrs = [blocks(s) for s in range(N_STAGES)]
    first = gather_start(arrs[0], None, "gather0_start")
    passing = pass_start(gather_wait(first, first["token"], "gather0_wait"), "pass0_start")
    crossing = {1: gather_start(arrs[1], passing["token"], "gather1_start")}
    crossing[2] = gather_start(arrs[2], crossing[1]["token"], "gather2_start")
    small = {n: w[n] for n in SMALL_REPL}
    flight = dict(passing=passing)

    def advance(s, after):
        tokens = []
        if s + 1 < N_STAGES:
            landed = gather_wait(crossing.pop(s + 1), after, f"gather{s + 1}_wait")
            flight["passing"] = pass_start(landed, f"pass{s + 1}_start")
            tokens.append(flight["passing"]["token"])
        if s + 3 < N_STAGES:
            crossing[s + 3] = gather_start(arrs[s + 3], flight["passing"]["token"], f"gather{s + 3}_start")
            tokens.append(crossing[s + 3]["token"])
        return tokens

    h, tape, after = x, [], crossing[2]["token"]
    for s in range(N_STAGES):
        zones = pass_wait(flight["passing"], after, f"pass{s}_wait")
        order = advance(s, zones[0]) if s else []
        mid = functools.partial(advance, 0) if s == 0 else None
        zones = [lax.dynamic_update_slice_in_dim(z, a[None], idx, 0) for z, a in zip(zones, arrs[s])]
        if s == 0:
            per_dev = [_unpack(zones[-1][k], small_shapes) for k in range(N_DEV)]
            for q, n in enumerate(SMALL_SHARDED):
                small[n] = _unshard(jnp.stack([per_dev[k][q] for k in range(N_DEV)]), -1)
        big = {n: _unshard(z, LAYER_AXIS[n]) for (n, _), z in zip(_stage_group(s), zones)}
        p = _stage_params(s, big, small)
        h, saved = _stage_fwd(s, h, p, order, mid)
        tape.append((p, saved))
        after = h

    dh, dh_b, d_final, loss_row = loss_head(h, _row(w["final_norm"]), tgt, "loss_head")

    out = {}
    small_g = {n: [None] * w[n].shape[0] for n in SMALL if n != "final_norm"}

    def finish(s, st, after):
        by_chip, recv = scatter_wait(st, after, f"scatter{s}_wait")
        names = [n for n, _ in _stage_group(s)]
        res = adamw_stage(recv, by_chip, [wv[n] for n in names], [mv[n] for n in names], [vv[n] for n in names],
                          [l for _, l in _stage_group(s)], [out.get(n) for n in names], f"adamw_stage{s}")
        out.update(zip(names, res))

    started, order, summed = [], [], None
    for s in reversed(range(N_STAGES)):
        dh, dh_b, g = _stage_bwd(s, dh, dh_b, *tape[s], order)
        for n, val in g.items():
            if n not in BIG:
                small_g[n][_stage_layer(s)[1]] = val.reshape(small[n].shape[1:])
        if s == 0:
            grads = {n: jnp.stack(vs) for n, vs in small_g.items()}
            grads["final_norm"] = d_final.reshape(D)
            small_full_shapes = [grads[n].shape for n in SMALL] + [(1,)]
            packed = _pack([grads[n] for n in SMALL] + [loss_row[0, :1]], _rows_for(small_full_shapes))
            summed = sum_leading(all_gather([packed], "gather_small_grads")[0], "sum_small_grads")
        gsend = [_to_shards(g[n], LAYER_AXIS[n]) for n, _ in _stage_group(s)]
        from_sibling = sibling_exchange(gsend, f"scatter{s}_sibling", summed)
        by_chip = pair_sum(gsend, from_sibling, f"pair_sum{s}")
        started.append((s, scatter_start(by_chip, f"scatter{s}_start")))
        order = [started[-1][1]["token"]]
    last = started[-1][1]["token"]
    for s, st in started[:-1]:
        finish(s, st, last)
    parts = _unpack(summed + last[0:1, 0:1], small_full_shapes)
    loss = parts[-1][0]
    for n, g in zip(SMALL, parts[:-1]):
        if n in SMALL_SHARDED:
            s = w[n].shape[-1]
            g = lax.dynamic_slice_in_dim(g, idx * s, s, axis=g.ndim - 1)
        out[n] = adamw(g[None], w[n], m[n], v[n], f"adamw_{n}")
    done = [out[n][0].reshape(-1)[:1] for n in out]
    finish(*started[-1], functools.reduce(jnp.add, done))
    for n in TRANSPOSED:
        out[n] = [view(n, a) for a in out[n]]
    return loss, dh, out


def kernel(x, ssm_norm, ssm_w_in, ssm_conv_w, ssm_conv_b, ssm_dt_bias, ssm_a_log, ssm_d, ssm_gate_norm, ssm_w_out, cv_norm, cv_w_pw1, cv_b_pw1, cv_dw_w, cv_dw_b, cv_ln_g, cv_ln_b, cv_w_pw2, cv_b_pw2, ffn_norm, ffn_w_gate, ffn_w_up, ffn_w_down, final_norm, loss_target, m_ssm_norm, m_ssm_w_in, m_ssm_conv_w, m_ssm_conv_b, m_ssm_dt_bias, m_ssm_a_log, m_ssm_d, m_ssm_gate_norm, m_ssm_w_out, m_cv_norm, m_cv_w_pw1, m_cv_b_pw1, m_cv_dw_w, m_cv_dw_b, m_cv_ln_g, m_cv_ln_b, m_cv_w_pw2, m_cv_b_pw2, m_ffn_norm, m_ffn_w_gate, m_ffn_w_up, m_ffn_w_down, m_final_norm, v_ssm_norm, v_ssm_w_in, v_ssm_conv_w, v_ssm_conv_b, v_ssm_dt_bias, v_ssm_a_log, v_ssm_d, v_ssm_gate_norm, v_ssm_w_out, v_cv_norm, v_cv_w_pw1, v_cv_b_pw1, v_cv_dw_w, v_cv_dw_b, v_cv_ln_g, v_cv_ln_b, v_cv_w_pw2, v_cv_b_pw2, v_ffn_norm, v_ffn_w_gate, v_ffn_w_up, v_ffn_w_down, v_final_norm):
    args = locals()
    w = {n: args[n] for n in WEIGHTS}
    m = {n: args["m_" + n] for n in WEIGHTS}
    v = {n: args["v_" + n] for n in WEIGHTS}
    loss, grad_x, out = _step(x[0], loss_target[0], w, m, v)
    res = [loss, grad_x[None]]
    for k in range(4):
        res += [out[n][k] for n in WEIGHTS]
    return tuple(res)
```

```python
import functools
import math

import jax
import jax.numpy as jnp
from jax import lax
from jax.experimental import pallas as pl
from jax.experimental.pallas import tpu as pltpu

F32 = jnp.float32
BF16 = jnp.bfloat16

N_DEV = 8
T = 2048
D = 1024
DI = 2048
NH = 32
HD = 64
NG = 4
GW = DI // NG
DS = 128
CONVD = DI + 2 * NG * DS
DINP = 2 * DI + 2 * NG * DS + NH
DINP_PAD = 5376
CH = 128
NCH = T // CH
DFF = 2816
KSSM = 4
KCV = 31
EPS = 1e-5
LANES = 128
VMEM_LIMIT = 56 * 1024 * 1024

ADAM_LR = 0.001
ADAM_B1 = 0.9
ADAM_B2 = 0.999
ADAM_EPS = 1e-08
ADAM_WD = 0.01
ADAM_STEP = 10

MESH = pl.DeviceIdType.MESH
ANY = pl.BlockSpec(memory_space=pl.ANY)


def _pcall(body, **kw):
    return pl.pallas_call(body, **kw)


def _cparams(sem):
    return pltpu.CompilerParams(dimension_semantics=sem, vmem_limit_bytes=VMEM_LIMIT)


def _pick(n, cands):
    for c in cands:
        if n % c == 0:
            return c
    raise ValueError(f"no tile for {n}")


def _sigmoid(x):
    return 1.0 / (1.0 + jnp.exp(-x))


def _silu(x):
    return x * _sigmoid(x)


def _dsilu(x):
    s = _sigmoid(x)
    return s * (1.0 + x * (1.0 - s))


_DIMS = {"nn": (((1,), (0,)), ((), ())), "nt": (((1,), (1,)), ((), ())), "tn": (((0,), (0,)), ((), ()))}


MM_VMEM_BUDGET = 40 * 1024 * 1024
MM_MAX_K = 3072


def _mm_tiles(M, N, K, out_bytes, has_res):
    tk = K if K <= MM_MAX_K else K // 2
    assert K % tk == 0 and tk % LANES == 0
    nk = K // tk
    best = None
    for tm in (2048, 1792, 1408, 1024, 768, 512, 256, 128):
        if M % tm:
            continue
        for tn in (1408, 1024, 768, 512, 384, 256, 128):
            if N % tn:
                continue
            blocks = tm * tk * 2 + tk * tn * 2 + tm * tn * out_bytes + (tm * tn * 4 if has_res else 0)
            vmem = 2 * blocks + tm * tn * 4 * (2 if nk > 1 else 1)
            if vmem > MM_VMEM_BUDGET:
                continue
            traffic = (N // tn if nk > 1 else 1) * M * K + (M // tm) * N * K
            key = (-traffic, tm * tn)
            if best is None or key > best[0]:
                best = (key, tm, tn)
    assert best is not None, (M, N, K)
    return best[1], best[2], tk


def matmul(a, b, mode, *, name, bias=None, residual=None, out_dtype=F32):
    assert a.dtype == BF16 and b.dtype == BF16
    if mode == "nn":
        (M, K), (K2, N) = a.shape, b.shape
    elif mode == "nt":
        (M, K), (N, K2) = a.shape, b.shape
    else:
        (K, M), (K2, N) = a.shape, b.shape
    assert K == K2
    has_bias, has_res = bias is not None, residual is not None
    tm, tn, tk = _mm_tiles(M, N, K, jnp.dtype(out_dtype).itemsize, has_res)
    nk = K // tk
    dims = _DIMS[mode]

    def body(*refs):
        a_ref, b_ref = refs[0], refs[1]
        pos = 2
        bias_ref = res_ref = None
        if has_bias:
            bias_ref = refs[pos]
            pos += 1
        if has_res:
            res_ref = refs[pos]
            pos += 1
        o_ref = refs[pos]

        def finish(out):
            if has_bias:
                out = out + bias_ref[...]
            if has_res:
                out = out + res_ref[...]
            o_ref[...] = out.astype(o_ref.dtype)

        part = lax.dot_general(a_ref[...], b_ref[...], dims, preferred_element_type=F32)
        if nk == 1:
            finish(part)
            return
        acc = refs[pos + 1]
        k = pl.program_id(2)

        @pl.when(k == 0)
        def _():
            acc[...] = part

        @pl.when(jnp.logical_and(k > 0, k < nk - 1))
        def _():
            acc[...] += part

        @pl.when(k == nk - 1)
        def _():
            finish(acc[...] + part)

    if mode == "tn":
        a_spec = pl.BlockSpec((tk, tm), lambda i, j, k: (k, i))
    else:
        a_spec = pl.BlockSpec((tm, tk), lambda i, j, k: (i, k))
    if mode == "nt":
        b_spec = pl.BlockSpec((tn, tk), lambda i, j, k: (j, k))
    else:
        b_spec = pl.BlockSpec((tk, tn), lambda i, j, k: (k, j))
    in_specs, args = [a_spec, b_spec], [a, b]
    if has_bias:
        in_specs.append(pl.BlockSpec((1, tn), lambda i, j, k: (0, j)))
        args.append(bias.reshape(1, N).astype(F32))
    if has_res:
        in_specs.append(pl.BlockSpec((tm, tn), lambda i, j, k: (i, j)))
        args.append(residual)
    return _pcall(
        body, name=name, grid=(M // tm, N // tn, nk), in_specs=in_specs,
        out_specs=pl.BlockSpec((tm, tn), lambda i, j, k: (i, j)),
        out_shape=jax.ShapeDtypeStruct((M, N), out_dtype),
        scratch_shapes=[pltpu.VMEM((tm, tn), F32)] if nk > 1 else [],
        compiler_params=_cparams(("parallel", "parallel", "arbitrary")),
    )(*args)


def rowwise(fn, rows, bcasts, outs, accs=(), *, name, tm=256, fill=None):
    n_rows, n_b, n_o, n_a = len(rows), len(bcasts), len(outs), len(accs)
    n_in = n_rows + n_b + (1 if fill is not None else 0)
    outs = [o if len(o) == 4 else (o[0], o[1], o[0], 0) for o in outs]
    nt = T // tm

    def body(*refs):
        ins = [r[...] for r in refs[:n_rows + n_b]]
        res = fn(*ins)
        o_refs = refs[n_in:n_in + n_o]
        a_refs = refs[n_in + n_o:]
        for r, v in zip(o_refs, res[:n_o]):
            r[...] = v.astype(r.dtype)
        if n_a:
            i = pl.program_id(0)

            @pl.when(i == 0)
            def _():
                for r in a_refs:
                    r[...] = jnp.zeros_like(r)

            for r, v in zip(a_refs, res[n_o:]):
                r[...] += v

    in_specs = [pl.BlockSpec((tm, w), functools.partial(lambda i, cb: (i, cb), cb=cb)) for (_, w, cb) in rows]
    in_specs += [pl.BlockSpec(b.shape, lambda i: (0, 0)) for b in bcasts]
    out_specs = [pl.BlockSpec((tm, w), functools.partial(lambda i, cb: (i, cb), cb=cb)) for (w, _, _, cb) in outs]
    out_specs += [pl.BlockSpec((1, w), lambda i: (0, 0)) for w in accs]
    out_shape = [jax.ShapeDtypeStruct((T, whole), dt) for (_, dt, whole, _) in outs]
    out_shape += [jax.ShapeDtypeStruct((1, w), F32) for w in accs]
    args = [r[0] for r in rows] + list(bcasts)
    aliases = {}
    if fill is not None:
        in_specs.append(ANY)
        args.append(fill[0])
        aliases = {n_in - 1: fill[1]}
    return _pcall(
        body, name=name, grid=(nt,), in_specs=in_specs, out_specs=out_specs, out_shape=out_shape,
        input_output_aliases=aliases, compiler_params=_cparams(("arbitrary",)),
    )(*args)


def _full(a):
    return (a, a.shape[1], 0)


def _rsum(v):
    return jnp.sum(v, axis=0, keepdims=True)


def rms_fwd(h, g, name):
    def fn(x, g):
        r = lax.rsqrt(jnp.mean(x * x, axis=-1, keepdims=True) + EPS)
        return (x * r * g,)
    return rowwise(fn, [_full(h)], [g], [(D, BF16)], name=name)[0]


def rms_bwd(du, h, g, dres, name):
    def fn(du, x, dres, g):
        r = lax.rsqrt(jnp.mean(x * x, axis=-1, keepdims=True) + EPS)
        xh = x * r
        dxh = du * g
        dx = r * (dxh - xh * jnp.mean(dxh * xh, axis=-1, keepdims=True))
        dh = dres + dx
        return dh, dh, _rsum(du * xh)
    return rowwise(fn, [_full(du), _full(h), _full(dres)], [g], [(D, F32), (D, BF16)], [D], name=name)


def loss_head(h, g, tgt, name):
    def fn(x, tgt, g):
        r = lax.rsqrt(jnp.mean(x * x, axis=-1, keepdims=True) + EPS)
        xh = x * r
        err = xh * g - tgt
        lsum = jnp.sum(jnp.sum(err * err, axis=-1, keepdims=True), axis=0, keepdims=True) * (0.5 / D)
        dy = err * (1.0 / D)
        dxh = dy * g
        dx = r * (dxh - xh * jnp.mean(dxh * xh, axis=-1, keepdims=True))
        return dx, dx, _rsum(dy * xh), jnp.broadcast_to(lsum, (1, LANES))
    return rowwise(fn, [_full(h), _full(tgt)], [g], [(D, F32), (D, BF16)], [D, LANES], name=name)


def glu_fwd(hh, name):
    def fn(a, g):
        return (a * _sigmoid(g),)
    return rowwise(fn, [(hh, D, 0), (hh, D, 1)], [], [(D, F32)], name=name)[0]


def glu_bwd(dgl, hh, name):
    def fn(dgl, a, g):
        s = _sigmoid(g)
        dhh = jnp.concatenate([dgl * s, dgl * a * s * (1.0 - s)], axis=1)
        return dhh, _rsum(dhh)
    return rowwise(fn, [_full(dgl), (hh, D, 0), (hh, D, 1)], [], [(2 * D, BF16)], [2 * D], name=name)


def ln_silu_fwd(c2, g, b, name):
    def fn(x, g, b):
        mu = jnp.mean(x, axis=-1, keepdims=True)
        xc = x - mu
        r = lax.rsqrt(jnp.mean(xc * xc, axis=-1, keepdims=True) + EPS)
        return (_silu(xc * r * g + b),)
    return rowwise(fn, [_full(c2)], [g, b], [(D, BF16)], name=name)[0]


def ln_silu_bwd(ds, c2, dh, g, b, name):
    def fn(ds, x, dh, g, b):
        mu = jnp.mean(x, axis=-1, keepdims=True)
        xc = x - mu
        r = lax.rsqrt(jnp.mean(xc * xc, axis=-1, keepdims=True) + EPS)
        xh = xc * r
        dn = ds * _dsilu(xh * g + b)
        dxh = dn * g
        dx = r * (dxh - jnp.mean(dxh, axis=-1, keepdims=True) - xh * jnp.mean(dxh * xh, axis=-1, keepdims=True))
        return dx, _rsum(dn * xh), _rsum(dn), _rsum(dh)
    return rowwise(fn, [_full(ds), _full(c2), _full(dh)], [g, b], [(D, F32)], [D, D, D], name=name)


def gatenorm_fwd(y, zx, gn, name):
    def fn(y, z, gn):
        hg = y * _silu(z)
        parts = []
        for k in range(NG):
            hk = hg[:, k * GW:(k + 1) * GW]
            parts.append(hk * lax.rsqrt(jnp.mean(hk * hk, axis=-1, keepdims=True) + EPS))
        return (jnp.concatenate(parts, axis=1) * gn,)
    return rowwise(fn, [_full(y), (zx, DI, 0)], [gn], [(DI, BF16)], name=name)[0]


def gatenorm_bwd(dyn, y, zx, gn, name):
    def fn(dyn, y, z, gn):
        sz = _silu(z)
        hg = y * sz
        dxh = dyn * gn
        dhg, xhs = [], []
        for k in range(NG):
            sl = slice(k * GW, (k + 1) * GW)
            hk = hg[:, sl]
            r = lax.rsqrt(jnp.mean(hk * hk, axis=-1, keepdims=True) + EPS)
            xh = hk * r
            dk = dxh[:, sl]
            dhg.append(r * (dk - xh * jnp.mean(dk * xh, axis=-1, keepdims=True)))
            xhs.append(xh)
        dhg = jnp.concatenate(dhg, axis=1)
        xh = jnp.concatenate(xhs, axis=1)
        return dhg * sz, dhg * y * _dsilu(z), _rsum(dyn * xh)
    return rowwise(fn, [_full(dyn), _full(y), (zx, DI, 0)], [gn], [(DI, F32), (DI, BF16, DINP_PAD, 0)], [DI], name=name)


def _softplus(x):
    return jnp.maximum(x, 0.0) + jnp.log(1.0 + jnp.exp(-jnp.abs(x)))


def _spread(v, e):
    hi = v.astype(BF16)
    r = v - hi.astype(F32)
    mid = r.astype(BF16)
    lo = (r - mid.astype(F32)).astype(BF16)
    return _dot(hi, e) + _dot(mid, e) + _dot(lo, e)


def _spread2(v, e):
    hi = v.astype(BF16)
    lo = (v - hi.astype(F32)).astype(BF16)
    return _dot(hi, e) + _dot(lo, e)


def dt_fwd(zx, dt_bias, a_log, name):
    heads = (jnp.arange(DI)[None, :] // HD == jnp.arange(LANES)[:, None]).astype(BF16)

    def fn(raw, bias, a_log, e):
        dt = _softplus(raw + bias)
        da = dt * (-jnp.exp(a_log))
        return dt, da, _spread(dt, e), _spread(da, e)

    return rowwise(fn, [(zx, LANES, (2 * DI + 2 * NG * DS) // LANES)], [dt_bias, a_log, heads],
                   [(LANES, F32), (LANES, F32), (DI, F32), (DI, F32)], name=name)


def dt_bwd(ddt, dda, dt, zx, dt_bias, a_log, dzx, name):
    def fn(ddt, dda, dt, raw, bias, a_log):
        a = -jnp.exp(a_log)
        draw = (ddt + dda * a) * _sigmoid(raw + bias)
        return jnp.concatenate([draw, jnp.zeros_like(draw)], axis=1), _rsum(draw), _rsum(dda * dt) * a
    return rowwise(fn, [_full(ddt), _full(dda), _full(dt), (zx, LANES, (2 * DI + 2 * NG * DS) // LANES)],
                   [dt_bias, a_log], [(2 * LANES, BF16, DINP_PAD, DINP_PAD // (2 * LANES) - 1)], [LANES, LANES],
                   name=name, fill=(dzx, 0))


def headsum(v, name):
    def body(v_ref, o_ref):
        o_ref[...] = jnp.sum(v_ref[...], axis=1, keepdims=True)
    return _pcall(body, name=name, out_shape=jax.ShapeDtypeStruct((v.shape[0], 1), F32))(v)


CONV_ROWS = 256


def _shifted(win, o, rows):
    if o == 0:
        return win[0:rows]
    n = win.shape[0]
    return pltpu.roll(win, shift=n - o, axis=0)[0:rows]


def dwconv_fwd(x, x_cb0, w, b, K, ct, act, name):
    C = w.shape[1]
    pad = 8 if K <= 8 else 32
    KP = w.shape[0]
    n_out = 2 if act else 1

    def body(x_ref, w_ref, b_ref, *rest):
        o_refs, px = rest[:n_out], rest[n_out]
        px[0:pad, :] = jnp.zeros((pad, ct), F32)
        px[pad:pad + T, :] = x_ref[...]
        wv = w_ref[...]
        bv = b_ref[...]
        for r0 in range(0, T, CONV_ROWS):
            win = px[r0:r0 + CONV_ROWS + pad, :]
            acc = jnp.broadcast_to(bv, (CONV_ROWS, ct))
            for k in range(K):
                acc = acc + wv[k:k + 1, :] * _shifted(win, pad - (K - 1) + k, CONV_ROWS)
            o_refs[0][r0:r0 + CONV_ROWS, :] = acc
            if act:
                o_refs[1][r0:r0 + CONV_ROWS, :] = _silu(acc)

    return _pcall(
        body, name=name, grid=(C // ct,),
        in_specs=[pl.BlockSpec((T, ct), lambda j: (0, x_cb0 + j)), pl.BlockSpec((KP, ct), lambda j: (0, j)),
                  pl.BlockSpec((1, ct), lambda j: (0, j))],
        out_specs=[pl.BlockSpec((T, ct), lambda j: (0, j))] * n_out,
        out_shape=[jax.ShapeDtypeStruct((T, C), F32)] * n_out,
        scratch_shapes=[pltpu.VMEM((T + pad, ct), F32)],
        compiler_params=_cparams(("parallel",)),
    )(x, w, b)


def dwconv_bwd(dout, cpre, x, x_cb0, w, K, ct, act, out_dtype, name, into=None):
    C = w.shape[1]
    pad = 8 if K <= 8 else 32
    KP = w.shape[0]

    def body(*refs):
        dx_ref, dw_ref, db_ref, px, pd = refs[-5:]
        if act:
            d_ref, c_ref, x_ref, w_ref = refs[:4]
        else:
            d_ref, x_ref, w_ref = refs[:3]
        px[0:pad, :] = jnp.zeros((pad, ct), F32)
        px[pad:pad + T, :] = x_ref[...]
        pd[T:T + pad, :] = jnp.zeros((pad, ct), F32)
        if act:
            pd[0:T, :] = d_ref[...] * _dsilu(c_ref[...])
        else:
            pd[0:T, :] = d_ref[...]
        wv = w_ref[...]
        dws = [jnp.zeros((1, ct), F32) for _ in range(K)]
        db = jnp.zeros((1, ct), F32)
        for r0 in range(0, T, CONV_ROWS):
            dwin = pd[r0:r0 + CONV_ROWS + pad, :]
            xwin = px[r0:r0 + CONV_ROWS + pad, :]
            dc = dwin[0:CONV_ROWS]
            db = db + _rsum(dc)
            acc = jnp.zeros((CONV_ROWS, ct), F32)
            for k in range(K):
                acc = acc + wv[k:k + 1, :] * _shifted(dwin, K - 1 - k, CONV_ROWS)
                dws[k] = dws[k] + _rsum(dc * _shifted(xwin, pad - (K - 1) + k, CONV_ROWS))
            dx_ref[r0:r0 + CONV_ROWS, :] = acc.astype(dx_ref.dtype)
        dw_ref[...] = jnp.zeros((KP, ct), F32)
        for k in range(K):
            dw_ref[k:k + 1, :] = dws[k]
        db_ref[...] = db

    col = pl.BlockSpec((T, ct), lambda j: (0, j))
    in_specs = [col] + ([col] if act else []) + [pl.BlockSpec((T, ct), lambda j: (0, x_cb0 + j)),
                                                 pl.BlockSpec((KP, ct), lambda j: (0, j))]
    args = [dout] + ([cpre] if act else []) + [x, w]
    dx_spec, dx_shape, aliases = col, jax.ShapeDtypeStruct((T, C), out_dtype), {}
    if into is not None:
        dx_spec = pl.BlockSpec((T, ct), lambda j: (0, into[1] + j))
        dx_shape = jax.ShapeDtypeStruct(into[0].shape, into[0].dtype)
        aliases = {len(args): 0}
        in_specs.append(ANY)
        args.append(into[0])
    return _pcall(
        body, name=name, grid=(C // ct,), in_specs=in_specs,
        out_specs=[dx_spec, pl.BlockSpec((KP, ct), lambda j: (0, j)), pl.BlockSpec((1, ct), lambda j: (0, j))],
        out_shape=[dx_shape, jax.ShapeDtypeStruct((KP, C), F32), jax.ShapeDtypeStruct((1, C), F32)],
        input_output_aliases=aliases,
        scratch_shapes=[pltpu.VMEM((T + pad, ct), F32), pltpu.VMEM((T + pad, ct), F32)],
        compiler_params=_cparams(("parallel",)),
    )(*args)


def _scan(a, axis, reverse=False):
    n = a.shape[axis]
    idx = lax.broadcasted_iota(jnp.int32, a.shape, axis)
    s = 1
    while s < n:
        if reverse:
            a = a + jnp.where(idx < n - s, pltpu.roll(a, shift=n - s, axis=axis), 0.0)
        else:
            a = a + jnp.where(idx >= s, pltpu.roll(a, shift=s, axis=axis), 0.0)
        s *= 2
    return a


_NT = _DIMS["nt"]
_TN = _DIMS["tn"]


def _dot(a, b, dims=_DIMS["nn"]):
    return lax.dot_general(a, b, dims, preferred_element_type=F32)


def ssd_fwd(xbc, dtx, dax, daT, dfull, name):
    def body(xbc_ref, dtx_ref, dax_ref, daT_ref, df_ref, y_ref, st_ref, S):
        ci = pl.program_id(0)

        @pl.when(ci == 0)
        def _():
            S[...] = jnp.zeros_like(S)

        row = lax.broadcasted_iota(jnp.int32, (CH, CH), 0)
        lane = lax.broadcasted_iota(jnp.int32, (CH, CH), 1)
        acsT = _scan(daT_ref[...], 1)
        for g in range(NG):
            c0 = g * GW
            xs = xbc_ref[:, c0:c0 + GW]
            acs = _scan(dax_ref[:, c0:c0 + GW], 0)
            Bm = xbc_ref[:, DI + g * DS:DI + (g + 1) * DS].astype(BF16)
            Cm = xbc_ref[:, DI + NG * DS + g * DS:DI + NG * DS + (g + 1) * DS].astype(BF16)
            xdt = xs * dtx_ref[:, c0:c0 + GW]
            atot = acs[CH - 1:CH, :]
            Sg = S[:, c0:c0 + GW]
            st_ref[:, c0:c0 + GW] = Sg
            CB = _dot(Cm, Bm, _NT)
            yg = jnp.exp(acs) * _dot(Cm, Sg.astype(BF16)) + xs * df_ref[:, c0:c0 + GW]
            xd = (xdt * jnp.exp(atot - acs)).astype(BF16)
            S[:, c0:c0 + GW] = jnp.exp(atot) * Sg + _dot(Bm, xd, _TN)
            xdt_b = xdt.astype(BF16)
            for r in range(NH // NG):
                h = g * (NH // NG) + r
                hs = slice(r * HD, (r + 1) * HD)
                seg = acs[:, r * HD:r * HD + 1] - acsT[h:h + 1, :]
                Lm = jnp.where(row >= lane, jnp.exp(jnp.minimum(seg, 0.0)), 0.0)
                yd = _dot((CB * Lm).astype(BF16), xdt_b[:, hs])
                y_ref[:, c0 + r * HD:c0 + (r + 1) * HD] = yg[:, hs] + yd

    return _pcall(
        body, name=name, grid=(NCH,),
        in_specs=[pl.BlockSpec((CH, CONVD), lambda i: (i, 0)), pl.BlockSpec((CH, DI), lambda i: (i, 0)),
                  pl.BlockSpec((CH, DI), lambda i: (i, 0)), pl.BlockSpec((NH, CH), lambda i: (0, i)),
                  pl.BlockSpec((1, DI), lambda i: (0, 0))],
        out_specs=[pl.BlockSpec((CH, DI), lambda i: (i, 0)), pl.BlockSpec((None, DS, DI), lambda i: (i, 0, 0))],
        out_shape=[jax.ShapeDtypeStruct((T, DI), F32), jax.ShapeDtypeStruct((NCH, DS, DI), F32)],
        scratch_shapes=[pltpu.VMEM((DS, DI), F32)],
        compiler_params=_cparams(("arbitrary",)),
    )(xbc, dtx, dax, daT, dfull)


def ssd_bwd(dy, xbc, dtx, dax, daT, dfull, states, name):
    hsum = (jnp.arange(DI)[:, None] // HD == jnp.arange(LANES)[None, :]).astype(BF16).reshape(NG, GW, LANES)

    def body(dy_ref, xbc_ref, dtx_ref, dax_ref, daT_ref, df_ref, st_ref, hsum_ref, dxbc_ref, ddt_ref, dda_ref, dD_ref, dS):
        i = pl.program_id(0)

        @pl.when(i == 0)
        def _():
            dS[...] = jnp.zeros_like(dS)
            dD_ref[...] = jnp.zeros_like(dD_ref)

        row = lax.broadcasted_iota(jnp.int32, (CH, CH), 0)
        lane = lax.broadcasted_iota(jnp.int32, (CH, CH), 1)
        acsT = _scan(daT_ref[...], 1)
        ddt_all = jnp.zeros((CH, LANES), F32)
        dacs_all = jnp.zeros((CH, LANES), F32)
        colacc = jnp.zeros((CH, CH), F32)
        for g in range(NG):
            c0 = g * GW
            xs = xbc_ref[:, c0:c0 + GW]
            dtx = dtx_ref[:, c0:c0 + GW]
            acs = _scan(dax_ref[:, c0:c0 + GW], 0)
            Bm = xbc_ref[:, DI + g * DS:DI + (g + 1) * DS].astype(BF16)
            Cm = xbc_ref[:, DI + NG * DS + g * DS:DI + NG * DS + (g + 1) * DS].astype(BF16)
            xdt = xs * dtx
            atot = acs[CH - 1:CH, :]
            Sin = st_ref[:, c0:c0 + GW]
            dyg = dy_ref[:, c0:c0 + GW]
            dSo = dS[:, c0:c0 + GW]
            E = jnp.exp(acs)
            Etot = jnp.exp(atot)
            dec = jnp.exp(atot - acs)
            dD_ref[:, c0:c0 + GW] += _rsum(dyg * xs)
            dxs = dyg * df_ref[:, c0:c0 + GW]
            Sin_b = Sin.astype(BF16)
            dSo_b = dSo.astype(BF16)
            dY0 = dyg * E
            dY0_b = dY0.astype(BF16)
            dC = _dot(dY0_b, Sin_b, _NT)
            dS[:, c0:c0 + GW] = _dot(Cm, dY0_b, _TN) + Etot * dSo
            XD = xdt * dec
            dXD = _dot(Bm, dSo_b)
            dB = _dot(XD.astype(BF16), dSo_b, _NT)
            dxdt = dXD * dec
            Gq = dXD * XD
            dacs_x = dY0 * _dot(Cm, Sin_b) - Gq
            datot_x = _rsum(dSo * Sin) * Etot + _rsum(Gq)
            dacs_all = dacs_all + _spread2(dacs_x, hsum_ref[g])
            dtot8 = _spread2(jnp.broadcast_to(datot_x, (8, GW)), hsum_ref[g])
            dacs_all = dacs_all + jnp.where(row == CH - 1, jnp.broadcast_to(dtot8[0:1, :], (CH, LANES)), 0.0)
            CB = _dot(Cm, Bm, _NT)
            dCB = jnp.zeros((CH, CH), F32)
            xdt_b = xdt.astype(BF16)
            dy_b = dyg.astype(BF16)
            for r in range(NH // NG):
                h = g * (NH // NG) + r
                hs = slice(r * HD, (r + 1) * HD)
                seg = acs[:, r * HD:r * HD + 1] - acsT[h:h + 1, :]
                Lm = jnp.where(row >= lane, jnp.exp(jnp.minimum(seg, 0.0)), 0.0)
                dyr = dy_b[:, hs]
                dML = _dot(dyr, xdt_b[:, hs], _NT) * Lm
                dxbc_ref[:, c0 + r * HD:c0 + (r + 1) * HD] = _dot((CB * Lm).astype(BF16), dyr, _TN)
                dCB = dCB + dML
                dseg = dML * CB
                dacs_all = dacs_all + _spread2(dseg, (lane == h).astype(BF16))
                colacc = colacc + jnp.where(row == h, jnp.sum(dseg, axis=0, keepdims=True), 0.0)
            dxdt = dxdt + dxbc_ref[:, c0:c0 + GW]
            ddt_all = ddt_all + _spread2(dxdt * xs, hsum_ref[g])
            dxbc_ref[:, c0:c0 + GW] = dxs + dxdt * dtx
            dCB_b = dCB.astype(BF16)
            dxbc_ref[:, DI + g * DS:DI + (g + 1) * DS] = dB + _dot(dCB_b, Cm, _TN)
            dxbc_ref[:, DI + NG * DS + g * DS:DI + NG * DS + (g + 1) * DS] = dC + _dot(dCB_b, Bm)
        ddt_ref[...] = ddt_all
        dda_ref[...] = _scan(dacs_all - colacc.T, 0, reverse=True)

    last = NCH - 1
    return _pcall(
        body, name=name, grid=(NCH,),
        in_specs=[pl.BlockSpec((CH, DI), lambda i: (last - i, 0)), pl.BlockSpec((CH, CONVD), lambda i: (last - i, 0)),
                  pl.BlockSpec((CH, DI), lambda i: (last - i, 0)), pl.BlockSpec((CH, DI), lambda i: (last - i, 0)),
                  pl.BlockSpec((NH, CH), lambda i: (0, last - i)), pl.BlockSpec((1, DI), lambda i: (0, 0)),
                  pl.BlockSpec((None, DS, DI), lambda i: (last - i, 0, 0)),
                  pl.BlockSpec((NG, GW, LANES), lambda i: (0, 0, 0))],
        out_specs=[pl.BlockSpec((CH, CONVD), lambda i: (last - i, 0)), pl.BlockSpec((CH, LANES), lambda i: (last - i, 0)),
                   pl.BlockSpec((CH, LANES), lambda i: (last - i, 0)), pl.BlockSpec((1, DI), lambda i: (0, 0))],
        out_shape=[jax.ShapeDtypeStruct((T, CONVD), F32), jax.ShapeDtypeStruct((T, LANES), F32),
                   jax.ShapeDtypeStruct((T, LANES), F32), jax.ShapeDtypeStruct((1, DI), F32)],
        scratch_shapes=[pltpu.VMEM((DS, DI), F32)],
        compiler_params=_cparams(("arbitrary",)),
    )(dy, xbc, dtx, dax, daT, dfull, states, hsum)


def _as3d(shape):
    if len(shape) == 1:
        return (1, 1, shape[0])
    if len(shape) == 2:
        return (1, shape[0], shape[1])
    return (math.prod(shape[:-2]), shape[-2], shape[-1])


def adamw_small(gs, ws, ms, vs, name):
    n = len(ws)
    bc1 = 1.0 - ADAM_B1 ** ADAM_STEP
    bc2 = 1.0 - ADAM_B2 ** ADAM_STEP

    def body(*refs):
        for a in range(n):
            g_ref, w_ref, m_ref, v_ref = refs[4 * a:4 * a + 4]
            g_out, d_out, m_out, v_out = refs[4 * n + 4 * a:4 * n + 4 * a + 4]
            g = g_ref[...]
            mn = ADAM_B1 * m_ref[...] + (1.0 - ADAM_B1) * g
            vn = ADAM_B2 * v_ref[...] + (1.0 - ADAM_B2) * (g * g)
            g_out[...] = g
            m_out[...] = mn
            v_out[...] = vn
            d_out[...] = -ADAM_LR * ((mn / bc1) / (jnp.sqrt(vn / bc2) + ADAM_EPS) + ADAM_WD * w_ref[...])

    args, out_shape = [], []
    for g, w, m, v in zip(gs, ws, ms, vs):
        s3 = _as3d(w.shape)
        args += [t.reshape(s3) for t in (g, w, m, v)]
        out_shape += [jax.ShapeDtypeStruct(s3, F32)] * 4
    outs = _pcall(body, name=name, out_shape=out_shape)(*args)
    return [[o.reshape(w.shape) for o in outs[4 * a:4 * a + 4]] for a, w in enumerate(ws)]


ADAMW_STEPS = 4


def adamw_stage(recvs, owns, ws, ms, vs, layers, prevs, name):
    n = len(ws)
    chained = prevs[0] is not None
    assert all((p is not None) == chained for p in prevs)
    bc1 = 1.0 - ADAM_B1 ** ADAM_STEP
    bc2 = 1.0 - ADAM_B2 ** ADAM_STEP
    n_in = (9 if chained else 5) * n

    def body(*refs):
        for a in range(n):
            r_ref, o_ref, w_ref, m_ref, v_ref = refs[5 * a:5 * a + 5]
            g_out, d_out, m_out, v_out = refs[n_in + 4 * a:n_in + 4 * a + 4]
            g = o_ref[...].astype(F32)
            for k in range(r_ref.shape[0]):
                g = g + r_ref[k].astype(F32)
            mn = ADAM_B1 * m_ref[...] + (1.0 - ADAM_B1) * g
            vn = ADAM_B2 * v_ref[...] + (1.0 - ADAM_B2) * (g * g)
            g_out[...] = g
            m_out[...] = mn
            v_out[...] = vn
            d_out[...] = -ADAM_LR * ((mn / bc1) / (jnp.sqrt(vn / bc2) + ADAM_EPS) + ADAM_WD * w_ref[...])

    in_specs, args, out_specs, out_shape = [], [], [], []
    for a in range(n):
        _, R, C = ws[a].shape
        tr = R // ADAMW_STEPS
        assert tr * ADAMW_STEPS == R and tr % 8 == 0
        slot = pl.BlockSpec((None, tr, C), functools.partial(lambda r, l: (l, r, 0), l=layers[a]))
        own = pl.BlockSpec((None, tr, C), lambda r: (2 * lax.axis_index("x") + lax.axis_index("y"), r, 0))
        in_specs += [pl.BlockSpec((recvs[a].shape[0], tr, C), lambda r: (0, r, 0)), own, slot, slot, slot]
        args += [recvs[a], owns[a], ws[a], ms[a], vs[a]]
        out_specs += [slot] * 4
        out_shape += [jax.ShapeDtypeStruct(ws[a].shape, F32)] * 4
    aliases = {}
    if chained:
        for a in range(n):
            in_specs += [ANY] * 4
            args += list(prevs[a])
            aliases.update({5 * n + 4 * a + k: 4 * a + k for k in range(4)})
    outs = _pcall(
        body, name=name, grid=(ADAMW_STEPS,), in_specs=in_specs, out_specs=out_specs, out_shape=out_shape,
        input_output_aliases=aliases, compiler_params=_cparams(("parallel",)),
    )(*args)
    return [list(outs[4 * a:4 * a + 4]) for a in range(n)]


def sum_leading(parts, name):
    P, R, C = parts.shape

    def body(p_ref, o_ref):
        s = p_ref[0]
        for k in range(1, P):
            s = s + p_ref[k]
        o_ref[...] = s

    return _pcall(body, name=name, out_shape=jax.ShapeDtypeStruct((R, C), F32))(parts)


def pair_sum(gsends, recvs, name):
    n = len(gsends)

    def body(*refs):
        for g_ref, r_ref, o_ref in zip(refs[:n], refs[n:2 * n], refs[2 * n:]):
            o_ref[...] = (g_ref[...].astype(F32) + r_ref[...].astype(F32)).astype(o_ref.dtype)

    def slot(a):
        return pl.BlockSpec((None,) + a.shape[1:], lambda q: (q, 0, 0))

    def own(a):
        return pl.BlockSpec((None,) + a.shape[1:], lambda q: (2 * q + lax.axis_index("c"), 0, 0))

    return _pcall(
        body, name=name, grid=(4,), in_specs=[own(g) for g in gsends] + [slot(r) for r in recvs],
        out_specs=[slot(r) for r in recvs], out_shape=[jax.ShapeDtypeStruct(r.shape, BF16) for r in recvs],
        compiler_params=_cparams(("parallel",)),
    )(*gsends, *recvs)


def _place():
    return lax.axis_index("x"), lax.axis_index("y"), lax.axis_index("c")


def _other_chips(x, y):
    return [(1 - x, y), (x, 1 - y), (1 - x, 1 - y)]


def all_gather(arrs, name):
    n = len(arrs)

    def body(*refs):
        ins, outs = refs[:n], refs[n:2 * n]
        send_sems, recv_sems, local_sems = refs[2 * n:]
        x, y, c = _place()
        me, sibling = (x, y, c), (x, y, 1 - c)
        chips = _other_chips(x, y)

        def slot(a, px, py, pc):
            return outs[a].at[4 * px + 2 * py + pc]

        def copy(a, k, block, to, src=None):
            return pltpu.make_async_remote_copy(
                src_ref=slot(a, *block) if src is None else src, dst_ref=slot(a, *block),
                send_sem=send_sems.at[a, k], recv_sem=recv_sems.at[a, k], device_id=to, device_id_type=MESH)

        mine, first, passed = [], [], []
        for a in range(n):
            cp = pltpu.make_async_copy(ins[a], slot(a, *me), local_sems.at[a])
            cp.start()
            mine.append(cp)
            first.append(copy(a, 0, me, sibling, src=ins[a]))
            first += [copy(a, 1 + j, me, (*chip, c), src=ins[a]) for j, chip in enumerate(chips)]
        for cp in first:
            cp.start()
        for j, chip in enumerate(chips):
            for a in range(n):
                copy(a, 1 + j, (*chip, c), me).wait_recv()
                cp = copy(a, 4 + j, (*chip, c), sibling)
                cp.start()
                passed.append(cp)
        for a in range(n):
            copy(a, 0, sibling, me).wait_recv()
            for j, chip in enumerate(chips):
                copy(a, 4 + j, (*chip, 1 - c), me).wait_recv()
        for cp in first + passed:
            cp.wait_send()
        for cp in mine:
            cp.wait()

    return _pcall(
        body, name=name, in_specs=[ANY] * n, out_specs=[ANY] * n,
        out_shape=[jax.ShapeDtypeStruct((N_DEV,) + a.shape, a.dtype) for a in arrs],
        scratch_shapes=[pltpu.SemaphoreType.DMA((n, 7)), pltpu.SemaphoreType.DMA((n, 7)), pltpu.SemaphoreType.DMA((n,))],
    )(*arrs)


def sibling_exchange(gsends, name, after=None):
    n = len(gsends)
    n_in = n + (1 if after is not None else 0)

    def body(*refs):
        ins, outs = refs[:n], refs[n_in:n_in + n]
        send_sems, recv_sems = refs[n_in + n:]
        x, y, c = _place()
        copies = []
        for a in range(n):
            for q in range(4):
                cp = pltpu.make_async_remote_copy(
                    src_ref=ins[a].at[2 * q + 1 - c], dst_ref=outs[a].at[q],
                    send_sem=send_sems.at[a, q], recv_sem=recv_sems.at[a, q],
                    device_id=(x, y, 1 - c), device_id_type=MESH)
                cp.start()
                copies.append(cp)
        for cp in copies:
            cp.wait()

    return _pcall(
        body, name=name, in_specs=[ANY] * n_in, out_specs=[ANY] * n,
        out_shape=[jax.ShapeDtypeStruct((4,) + g.shape[1:], g.dtype) for g in gsends],
        scratch_shapes=[pltpu.SemaphoreType.DMA((n, 4)), pltpu.SemaphoreType.DMA((n, 4))],
    )(*gsends, *([after] if after is not None else []))


HBM =pl.BlockSpec(memory_space=pltpu.HBM)
SEM = pl.BlockSpec(memory_space=pltpu.SEMAPHORE)
EFFECT = pltpu.SideEffectType.DATAFLOW_SIDE_EFFECTING


def _in_hbm(a):
    return pltpu.with_memory_space_constraint(a, pltpu.HBM)


def _gather_peers(x, y, c):
    to = [(x, y, 1 - c)] + [(px, py, c) for px, py in _other_chips(x, y)]
    return to, [4 * px + 2 * py + pc for px, py, pc in to]


def gather_start(arrs, after, name):
    n = len(arrs)
    n_in = 2 * n + (1 if after is not None else 0)

    def body(*refs):
        srcs, lands = refs[:n], refs[n:2 * n]
        send_sems, recv_sems = refs[n_in], refs[n_in + 1]
        token = refs[-1]
        x, y, c = _place()
        to, _ = _gather_peers(x, y, c)
        me = 4 * x + 2 * y + c
        for a in range(n):
            for k, dev in enumerate(to):
                pltpu.make_async_remote_copy(
                    src_ref=srcs[a], dst_ref=lands[a].at[me], send_sem=send_sems.at[4 * a + k], recv_sem=recv_sems.at[4 * a + k],
                    device_id=dev, device_id_type=MESH).start()
        token[...] = jnp.zeros_like(token)

    zones = [lax.empty((N_DEV,) + a.shape, a.dtype) for a in arrs]
    args = [_in_hbm(a) for a in arrs] + [_in_hbm(z) for z in zones] + ([after] if after is not None else [])
    outs = _pcall(
        body, name=name,
        out_shape=(pltpu.SemaphoreType.DMA((4 * n,)), pltpu.SemaphoreType.DMA((4 * n,)),
                   *[pltpu.HBM(a.shape, a.dtype) for a in arrs], *[pltpu.HBM(z.shape, z.dtype) for z in zones],
                   jax.ShapeDtypeStruct((8, LANES), F32)),
        in_specs=[HBM] * (2 * n) + ([ANY] if after is not None else []),
        out_specs=(SEM, SEM, *[HBM] * (2 * n), pl.BlockSpec(memory_space=pltpu.VMEM)),
        input_output_aliases={i: 2 + i for i in range(2 * n)},
        compiler_params=pltpu.CompilerParams(has_side_effects=EFFECT),
    )(*args)
    return dict(send=outs[0], recv=outs[1], srcs=list(outs[2:2 + n]), lands=list(outs[2 + n:2 + 2 * n]), token=outs[-1])


def gather_wait(st, after, name):
    n = len(st["srcs"])

    def body(*refs):
        srcs, lands = refs[:n], refs[n:2 * n]
        send_sems, recv_sems = refs[2 * n], refs[2 * n + 1]
        x, y, c = _place()
        to, slots = _gather_peers(x, y, c)
        for a in range(n):
            for k, dev in enumerate(to):
                cp = pltpu.make_async_remote_copy(
                    src_ref=srcs[a], dst_ref=lands[a].at[slots[k]], send_sem=send_sems.at[4 * a + k],
                    recv_sem=recv_sems.at[4 * a + k], device_id=dev, device_id_type=MESH)
                cp.wait_send()
                cp.wait_recv()

    outs = _pcall(
        body, name=name,
        out_shape=(*[pltpu.HBM(a.shape, a.dtype) for a in st["srcs"]], *[pltpu.HBM(z.shape, z.dtype) for z in st["lands"]]),
        in_specs=[HBM] * (2 * n) + [SEM, SEM, ANY], out_specs=tuple([HBM] * (2 * n)),
        input_output_aliases={i: i for i in range(2 * n)},
        compiler_params=pltpu.CompilerParams(has_side_effects=EFFECT),
    )(*st["srcs"], *st["lands"], st["send"], st["recv"], after)
    return list(outs[n:])


def pass_start(zones, name):
    n = len(zones)

    def body(*refs):
        zs = refs[:n]
        send_sems, recv_sems = refs[n], refs[n + 1]
        token = refs[-1]
        x, y, c = _place()
        for a in range(n):
            for j, (px, py) in enumerate(_other_chips(x, y)):
                blk = zs[a].at[4 * px + 2 * py + c]
                pltpu.make_async_remote_copy(
                    src_ref=blk, dst_ref=blk, send_sem=send_sems.at[3 * a + j], recv_sem=recv_sems.at[3 * a + j],
                    device_id=(x, y, 1 - c), device_id_type=MESH).start()
        token[...] = jnp.zeros_like(token)

    outs = _pcall(
        body, name=name,
        out_shape=(pltpu.SemaphoreType.DMA((3 * n,)), pltpu.SemaphoreType.DMA((3 * n,)),
                   *[pltpu.HBM(z.shape, z.dtype) for z in zones], jax.ShapeDtypeStruct((8, LANES), F32)),
        in_specs=[HBM] * n, out_specs=(SEM, SEM, *[HBM] * n, pl.BlockSpec(memory_space=pltpu.VMEM)),
        input_output_aliases={i: 2 + i for i in range(n)},
        compiler_params=pltpu.CompilerParams(has_side_effects=EFFECT),
    )(*zones)
    return dict(send=outs[0], recv=outs[1], zones=list(outs[2:2 + n]), token=outs[-1])


def pass_wait(st, after, name):
    n = len(st["zones"])

    def body(*refs):
        zs = refs[:n]
        send_sems, recv_sems = refs[n], refs[n + 1]
        x, y, c = _place()
        for a in range(n):
            for j, (px, py) in enumerate(_other_chips(x, y)):
                cp = pltpu.make_async_remote_copy(
                    src_ref=zs[a].at[4 * px + 2 * py + c], dst_ref=zs[a].at[4 * px + 2 * py + 1 - c],
                    send_sem=send_sems.at[3 * a + j], recv_sem=recv_sems.at[3 * a + j],
                    device_id=(x, y, 1 - c), device_id_type=MESH)
                cp.wait_send()
                cp.wait_recv()

    outs = _pcall(
        body, name=name, out_shape=tuple(pltpu.HBM(z.shape, z.dtype) for z in st["zones"]),
        in_specs=[HBM] * n + [SEM, SEM, ANY], out_specs=tuple([HBM] * n),
        input_output_aliases={i: i for i in range(n)},
        compiler_params=pltpu.CompilerParams(has_side_effects=EFFECT),
    )(*st["zones"], st["send"], st["recv"], after)
    return list(outs)


def scatter_start(parts, name):
    n = len(parts)

    def body(*refs):
        srcs, lands = refs[:n], refs[n:2 * n]
        send_sems, recv_sems = refs[2 * n], refs[2 * n + 1]
        token = refs[-1]
        x, y, c = _place()
        for a in range(n):
            for j, (px, py) in enumerate(_other_chips(x, y)):
                pltpu.make_async_remote_copy(
                    src_ref=srcs[a].at[2 * px + py], dst_ref=lands[a].at[j], send_sem=send_sems.at[3 * a + j],
                    recv_sem=recv_sems.at[3 * a + j], device_id=(px, py, c), device_id_type=MESH).start()
        token[...] = jnp.zeros_like(token)

    zones = [lax.empty((3,) + p.shape[1:], p.dtype) for p in parts]
    outs = _pcall(
        body, name=name,
        out_shape=(pltpu.SemaphoreType.DMA((3 * n,)), pltpu.SemaphoreType.DMA((3 * n,)),
                   *[pltpu.HBM(p.shape, p.dtype) for p in parts], *[pltpu.HBM(z.shape, z.dtype) for z in zones],
                   jax.ShapeDtypeStruct((8, LANES), F32)),
        in_specs=[HBM] * (2 * n), out_specs=(SEM, SEM, *[HBM] * (2 * n), pl.BlockSpec(memory_space=pltpu.VMEM)),
        input_output_aliases={i: 2 + i for i in range(2 * n)},
        compiler_params=pltpu.CompilerParams(has_side_effects=EFFECT),
    )(*[_in_hbm(p) for p in parts], *[_in_hbm(z) for z in zones])
    return dict(send=outs[0], recv=outs[1], srcs=list(outs[2:2 + n]), lands=list(outs[2 + n:2 + 2 * n]), token=outs[-1])


def scatter_wait(st, after, name):
    n = len(st["srcs"])

    def body(*refs):
        srcs, lands = refs[:n], refs[n:2 * n]
        send_sems, recv_sems = refs[2 * n], refs[2 * n + 1]
        x, y, c = _place()
        for a in range(n):
            for j, (px, py) in enumerate(_other_chips(x, y)):
                cp = pltpu.make_async_remote_copy(
                    src_ref=srcs[a].at[2 * px + py], dst_ref=lands[a].at[j], send_sem=send_sems.at[3 * a + j],
                    recv_sem=recv_sems.at[3 * a + j], device_id=(px, py, c), device_id_type=MESH)
                cp.wait_send()
                cp.wait_recv()

    outs = _pcall(
        body, name=name,
        out_shape=(*[pltpu.HBM(a.shape, a.dtype) for a in st["srcs"]], *[pltpu.HBM(z.shape, z.dtype) for z in st["lands"]]),
        in_specs=[HBM] * (2 * n) + [SEM, SEM, ANY], out_specs=tuple([HBM] * (2 * n)),
        input_output_aliases={i: i for i in range(2 * n)},
        compiler_params=pltpu.CompilerParams(has_side_effects=EFFECT),
    )(*st["srcs"], *st["lands"], st["send"], st["recv"], after)
    return list(outs[:n]), list(outs[n:])


def _unshard(g, axis):
    nd = g.ndim - 1
    axis = axis % nd
    t = jnp.moveaxis(g, 0, axis)
    shp = list(g.shape[1:])
    shp[axis] *= N_DEV
    return t.reshape(shp)


def _to_shards(full, axis):
    axis = axis % full.ndim
    shp = list(full.shape)
    shp[axis:axis + 1] = [N_DEV, shp[axis] // N_DEV]
    return jnp.moveaxis(full.reshape(shp), axis, 0)


def _pack(arrs, rows):
    flat = jnp.concatenate([a.reshape(-1).astype(F32) for a in arrs])
    return jnp.pad(flat, (0, rows * LANES - flat.shape[0])).reshape(rows, LANES)


def _unpack(buf, shapes):
    flat = buf.reshape(-1)
    out, off = [], 0
    for s in shapes:
        n = math.prod(s)
        out.append(flat[off:off + n].reshape(s))
        off += n
    return out


def _rows_for(shapes):
    n = sum(math.prod(s) for s in shapes)
    return -(-n // (8 * LANES)) * 8


def _row(v, width=None):
    v = v.reshape(1, -1).astype(F32)
    if width is not None and v.shape[1] < width:
        v = jnp.pad(v, ((0, 0), (0, width - v.shape[1])))
    return v


def _after(order, width):
    if not order:
        return None
    t = order[0][0:1, 0:1]
    for o in order[1:]:
        t = t + o[0:1, 0:1]
    return jnp.broadcast_to(t, (1, width))


def _norm_after(norm, order):
    row = _after(order, norm.shape[1])
    return norm if row is None else norm + row


FFN_TN = 256


def ffn_in(h, norm, w_gate, w_up, name):
    def body(h_ref, n_ref, wg_ref, wu_ref, u_ref, g_ref, up_ref, act_ref, u_s):
        @pl.when(pl.program_id(0) == 0)
        def _():
            x = h_ref[...]
            r = lax.rsqrt(jnp.mean(x * x, axis=-1, keepdims=True) + EPS)
            u_s[...] = (x * r * n_ref[...]).astype(BF16)
            u_ref[...] = u_s[...]

        u = u_s[...]
        g = _dot(u, wg_ref[...], _NT)
        up = _dot(u, wu_ref[...], _NT)
        g_ref[...] = g.astype(BF16)
        up_ref[...] = up.astype(BF16)
        act_ref[...] = (_silu(g) * up).astype(BF16)

    whole = pl.BlockSpec((T, D), lambda j: (0, 0))
    wspec = pl.BlockSpec((FFN_TN, D), lambda j: (j, 0))
    col = pl.BlockSpec((T, FFN_TN), lambda j: (0, j))
    return _pcall(
        body, name=name, grid=(DFF // FFN_TN,), in_specs=[whole, pl.BlockSpec((1, D), lambda j: (0, 0)), wspec, wspec],
        out_specs=[whole, col, col, col],
        out_shape=[jax.ShapeDtypeStruct((T, D), BF16)] + [jax.ShapeDtypeStruct((T, DFF), BF16)] * 3,
        scratch_shapes=[pltpu.VMEM((T, D), BF16)], compiler_params=_cparams(("arbitrary",)),
    )(h, norm, w_gate, w_up)


def ffn_back(dh_b, w_down, g, up, after_row, name):
    has_row = after_row is not None

    def body(*refs):
        dh_ref, wd_ref, g_ref, up_ref = refs[:4]
        dg_ref, dup_ref = refs[-2:]
        da = _dot(dh_ref[...], wd_ref[...], _NT)
        if has_row:
            da = da + refs[4][...]
        g = g_ref[...].astype(F32)
        dg_ref[...] = (da * up_ref[...].astype(F32) * _dsilu(g)).astype(BF16)
        dup_ref[...] = (da * _silu(g)).astype(BF16)

    col = pl.BlockSpec((T, FFN_TN), lambda j: (0, j))
    in_specs = [pl.BlockSpec((T, D), lambda j: (0, 0)), pl.BlockSpec((FFN_TN, D), lambda j: (j, 0)), col, col]
    args = [dh_b, w_down, g, up]
    if has_row:
        in_specs.append(pl.BlockSpec((1, FFN_TN), lambda j: (0, j)))
        args.append(after_row)
    return _pcall(
        body, name=name, grid=(DFF // FFN_TN,), in_specs=in_specs, out_specs=[col, col],
        out_shape=[jax.ShapeDtypeStruct((T, DFF), BF16)] * 2, compiler_params=_cparams(("parallel",)),
    )(*args)


def ffn_layer_fwd(h, p, tag, order=()):
    u, g, up, act = ffn_in(h, _norm_after(p["norm"], order), p["w_gate"], p["w_up"], f"{tag}_in")
    h2 = matmul(act, p["w_down"], "nn", residual=h, name=f"{tag}_down")
    return h2, (h, u, g, up, act)


def ffn_layer_bwd(dh, dh_b, saved, p, tag, order=()):
    h, u, g, up, act = saved
    d_down = matmul(act, dh_b, "tn", out_dtype=BF16, name=f"{tag}_dwd")
    dg, dup = ffn_back(dh_b, p["w_down"], g, up, _after(order, DFF), f"{tag}_back")
    du = matmul(dg, p["w_gate"], "nn", name=f"{tag}_dug")
    du = matmul(dup, p["w_up"], "nn", residual=du, name=f"{tag}_duu")
    d_gate = matmul(dg, u, "tn", out_dtype=BF16, name=f"{tag}_dwg")
    d_up = matmul(dup, u, "tn", out_dtype=BF16, name=f"{tag}_dwu")
    dh2, dh2_b, d_norm = rms_bwd(du, h, p["norm"], dh, f"{tag}_drms")
    return dh2, dh2_b, dict(norm=d_norm, w_gate=d_gate, w_up=d_up, w_down=d_down)


def conv_layer_fwd(h, p, tag, order=()):
    u = rms_fwd(h, _norm_after(p["norm"], order), f"{tag}_rms")
    hh = matmul(u, p["w_pw1"], "nn", bias=p["b_pw1"], name=f"{tag}_pw1")
    gl = glu_fwd(hh, f"{tag}_glu")
    c2 = dwconv_fwd(gl, 0, p["dw_w"], p["dw_b"], KCV, 128, False, f"{tag}_dw")[0]
    s = ln_silu_fwd(c2, p["ln_g"], p["ln_b"], f"{tag}_ln")
    h2 = matmul(s, p["w_pw2"], "nn", bias=p["b_pw2"], residual=h, name=f"{tag}_pw2")
    return h2, (h, u, hh, gl, c2, s)


def conv_layer_bwd(dh, dh_b, saved, p, tag, order=()):
    h, u, hh, gl, c2, s = saved
    ds = matmul(dh_b, p["w_pw2"], "nt", bias=_after(order, D), name=f"{tag}_ds")
    d_pw2 = matmul(s, dh_b, "tn", out_dtype=BF16, name=f"{tag}_dwpw2")
    dc2, d_lng, d_lnb, d_bpw2 = ln_silu_bwd(ds, c2, dh, p["ln_g"], p["ln_b"], f"{tag}_dln")
    dgl, d_dww, d_dwb = dwconv_bwd(dc2, None, gl, 0, p["dw_w"], KCV, 128, False, F32, f"{tag}_ddw")
    dhh, d_bpw1 = glu_bwd(dgl, hh, f"{tag}_dglu")
    du = matmul(dhh, p["w_pw1"], "nt", name=f"{tag}_du")
    d_pw1 = matmul(u, dhh, "tn", out_dtype=BF16, name=f"{tag}_dwpw1")
    dh2, dh2_b, d_norm = rms_bwd(du, h, p["norm"], dh, f"{tag}_drms")
    grads = dict(norm=d_norm, w_pw1=d_pw1, b_pw1=d_bpw1, dw_w=d_dww[:KCV], dw_b=d_dwb, ln_g=d_lng, ln_b=d_lnb,
                 w_pw2=d_pw2, b_pw2=d_bpw2)
    return dh2, dh2_b, grads


def ssm_layer_fwd(h, p, tag, order=(), mid=None):
    u = rms_fwd(h, _norm_after(p["norm"], order), f"{tag}_rms")
    zx = matmul(u, p["w_in"], "nn", name=f"{tag}_in")
    cpre, xbc = dwconv_fwd(zx, DI // 512, p["conv_w"], p["conv_b"], KSSM, 512, True, f"{tag}_conv")
    dt, da, dtx, dax = dt_fwd(zx, p["dt_bias"], p["a_log"], f"{tag}_dt")
    daT = da[:, :NH].T
    y, states = ssd_fwd(xbc, dtx, dax, daT, p["d_full"], f"{tag}_ssd")
    gate_norm = p["gate_norm"] if mid is None else _norm_after(p["gate_norm"], mid(y))
    yn = gatenorm_fwd(y, zx, gate_norm, f"{tag}_gn")
    h2 = matmul(yn, p["w_out"], "nn", residual=h, name=f"{tag}_out")
    return h2, (h, u, zx, cpre, xbc, dt, dtx, dax, daT, y, states, yn)


def ssm_layer_bwd(dh, dh_b, saved, p, tag, order=()):
    h, u, zx, cpre, xbc, dt, dtx, dax, daT, y, states, yn = saved
    dyn = matmul(dh_b, p["w_out"], "nt", bias=_after(order, DI), name=f"{tag}_dyn")
    d_wout = matmul(yn, dh_b, "tn", out_dtype=BF16, name=f"{tag}_dwout")
    dy, dzx, d_gn = gatenorm_bwd(dyn, y, zx, p["gate_norm"], f"{tag}_dgn")
    dxbc, ddt, dda, dD = ssd_bwd(dy, xbc, dtx, dax, daT, p["d_full"], states, f"{tag}_dssd")
    dzx, d_dtb, d_alog = dt_bwd(ddt, dda, dt, zx, p["dt_bias"], p["a_log"], dzx, f"{tag}_ddt")
    dzx, d_cw, d_cb = dwconv_bwd(dxbc, cpre, zx, DI // 512, p["conv_w"], KSSM, 512, True, BF16, f"{tag}_dconv",
                                 into=(dzx, DI // 512))
    du = matmul(dzx, p["w_in"], "nt", name=f"{tag}_du")
    d_win = matmul(u, dzx, "tn", out_dtype=BF16, name=f"{tag}_dwin")
    dh2, dh2_b, d_norm = rms_bwd(du, h, p["norm"], dh, f"{tag}_drms")
    d_d = headsum(dD.reshape(NH, HD), f"{tag}_dD").reshape(NH)
    grads = dict(norm=d_norm, w_in=d_win[:, :DINP], conv_w=d_cw[:KSSM], conv_b=d_cb, dt_bias=d_dtb[0, :NH],
                 a_log=d_alog[0, :NH], d=d_d, gate_norm=d_gn, w_out=d_wout)
    return dh2, dh2_b, grads


BIG = ["ssm_w_in", "ssm_w_out", "cv_w_pw1", "cv_w_pw2", "ffn_w_gate", "ffn_w_up", "ffn_w_down"]
TRANSPOSED = ("ffn_w_gate", "ffn_w_up")
LAYER_AXIS = {"ssm_w_in": -1, "ssm_w_out": 0, "cv_w_pw1": -1, "cv_w_pw2": 0, "ffn_w_gate": 0, "ffn_w_up": 0,
              "ffn_w_down": 0}
SMALL_SHARDED = ["ssm_conv_w", "cv_norm", "cv_b_pw1", "cv_dw_w", "cv_dw_b", "cv_ln_g", "cv_ln_b", "cv_b_pw2"]
SMALL_REPL = ["ssm_norm", "ssm_conv_b", "ssm_dt_bias", "ssm_a_log", "ssm_d", "ssm_gate_norm", "ffn_norm", "final_norm"]
WEIGHTS = ["ssm_norm", "ssm_w_in", "ssm_conv_w", "ssm_conv_b", "ssm_dt_bias", "ssm_a_log", "ssm_d", "ssm_gate_norm",
           "ssm_w_out", "cv_norm", "cv_w_pw1", "cv_b_pw1", "cv_dw_w", "cv_dw_b", "cv_ln_g", "cv_ln_b", "cv_w_pw2",
           "cv_b_pw2", "ffn_norm", "ffn_w_gate", "ffn_w_up", "ffn_w_down", "final_norm"]
SMALL = [n for n in WEIGHTS if n not in BIG]


N_STAGES = 8


def _stage_layer(s):
    i = s // 2
    if s % 2:
        return "ffn", i
    return ("ssm" if i % 2 == 0 else "cv"), i // 2


def _stage_group(s):
    fam, l = _stage_layer(s)
    names = {"ffn": ["ffn_w_gate", "ffn_w_up", "ffn_w_down"], "ssm": ["ssm_w_in", "ssm_w_out"],
             "cv": ["cv_w_pw1", "cv_w_pw2"]}[fam]
    return [(n, l) for n in names]


def _stage_params(s, big, small):
    fam, l = _stage_layer(s)
    if fam == "ffn":
        return dict(norm=_row(small["ffn_norm"][l]), w_gate=big["ffn_w_gate"], w_up=big["ffn_w_up"],
                    w_down=big["ffn_w_down"])
    if fam == "ssm":
        return dict(norm=_row(small["ssm_norm"][l]), w_in=jnp.pad(big["ssm_w_in"], ((0, 0), (0, DINP_PAD - DINP))),
                    conv_w=jnp.pad(small["ssm_conv_w"][l], ((0, 8 - KSSM), (0, 0))), conv_b=_row(small["ssm_conv_b"][l]),
                    dt_bias=_row(small["ssm_dt_bias"][l], LANES), a_log=_row(small["ssm_a_log"][l], LANES),
                    d_full=_row(jnp.repeat(small["ssm_d"][l], HD)), gate_norm=_row(small["ssm_gate_norm"][l]),
                    w_out=big["ssm_w_out"])
    return dict(norm=_row(small["cv_norm"][l]), w_pw1=big["cv_w_pw1"], b_pw1=_row(small["cv_b_pw1"][l]),
                dw_w=jnp.pad(small["cv_dw_w"][l], ((0, 32 - KCV), (0, 0))), dw_b=_row(small["cv_dw_b"][l]),
                ln_g=_row(small["cv_ln_g"][l]), ln_b=_row(small["cv_ln_b"][l]), w_pw2=big["cv_w_pw2"],
                b_pw2=_row(small["cv_b_pw2"][l]))


_STAGE_FWD = {"ffn": ffn_layer_fwd, "ssm": ssm_layer_fwd, "cv": conv_layer_fwd}
_STAGE_BWD = {"ffn": ffn_layer_bwd, "ssm": ssm_layer_bwd, "cv": conv_layer_bwd}


def _stage_fwd(s, h, p, order=(), mid=None):
    fam, l = _stage_layer(s)
    if mid is not None:
        return ssm_layer_fwd(h, p, f"{fam}{l}", order, mid)
    return _STAGE_FWD[fam](h, p, f"{fam}{l}", order)


def _stage_bwd(s, dh, dh_b, p, saved, order=()):
    fam, l = _stage_layer(s)
    dh, dh_b, g = _STAGE_BWD[fam](dh, dh_b, saved, p, f"{fam}{l}", order)
    return dh, dh_b, {f"{fam}_{k}": val for k, val in g.items()}


def _local(x, tgt, full):
    h, tape = x, []
    for s in range(N_STAGES):
        big = {n: (full[n][l].T if n in TRANSPOSED else full[n][l]) for n, l in _stage_group(s)}
        p = _stage_params(s, big, full)
        h, saved = _stage_fwd(s, h, p)
        tape.append((p, saved))
    dh, dh_b, d_final, loss_row = loss_head(h, _row(full["final_norm"]), tgt, "loss_head")
    gl = {n: [None] * full[n].shape[0] for n in WEIGHTS if n != "final_norm"}
    for s in reversed(range(N_STAGES)):
        dh, dh_b, g = _stage_bwd(s, dh, dh_b, *tape[s])
        for n, val in g.items():
            val = val.T if n in TRANSPOSED else val
            gl[n][_stage_layer(s)[1]] = val.reshape(full[n].shape[1:])
    grads = {n: jnp.stack(vs) for n, vs in gl.items()}
    grads["final_norm"] = d_final.reshape(D)
    return loss_row, dh, grads


def _step(x, tgt, w, m, v):
    idx = 4 * lax.axis_index("x") + 2 * lax.axis_index("y") + lax.axis_index("c")
    small_shapes = [w[n].shape for n in SMALL_SHARDED]
    small_pack = _pack([w[n] for n in SMALL_SHARDED], _rows_for(small_shapes))

    def view(n, a):
        return jnp.swapaxes(a, 1, 2) if n in TRANSPOSED else a

    wv, mv, vv = ({n: view(n, t[n]) for n in BIG} for t in (w, m, v))

    def blocks(s):
        return [wv[n][l].astype(BF16) for n, l in _stage_group(s)] + ([small_pack] if s == 0 else [])

    arrs = [blocks(s) for s in range(N_STAGES)]
    first = gather_start(arrs[0], None, "gather0_start")
    passing = pass_start(gather_wait(first, first["token"], "gather0_wait"), "pass0_start")
    crossing = {1: gather_start(arrs[1], passing["token"], "gather1_start")}
    crossing[2] = gather_start(arrs[2], crossing[1]["token"], "gather2_start")
    small = {n: w[n] for n in SMALL_REPL}
    flight = dict(passing=passing)

    def advance(s, after):
        tokens = []
        if s + 1 < N_STAGES:
            landed = gather_wait(crossing.pop(s + 1), after, f"gather{s + 1}_wait")
            flight["passing"] = pass_start(landed, f"pass{s + 1}_start")
            tokens.append(flight["passing"]["token"])
        if s + 3 < N_STAGES:
            crossing[s + 3] = gather_start(arrs[s + 3], flight["passing"]["token"], f"gather{s + 3}_start")
            tokens.append(crossing[s + 3]["token"])
        return tokens

    h, tape, after = x, [], crossing[2]["token"]
    for s in range(N_STAGES):
        zones = pass_wait(flight["passing"], after, f"pass{s}_wait")
        order = advance(s, zones[0]) if s else []
        mid = functools.partial(advance, 0) if s == 0 else None
        zones = [lax.dynamic_update_slice_in_dim(z, a[None], idx, 0) for z, a in zip(zones, arrs[s])]
        if s == 0:
            per_dev = [_unpack(zones[-1][k], small_shapes) for k in range(N_DEV)]
            for q, n in enumerate(SMALL_SHARDED):
                small[n] = _unshard(jnp.stack([per_dev[k][q] for k in range(N_DEV)]), -1)
        big = {n: _unshard(z, LAYER_AXIS[n]) for (n, _), z in zip(_stage_group(s), zones)}
        p = _stage_params(s, big, small)
        h, saved = _stage_fwd(s, h, p, order, mid)
        tape.append((p, saved))
        after = h

    dh, dh_b, d_final, loss_row = loss_head(h, _row(w["final_norm"]), tgt, "loss_head")

    out = {}
    small_g = {n: [None] * w[n].shape[0] for n in SMALL if n != "final_norm"}

    def finish(s, st, after):
        by_chip, recv = scatter_wait(st, after, f"scatter{s}_wait")
        names = [n for n, _ in _stage_group(s)]
        res = adamw_stage(recv, by_chip, [wv[n] for n in names], [mv[n] for n in names], [vv[n] for n in names],
                          [l for _, l in _stage_group(s)], [out.get(n) for n in names], f"adamw_stage{s}")
        out.update(zip(names, res))

    started, order, summed = [], [], None
    for s in reversed(range(N_STAGES)):
        dh, dh_b, g = _stage_bwd(s, dh, dh_b, *tape[s], order)
        for n, val in g.items():
            if n not in BIG:
                small_g[n][_stage_layer(s)[1]] = val.reshape(small[n].shape[1:])
        if s == 0:
            grads = {n: jnp.stack(vs) for n, vs in small_g.items()}
            grads["final_norm"] = d_final.reshape(D)
            small_full_shapes = [grads[n].shape for n in SMALL] + [(1,)]
            packed = _pack([grads[n] for n in SMALL] + [loss_row[0, :1]], _rows_for(small_full_shapes))
            summed = sum_leading(all_gather([packed], "gather_small_grads")[0], "sum_small_grads")
        gsend = [_to_shards(g[n], LAYER_AXIS[n]) for n, _ in _stage_group(s)]
        from_sibling = sibling_exchange(gsend, f"scatter{s}_sibling", summed)
        by_chip = pair_sum(gsend, from_sibling, f"pair_sum{s}")
        started.append((s, scatter_start(by_chip, f"scatter{s}_start")))
        order = [started[-1][1]["token"]]
    last = started[-1][1]["token"]
    for s, st in started[:-1]:
        finish(s, st, last)
    parts = _unpack(summed + last[0:1, 0:1], small_full_shapes)
    loss = parts[-1][0]
    mine = []
    for n, g in zip(SMALL, parts[:-1]):
        if n in SMALL_SHARDED:
            s = w[n].shape[-1]
            g = lax.dynamic_slice_in_dim(g, idx * s, s, axis=g.ndim - 1)
        mine.append(g)
    out.update(zip(SMALL, adamw_small(mine, [w[n] for n in SMALL], [m[n] for n in SMALL], [v[n] for n in SMALL],
                                      "adamw_small")))
    done = [out[n][0].reshape(-1)[:1] for n in out]
    finish(*started[-1], functools.reduce(jnp.add, done))
    for n in TRANSPOSED:
        out[n] = [view(n, a) for a in out[n]]
    return loss, dh, out


def kernel(x, ssm_norm, ssm_w_in, ssm_conv_w, ssm_conv_b, ssm_dt_bias, ssm_a_log, ssm_d, ssm_gate_norm, ssm_w_out, cv_norm, cv_w_pw1, cv_b_pw1, cv_dw_w, cv_dw_b, cv_ln_g, cv_ln_b, cv_w_pw2, cv_b_pw2, ffn_norm, ffn_w_gate, ffn_w_up, ffn_w_down, final_norm, loss_target, m_ssm_norm, m_ssm_w_in, m_ssm_conv_w, m_ssm_conv_b, m_ssm_dt_bias, m_ssm_a_log, m_ssm_d, m_ssm_gate_norm, m_ssm_w_out, m_cv_norm, m_cv_w_pw1, m_cv_b_pw1, m_cv_dw_w, m_cv_dw_b, m_cv_ln_g, m_cv_ln_b, m_cv_w_pw2, m_cv_b_pw2, m_ffn_norm, m_ffn_w_gate, m_ffn_w_up, m_ffn_w_down, m_final_norm, v_ssm_norm, v_ssm_w_in, v_ssm_conv_w, v_ssm_conv_b, v_ssm_dt_bias, v_ssm_a_log, v_ssm_d, v_ssm_gate_norm, v_ssm_w_out, v_cv_norm, v_cv_w_pw1, v_cv_b_pw1, v_cv_dw_w, v_cv_dw_b, v_cv_ln_g, v_cv_ln_b, v_cv_w_pw2, v_cv_b_pw2, v_ffn_norm, v_ffn_w_gate, v_ffn_w_up, v_ffn_w_down, v_final_norm):
    args = locals()
    w = {n: args[n] for n in WEIGHTS}
    m = {n: args["m_" + n] for n in WEIGHTS}
    v = {n: args["v_" + n] for n in WEIGHTS}
    loss, grad_x, out = _step(x[0], loss_target[0], w, m, v)
    res = [loss, grad_x[None]]
    for k in range(4):
        res += [out[n][k] for n in WEIGHTS]
    return tuple(res)
```

```python
import functools
import math

import jax
import jax.numpy as jnp
from jax import lax
from jax.experimental import pallas as pl
from jax.experimental.pallas import tpu as pltpu

F32 = jnp.float32
BF16 = jnp.bfloat16

N_DEV = 8
T = 2048
D = 1024
DI = 2048
NH = 32
HD = 64
NG = 4
GW = DI // NG
DS = 128
CONVD = DI + 2 * NG * DS
DINP = 2 * DI + 2 * NG * DS + NH
DINP_PAD = 5376
CH = 128
NCH = T // CH
DFF = 2816
KSSM = 4
KCV = 31
EPS = 1e-5
LANES = 128
VMEM_LIMIT = 56 * 1024 * 1024

ADAM_LR = 0.001
ADAM_B1 = 0.9
ADAM_B2 = 0.999
ADAM_EPS = 1e-08
ADAM_WD = 0.01
ADAM_STEP = 10

MESH = pl.DeviceIdType.MESH
ANY = pl.BlockSpec(memory_space=pl.ANY)


def _pcall(body, **kw):
    return pl.pallas_call(body, **kw)


def _cparams(sem):
    return pltpu.CompilerParams(dimension_semantics=sem, vmem_limit_bytes=VMEM_LIMIT)


def _pick(n, cands):
    for c in cands:
        if n % c == 0:
            return c
    raise ValueError(f"no tile for {n}")


def _sigmoid(x):
    return 1.0 / (1.0 + jnp.exp(-x))


def _silu(x):
    return x * _sigmoid(x)


def _dsilu(x):
    s = _sigmoid(x)
    return s * (1.0 + x * (1.0 - s))


_DIMS = {"nn": (((1,), (0,)), ((), ())), "nt": (((1,), (1,)), ((), ())), "tn": (((0,), (0,)), ((), ()))}


MM_VMEM_BUDGET = 40 * 1024 * 1024
MM_MAX_K = 3072


def _mm_tiles(M, N, K, out_bytes, has_res):
    tk = K if K <= MM_MAX_K else K // 2
    assert K % tk == 0 and tk % LANES == 0
    nk = K // tk
    best = None
    for tm in (2048, 1792, 1408, 1024, 768, 512, 256, 128):
        if M % tm:
            continue
        for tn in (1408, 1024, 768, 512, 384, 256, 128):
            if N % tn:
                continue
            blocks = tm * tk * 2 + tk * tn * 2 + tm * tn * out_bytes + (tm * tn * 4 if has_res else 0)
            vmem = 2 * blocks + tm * tn * 4 * (2 if nk > 1 else 1)
            if vmem > MM_VMEM_BUDGET:
                continue
            traffic = (N // tn if nk > 1 else 1) * M * K + (M // tm) * N * K
            key = (-traffic, tm * tn)
            if best is None or key > best[0]:
                best = (key, tm, tn)
    assert best is not None, (M, N, K)
    return best[1], best[2], tk


def matmul(a, b, mode, *, name, bias=None, residual=None, out_dtype=F32):
    assert a.dtype == BF16 and b.dtype == BF16
    if mode == "nn":
        (M, K), (K2, N) = a.shape, b.shape
    elif mode == "nt":
        (M, K), (N, K2) = a.shape, b.shape
    else:
        (K, M), (K2, N) = a.shape, b.shape
    assert K == K2
    has_bias, has_res = bias is not None, residual is not None
    tm, tn, tk = _mm_tiles(M, N, K, jnp.dtype(out_dtype).itemsize, has_res)
    nk = K // tk
    dims = _DIMS[mode]

    def body(*refs):
        a_ref, b_ref = refs[0], refs[1]
        pos = 2
        bias_ref = res_ref = None
        if has_bias:
            bias_ref = refs[pos]
            pos += 1
        if has_res:
            res_ref = refs[pos]
            pos += 1
        o_ref = refs[pos]

        def finish(out):
            if has_bias:
                out = out + bias_ref[...]
            if has_res:
                out = out + res_ref[...]
            o_ref[...] = out.astype(o_ref.dtype)

        part = lax.dot_general(a_ref[...], b_ref[...], dims, preferred_element_type=F32)
        if nk == 1:
            finish(part)
            return
        acc = refs[pos + 1]
        k = pl.program_id(2)

        @pl.when(k == 0)
        def _():
            acc[...] = part

        @pl.when(jnp.logical_and(k > 0, k < nk - 1))
        def _():
            acc[...] += part

        @pl.when(k == nk - 1)
        def _():
            finish(acc[...] + part)

    if mode == "tn":
        a_spec = pl.BlockSpec((tk, tm), lambda i, j, k: (k, i))
    else:
        a_spec = pl.BlockSpec((tm, tk), lambda i, j, k: (i, k))
    if mode == "nt":
        b_spec = pl.BlockSpec((tn, tk), lambda i, j, k: (j, k))
    else:
        b_spec = pl.BlockSpec((tk, tn), lambda i, j, k: (k, j))
    in_specs, args = [a_spec, b_spec], [a, b]
    if has_bias:
        in_specs.append(pl.BlockSpec((1, tn), lambda i, j, k: (0, j)))
        args.append(bias.reshape(1, N).astype(F32))
    if has_res:
        in_specs.append(pl.BlockSpec((tm, tn), lambda i, j, k: (i, j)))
        args.append(residual)
    return _pcall(
        body, name=name, grid=(M // tm, N // tn, nk), in_specs=in_specs,
        out_specs=pl.BlockSpec((tm, tn), lambda i, j, k: (i, j)),
        out_shape=jax.ShapeDtypeStruct((M, N), out_dtype),
        scratch_shapes=[pltpu.VMEM((tm, tn), F32)] if nk > 1 else [],
        compiler_params=_cparams(("parallel", "parallel", "arbitrary")),
    )(*args)


def rowwise(fn, rows, bcasts, outs, accs=(), *, name, tm=256, fill=None):
    n_rows, n_b, n_o, n_a = len(rows), len(bcasts), len(outs), len(accs)
    n_in = n_rows + n_b + (1 if fill is not None else 0)
    outs = [o if len(o) == 4 else (o[0], o[1], o[0], 0) for o in outs]
    nt = T // tm

    def body(*refs):
        ins = [r[...] for r in refs[:n_rows + n_b]]
        res = fn(*ins)
        o_refs = refs[n_in:n_in + n_o]
        a_refs = refs[n_in + n_o:]
        for r, v in zip(o_refs, res[:n_o]):
            r[...] = v.astype(r.dtype)
        if n_a:
            i = pl.program_id(0)

            @pl.when(i == 0)
            def _():
                for r in a_refs:
                    r[...] = jnp.zeros_like(r)

            for r, v in zip(a_refs, res[n_o:]):
                r[...] += v

    in_specs = [pl.BlockSpec((tm, w), functools.partial(lambda i, cb: (i, cb), cb=cb)) for (_, w, cb) in rows]
    in_specs += [pl.BlockSpec(b.shape, lambda i: (0, 0)) for b in bcasts]
    out_specs = [pl.BlockSpec((tm, w), functools.partial(lambda i, cb: (i, cb), cb=cb)) for (w, _, _, cb) in outs]
    out_specs += [pl.BlockSpec((1, w), lambda i: (0, 0)) for w in accs]
    out_shape = [jax.ShapeDtypeStruct((T, whole), dt) for (_, dt, whole, _) in outs]
    out_shape += [jax.ShapeDtypeStruct((1, w), F32) for w in accs]
    args = [r[0] for r in rows] + list(bcasts)
    aliases = {}
    if fill is not None:
        in_specs.append(ANY)
        args.append(fill[0])
        aliases = {n_in - 1: fill[1]}
    return _pcall(
        body, name=name, grid=(nt,), in_specs=in_specs, out_specs=out_specs, out_shape=out_shape,
        input_output_aliases=aliases, compiler_params=_cparams(("arbitrary",)),
    )(*args)


def _full(a):
    return (a, a.shape[1], 0)


def _rsum(v):
    return jnp.sum(v, axis=0, keepdims=True)


def rms_fwd(h, g, name):
    def fn(x, g):
        r = lax.rsqrt(jnp.mean(x * x, axis=-1, keepdims=True) + EPS)
        return (x * r * g,)
    return rowwise(fn, [_full(h)], [g], [(D, BF16)], name=name)[0]


def rms_bwd(du, h, g, dres, name):
    def fn(du, x, dres, g):
        r = lax.rsqrt(jnp.mean(x * x, axis=-1, keepdims=True) + EPS)
        xh = x * r
        dxh = du * g
        dx = r * (dxh - xh * jnp.mean(dxh * xh, axis=-1, keepdims=True))
        dh = dres + dx
        return dh, dh, _rsum(du * xh)
    return rowwise(fn, [_full(du), _full(h), _full(dres)], [g], [(D, F32), (D, BF16)], [D], name=name)


def loss_head(h, g, tgt, name):
    def fn(x, tgt, g):
        r = lax.rsqrt(jnp.mean(x * x, axis=-1, keepdims=True) + EPS)
        xh = x * r
        err = xh * g - tgt
        lsum = jnp.sum(jnp.sum(err * err, axis=-1, keepdims=True), axis=0, keepdims=True) * (0.5 / D)
        dy = err * (1.0 / D)
        dxh = dy * g
        dx = r * (dxh - xh * jnp.mean(dxh * xh, axis=-1, keepdims=True))
        return dx, dx, _rsum(dy * xh), jnp.broadcast_to(lsum, (1, LANES))
    return rowwise(fn, [_full(h), _full(tgt)], [g], [(D, F32), (D, BF16)], [D, LANES], name=name)


def glu_fwd(hh, name):
    def fn(a, g):
        return (a * _sigmoid(g),)
    return rowwise(fn, [(hh, D, 0), (hh, D, 1)], [], [(D, F32)], name=name)[0]


def glu_bwd(dgl, hh, name):
    def fn(dgl, a, g):
        s = _sigmoid(g)
        dhh = jnp.concatenate([dgl * s, dgl * a * s * (1.0 - s)], axis=1)
        return dhh, _rsum(dhh)
    return rowwise(fn, [_full(dgl), (hh, D, 0), (hh, D, 1)], [], [(2 * D, BF16)], [2 * D], name=name)


def ln_silu_fwd(c2, g, b, name):
    def fn(x, g, b):
        mu = jnp.mean(x, axis=-1, keepdims=True)
        xc = x - mu
        r = lax.rsqrt(jnp.mean(xc * xc, axis=-1, keepdims=True) + EPS)
        return (_silu(xc * r * g + b),)
    return rowwise(fn, [_full(c2)], [g, b], [(D, BF16)], name=name)[0]


def ln_silu_bwd(ds, c2, dh, g, b, name):
    def fn(ds, x, dh, g, b):
        mu = jnp.mean(x, axis=-1, keepdims=True)
        xc = x - mu
        r = lax.rsqrt(jnp.mean(xc * xc, axis=-1, keepdims=True) + EPS)
        xh = xc * r
        dn = ds * _dsilu(xh * g + b)
        dxh = dn * g
        dx = r * (dxh - jnp.mean(dxh, axis=-1, keepdims=True) - xh * jnp.mean(dxh * xh, axis=-1, keepdims=True))
        return dx, _rsum(dn * xh), _rsum(dn), _rsum(dh)
    return rowwise(fn, [_full(ds), _full(c2), _full(dh)], [g, b], [(D, F32)], [D, D, D], name=name)


def gatenorm_fwd(y, zx, gn, name):
    def fn(y, z, gn):
        hg = y * _silu(z)
        parts = []
        for k in range(NG):
            hk = hg[:, k * GW:(k + 1) * GW]
            parts.append(hk * lax.rsqrt(jnp.mean(hk * hk, axis=-1, keepdims=True) + EPS))
        return (jnp.concatenate(parts, axis=1) * gn,)
    return rowwise(fn, [_full(y), (zx, DI, 0)], [gn], [(DI, BF16)], name=name)[0]


def gatenorm_bwd(dyn, y, zx, gn, name):
    def fn(dyn, y, z, gn):
        sz = _silu(z)
        hg = y * sz
        dxh = dyn * gn
        dhg, xhs = [], []
        for k in range(NG):
            sl = slice(k * GW, (k + 1) * GW)
            hk = hg[:, sl]
            r = lax.rsqrt(jnp.mean(hk * hk, axis=-1, keepdims=True) + EPS)
            xh = hk * r
            dk = dxh[:, sl]
            dhg.append(r * (dk - xh * jnp.mean(dk * xh, axis=-1, keepdims=True)))
            xhs.append(xh)
        dhg = jnp.concatenate(dhg, axis=1)
        xh = jnp.concatenate(xhs, axis=1)
        return dhg * sz, dhg * y * _dsilu(z), _rsum(dyn * xh)
    return rowwise(fn, [_full(dyn), _full(y), (zx, DI, 0)], [gn], [(DI, F32), (DI, BF16, DINP_PAD, 0)], [DI], name=name)


def _softplus(x):
    return jnp.maximum(x, 0.0) + jnp.log(1.0 + jnp.exp(-jnp.abs(x)))


def _spread(v, e):
    hi = v.astype(BF16)
    r = v - hi.astype(F32)
    mid = r.astype(BF16)
    lo = (r - mid.astype(F32)).astype(BF16)
    return _dot(hi, e) + _dot(mid, e) + _dot(lo, e)


def _spread2(v, e):
    hi = v.astype(BF16)
    lo = (v - hi.astype(F32)).astype(BF16)
    return _dot(hi, e) + _dot(lo, e)


def dt_fwd(zx, dt_bias, a_log, name):
    heads = (jnp.arange(DI)[None, :] // HD == jnp.arange(LANES)[:, None]).astype(BF16)

    def fn(raw, bias, a_log, e):
        dt = _softplus(raw + bias)
        da = dt * (-jnp.exp(a_log))
        return dt, da, _spread(dt, e), _spread(da, e)

    return rowwise(fn, [(zx, LANES, (2 * DI + 2 * NG * DS) // LANES)], [dt_bias, a_log, heads],
                   [(LANES, F32), (LANES, F32), (DI, F32), (DI, F32)], name=name)


def dt_bwd(ddt, dda, dt, zx, dt_bias, a_log, dzx, name):
    def fn(ddt, dda, dt, raw, bias, a_log):
        a = -jnp.exp(a_log)
        draw = (ddt + dda * a) * _sigmoid(raw + bias)
        return jnp.concatenate([draw, jnp.zeros_like(draw)], axis=1), _rsum(draw), _rsum(dda * dt) * a
    return rowwise(fn, [_full(ddt), _full(dda), _full(dt), (zx, LANES, (2 * DI + 2 * NG * DS) // LANES)],
                   [dt_bias, a_log], [(2 * LANES, BF16, DINP_PAD, DINP_PAD // (2 * LANES) - 1)], [LANES, LANES],
                   name=name, fill=(dzx, 0))


def headsum(v, name):
    def body(v_ref, o_ref):
        o_ref[...] = jnp.sum(v_ref[...], axis=1, keepdims=True)
    return _pcall(body, name=name, out_shape=jax.ShapeDtypeStruct((v.shape[0], 1), F32))(v)


CONV_ROWS = 256


def _shifted(win, o, rows):
    if o == 0:
        return win[0:rows]
    n = win.shape[0]
    return pltpu.roll(win, shift=n - o, axis=0)[0:rows]


def dwconv_fwd(x, x_cb0, w, b, K, ct, act, name):
    C = w.shape[1]
    pad = 8 if K <= 8 else 32
    KP = w.shape[0]
    n_out = 2 if act else 1

    def body(x_ref, w_ref, b_ref, *rest):
        o_refs, px = rest[:n_out], rest[n_out]
        px[0:pad, :] = jnp.zeros((pad, ct), F32)
        px[pad:pad + T, :] = x_ref[...]
        wv = w_ref[...]
        bv = b_ref[...]
        for r0 in range(0, T, CONV_ROWS):
            win = px[r0:r0 + CONV_ROWS + pad, :]
            acc = jnp.broadcast_to(bv, (CONV_ROWS, ct))
            for k in range(K):
                acc = acc + wv[k:k + 1, :] * _shifted(win, pad - (K - 1) + k, CONV_ROWS)
            o_refs[0][r0:r0 + CONV_ROWS, :] = acc
            if act:
                o_refs[1][r0:r0 + CONV_ROWS, :] = _silu(acc)

    return _pcall(
        body, name=name, grid=(C // ct,),
        in_specs=[pl.BlockSpec((T, ct), lambda j: (0, x_cb0 + j)), pl.BlockSpec((KP, ct), lambda j: (0, j)),
                  pl.BlockSpec((1, ct), lambda j: (0, j))],
        out_specs=[pl.BlockSpec((T, ct), lambda j: (0, j))] * n_out,
        out_shape=[jax.ShapeDtypeStruct((T, C), F32)] * n_out,
        scratch_shapes=[pltpu.VMEM((T + pad, ct), F32)],
        compiler_params=_cparams(("parallel",)),
    )(x, w, b)


def dwconv_bwd(dout, cpre, x, x_cb0, w, K, ct, act, out_dtype, name, into=None):
    C = w.shape[1]
    pad = 8 if K <= 8 else 32
    KP = w.shape[0]

    def body(*refs):
        dx_ref, dw_ref, db_ref, px, pd = refs[-5:]
        if act:
            d_ref, c_ref, x_ref, w_ref = refs[:4]
        else:
            d_ref, x_ref, w_ref = refs[:3]
        px[0:pad, :] = jnp.zeros((pad, ct), F32)
        px[pad:pad + T, :] = x_ref[...]
        pd[T:T + pad, :] = jnp.zeros((pad, ct), F32)
        if act:
            pd[0:T, :] = d_ref[...] * _dsilu(c_ref[...])
        else:
            pd[0:T, :] = d_ref[...]
        wv = w_ref[...]
        dws = [jnp.zeros((1, ct), F32) for _ in range(K)]
        db = jnp.zeros((1, ct), F32)
        for r0 in range(0, T, CONV_ROWS):
            dwin = pd[r0:r0 + CONV_ROWS + pad, :]
            xwin = px[r0:r0 + CONV_ROWS + pad, :]
            dc = dwin[0:CONV_ROWS]
            db = db + _rsum(dc)
            acc = jnp.zeros((CONV_ROWS, ct), F32)
            for k in range(K):
                acc = acc + wv[k:k + 1, :] * _shifted(dwin, K - 1 - k, CONV_ROWS)
                dws[k] = dws[k] + _rsum(dc * _shifted(xwin, pad - (K - 1) + k, CONV_ROWS))
            dx_ref[r0:r0 + CONV_ROWS, :] = acc.astype(dx_ref.dtype)
        dw_ref[...] = jnp.zeros((KP, ct), F32)
        for k in range(K):
            dw_ref[k:k + 1, :] = dws[k]
        db_ref[...] = db

    col = pl.BlockSpec((T, ct), lambda j: (0, j))
    in_specs = [col] + ([col] if act else []) + [pl.BlockSpec((T, ct), lambda j: (0, x_cb0 + j)),
                                                 pl.BlockSpec((KP, ct), lambda j: (0, j))]
    args = [dout] + ([cpre] if act else []) + [x, w]
    dx_spec, dx_shape, aliases = col, jax.ShapeDtypeStruct((T, C), out_dtype), {}
    if into is not None:
        dx_spec = pl.BlockSpec((T, ct), lambda j: (0, into[1] + j))
        dx_shape = jax.ShapeDtypeStruct(into[0].shape, into[0].dtype)
        aliases = {len(args): 0}
        in_specs.append(ANY)
        args.append(into[0])
    return _pcall(
        body, name=name, grid=(C // ct,), in_specs=in_specs,
        out_specs=[dx_spec, pl.BlockSpec((KP, ct), lambda j: (0, j)), pl.BlockSpec((1, ct), lambda j: (0, j))],
        out_shape=[dx_shape, jax.ShapeDtypeStruct((KP, C), F32), jax.ShapeDtypeStruct((1, C), F32)],
        input_output_aliases=aliases,
        scratch_shapes=[pltpu.VMEM((T + pad, ct), F32), pltpu.VMEM((T + pad, ct), F32)],
        compiler_params=_cparams(("parallel",)),
    )(*args)


def _scan(a, axis, reverse=False):
    n = a.shape[axis]
    idx = lax.broadcasted_iota(jnp.int32, a.shape, axis)
    s = 1
    while s < n:
        if reverse:
            a = a + jnp.where(idx < n - s, pltpu.roll(a, shift=n - s, axis=axis), 0.0)
        else:
            a = a + jnp.where(idx >= s, pltpu.roll(a, shift=s, axis=axis), 0.0)
        s *= 2
    return a


_NT = _DIMS["nt"]
_TN = _DIMS["tn"]


def _dot(a, b, dims=_DIMS["nn"]):
    return lax.dot_general(a, b, dims, preferred_element_type=F32)


def ssd_fwd(xbc, dtx, dax, daT, dfull, name):
    def body(xbc_ref, dtx_ref, dax_ref, daT_ref, df_ref, y_ref, st_ref, S):
        ci = pl.program_id(0)

        @pl.when(ci == 0)
        def _():
            S[...] = jnp.zeros_like(S)

        row = lax.broadcasted_iota(jnp.int32, (CH, CH), 0)
        lane = lax.broadcasted_iota(jnp.int32, (CH, CH), 1)
        acsT = _scan(daT_ref[...], 1)
        for g in range(NG):
            c0 = g * GW
            xs = xbc_ref[:, c0:c0 + GW]
            acs = _scan(dax_ref[:, c0:c0 + GW], 0)
            Bm = xbc_ref[:, DI + g * DS:DI + (g + 1) * DS].astype(BF16)
            Cm = xbc_ref[:, DI + NG * DS + g * DS:DI + NG * DS + (g + 1) * DS].astype(BF16)
            xdt = xs * dtx_ref[:, c0:c0 + GW]
            atot = acs[CH - 1:CH, :]
            Sg = S[:, c0:c0 + GW]
            st_ref[:, c0:c0 + GW] = Sg
            CB = _dot(Cm, Bm, _NT)
            yg = jnp.exp(acs) * _dot(Cm, Sg.astype(BF16)) + xs * df_ref[:, c0:c0 + GW]
            xd = (xdt * jnp.exp(atot - acs)).astype(BF16)
            S[:, c0:c0 + GW] = jnp.exp(atot) * Sg + _dot(Bm, xd, _TN)
            xdt_b = xdt.astype(BF16)
            for r in range(NH // NG):
                h = g * (NH // NG) + r
                hs = slice(r * HD, (r + 1) * HD)
                seg = acs[:, r * HD:r * HD + 1] - acsT[h:h + 1, :]
                Lm = jnp.where(row >= lane, jnp.exp(jnp.minimum(seg, 0.0)), 0.0)
                yd = _dot((CB * Lm).astype(BF16), xdt_b[:, hs])
                y_ref[:, c0 + r * HD:c0 + (r + 1) * HD] = yg[:, hs] + yd

    return _pcall(
        body, name=name, grid=(NCH,),
        in_specs=[pl.BlockSpec((CH, CONVD), lambda i: (i, 0)), pl.BlockSpec((CH, DI), lambda i: (i, 0)),
                  pl.BlockSpec((CH, DI), lambda i: (i, 0)), pl.BlockSpec((NH, CH), lambda i: (0, i)),
                  pl.BlockSpec((1, DI), lambda i: (0, 0))],
        out_specs=[pl.BlockSpec((CH, DI), lambda i: (i, 0)), pl.BlockSpec((None, DS, DI), lambda i: (i, 0, 0))],
        out_shape=[jax.ShapeDtypeStruct((T, DI), F32), jax.ShapeDtypeStruct((NCH, DS, DI), F32)],
        scratch_shapes=[pltpu.VMEM((DS, DI), F32)],
        compiler_params=_cparams(("arbitrary",)),
    )(xbc, dtx, dax, daT, dfull)


def ssd_bwd(dy, xbc, dtx, dax, daT, dfull, states, name):
    hsum = (jnp.arange(DI)[:, None] // HD == jnp.arange(LANES)[None, :]).astype(BF16).reshape(NG, GW, LANES)

    def body(dy_ref, xbc_ref, dtx_ref, dax_ref, daT_ref, df_ref, st_ref, hsum_ref, dxbc_ref, ddt_ref, dda_ref, dD_ref, dS):
        i = pl.program_id(0)

        @pl.when(i == 0)
        def _():
            dS[...] = jnp.zeros_like(dS)
            dD_ref[...] = jnp.zeros_like(dD_ref)

        row = lax.broadcasted_iota(jnp.int32, (CH, CH), 0)
        lane = lax.broadcasted_iota(jnp.int32, (CH, CH), 1)
        acsT = _scan(daT_ref[...], 1)
        ddt_all = jnp.zeros((CH, LANES), F32)
        dacs_all = jnp.zeros((CH, LANES), F32)
        colacc = jnp.zeros((CH, CH), F32)
        for g in range(NG):
            c0 = g * GW
            xs = xbc_ref[:, c0:c0 + GW]
            dtx = dtx_ref[:, c0:c0 + GW]
            acs = _scan(dax_ref[:, c0:c0 + GW], 0)
            Bm = xbc_ref[:, DI + g * DS:DI + (g + 1) * DS].astype(BF16)
            Cm = xbc_ref[:, DI + NG * DS + g * DS:DI + NG * DS + (g + 1) * DS].astype(BF16)
            xdt = xs * dtx
            atot = acs[CH - 1:CH, :]
            Sin = st_ref[:, c0:c0 + GW]
            dyg = dy_ref[:, c0:c0 + GW]
            dSo = dS[:, c0:c0 + GW]
            E = jnp.exp(acs)
            Etot = jnp.exp(atot)
            dec = jnp.exp(atot - acs)
            dD_ref[:, c0:c0 + GW] += _rsum(dyg * xs)
            dxs = dyg * df_ref[:, c0:c0 + GW]
            Sin_b = Sin.astype(BF16)
            dSo_b = dSo.astype(BF16)
            dY0 = dyg * E
            dY0_b = dY0.astype(BF16)
            dC = _dot(dY0_b, Sin_b, _NT)
            dS[:, c0:c0 + GW] = _dot(Cm, dY0_b, _TN) + Etot * dSo
            XD = xdt * dec
            dXD = _dot(Bm, dSo_b)
            dB = _dot(XD.astype(BF16), dSo_b, _NT)
            dxdt = dXD * dec
            Gq = dXD * XD
            dacs_x = dY0 * _dot(Cm, Sin_b) - Gq
            datot_x = _rsum(dSo * Sin) * Etot + _rsum(Gq)
            dacs_all = dacs_all + _spread2(dacs_x, hsum_ref[g])
            dtot8 = _spread2(jnp.broadcast_to(datot_x, (8, GW)), hsum_ref[g])
            dacs_all = dacs_all + jnp.where(row == CH - 1, jnp.broadcast_to(dtot8[0:1, :], (CH, LANES)), 0.0)
            CB = _dot(Cm, Bm, _NT)
            dCB = jnp.zeros((CH, CH), F32)
            xdt_b = xdt.astype(BF16)
            dy_b = dyg.astype(BF16)
            for r in range(NH // NG):
                h = g * (NH // NG) + r
                hs = slice(r * HD, (r + 1) * HD)
                seg = acs[:, r * HD:r * HD + 1] - acsT[h:h + 1, :]
                Lm = jnp.where(row >= lane, jnp.exp(jnp.minimum(seg, 0.0)), 0.0)
                dyr = dy_b[:, hs]
                dML = _dot(dyr, xdt_b[:, hs], _NT) * Lm
                dxbc_ref[:, c0 + r * HD:c0 + (r + 1) * HD] = _dot((CB * Lm).astype(BF16), dyr, _TN)
                dCB = dCB + dML
                dseg = dML * CB
                dacs_all = dacs_all + _spread2(dseg, (lane == h).astype(BF16))
                colacc = colacc + jnp.where(row == h, jnp.sum(dseg, axis=0, keepdims=True), 0.0)
            dxdt = dxdt + dxbc_ref[:, c0:c0 + GW]
            ddt_all = ddt_all + _spread2(dxdt * xs, hsum_ref[g])
            dxbc_ref[:, c0:c0 + GW] = dxs + dxdt * dtx
            dCB_b = dCB.astype(BF16)
            dxbc_ref[:, DI + g * DS:DI + (g + 1) * DS] = dB + _dot(dCB_b, Cm, _TN)
            dxbc_ref[:, DI + NG * DS + g * DS:DI + NG * DS + (g + 1) * DS] = dC + _dot(dCB_b, Bm)
        ddt_ref[...] = ddt_all
        dda_ref[...] = _scan(dacs_all - colacc.T, 0, reverse=True)

    last = NCH - 1
    return _pcall(
        body, name=name, grid=(NCH,),
        in_specs=[pl.BlockSpec((CH, DI), lambda i: (last - i, 0)), pl.BlockSpec((CH, CONVD), lambda i: (last - i, 0)),
                  pl.BlockSpec((CH, DI), lambda i: (last - i, 0)), pl.BlockSpec((CH, DI), lambda i: (last - i, 0)),
                  pl.BlockSpec((NH, CH), lambda i: (0, last - i)), pl.BlockSpec((1, DI), lambda i: (0, 0)),
                  pl.BlockSpec((None, DS, DI), lambda i: (last - i, 0, 0)),
                  pl.BlockSpec((NG, GW, LANES), lambda i: (0, 0, 0))],
        out_specs=[pl.BlockSpec((CH, CONVD), lambda i: (last - i, 0)), pl.BlockSpec((CH, LANES), lambda i: (last - i, 0)),
                   pl.BlockSpec((CH, LANES), lambda i: (last - i, 0)), pl.BlockSpec((1, DI), lambda i: (0, 0))],
        out_shape=[jax.ShapeDtypeStruct((T, CONVD), F32), jax.ShapeDtypeStruct((T, LANES), F32),
                   jax.ShapeDtypeStruct((T, LANES), F32), jax.ShapeDtypeStruct((1, DI), F32)],
        scratch_shapes=[pltpu.VMEM((DS, DI), F32)],
        compiler_params=_cparams(("arbitrary",)),
    )(dy, xbc, dtx, dax, daT, dfull, states, hsum)


def _as3d(shape):
    if len(shape) == 1:
        return (1, 1, shape[0])
    if len(shape) == 2:
        return (1, shape[0], shape[1])
    return (math.prod(shape[:-2]), shape[-2], shape[-1])


def adamw_small(gs, ws, ms, vs, name):
    n = len(ws)
    bc1 = 1.0 - ADAM_B1 ** ADAM_STEP
    bc2 = 1.0 - ADAM_B2 ** ADAM_STEP

    def body(*refs):
        for a in range(n):
            g_ref, w_ref, m_ref, v_ref = refs[4 * a:4 * a + 4]
            g_out, d_out, m_out, v_out = refs[4 * n + 4 * a:4 * n + 4 * a + 4]
            g = g_ref[...]
            mn = ADAM_B1 * m_ref[...] + (1.0 - ADAM_B1) * g
            vn = ADAM_B2 * v_ref[...] + (1.0 - ADAM_B2) * (g * g)
            g_out[...] = g
            m_out[...] = mn
            v_out[...] = vn
            d_out[...] = -ADAM_LR * ((mn / bc1) / (jnp.sqrt(vn / bc2) + ADAM_EPS) + ADAM_WD * w_ref[...])

    args, out_shape = [], []
    for g, w, m, v in zip(gs, ws, ms, vs):
        s3 = _as3d(w.shape)
        args += [t.reshape(s3) for t in (g, w, m, v)]
        out_shape += [jax.ShapeDtypeStruct(s3, F32)] * 4
    outs = _pcall(body, name=name, out_shape=out_shape)(*args)
    return [[o.reshape(w.shape) for o in outs[4 * a:4 * a + 4]] for a, w in enumerate(ws)]


ADAMW_STEPS = 4


def adamw_stage(recvs, owns, ws, ms, vs, layers, prevs, name):
    n = len(ws)
    chained = prevs[0] is not None
    assert all((p is not None) == chained for p in prevs)
    bc1 = 1.0 - ADAM_B1 ** ADAM_STEP
    bc2 = 1.0 - ADAM_B2 ** ADAM_STEP
    n_in = (9 if chained else 5) * n

    def body(*refs):
        for a in range(n):
            r_ref, o_ref, w_ref, m_ref, v_ref = refs[5 * a:5 * a + 5]
            g_out, d_out, m_out, v_out = refs[n_in + 4 * a:n_in + 4 * a + 4]
            g = o_ref[...].astype(F32)
            for k in range(r_ref.shape[0]):
                g = g + r_ref[k].astype(F32)
            mn = ADAM_B1 * m_ref[...] + (1.0 - ADAM_B1) * g
            vn = ADAM_B2 * v_ref[...] + (1.0 - ADAM_B2) * (g * g)
            g_out[...] = g
            m_out[...] = mn
            v_out[...] = vn
            d_out[...] = -ADAM_LR * ((mn / bc1) / (jnp.sqrt(vn / bc2) + ADAM_EPS) + ADAM_WD * w_ref[...])

    in_specs, args, out_specs, out_shape = [], [], [], []
    for a in range(n):
        _, R, C = ws[a].shape
        tr = R // ADAMW_STEPS
        assert tr * ADAMW_STEPS == R and tr % 8 == 0
        slot = pl.BlockSpec((None, tr, C), functools.partial(lambda r, l: (l, r, 0), l=layers[a]))
        own = pl.BlockSpec((None, tr, C), lambda r: (2 * lax.axis_index("x") + lax.axis_index("y"), r, 0))
        in_specs += [pl.BlockSpec((recvs[a].shape[0], tr, C), lambda r: (0, r, 0)), own, slot, slot, slot]
        args += [recvs[a], owns[a], ws[a], ms[a], vs[a]]
        out_specs += [slot] * 4
        out_shape += [jax.ShapeDtypeStruct(ws[a].shape, F32)] * 4
    aliases = {}
    if chained:
        for a in range(n):
            in_specs += [ANY] * 4
            args += list(prevs[a])
            aliases.update({5 * n + 4 * a + k: 4 * a + k for k in range(4)})
    outs = _pcall(
        body, name=name, grid=(ADAMW_STEPS,), in_specs=in_specs, out_specs=out_specs, out_shape=out_shape,
        input_output_aliases=aliases, compiler_params=_cparams(("parallel",)),
    )(*args)
    return [list(outs[4 * a:4 * a + 4]) for a in range(n)]


def sum_leading(parts, name):
    P, R, C = parts.shape

    def body(p_ref, o_ref):
        s = p_ref[0]
        for k in range(1, P):
            s = s + p_ref[k]
        o_ref[...] = s

    return _pcall(body, name=name, out_shape=jax.ShapeDtypeStruct((R, C), F32))(parts)


def pair_sum(gsends, recvs, name):
    n = len(gsends)

    def body(*refs):
        for g_ref, r_ref, o_ref in zip(refs[:n], refs[n:2 * n], refs[2 * n:]):
            o_ref[...] = (g_ref[...].astype(F32) + r_ref[...].astype(F32)).astype(o_ref.dtype)

    def slot(a):
        return pl.BlockSpec((None,) + a.shape[1:], lambda q: (q, 0, 0))

    def own(a):
        return pl.BlockSpec((None,) + a.shape[1:], lambda q: (2 * q + lax.axis_index("c"), 0, 0))

    return _pcall(
        body, name=name, grid=(4,), in_specs=[own(g) for g in gsends] + [slot(r) for r in recvs],
        out_specs=[slot(r) for r in recvs], out_shape=[jax.ShapeDtypeStruct(r.shape, BF16) for r in recvs],
        compiler_params=_cparams(("parallel",)),
    )(*gsends, *recvs)


def _place():
    return lax.axis_index("x"), lax.axis_index("y"), lax.axis_index("c")


def _other_chips(x, y):
    return [(1 - x, y), (x, 1 - y), (1 - x, 1 - y)]


def all_gather(arrs, name):
    n = len(arrs)

    def body(*refs):
        ins, outs = refs[:n], refs[n:2 * n]
        send_sems, recv_sems, local_sems = refs[2 * n:]
        x, y, c = _place()
        me, sibling = (x, y, c), (x, y, 1 - c)
        chips = _other_chips(x, y)

        def slot(a, px, py, pc):
            return outs[a].at[4 * px + 2 * py + pc]

        def copy(a, k, block, to, src=None):
            return pltpu.make_async_remote_copy(
                src_ref=slot(a, *block) if src is None else src, dst_ref=slot(a, *block),
                send_sem=send_sems.at[a, k], recv_sem=recv_sems.at[a, k], device_id=to, device_id_type=MESH)

        mine, first, passed = [], [], []
        for a in range(n):
            cp = pltpu.make_async_copy(ins[a], slot(a, *me), local_sems.at[a])
            cp.start()
            mine.append(cp)
            first.append(copy(a, 0, me, sibling, src=ins[a]))
            first += [copy(a, 1 + j, me, (*chip, c), src=ins[a]) for j, chip in enumerate(chips)]
        for cp in first:
            cp.start()
        for j, chip in enumerate(chips):
            for a in range(n):
                copy(a, 1 + j, (*chip, c), me).wait_recv()
                cp = copy(a, 4 + j, (*chip, c), sibling)
                cp.start()
                passed.append(cp)
        for a in range(n):
            copy(a, 0, sibling, me).wait_recv()
            for j, chip in enumerate(chips):
                copy(a, 4 + j, (*chip, 1 - c), me).wait_recv()
        for cp in first + passed:
            cp.wait_send()
        for cp in mine:
            cp.wait()

    return _pcall(
        body, name=name, in_specs=[ANY] * n, out_specs=[ANY] * n,
        out_shape=[jax.ShapeDtypeStruct((N_DEV,) + a.shape, a.dtype) for a in arrs],
        scratch_shapes=[pltpu.SemaphoreType.DMA((n, 7)), pltpu.SemaphoreType.DMA((n, 7)), pltpu.SemaphoreType.DMA((n,))],
    )(*arrs)


def sibling_exchange(gsends, name, after=None):
    n = len(gsends)
    n_in = n + (1 if after is not None else 0)

    def body(*refs):
        ins, outs = refs[:n], refs[n_in:n_in + n]
        send_sems, recv_sems = refs[n_in + n:]
        x, y, c = _place()
        copies = []
        for a in range(n):
            for q in range(4):
                cp = pltpu.make_async_remote_copy(
                    src_ref=ins[a].at[2 * q + 1 - c], dst_ref=outs[a].at[q],
                    send_sem=send_sems.at[a, q], recv_sem=recv_sems.at[a, q],
                    device_id=(x, y, 1 - c), device_id_type=MESH)
                cp.start()
                copies.append(cp)
        for cp in copies:
            cp.wait()

    return _pcall(
        body, name=name, in_specs=[ANY] * n_in, out_specs=[ANY] * n,
        out_shape=[jax.ShapeDtypeStruct((4,) + g.shape[1:], g.dtype) for g in gsends],
        scratch_shapes=[pltpu.SemaphoreType.DMA((n, 4)), pltpu.SemaphoreType.DMA((n, 4))],
    )(*gsends, *([after] if after is not None else []))


HBM =pl.BlockSpec(memory_space=pltpu.HBM)
SEM = pl.BlockSpec(memory_space=pltpu.SEMAPHORE)
EFFECT = pltpu.SideEffectType.DATAFLOW_SIDE_EFFECTING


def _in_hbm(a):
    return pltpu.with_memory_space_constraint(a, pltpu.HBM)


def _gather_peers(x, y, c):
    to = [(x, y, 1 - c)] + [(px, py, c) for px, py in _other_chips(x, y)]
    return to, [4 * px + 2 * py + pc for px, py, pc in to]


def gather_start(arrs, after, name):
    n = len(arrs)
    n_in = 2 * n + (1 if after is not None else 0)

    def body(*refs):
        srcs, lands = refs[:n], refs[n:2 * n]
        send_sems, recv_sems = refs[n_in], refs[n_in + 1]
        token = refs[-1]
        x, y, c = _place()
        to, _ = _gather_peers(x, y, c)
        me = 4 * x + 2 * y + c
        for a in range(n):
            for k, dev in enumerate(to):
                pltpu.make_async_remote_copy(
                    src_ref=srcs[a], dst_ref=lands[a].at[me], send_sem=send_sems.at[4 * a + k], recv_sem=recv_sems.at[4 * a + k],
                    device_id=dev, device_id_type=MESH).start()
        token[...] = jnp.zeros_like(token)

    zones = [lax.empty((N_DEV,) + a.shape, a.dtype) for a in arrs]
    args = [_in_hbm(a) for a in arrs] + [_in_hbm(z) for z in zones] + ([after] if after is not None else [])
    outs = _pcall(
        body, name=name,
        out_shape=(pltpu.SemaphoreType.DMA((4 * n,)), pltpu.SemaphoreType.DMA((4 * n,)),
                   *[pltpu.HBM(a.shape, a.dtype) for a in arrs], *[pltpu.HBM(z.shape, z.dtype) for z in zones],
                   jax.ShapeDtypeStruct((8, LANES), F32)),
        in_specs=[HBM] * (2 * n) + ([ANY] if after is not None else []),
        out_specs=(SEM, SEM, *[HBM] * (2 * n), pl.BlockSpec(memory_space=pltpu.VMEM)),
        input_output_aliases={i: 2 + i for i in range(2 * n)},
        compiler_params=pltpu.CompilerParams(has_side_effects=EFFECT),
    )(*args)
    return dict(send=outs[0], recv=outs[1], srcs=list(outs[2:2 + n]), lands=list(outs[2 + n:2 + 2 * n]), token=outs[-1])


def gather_wait(st, after, name):
    n = len(st["srcs"])

    def body(*refs):
        srcs, lands = refs[:n], refs[n:2 * n]
        send_sems, recv_sems = refs[2 * n], refs[2 * n + 1]
        x, y, c = _place()
        to, slots = _gather_peers(x, y, c)
        for a in range(n):
            for k, dev in enumerate(to):
                cp = pltpu.make_async_remote_copy(
                    src_ref=srcs[a], dst_ref=lands[a].at[slots[k]], send_sem=send_sems.at[4 * a + k],
                    recv_sem=recv_sems.at[4 * a + k], device_id=dev, device_id_type=MESH)
                cp.wait_send()
                cp.wait_recv()

    outs = _pcall(
        body, name=name,
        out_shape=(*[pltpu.HBM(a.shape, a.dtype) for a in st["srcs"]], *[pltpu.HBM(z.shape, z.dtype) for z in st["lands"]]),
        in_specs=[HBM] * (2 * n) + [SEM, SEM, ANY], out_specs=tuple([HBM] * (2 * n)),
        input_output_aliases={i: i for i in range(2 * n)},
        compiler_params=pltpu.CompilerParams(has_side_effects=EFFECT),
    )(*st["srcs"], *st["lands"], st["send"], st["recv"], after)
    return list(outs[n:])


def pass_start(zones, name):
    n = len(zones)

    def body(*refs):
        zs = refs[:n]
        send_sems, recv_sems = refs[n], refs[n + 1]
        token = refs[-1]
        x, y, c = _place()
        for a in range(n):
            for j, (px, py) in enumerate(_other_chips(x, y)):
                blk = zs[a].at[4 * px + 2 * py + c]
                pltpu.make_async_remote_copy(
                    src_ref=blk, dst_ref=blk, send_sem=send_sems.at[3 * a + j], recv_sem=recv_sems.at[3 * a + j],
                    device_id=(x, y, 1 - c), device_id_type=MESH).start()
        token[...] = jnp.zeros_like(token)

    outs = _pcall(
        body, name=name,
        out_shape=(pltpu.SemaphoreType.DMA((3 * n,)), pltpu.SemaphoreType.DMA((3 * n,)),
                   *[pltpu.HBM(z.shape, z.dtype) for z in zones], jax.ShapeDtypeStruct((8, LANES), F32)),
        in_specs=[HBM] * n, out_specs=(SEM, SEM, *[HBM] * n, pl.BlockSpec(memory_space=pltpu.VMEM)),
        input_output_aliases={i: 2 + i for i in range(n)},
        compiler_params=pltpu.CompilerParams(has_side_effects=EFFECT),
    )(*zones)
    return dict(send=outs[0], recv=outs[1], zones=list(outs[2:2 + n]), token=outs[-1])


def pass_wait(st, after, name):
    n = len(st["zones"])

    def body(*refs):
        zs = refs[:n]
        send_sems, recv_sems = refs[n], refs[n + 1]
        x, y, c = _place()
        for a in range(n):
            for j, (px, py) in enumerate(_other_chips(x, y)):
                cp = pltpu.make_async_remote_copy(
                    src_ref=zs[a].at[4 * px + 2 * py + c], dst_ref=zs[a].at[4 * px + 2 * py + 1 - c],
                    send_sem=send_sems.at[3 * a + j], recv_sem=recv_sems.at[3 * a + j],
                    device_id=(x, y, 1 - c), device_id_type=MESH)
                cp.wait_send()
                cp.wait_recv()

    outs = _pcall(
        body, name=name, out_shape=tuple(pltpu.HBM(z.shape, z.dtype) for z in st["zones"]),
        in_specs=[HBM] * n + [SEM, SEM, ANY], out_specs=tuple([HBM] * n),
        input_output_aliases={i: i for i in range(n)},
        compiler_params=pltpu.CompilerParams(has_side_effects=EFFECT),
    )(*st["zones"], st["send"], st["recv"], after)
    return list(outs)


def sibling_start(gsends, name):
    n = len(gsends)

    def body(*refs):
        srcs, lands = refs[:n], refs[n:2 * n]
        send_sems, recv_sems = refs[2 * n], refs[2 * n + 1]
        token = refs[-1]
        x, y, c = _place()
        for a in range(n):
            for q in range(4):
                pltpu.make_async_remote_copy(
                    src_ref=srcs[a].at[2 * q + 1 - c], dst_ref=lands[a].at[q], send_sem=send_sems.at[4 * a + q],
                    recv_sem=recv_sems.at[4 * a + q], device_id=(x, y, 1 - c), device_id_type=MESH).start()
        token[...] = jnp.zeros_like(token)

    zones = [lax.empty((4,) + g.shape[1:], g.dtype) for g in gsends]
    outs = _pcall(
        body, name=name,
        out_shape=(pltpu.SemaphoreType.DMA((4 * n,)), pltpu.SemaphoreType.DMA((4 * n,)),
                   *[pltpu.HBM(g.shape, g.dtype) for g in gsends], *[pltpu.HBM(z.shape, z.dtype) for z in zones],
                   jax.ShapeDtypeStruct((8, LANES), F32)),
        in_specs=[HBM] * (2 * n), out_specs=(SEM, SEM, *[HBM] * (2 * n), pl.BlockSpec(memory_space=pltpu.VMEM)),
        input_output_aliases={i: 2 + i for i in range(2 * n)},
        compiler_params=pltpu.CompilerParams(has_side_effects=EFFECT),
    )(*[_in_hbm(g) for g in gsends], *[_in_hbm(z) for z in zones])
    return dict(send=outs[0], recv=outs[1], srcs=list(outs[2:2 + n]), lands=list(outs[2 + n:2 + 2 * n]), token=outs[-1])


def sibling_wait(st, after, name):
    n = len(st["srcs"])

    def body(*refs):
        srcs, lands = refs[:n], refs[n:2 * n]
        send_sems, recv_sems = refs[2 * n], refs[2 * n + 1]
        x, y, c = _place()
        for a in range(n):
            for q in range(4):
                cp = pltpu.make_async_remote_copy(
                    src_ref=srcs[a].at[2 * q + 1 - c], dst_ref=lands[a].at[q], send_sem=send_sems.at[4 * a + q],
                    recv_sem=recv_sems.at[4 * a + q], device_id=(x, y, 1 - c), device_id_type=MESH)
                cp.wait_send()
                cp.wait_recv()

    outs = _pcall(
        body, name=name,
        out_shape=(*[pltpu.HBM(a.shape, a.dtype) for a in st["srcs"]], *[pltpu.HBM(z.shape, z.dtype) for z in st["lands"]]),
        in_specs=[HBM] * (2 * n) + [SEM, SEM, ANY], out_specs=tuple([HBM] * (2 * n)),
        input_output_aliases={i: i for i in range(2 * n)},
        compiler_params=pltpu.CompilerParams(has_side_effects=EFFECT),
    )(*st["srcs"], *st["lands"], st["send"], st["recv"], after)
    return list(outs[:n]), list(outs[n:])


def scatter_start(parts, name):
    n = len(parts)

    def body(*refs):
        srcs, lands = refs[:n], refs[n:2 * n]
        send_sems, recv_sems = refs[2 * n], refs[2 * n + 1]
        token = refs[-1]
        x, y, c = _place()
        for a in range(n):
            for j, (px, py) in enumerate(_other_chips(x, y)):
                pltpu.make_async_remote_copy(
                    src_ref=srcs[a].at[2 * px + py], dst_ref=lands[a].at[j], send_sem=send_sems.at[3 * a + j],
                    recv_sem=recv_sems.at[3 * a + j], device_id=(px, py, c), device_id_type=MESH).start()
        token[...] = jnp.zeros_like(token)

    zones = [lax.empty((3,) + p.shape[1:], p.dtype) for p in parts]
    outs = _pcall(
        body, name=name,
        out_shape=(pltpu.SemaphoreType.DMA((3 * n,)), pltpu.SemaphoreType.DMA((3 * n,)),
                   *[pltpu.HBM(p.shape, p.dtype) for p in parts], *[pltpu.HBM(z.shape, z.dtype) for z in zones],
                   jax.ShapeDtypeStruct((8, LANES), F32)),
        in_specs=[HBM] * (2 * n), out_specs=(SEM, SEM, *[HBM] * (2 * n), pl.BlockSpec(memory_space=pltpu.VMEM)),
        input_output_aliases={i: 2 + i for i in range(2 * n)},
        compiler_params=pltpu.CompilerParams(has_side_effects=EFFECT),
    )(*[_in_hbm(p) for p in parts], *[_in_hbm(z) for z in zones])
    return dict(send=outs[0], recv=outs[1], srcs=list(outs[2:2 + n]), lands=list(outs[2 + n:2 + 2 * n]), token=outs[-1])


def scatter_wait(st, after, name):
    n = len(st["srcs"])

    def body(*refs):
        srcs, lands = refs[:n], refs[n:2 * n]
        send_sems, recv_sems = refs[2 * n], refs[2 * n + 1]
        x, y, c = _place()
        for a in range(n):
            for j, (px, py) in enumerate(_other_chips(x, y)):
                cp = pltpu.make_async_remote_copy(
                    src_ref=srcs[a].at[2 * px + py], dst_ref=lands[a].at[j], send_sem=send_sems.at[3 * a + j],
                    recv_sem=recv_sems.at[3 * a + j], device_id=(px, py, c), device_id_type=MESH)
                cp.wait_send()
                cp.wait_recv()

    outs = _pcall(
        body, name=name,
        out_shape=(*[pltpu.HBM(a.shape, a.dtype) for a in st["srcs"]], *[pltpu.HBM(z.shape, z.dtype) for z in st["lands"]]),
        in_specs=[HBM] * (2 * n) + [SEM, SEM, ANY], out_specs=tuple([HBM] * (2 * n)),
        input_output_aliases={i: i for i in range(2 * n)},
        compiler_params=pltpu.CompilerParams(has_side_effects=EFFECT),
    )(*st["srcs"], *st["lands"], st["send"], st["recv"], after)
    return list(outs[:n]), list(outs[n:])


def _unshard(g, axis):
    nd = g.ndim - 1
    axis = axis % nd
    t = jnp.moveaxis(g, 0, axis)
    shp = list(g.shape[1:])
    shp[axis] *= N_DEV
    return t.reshape(shp)


def _to_shards(full, axis):
    axis = axis % full.ndim
    shp = list(full.shape)
    shp[axis:axis + 1] = [N_DEV, shp[axis] // N_DEV]
    return jnp.moveaxis(full.reshape(shp), axis, 0)


def _pack(arrs, rows):
    flat = jnp.concatenate([a.reshape(-1).astype(F32) for a in arrs])
    return jnp.pad(flat, (0, rows * LANES - flat.shape[0])).reshape(rows, LANES)


def _unpack(buf, shapes):
    flat = buf.reshape(-1)
    out, off = [], 0
    for s in shapes:
        n = math.prod(s)
        out.append(flat[off:off + n].reshape(s))
        off += n
    return out


def _rows_for(shapes):
    n = sum(math.prod(s) for s in shapes)
    return -(-n // (8 * LANES)) * 8


def _row(v, width=None):
    v = v.reshape(1, -1).astype(F32)
    if width is not None and v.shape[1] < width:
        v = jnp.pad(v, ((0, 0), (0, width - v.shape[1])))
    return v


def _after(order, width):
    if not order:
        return None
    t = order[0][0:1, 0:1]
    for o in order[1:]:
        t = t + o[0:1, 0:1]
    return jnp.broadcast_to(t, (1, width))


def _norm_after(norm, order):
    row = _after(order, norm.shape[1])
    return norm if row is None else norm + row


FFN_TN = 256


def ffn_in(h, norm, w_gate, w_up, name):
    def body(h_ref, n_ref, wg_ref, wu_ref, u_ref, g_ref, up_ref, act_ref, u_s):
        @pl.when(pl.program_id(0) == 0)
        def _():
            x = h_ref[...]
            r = lax.rsqrt(jnp.mean(x * x, axis=-1, keepdims=True) + EPS)
            u_s[...] = (x * r * n_ref[...]).astype(BF16)
            u_ref[...] = u_s[...]

        u = u_s[...]
        g = _dot(u, wg_ref[...], _NT)
        up = _dot(u, wu_ref[...], _NT)
        g_ref[...] = g.astype(BF16)
        up_ref[...] = up.astype(BF16)
        act_ref[...] = (_silu(g) * up).astype(BF16)

    whole = pl.BlockSpec((T, D), lambda j: (0, 0))
    wspec = pl.BlockSpec((FFN_TN, D), lambda j: (j, 0))
    col = pl.BlockSpec((T, FFN_TN), lambda j: (0, j))
    return _pcall(
        body, name=name, grid=(DFF // FFN_TN,), in_specs=[whole, pl.BlockSpec((1, D), lambda j: (0, 0)), wspec, wspec],
        out_specs=[whole, col, col, col],
        out_shape=[jax.ShapeDtypeStruct((T, D), BF16)] + [jax.ShapeDtypeStruct((T, DFF), BF16)] * 3,
        scratch_shapes=[pltpu.VMEM((T, D), BF16)], compiler_params=_cparams(("arbitrary",)),
    )(h, norm, w_gate, w_up)


def ffn_back(dh_b, w_down, g, up, after_row, name):
    has_row = after_row is not None

    def body(*refs):
        dh_ref, wd_ref, g_ref, up_ref = refs[:4]
        dg_ref, dup_ref = refs[-2:]
        da = _dot(dh_ref[...], wd_ref[...], _NT)
        if has_row:
            da = da + refs[4][...]
        g = g_ref[...].astype(F32)
        dg_ref[...] = (da * up_ref[...].astype(F32) * _dsilu(g)).astype(BF16)
        dup_ref[...] = (da * _silu(g)).astype(BF16)

    col = pl.BlockSpec((T, FFN_TN), lambda j: (0, j))
    in_specs = [pl.BlockSpec((T, D), lambda j: (0, 0)), pl.BlockSpec((FFN_TN, D), lambda j: (j, 0)), col, col]
    args = [dh_b, w_down, g, up]
    if has_row:
        in_specs.append(pl.BlockSpec((1, FFN_TN), lambda j: (0, j)))
        args.append(after_row)
    return _pcall(
        body, name=name, grid=(DFF // FFN_TN,), in_specs=in_specs, out_specs=[col, col],
        out_shape=[jax.ShapeDtypeStruct((T, DFF), BF16)] * 2, compiler_params=_cparams(("parallel",)),
    )(*args)


def ffn_layer_fwd(h, p, tag, order=()):
    u, g, up, act = ffn_in(h, _norm_after(p["norm"], order), p["w_gate"], p["w_up"], f"{tag}_in")
    h2 = matmul(act, p["w_down"], "nn", residual=h, name=f"{tag}_down")
    return h2, (h, u, g, up, act)


def ffn_layer_bwd(dh, dh_b, saved, p, tag, order=()):
    h, u, g, up, act = saved
    d_down = matmul(act, dh_b, "tn", out_dtype=BF16, name=f"{tag}_dwd")
    dg, dup = ffn_back(dh_b, p["w_down"], g, up, _after(order, DFF), f"{tag}_back")
    du = matmul(dg, p["w_gate"], "nn", name=f"{tag}_dug")
    du = matmul(dup, p["w_up"], "nn", residual=du, name=f"{tag}_duu")
    d_gate = matmul(dg, u, "tn", out_dtype=BF16, name=f"{tag}_dwg")
    d_up = matmul(dup, u, "tn", out_dtype=BF16, name=f"{tag}_dwu")
    dh2, dh2_b, d_norm = rms_bwd(du, h, p["norm"], dh, f"{tag}_drms")
    return dh2, dh2_b, dict(norm=d_norm, w_gate=d_gate, w_up=d_up, w_down=d_down)


def conv_layer_fwd(h, p, tag, order=()):
    u = rms_fwd(h, _norm_after(p["norm"], order), f"{tag}_rms")
    hh = matmul(u, p["w_pw1"], "nn", bias=p["b_pw1"], name=f"{tag}_pw1")
    gl = glu_fwd(hh, f"{tag}_glu")
    c2 = dwconv_fwd(gl, 0, p["dw_w"], p["dw_b"], KCV, 128, False, f"{tag}_dw")[0]
    s = ln_silu_fwd(c2, p["ln_g"], p["ln_b"], f"{tag}_ln")
    h2 = matmul(s, p["w_pw2"], "nn", bias=p["b_pw2"], residual=h, name=f"{tag}_pw2")
    return h2, (h, u, hh, gl, c2, s)


def conv_layer_bwd(dh, dh_b, saved, p, tag, order=()):
    h, u, hh, gl, c2, s = saved
    ds = matmul(dh_b, p["w_pw2"], "nt", bias=_after(order, D), name=f"{tag}_ds")
    d_pw2 = matmul(s, dh_b, "tn", out_dtype=BF16, name=f"{tag}_dwpw2")
    dc2, d_lng, d_lnb, d_bpw2 = ln_silu_bwd(ds, c2, dh, p["ln_g"], p["ln_b"], f"{tag}_dln")
    dgl, d_dww, d_dwb = dwconv_bwd(dc2, None, gl, 0, p["dw_w"], KCV, 128, False, F32, f"{tag}_ddw")
    dhh, d_bpw1 = glu_bwd(dgl, hh, f"{tag}_dglu")
    du = matmul(dhh, p["w_pw1"], "nt", name=f"{tag}_du")
    d_pw1 = matmul(u, dhh, "tn", out_dtype=BF16, name=f"{tag}_dwpw1")
    dh2, dh2_b, d_norm = rms_bwd(du, h, p["norm"], dh, f"{tag}_drms")
    grads = dict(norm=d_norm, w_pw1=d_pw1, b_pw1=d_bpw1, dw_w=d_dww[:KCV], dw_b=d_dwb, ln_g=d_lng, ln_b=d_lnb,
                 w_pw2=d_pw2, b_pw2=d_bpw2)
    return dh2, dh2_b, grads


def ssm_layer_fwd(h, p, tag, order=(), mid=None):
    u = rms_fwd(h, _norm_after(p["norm"], order), f"{tag}_rms")
    zx = matmul(u, p["w_in"], "nn", name=f"{tag}_in")
    cpre, xbc = dwconv_fwd(zx, DI // 512, p["conv_w"], p["conv_b"], KSSM, 512, True, f"{tag}_conv")
    dt, da, dtx, dax = dt_fwd(zx, p["dt_bias"], p["a_log"], f"{tag}_dt")
    daT = da[:, :NH].T
    y, states = ssd_fwd(xbc, dtx, dax, daT, p["d_full"], f"{tag}_ssd")
    gate_norm = p["gate_norm"] if mid is None else _norm_after(p["gate_norm"], mid(y))
    yn = gatenorm_fwd(y, zx, gate_norm, f"{tag}_gn")
    h2 = matmul(yn, p["w_out"], "nn", residual=h, name=f"{tag}_out")
    return h2, (h, u, zx, cpre, xbc, dt, dtx, dax, daT, y, states, yn)


def ssm_layer_bwd(dh, dh_b, saved, p, tag, order=()):
    h, u, zx, cpre, xbc, dt, dtx, dax, daT, y, states, yn = saved
    dyn = matmul(dh_b, p["w_out"], "nt", bias=_after(order, DI), name=f"{tag}_dyn")
    d_wout = matmul(yn, dh_b, "tn", out_dtype=BF16, name=f"{tag}_dwout")
    dy, dzx, d_gn = gatenorm_bwd(dyn, y, zx, p["gate_norm"], f"{tag}_dgn")
    dxbc, ddt, dda, dD = ssd_bwd(dy, xbc, dtx, dax, daT, p["d_full"], states, f"{tag}_dssd")
    dzx, d_dtb, d_alog = dt_bwd(ddt, dda, dt, zx, p["dt_bias"], p["a_log"], dzx, f"{tag}_ddt")
    dzx, d_cw, d_cb = dwconv_bwd(dxbc, cpre, zx, DI // 512, p["conv_w"], KSSM, 512, True, BF16, f"{tag}_dconv",
                                 into=(dzx, DI // 512))
    du = matmul(dzx, p["w_in"], "nt", name=f"{tag}_du")
    d_win = matmul(u, dzx, "tn", out_dtype=BF16, name=f"{tag}_dwin")
    dh2, dh2_b, d_norm = rms_bwd(du, h, p["norm"], dh, f"{tag}_drms")
    d_d = headsum(dD.reshape(NH, HD), f"{tag}_dD").reshape(NH)
    grads = dict(norm=d_norm, w_in=d_win[:, :DINP], conv_w=d_cw[:KSSM], conv_b=d_cb, dt_bias=d_dtb[0, :NH],
                 a_log=d_alog[0, :NH], d=d_d, gate_norm=d_gn, w_out=d_wout)
    return dh2, dh2_b, grads


BIG = ["ssm_w_in", "ssm_w_out", "cv_w_pw1", "cv_w_pw2", "ffn_w_gate", "ffn_w_up", "ffn_w_down"]
TRANSPOSED = ("ffn_w_gate", "ffn_w_up")
LAYER_AXIS = {"ssm_w_in": -1, "ssm_w_out": 0, "cv_w_pw1": -1, "cv_w_pw2": 0, "ffn_w_gate": 0, "ffn_w_up": 0,
              "ffn_w_down": 0}
SMALL_SHARDED = ["ssm_conv_w", "cv_norm", "cv_b_pw1", "cv_dw_w", "cv_dw_b", "cv_ln_g", "cv_ln_b", "cv_b_pw2"]
SMALL_REPL = ["ssm_norm", "ssm_conv_b", "ssm_dt_bias", "ssm_a_log", "ssm_d", "ssm_gate_norm", "ffn_norm", "final_norm"]
WEIGHTS = ["ssm_norm", "ssm_w_in", "ssm_conv_w", "ssm_conv_b", "ssm_dt_bias", "ssm_a_log", "ssm_d", "ssm_gate_norm",
           "ssm_w_out", "cv_norm", "cv_w_pw1", "cv_b_pw1", "cv_dw_w", "cv_dw_b", "cv_ln_g", "cv_ln_b", "cv_w_pw2",
           "cv_b_pw2", "ffn_norm", "ffn_w_gate", "ffn_w_up", "ffn_w_down", "final_norm"]
SMALL = [n for n in WEIGHTS if n not in BIG]


N_STAGES = 8
SYNC_SIBLING_STAGES = 2


def _stage_layer(s):
    i = s // 2
    if s % 2:
        return "ffn", i
    return ("ssm" if i % 2 == 0 else "cv"), i // 2


def _stage_group(s):
    fam, l = _stage_layer(s)
    names = {"ffn": ["ffn_w_gate", "ffn_w_up", "ffn_w_down"], "ssm": ["ssm_w_in", "ssm_w_out"],
             "cv": ["cv_w_pw1", "cv_w_pw2"]}[fam]
    return [(n, l) for n in names]


def _stage_params(s, big, small):
    fam, l = _stage_layer(s)
    if fam == "ffn":
        return dict(norm=_row(small["ffn_norm"][l]), w_gate=big["ffn_w_gate"], w_up=big["ffn_w_up"],
                    w_down=big["ffn_w_down"])
    if fam == "ssm":
        return dict(norm=_row(small["ssm_norm"][l]), w_in=jnp.pad(big["ssm_w_in"], ((0, 0), (0, DINP_PAD - DINP))),
                    conv_w=jnp.pad(small["ssm_conv_w"][l], ((0, 8 - KSSM), (0, 0))), conv_b=_row(small["ssm_conv_b"][l]),
                    dt_bias=_row(small["ssm_dt_bias"][l], LANES), a_log=_row(small["ssm_a_log"][l], LANES),
                    d_full=_row(jnp.repeat(small["ssm_d"][l], HD)), gate_norm=_row(small["ssm_gate_norm"][l]),
                    w_out=big["ssm_w_out"])
    return dict(norm=_row(small["cv_norm"][l]), w_pw1=big["cv_w_pw1"], b_pw1=_row(small["cv_b_pw1"][l]),
                dw_w=jnp.pad(small["cv_dw_w"][l], ((0, 32 - KCV), (0, 0))), dw_b=_row(small["cv_dw_b"][l]),
                ln_g=_row(small["cv_ln_g"][l]), ln_b=_row(small["cv_ln_b"][l]), w_pw2=big["cv_w_pw2"],
                b_pw2=_row(small["cv_b_pw2"][l]))


_STAGE_FWD = {"ffn": ffn_layer_fwd, "ssm": ssm_layer_fwd, "cv": conv_layer_fwd}
_STAGE_BWD = {"ffn": ffn_layer_bwd, "ssm": ssm_layer_bwd, "cv": conv_layer_bwd}


def _stage_fwd(s, h, p, order=(), mid=None):
    fam, l = _stage_layer(s)
    if mid is not None:
        return ssm_layer_fwd(h, p, f"{fam}{l}", order, mid)
    return _STAGE_FWD[fam](h, p, f"{fam}{l}", order)


def _stage_bwd(s, dh, dh_b, p, saved, order=()):
    fam, l = _stage_layer(s)
    dh, dh_b, g = _STAGE_BWD[fam](dh, dh_b, saved, p, f"{fam}{l}", order)
    return dh, dh_b, {f"{fam}_{k}": val for k, val in g.items()}


def _local(x, tgt, full):
    h, tape = x, []
    for s in range(N_STAGES):
        big = {n: (full[n][l].T if n in TRANSPOSED else full[n][l]) for n, l in _stage_group(s)}
        p = _stage_params(s, big, full)
        h, saved = _stage_fwd(s, h, p)
        tape.append((p, saved))
    dh, dh_b, d_final, loss_row = loss_head(h, _row(full["final_norm"]), tgt, "loss_head")
    gl = {n: [None] * full[n].shape[0] for n in WEIGHTS if n != "final_norm"}
    for s in reversed(range(N_STAGES)):
        dh, dh_b, g = _stage_bwd(s, dh, dh_b, *tape[s])
        for n, val in g.items():
            val = val.T if n in TRANSPOSED else val
            gl[n][_stage_layer(s)[1]] = val.reshape(full[n].shape[1:])
    grads = {n: jnp.stack(vs) for n, vs in gl.items()}
    grads["final_norm"] = d_final.reshape(D)
    return loss_row, dh, grads


def _step(x, tgt, w, m, v):
    idx = 4 * lax.axis_index("x") + 2 * lax.axis_index("y") + lax.axis_index("c")
    small_shapes = [w[n].shape for n in SMALL_SHARDED]
    small_pack = _pack([w[n] for n in SMALL_SHARDED], _rows_for(small_shapes))

    def view(n, a):
        return jnp.swapaxes(a, 1, 2) if n in TRANSPOSED else a

    wv, mv, vv = ({n: view(n, t[n]) for n in BIG} for t in (w, m, v))

    def blocks(s):
        return [wv[n][l].astype(BF16) for n, l in _stage_group(s)] + ([small_pack] if s == 0 else [])

    arrs = [blocks(s) for s in range(N_STAGES)]
    first = gather_start(arrs[0], None, "gather0_start")
    passing = pass_start(gather_wait(first, first["token"], "gather0_wait"), "pass0_start")
    crossing = {1: gather_start(arrs[1], passing["token"], "gather1_start")}
    crossing[2] = gather_start(arrs[2], crossing[1]["token"], "gather2_start")
    small = {n: w[n] for n in SMALL_REPL}
    flight = dict(passing=passing)

    def advance(s, after):
        tokens = []
        if s + 1 < N_STAGES:
            landed = gather_wait(crossing.pop(s + 1), after, f"gather{s + 1}_wait")
            flight["passing"] = pass_start(landed, f"pass{s + 1}_start")
            tokens.append(flight["passing"]["token"])
        if s + 3 < N_STAGES:
            crossing[s + 3] = gather_start(arrs[s + 3], flight["passing"]["token"], f"gather{s + 3}_start")
            tokens.append(crossing[s + 3]["token"])
        return tokens

    h, tape, after = x, [], crossing[2]["token"]
    for s in range(N_STAGES):
        zones = pass_wait(flight["passing"], after, f"pass{s}_wait")
        order = advance(s, zones[0]) if s else []
        mid = functools.partial(advance, 0) if s == 0 else None
        zones = [lax.dynamic_update_slice_in_dim(z, a[None], idx, 0) for z, a in zip(zones, arrs[s])]
        if s == 0:
            per_dev = [_unpack(zones[-1][k], small_shapes) for k in range(N_DEV)]
            for q, n in enumerate(SMALL_SHARDED):
                small[n] = _unshard(jnp.stack([per_dev[k][q] for k in range(N_DEV)]), -1)
        big = {n: _unshard(z, LAYER_AXIS[n]) for (n, _), z in zip(_stage_group(s), zones)}
        p = _stage_params(s, big, small)
        h, saved = _stage_fwd(s, h, p, order, mid)
        tape.append((p, saved))
        after = h

    dh, dh_b, d_final, loss_row = loss_head(h, _row(w["final_norm"]), tgt, "loss_head")

    out = {}
    small_g = {n: [None] * w[n].shape[0] for n in SMALL if n != "final_norm"}

    def finish(s, st, after):
        by_chip, recv = scatter_wait(st, after, f"scatter{s}_wait")
        names = [n for n, _ in _stage_group(s)]
        res = adamw_stage(recv, by_chip, [wv[n] for n in names], [mv[n] for n in names], [vv[n] for n in names],
                          [l for _, l in _stage_group(s)], [out.get(n) for n in names], f"adamw_stage{s}")
        out.update(zip(names, res))

    started, order, summed, to_sibling = [], [], None, None
    for s in reversed(range(N_STAGES)):
        dh, dh_b, g = _stage_bwd(s, dh, dh_b, *tape[s], order)
        order = []
        if to_sibling is not None:
            gsend, from_sibling = sibling_wait(to_sibling, dh, f"sibling{s + 1}_wait")
            by_chip = pair_sum(gsend, from_sibling, f"pair_sum{s + 1}")
            started.append((s + 1, scatter_start(by_chip, f"scatter{s + 1}_start")))
            order.append(started[-1][1]["token"])
            to_sibling = None
        for n, val in g.items():
            if n not in BIG:
                small_g[n][_stage_layer(s)[1]] = val.reshape(small[n].shape[1:])
        if s == 0:
            grads = {n: jnp.stack(vs) for n, vs in small_g.items()}
            grads["final_norm"] = d_final.reshape(D)
            small_full_shapes = [grads[n].shape for n in SMALL] + [(1,)]
            packed = _pack([grads[n] for n in SMALL] + [loss_row[0, :1]], _rows_for(small_full_shapes))
            summed = sum_leading(all_gather([packed], "gather_small_grads")[0], "sum_small_grads")
        gsend = [_to_shards(g[n], LAYER_AXIS[n]) for n, _ in _stage_group(s)]
        if s >= SYNC_SIBLING_STAGES:
            to_sibling = sibling_start(gsend, f"sibling{s}_start")
            order.append(to_sibling["token"])
            continue
        from_sibling = sibling_exchange(gsend, f"scatter{s}_sibling", summed)
        by_chip = pair_sum(gsend, from_sibling, f"pair_sum{s}")
        started.append((s, scatter_start(by_chip, f"scatter{s}_start")))
        order.append(started[-1][1]["token"])
    last = started[-1][1]["token"]
    for s, st in started[:-1]:
        finish(s, st, last)
    parts = _unpack(summed + last[0:1, 0:1], small_full_shapes)
    loss = parts[-1][0]
    mine = []
    for n, g in zip(SMALL, parts[:-1]):
        if n in SMALL_SHARDED:
            s = w[n].shape[-1]
            g = lax.dynamic_slice_in_dim(g, idx * s, s, axis=g.ndim - 1)
        mine.append(g)
    out.update(zip(SMALL, adamw_small(mine, [w[n] for n in SMALL], [m[n] for n in SMALL], [v[n] for n in SMALL],
                                      "adamw_small")))
    done = [out[n][0].reshape(-1)[:1] for n in out]
    finish(*started[-1], functools.reduce(jnp.add, done))
    for n in TRANSPOSED:
        out[n] = [view(n, a) for a in out[n]]
    return loss, dh, out


def kernel(x, ssm_norm, ssm_w_in, ssm_conv_w, ssm_conv_b, ssm_dt_bias, ssm_a_log, ssm_d, ssm_gate_norm, ssm_w_out, cv_norm, cv_w_pw1, cv_b_pw1, cv_dw_w, cv_dw_b, cv_ln_g, cv_ln_b, cv_w_pw2, cv_b_pw2, ffn_norm, ffn_w_gate, ffn_w_up, ffn_w_down, final_norm, loss_target, m_ssm_norm, m_ssm_w_in, m_ssm_conv_w, m_ssm_conv_b, m_ssm_dt_bias, m_ssm_a_log, m_ssm_d, m_ssm_gate_norm, m_ssm_w_out, m_cv_norm, m_cv_w_pw1, m_cv_b_pw1, m_cv_dw_w, m_cv_dw_b, m_cv_ln_g, m_cv_ln_b, m_cv_w_pw2, m_cv_b_pw2, m_ffn_norm, m_ffn_w_gate, m_ffn_w_up, m_ffn_w_down, m_final_norm, v_ssm_norm, v_ssm_w_in, v_ssm_conv_w, v_ssm_conv_b, v_ssm_dt_bias, v_ssm_a_log, v_ssm_d, v_ssm_gate_norm, v_ssm_w_out, v_cv_norm, v_cv_w_pw1, v_cv_b_pw1, v_cv_dw_w, v_cv_dw_b, v_cv_ln_g, v_cv_ln_b, v_cv_w_pw2, v_cv_b_pw2, v_ffn_norm, v_ffn_w_gate, v_ffn_w_up, v_ffn_w_down, v_final_norm):
    args = locals()
    w = {n: args[n] for n in WEIGHTS}
    m = {n: args["m_" + n] for n in WEIGHTS}
    v = {n: args["v_" + n] for n in WEIGHTS}
    loss, grad_x, out = _step(x[0], loss_target[0], w, m, v)
    res = [loss, grad_x[None]]
    for k in range(4):
        res += [out[n][k] for n in WEIGHTS]
    return tuple(res)
```

```python
import functools
import math

import jax
import jax.numpy as jnp
from jax import lax
from jax.experimental import pallas as pl
from jax.experimental.pallas import tpu as pltpu

F32 = jnp.float32
BF16 = jnp.bfloat16

N_DEV = 8
T = 2048
D = 1024
DI = 2048
NH = 32
HD = 64
NG = 4
GW = DI // NG
DS = 128
CONVD = DI + 2 * NG * DS
DINP = 2 * DI + 2 * NG * DS + NH
DINP_PAD = 5376
CH = 128
NCH = T // CH
DFF = 2816
KSSM = 4
KCV = 31
EPS = 1e-5
LANES = 128
VMEM_LIMIT = 56 * 1024 * 1024

ADAM_LR = 0.001
ADAM_B1 = 0.9
ADAM_B2 = 0.999
ADAM_EPS = 1e-08
ADAM_WD = 0.01
ADAM_STEP = 10

MESH = pl.DeviceIdType.MESH
ANY = pl.BlockSpec(memory_space=pl.ANY)


def _pcall(body, **kw):
    return pl.pallas_call(body, **kw)


def _cparams(sem):
    return pltpu.CompilerParams(dimension_semantics=sem, vmem_limit_bytes=VMEM_LIMIT)


def _pick(n, cands):
    for c in cands:
        if n % c == 0:
            return c
    raise ValueError(f"no tile for {n}")


def _sigmoid(x):
    return 1.0 / (1.0 + jnp.exp(-x))


def _silu(x):
    return x * _sigmoid(x)


def _dsilu(x):
    s = _sigmoid(x)
    return s * (1.0 + x * (1.0 - s))


_DIMS = {"nn": (((1,), (0,)), ((), ())), "nt": (((1,), (1,)), ((), ())), "tn": (((0,), (0,)), ((), ()))}


MM_VMEM_BUDGET = 40 * 1024 * 1024
MM_MAX_K = 3072


def _mm_tiles(M, N, K, out_bytes, has_res):
    tk = K if K <= MM_MAX_K else K // 2
    assert K % tk == 0 and tk % LANES == 0
    nk = K // tk
    best = None
    for tm in (2048, 1792, 1408, 1024, 768, 512, 256, 128):
        if M % tm:
            continue
        for tn in (1408, 1024, 768, 512, 384, 256, 128):
            if N % tn:
                continue
            blocks = tm * tk * 2 + tk * tn * 2 + tm * tn * out_bytes + (tm * tn * 4 if has_res else 0)
            vmem = 2 * blocks + tm * tn * 4 * (2 if nk > 1 else 1)
            if vmem > MM_VMEM_BUDGET:
                continue
            traffic = (N // tn if nk > 1 else 1) * M * K + (M // tm) * N * K
            key = (-traffic, tm * tn)
            if best is None or key > best[0]:
                best = (key, tm, tn)
    assert best is not None, (M, N, K)
    return best[1], best[2], tk


def matmul(a, b, mode, *, name, bias=None, residual=None, out_dtype=F32):
    assert a.dtype == BF16 and b.dtype == BF16
    if mode == "nn":
        (M, K), (K2, N) = a.shape, b.shape
    elif mode == "nt":
        (M, K), (N, K2) = a.shape, b.shape
    else:
        (K, M), (K2, N) = a.shape, b.shape
    assert K == K2
    has_bias, has_res = bias is not None, residual is not None
    tm, tn, tk = _mm_tiles(M, N, K, jnp.dtype(out_dtype).itemsize, has_res)
    nk = K // tk
    dims = _DIMS[mode]

    def body(*refs):
        a_ref, b_ref = refs[0], refs[1]
        pos = 2
        bias_ref = res_ref = None
        if has_bias:
            bias_ref = refs[pos]
            pos += 1
        if has_res:
            res_ref = refs[pos]
            pos += 1
        o_ref = refs[pos]

        def finish(out):
            if has_bias:
                out = out + bias_ref[...]
            if has_res:
                out = out + res_ref[...]
            o_ref[...] = out.astype(o_ref.dtype)

        part = lax.dot_general(a_ref[...], b_ref[...], dims, preferred_element_type=F32)
        if nk == 1:
            finish(part)
            return
        acc = refs[pos + 1]
        k = pl.program_id(2)

        @pl.when(k == 0)
        def _():
            acc[...] = part

        @pl.when(jnp.logical_and(k > 0, k < nk - 1))
        def _():
            acc[...] += part

        @pl.when(k == nk - 1)
        def _():
            finish(acc[...] + part)

    if mode == "tn":
        a_spec = pl.BlockSpec((tk, tm), lambda i, j, k: (k, i))
    else:
        a_spec = pl.BlockSpec((tm, tk), lambda i, j, k: (i, k))
    if mode == "nt":
        b_spec = pl.BlockSpec((tn, tk), lambda i, j, k: (j, k))
    else:
        b_spec = pl.BlockSpec((tk, tn), lambda i, j, k: (k, j))
    in_specs, args = [a_spec, b_spec], [a, b]
    if has_bias:
        in_specs.append(pl.BlockSpec((1, tn), lambda i, j, k: (0, j)))
        args.append(bias.reshape(1, N).astype(F32))
    if has_res:
        in_specs.append(pl.BlockSpec((tm, tn), lambda i, j, k: (i, j)))
        args.append(residual)
    return _pcall(
        body, name=name, grid=(M // tm, N // tn, nk), in_specs=in_specs,
        out_specs=pl.BlockSpec((tm, tn), lambda i, j, k: (i, j)),
        out_shape=jax.ShapeDtypeStruct((M, N), out_dtype),
        scratch_shapes=[pltpu.VMEM((tm, tn), F32)] if nk > 1 else [],
        compiler_params=_cparams(("parallel", "parallel", "arbitrary")),
    )(*args)


def rowwise(fn, rows, bcasts, outs, accs=(), *, name, tm=256, fill=None):
    n_rows, n_b, n_o, n_a = len(rows), len(bcasts), len(outs), len(accs)
    n_in = n_rows + n_b + (1 if fill is not None else 0)
    outs = [o if len(o) == 4 else (o[0], o[1], o[0], 0) for o in outs]
    nt = T // tm

    def body(*refs):
        ins = [r[...] for r in refs[:n_rows + n_b]]
        res = fn(*ins)
        o_refs = refs[n_in:n_in + n_o]
        a_refs = refs[n_in + n_o:]
        for r, v in zip(o_refs, res[:n_o]):
            r[...] = v.astype(r.dtype)
        if n_a:
            i = pl.program_id(0)

            @pl.when(i == 0)
            def _():
                for r in a_refs:
                    r[...] = jnp.zeros_like(r)

            for r, v in zip(a_refs, res[n_o:]):
                r[...] += v

    in_specs = [pl.BlockSpec((tm, w), functools.partial(lambda i, cb: (i, cb), cb=cb)) for (_, w, cb) in rows]
    in_specs += [pl.BlockSpec(b.shape, lambda i: (0, 0)) for b in bcasts]
    out_specs = [pl.BlockSpec((tm, w), functools.partial(lambda i, cb: (i, cb), cb=cb)) for (w, _, _, cb) in outs]
    out_specs += [pl.BlockSpec((1, w), lambda i: (0, 0)) for w in accs]
    out_shape = [jax.ShapeDtypeStruct((T, whole), dt) for (_, dt, whole, _) in outs]
    out_shape += [jax.ShapeDtypeStruct((1, w), F32) for w in accs]
    args = [r[0] for r in rows] + list(bcasts)
    aliases = {}
    if fill is not None:
        in_specs.append(ANY)
        args.append(fill[0])
        aliases = {n_in - 1: fill[1]}
    return _pcall(
        body, name=name, grid=(nt,), in_specs=in_specs, out_specs=out_specs, out_shape=out_shape,
        input_output_aliases=aliases, compiler_params=_cparams(("arbitrary",)),
    )(*args)


def _full(a):
    return (a, a.shape[1], 0)


def _rsum(v):
    return jnp.sum(v, axis=0, keepdims=True)


def rms_fwd(h, g, name):
    def fn(x, g):
        r = lax.rsqrt(jnp.mean(x * x, axis=-1, keepdims=True) + EPS)
        return (x * r * g,)
    return rowwise(fn, [_full(h)], [g], [(D, BF16)], name=name)[0]


def rms_bwd(du, h, g, dres, name):
    def fn(du, x, dres, g):
        r = lax.rsqrt(jnp.mean(x * x, axis=-1, keepdims=True) + EPS)
        xh = x * r
        dxh = du * g
        dx = r * (dxh - xh * jnp.mean(dxh * xh, axis=-1, keepdims=True))
        dh = dres + dx
        return dh, dh, _rsum(du * xh)
    return rowwise(fn, [_full(du), _full(h), _full(dres)], [g], [(D, F32), (D, BF16)], [D], name=name)


def loss_head(h, g, tgt, name):
    def fn(x, tgt, g):
        r = lax.rsqrt(jnp.mean(x * x, axis=-1, keepdims=True) + EPS)
        xh = x * r
        err = xh * g - tgt
        lsum = jnp.sum(jnp.sum(err * err, axis=-1, keepdims=True), axis=0, keepdims=True) * (0.5 / D)
        dy = err * (1.0 / D)
        dxh = dy * g
        dx = r * (dxh - xh * jnp.mean(dxh * xh, axis=-1, keepdims=True))
        return dx, dx, _rsum(dy * xh), jnp.broadcast_to(lsum, (1, LANES))
    return rowwise(fn, [_full(h), _full(tgt)], [g], [(D, F32), (D, BF16)], [D, LANES], name=name)


def glu_fwd(hh, name):
    def fn(a, g):
        return (a * _sigmoid(g),)
    return rowwise(fn, [(hh, D, 0), (hh, D, 1)], [], [(D, F32)], name=name)[0]


def glu_bwd(dgl, hh, name):
    def fn(dgl, a, g):
        s = _sigmoid(g)
        dhh = jnp.concatenate([dgl * s, dgl * a * s * (1.0 - s)], axis=1)
        return dhh, _rsum(dhh)
    return rowwise(fn, [_full(dgl), (hh, D, 0), (hh, D, 1)], [], [(2 * D, BF16)], [2 * D], name=name)


def ln_silu_fwd(c2, g, b, name):
    def fn(x, g, b):
        mu = jnp.mean(x, axis=-1, keepdims=True)
        xc = x - mu
        r = lax.rsqrt(jnp.mean(xc * xc, axis=-1, keepdims=True) + EPS)
        return (_silu(xc * r * g + b),)
    return rowwise(fn, [_full(c2)], [g, b], [(D, BF16)], name=name)[0]


def ln_silu_bwd(ds, c2, dh, g, b, name):
    def fn(ds, x, dh, g, b):
        mu = jnp.mean(x, axis=-1, keepdims=True)
        xc = x - mu
        r = lax.rsqrt(jnp.mean(xc * xc, axis=-1, keepdims=True) + EPS)
        xh = xc * r
        dn = ds * _dsilu(xh * g + b)
        dxh = dn * g
        dx = r * (dxh - jnp.mean(dxh, axis=-1, keepdims=True) - xh * jnp.mean(dxh * xh, axis=-1, keepdims=True))
        return dx, _rsum(dn * xh), _rsum(dn), _rsum(dh)
    return rowwise(fn, [_full(ds), _full(c2), _full(dh)], [g, b], [(D, F32)], [D, D, D], name=name)


def gatenorm_fwd(y, zx, gn, name):
    def fn(y, z, gn):
        hg = y * _silu(z)
        parts = []
        for k in range(NG):
            hk = hg[:, k * GW:(k + 1) * GW]
            parts.append(hk * lax.rsqrt(jnp.mean(hk * hk, axis=-1, keepdims=True) + EPS))
        return (jnp.concatenate(parts, axis=1) * gn,)
    return rowwise(fn, [_full(y), (zx, DI, 0)], [gn], [(DI, BF16)], name=name)[0]


def gatenorm_bwd(dyn, y, zx, gn, name):
    def fn(dyn, y, z, gn):
        sz = _silu(z)
        hg = y * sz
        dxh = dyn * gn
        dhg, xhs = [], []
        for k in range(NG):
            sl = slice(k * GW, (k + 1) * GW)
            hk = hg[:, sl]
            r = lax.rsqrt(jnp.mean(hk * hk, axis=-1, keepdims=True) + EPS)
            xh = hk * r
            dk = dxh[:, sl]
            dhg.append(r * (dk - xh * jnp.mean(dk * xh, axis=-1, keepdims=True)))
            xhs.append(xh)
        dhg = jnp.concatenate(dhg, axis=1)
        xh = jnp.concatenate(xhs, axis=1)
        return dhg * sz, dhg * y * _dsilu(z), _rsum(dyn * xh)
    return rowwise(fn, [_full(dyn), _full(y), (zx, DI, 0)], [gn], [(DI, F32), (DI, BF16, DINP_PAD, 0)], [DI], name=name)


def _softplus(x):
    return jnp.maximum(x, 0.0) + jnp.log(1.0 + jnp.exp(-jnp.abs(x)))


def _spread(v, e):
    hi = v.astype(BF16)
    r = v - hi.astype(F32)
    mid = r.astype(BF16)
    lo = (r - mid.astype(F32)).astype(BF16)
    return _dot(hi, e) + _dot(mid, e) + _dot(lo, e)


def _spread2(v, e):
    hi = v.astype(BF16)
    lo = (v - hi.astype(F32)).astype(BF16)
    return _dot(hi, e) + _dot(lo, e)


def dt_fwd(zx, dt_bias, a_log, name):
    heads = (jnp.arange(DI)[None, :] // HD == jnp.arange(LANES)[:, None]).astype(BF16)

    def fn(raw, bias, a_log, e):
        dt = _softplus(raw + bias)
        da = dt * (-jnp.exp(a_log))
        return dt, da, _spread(dt, e), _spread(da, e)

    return rowwise(fn, [(zx, LANES, (2 * DI + 2 * NG * DS) // LANES)], [dt_bias, a_log, heads],
                   [(LANES, F32), (LANES, F32), (DI, F32), (DI, F32)], name=name)


def dt_bwd(ddt, dda, dt, zx, dt_bias, a_log, dzx, name):
    def fn(ddt, dda, dt, raw, bias, a_log):
        a = -jnp.exp(a_log)
        draw = (ddt + dda * a) * _sigmoid(raw + bias)
        return jnp.concatenate([draw, jnp.zeros_like(draw)], axis=1), _rsum(draw), _rsum(dda * dt) * a
    return rowwise(fn, [_full(ddt), _full(dda), _full(dt), (zx, LANES, (2 * DI + 2 * NG * DS) // LANES)],
                   [dt_bias, a_log], [(2 * LANES, BF16, DINP_PAD, DINP_PAD // (2 * LANES) - 1)], [LANES, LANES],
                   name=name, fill=(dzx, 0))


def headsum(v, name):
    def body(v_ref, o_ref):
        o_ref[...] = jnp.sum(v_ref[...], axis=1, keepdims=True)
    return _pcall(body, name=name, out_shape=jax.ShapeDtypeStruct((v.shape[0], 1), F32))(v)


CONV_ROWS = 256


def _shifted(win, o, rows):
    if o == 0:
        return win[0:rows]
    n = win.shape[0]
    return pltpu.roll(win, shift=n - o, axis=0)[0:rows]


def dwconv_fwd(x, x_cb0, w, b, K, ct, act, name):
    C = w.shape[1]
    pad = 8 if K <= 8 else 32
    KP = w.shape[0]
    n_out = 2 if act else 1

    def body(x_ref, w_ref, b_ref, *rest):
        o_refs, px = rest[:n_out], rest[n_out]
        px[0:pad, :] = jnp.zeros((pad, ct), F32)
        px[pad:pad + T, :] = x_ref[...]
        wv = w_ref[...]
        bv = b_ref[...]
        for r0 in range(0, T, CONV_ROWS):
            win = px[r0:r0 + CONV_ROWS + pad, :]
            acc = jnp.broadcast_to(bv, (CONV_ROWS, ct))
            for k in range(K):
                acc = acc + wv[k:k + 1, :] * _shifted(win, pad - (K - 1) + k, CONV_ROWS)
            o_refs[0][r0:r0 + CONV_ROWS, :] = acc
            if act:
                o_refs[1][r0:r0 + CONV_ROWS, :] = _silu(acc)

    return _pcall(
        body, name=name, grid=(C // ct,),
        in_specs=[pl.BlockSpec((T, ct), lambda j: (0, x_cb0 + j)), pl.BlockSpec((KP, ct), lambda j: (0, j)),
                  pl.BlockSpec((1, ct), lambda j: (0, j))],
        out_specs=[pl.BlockSpec((T, ct), lambda j: (0, j))] * n_out,
        out_shape=[jax.ShapeDtypeStruct((T, C), F32)] * n_out,
        scratch_shapes=[pltpu.VMEM((T + pad, ct), F32)],
        compiler_params=_cparams(("parallel",)),
    )(x, w, b)


def dwconv_bwd(dout, cpre, x, x_cb0, w, K, ct, act, out_dtype, name, into=None):
    C = w.shape[1]
    pad = 8 if K <= 8 else 32
    KP = w.shape[0]

    def body(*refs):
        dx_ref, dw_ref, db_ref, px, pd = refs[-5:]
        if act:
            d_ref, c_ref, x_ref, w_ref = refs[:4]
        else:
            d_ref, x_ref, w_ref = refs[:3]
        px[0:pad, :] = jnp.zeros((pad, ct), F32)
        px[pad:pad + T, :] = x_ref[...]
        pd[T:T + pad, :] = jnp.zeros((pad, ct), F32)
        if act:
            pd[0:T, :] = d_ref[...] * _dsilu(c_ref[...])
        else:
            pd[0:T, :] = d_ref[...]
        wv = w_ref[...]
        dws = [jnp.zeros((1, ct), F32) for _ in range(K)]
        db = jnp.zeros((1, ct), F32)
        for r0 in range(0, T, CONV_ROWS):
            dwin = pd[r0:r0 + CONV_ROWS + pad, :]
            xwin = px[r0:r0 + CONV_ROWS + pad, :]
            dc = dwin[0:CONV_ROWS]
            db = db + _rsum(dc)
            acc = jnp.zeros((CONV_ROWS, ct), F32)
            for k in range(K):
                acc = acc + wv[k:k + 1, :] * _shifted(dwin, K - 1 - k, CONV_ROWS)
                dws[k] = dws[k] + _rsum(dc * _shifted(xwin, pad - (K - 1) + k, CONV_ROWS))
            dx_ref[r0:r0 + CONV_ROWS, :] = acc.astype(dx_ref.dtype)
        dw_ref[...] = jnp.zeros((KP, ct), F32)
        for k in range(K):
            dw_ref[k:k + 1, :] = dws[k]
        db_ref[...] = db

    col = pl.BlockSpec((T, ct), lambda j: (0, j))
    in_specs = [col] + ([col] if act else []) + [pl.BlockSpec((T, ct), lambda j: (0, x_cb0 + j)),
                                                 pl.BlockSpec((KP, ct), lambda j: (0, j))]
    args = [dout] + ([cpre] if act else []) + [x, w]
    dx_spec, dx_shape, aliases = col, jax.ShapeDtypeStruct((T, C), out_dtype), {}
    if into is not None:
        dx_spec = pl.BlockSpec((T, ct), lambda j: (0, into[1] + j))
        dx_shape = jax.ShapeDtypeStruct(into[0].shape, into[0].dtype)
        aliases = {len(args): 0}
        in_specs.append(ANY)
        args.append(into[0])
    return _pcall(
        body, name=name, grid=(C // ct,), in_specs=in_specs,
        out_specs=[dx_spec, pl.BlockSpec((KP, ct), lambda j: (0, j)), pl.BlockSpec((1, ct), lambda j: (0, j))],
        out_shape=[dx_shape, jax.ShapeDtypeStruct((KP, C), F32), jax.ShapeDtypeStruct((1, C), F32)],
        input_output_aliases=aliases,
        scratch_shapes=[pltpu.VMEM((T + pad, ct), F32), pltpu.VMEM((T + pad, ct), F32)],
        compiler_params=_cparams(("parallel",)),
    )(*args)


def _scan(a, axis, reverse=False):
    n = a.shape[axis]
    idx = lax.broadcasted_iota(jnp.int32, a.shape, axis)
    s = 1
    while s < n:
        if reverse:
            a = a + jnp.where(idx < n - s, pltpu.roll(a, shift=n - s, axis=axis), 0.0)
        else:
            a = a + jnp.where(idx >= s, pltpu.roll(a, shift=s, axis=axis), 0.0)
        s *= 2
    return a


_NT = _DIMS["nt"]
_TN = _DIMS["tn"]


def _dot(a, b, dims=_DIMS["nn"]):
    return lax.dot_general(a, b, dims, preferred_element_type=F32)


def ssd_fwd(xbc, dtx, dax, daT, dfull, name):
    def body(xbc_ref, dtx_ref, dax_ref, daT_ref, df_ref, y_ref, st_ref, S):
        ci = pl.program_id(0)

        @pl.when(ci == 0)
        def _():
            S[...] = jnp.zeros_like(S)

        row = lax.broadcasted_iota(jnp.int32, (CH, CH), 0)
        lane = lax.broadcasted_iota(jnp.int32, (CH, CH), 1)
        acsT = _scan(daT_ref[...], 1)
        for g in range(NG):
            c0 = g * GW
            xs = xbc_ref[:, c0:c0 + GW]
            acs = _scan(dax_ref[:, c0:c0 + GW], 0)
            Bm = xbc_ref[:, DI + g * DS:DI + (g + 1) * DS].astype(BF16)
            Cm = xbc_ref[:, DI + NG * DS + g * DS:DI + NG * DS + (g + 1) * DS].astype(BF16)
            xdt = xs * dtx_ref[:, c0:c0 + GW]
            atot = acs[CH - 1:CH, :]
            Sg = S[:, c0:c0 + GW]
            st_ref[:, c0:c0 + GW] = Sg
            CB = _dot(Cm, Bm, _NT)
            yg = jnp.exp(acs) * _dot(Cm, Sg.astype(BF16)) + xs * df_ref[:, c0:c0 + GW]
            xd = (xdt * jnp.exp(atot - acs)).astype(BF16)
            S[:, c0:c0 + GW] = jnp.exp(atot) * Sg + _dot(Bm, xd, _TN)
            xdt_b = xdt.astype(BF16)
            for r in range(NH // NG):
                h = g * (NH // NG) + r
                hs = slice(r * HD, (r + 1) * HD)
                seg = acs[:, r * HD:r * HD + 1] - acsT[h:h + 1, :]
                Lm = jnp.where(row >= lane, jnp.exp(jnp.minimum(seg, 0.0)), 0.0)
                yd = _dot((CB * Lm).astype(BF16), xdt_b[:, hs])
                y_ref[:, c0 + r * HD:c0 + (r + 1) * HD] = yg[:, hs] + yd

    return _pcall(
        body, name=name, grid=(NCH,),
        in_specs=[pl.BlockSpec((CH, CONVD), lambda i: (i, 0)), pl.BlockSpec((CH, DI), lambda i: (i, 0)),
                  pl.BlockSpec((CH, DI), lambda i: (i, 0)), pl.BlockSpec((NH, CH), lambda i: (0, i)),
                  pl.BlockSpec((1, DI), lambda i: (0, 0))],
        out_specs=[pl.BlockSpec((CH, DI), lambda i: (i, 0)), pl.BlockSpec((None, DS, DI), lambda i: (i, 0, 0))],
        out_shape=[jax.ShapeDtypeStruct((T, DI), F32), jax.ShapeDtypeStruct((NCH, DS, DI), F32)],
        scratch_shapes=[pltpu.VMEM((DS, DI), F32)],
        compiler_params=_cparams(("arbitrary",)),
    )(xbc, dtx, dax, daT, dfull)


def ssd_bwd(dy, xbc, dtx, dax, daT, dfull, states, name):
    hsum = (jnp.arange(DI)[:, None] // HD == jnp.arange(LANES)[None, :]).astype(BF16).reshape(NG, GW, LANES)

    def body(dy_ref, xbc_ref, dtx_ref, dax_ref, daT_ref, df_ref, st_ref, hsum_ref, dxbc_ref, ddt_ref, dda_ref, dD_ref, dS):
        i = pl.program_id(0)

        @pl.when(i == 0)
        def _():
            dS[...] = jnp.zeros_like(dS)
            dD_ref[...] = jnp.zeros_like(dD_ref)

        row = lax.broadcasted_iota(jnp.int32, (CH, CH), 0)
        lane = lax.broadcasted_iota(jnp.int32, (CH, CH), 1)
        acsT = _scan(daT_ref[...], 1)
        ddt_all = jnp.zeros((CH, LANES), F32)
        dacs_all = jnp.zeros((CH, LANES), F32)
        colacc = jnp.zeros((CH, CH), F32)
        for g in range(NG):
            c0 = g * GW
            xs = xbc_ref[:, c0:c0 + GW]
            dtx = dtx_ref[:, c0:c0 + GW]
            acs = _scan(dax_ref[:, c0:c0 + GW], 0)
            Bm = xbc_ref[:, DI + g * DS:DI + (g + 1) * DS].astype(BF16)
            Cm = xbc_ref[:, DI + NG * DS + g * DS:DI + NG * DS + (g + 1) * DS].astype(BF16)
            xdt = xs * dtx
            atot = acs[CH - 1:CH, :]
            Sin = st_ref[:, c0:c0 + GW]
            dyg = dy_ref[:, c0:c0 + GW]
            dSo = dS[:, c0:c0 + GW]
            E = jnp.exp(acs)
            Etot = jnp.exp(atot)
            dec = jnp.exp(atot - acs)
            dD_ref[:, c0:c0 + GW] += _rsum(dyg * xs)
            dxs = dyg * df_ref[:, c0:c0 + GW]
            Sin_b = Sin.astype(BF16)
            dSo_b = dSo.astype(BF16)
            dY0 = dyg * E
            dY0_b = dY0.astype(BF16)
            dC = _dot(dY0_b, Sin_b, _NT)
            dS[:, c0:c0 + GW] = _dot(Cm, dY0_b, _TN) + Etot * dSo
            XD = xdt * dec
            dXD = _dot(Bm, dSo_b)
            dB = _dot(XD.astype(BF16), dSo_b, _NT)
            dxdt = dXD * dec
            Gq = dXD * XD
            dacs_x = dY0 * _dot(Cm, Sin_b) - Gq
            datot_x = _rsum(dSo * Sin) * Etot + _rsum(Gq)
            dacs_all = dacs_all + _spread2(dacs_x, hsum_ref[g])
            dtot8 = _spread2(jnp.broadcast_to(datot_x, (8, GW)), hsum_ref[g])
            dacs_all = dacs_all + jnp.where(row == CH - 1, jnp.broadcast_to(dtot8[0:1, :], (CH, LANES)), 0.0)
            CB = _dot(Cm, Bm, _NT)
            dCB = jnp.zeros((CH, CH), F32)
            xdt_b = xdt.astype(BF16)
            dy_b = dyg.astype(BF16)
            for r in range(NH // NG):
                h = g * (NH // NG) + r
                hs = slice(r * HD, (r + 1) * HD)
                seg = acs[:, r * HD:r * HD + 1] - acsT[h:h + 1, :]
                Lm = jnp.where(row >= lane, jnp.exp(jnp.minimum(seg, 0.0)), 0.0)
                dyr = dy_b[:, hs]
                dML = _dot(dyr, xdt_b[:, hs], _NT) * Lm
                dxbc_ref[:, c0 + r * HD:c0 + (r + 1) * HD] = _dot((CB * Lm).astype(BF16), dyr, _TN)
                dCB = dCB + dML
                dseg = dML * CB
                dacs_all = dacs_all + _spread2(dseg, (lane == h).astype(BF16))
                colacc = colacc + jnp.where(row == h, jnp.sum(dseg, axis=0, keepdims=True), 0.0)
            dxdt = dxdt + dxbc_ref[:, c0:c0 + GW]
            ddt_all = ddt_all + _spread2(dxdt * xs, hsum_ref[g])
            dxbc_ref[:, c0:c0 + GW] = dxs + dxdt * dtx
            dCB_b = dCB.astype(BF16)
            dxbc_ref[:, DI + g * DS:DI + (g + 1) * DS] = dB + _dot(dCB_b, Cm, _TN)
            dxbc_ref[:, DI + NG * DS + g * DS:DI + NG * DS + (g + 1) * DS] = dC + _dot(dCB_b, Bm)
        ddt_ref[...] = ddt_all
        dda_ref[...] = _scan(dacs_all - colacc.T, 0, reverse=True)

    last = NCH - 1
    return _pcall(
        body, name=name, grid=(NCH,),
        in_specs=[pl.BlockSpec((CH, DI), lambda i: (last - i, 0)), pl.BlockSpec((CH, CONVD), lambda i: (last - i, 0)),
                  pl.BlockSpec((CH, DI), lambda i: (last - i, 0)), pl.BlockSpec((CH, DI), lambda i: (last - i, 0)),
                  pl.BlockSpec((NH, CH), lambda i: (0, last - i)), pl.BlockSpec((1, DI), lambda i: (0, 0)),
                  pl.BlockSpec((None, DS, DI), lambda i: (last - i, 0, 0)),
                  pl.BlockSpec((NG, GW, LANES), lambda i: (0, 0, 0))],
        out_specs=[pl.BlockSpec((CH, CONVD), lambda i: (last - i, 0)), pl.BlockSpec((CH, LANES), lambda i: (last - i, 0)),
                   pl.BlockSpec((CH, LANES), lambda i: (last - i, 0)), pl.BlockSpec((1, DI), lambda i: (0, 0))],
        out_shape=[jax.ShapeDtypeStruct((T, CONVD), F32), jax.ShapeDtypeStruct((T, LANES), F32),
                   jax.ShapeDtypeStruct((T, LANES), F32), jax.ShapeDtypeStruct((1, DI), F32)],
        scratch_shapes=[pltpu.VMEM((DS, DI), F32)],
        compiler_params=_cparams(("arbitrary",)),
    )(dy, xbc, dtx, dax, daT, dfull, states, hsum)


def _as3d(shape):
    if len(shape) == 1:
        return (1, 1, shape[0])
    if len(shape) == 2:
        return (1, shape[0], shape[1])
    return (math.prod(shape[:-2]), shape[-2], shape[-1])


def adamw_small(gs, ws, ms, vs, name):
    n = len(ws)
    bc1 = 1.0 - ADAM_B1 ** ADAM_STEP
    bc2 = 1.0 - ADAM_B2 ** ADAM_STEP

    def body(*refs):
        for a in range(n):
            g_ref, w_ref, m_ref, v_ref = refs[4 * a:4 * a + 4]
            g_out, d_out, m_out, v_out = refs[4 * n + 4 * a:4 * n + 4 * a + 4]
            g = g_ref[...]
            mn = ADAM_B1 * m_ref[...] + (1.0 - ADAM_B1) * g
            vn = ADAM_B2 * v_ref[...] + (1.0 - ADAM_B2) * (g * g)
            g_out[...] = g
            m_out[...] = mn
            v_out[...] = vn
            d_out[...] = -ADAM_LR * ((mn / bc1) / (jnp.sqrt(vn / bc2) + ADAM_EPS) + ADAM_WD * w_ref[...])

    args, out_shape = [], []
    for g, w, m, v in zip(gs, ws, ms, vs):
        s3 = _as3d(w.shape)
        args += [t.reshape(s3) for t in (g, w, m, v)]
        out_shape += [jax.ShapeDtypeStruct(s3, F32)] * 4
    outs = _pcall(body, name=name, out_shape=out_shape)(*args)
    return [[o.reshape(w.shape) for o in outs[4 * a:4 * a + 4]] for a, w in enumerate(ws)]


ADAMW_STEPS = 4


def adamw_stage(recvs, owns, ws, ms, vs, layers, prevs, name):
    n = len(ws)
    chained = prevs[0] is not None
    assert all((p is not None) == chained for p in prevs)
    bc1 = 1.0 - ADAM_B1 ** ADAM_STEP
    bc2 = 1.0 - ADAM_B2 ** ADAM_STEP
    n_in = (9 if chained else 5) * n

    def body(*refs):
        for a in range(n):
            r_ref, o_ref, w_ref, m_ref, v_ref = refs[5 * a:5 * a + 5]
            g_out, d_out, m_out, v_out = refs[n_in + 4 * a:n_in + 4 * a + 4]
            g = o_ref[...].astype(F32)
            for k in range(r_ref.shape[0]):
                g = g + r_ref[k].astype(F32)
            mn = ADAM_B1 * m_ref[...] + (1.0 - ADAM_B1) * g
            vn = ADAM_B2 * v_ref[...] + (1.0 - ADAM_B2) * (g * g)
            g_out[...] = g
            m_out[...] = mn
            v_out[...] = vn
            d_out[...] = -ADAM_LR * ((mn / bc1) / (jnp.sqrt(vn / bc2) + ADAM_EPS) + ADAM_WD * w_ref[...])

    in_specs, args, out_specs, out_shape = [], [], [], []
    for a in range(n):
        _, R, C = ws[a].shape
        tr = R // ADAMW_STEPS
        assert tr * ADAMW_STEPS == R and tr % 8 == 0
        slot = pl.BlockSpec((None, tr, C), functools.partial(lambda r, l: (l, r, 0), l=layers[a]))
        own = pl.BlockSpec((None, tr, C), lambda r: (2 * lax.axis_index("x") + lax.axis_index("y"), r, 0))
        in_specs += [pl.BlockSpec((recvs[a].shape[0], tr, C), lambda r: (0, r, 0)), own, slot, slot, slot]
        args += [recvs[a], owns[a], ws[a], ms[a], vs[a]]
        out_specs += [slot] * 4
        out_shape += [jax.ShapeDtypeStruct(ws[a].shape, F32)] * 4
    aliases = {}
    if chained:
        for a in range(n):
            in_specs += [ANY] * 4
            args += list(prevs[a])
            aliases.update({5 * n + 4 * a + k: 4 * a + k for k in range(4)})
    outs = _pcall(
        body, name=name, grid=(ADAMW_STEPS,), in_specs=in_specs, out_specs=out_specs, out_shape=out_shape,
        input_output_aliases=aliases, compiler_params=_cparams(("parallel",)),
    )(*args)
    return [list(outs[4 * a:4 * a + 4]) for a in range(n)]


def sum_leading(parts, name):
    P, R, C = parts.shape

    def body(p_ref, o_ref):
        s = p_ref[0]
        for k in range(1, P):
            s = s + p_ref[k]
        o_ref[...] = s

    return _pcall(body, name=name, out_shape=jax.ShapeDtypeStruct((R, C), F32))(parts)


def pair_sum(gsends, recvs, name):
    n = len(gsends)

    def body(*refs):
        for g_ref, r_ref, o_ref in zip(refs[:n], refs[n:2 * n], refs[2 * n:]):
            o_ref[...] = (g_ref[...].astype(F32) + r_ref[...].astype(F32)).astype(o_ref.dtype)

    def slot(a):
        return pl.BlockSpec((None,) + a.shape[1:], lambda q: (q, 0, 0))

    def own(a):
        return pl.BlockSpec((None,) + a.shape[1:], lambda q: (2 * q + lax.axis_index("c"), 0, 0))

    return _pcall(
        body, name=name, grid=(4,), in_specs=[own(g) for g in gsends] + [slot(r) for r in recvs],
        out_specs=[slot(r) for r in recvs], out_shape=[jax.ShapeDtypeStruct(r.shape, BF16) for r in recvs],
        compiler_params=_cparams(("parallel",)),
    )(*gsends, *recvs)


def _place():
    return lax.axis_index("x"), lax.axis_index("y"), lax.axis_index("c")


def _other_chips(x, y):
    return [(1 - x, y), (x, 1 - y), (1 - x, 1 - y)]


def all_gather(arrs, name):
    n = len(arrs)

    def body(*refs):
        ins, outs = refs[:n], refs[n:2 * n]
        send_sems, recv_sems, local_sems = refs[2 * n:]
        x, y, c = _place()
        me, sibling = (x, y, c), (x, y, 1 - c)
        chips = _other_chips(x, y)

        def slot(a, px, py, pc):
            return outs[a].at[4 * px + 2 * py + pc]

        def copy(a, k, block, to, src=None):
            return pltpu.make_async_remote_copy(
                src_ref=slot(a, *block) if src is None else src, dst_ref=slot(a, *block),
                send_sem=send_sems.at[a, k], recv_sem=recv_sems.at[a, k], device_id=to, device_id_type=MESH)

        mine, first, passed = [], [], []
        for a in range(n):
            cp = pltpu.make_async_copy(ins[a], slot(a, *me), local_sems.at[a])
            cp.start()
            mine.append(cp)
            first.append(copy(a, 0, me, sibling, src=ins[a]))
            first += [copy(a, 1 + j, me, (*chip, c), src=ins[a]) for j, chip in enumerate(chips)]
        for cp in first:
            cp.start()
        for j, chip in enumerate(chips):
            for a in range(n):
                copy(a, 1 + j, (*chip, c), me).wait_recv()
                cp = copy(a, 4 + j, (*chip, c), sibling)
                cp.start()
                passed.append(cp)
        for a in range(n):
            copy(a, 0, sibling, me).wait_recv()
            for j, chip in enumerate(chips):
                copy(a, 4 + j, (*chip, 1 - c), me).wait_recv()
        for cp in first + passed:
            cp.wait_send()
        for cp in mine:
            cp.wait()

    return _pcall(
        body, name=name, in_specs=[ANY] * n, out_specs=[ANY] * n,
        out_shape=[jax.ShapeDtypeStruct((N_DEV,) + a.shape, a.dtype) for a in arrs],
        scratch_shapes=[pltpu.SemaphoreType.DMA((n, 7)), pltpu.SemaphoreType.DMA((n, 7)), pltpu.SemaphoreType.DMA((n,))],
    )(*arrs)


def sibling_exchange(gsends, name, after=None):
    n = len(gsends)
    n_in = n + (1 if after is not None else 0)

    def body(*refs):
        ins, outs = refs[:n], refs[n_in:n_in + n]
        send_sems, recv_sems = refs[n_in + n:]
        x, y, c = _place()
        copies = []
        for a in range(n):
            for q in range(4):
                cp = pltpu.make_async_remote_copy(
                    src_ref=ins[a].at[2 * q + 1 - c], dst_ref=outs[a].at[q],
                    send_sem=send_sems.at[a, q], recv_sem=recv_sems.at[a, q],
                    device_id=(x, y, 1 - c), device_id_type=MESH)
                cp.start()
                copies.append(cp)
        for cp in copies:
            cp.wait()

    return _pcall(
        body, name=name, in_specs=[ANY] * n_in, out_specs=[ANY] * n,
        out_shape=[jax.ShapeDtypeStruct((4,) + g.shape[1:], g.dtype) for g in gsends],
        scratch_shapes=[pltpu.SemaphoreType.DMA((n, 4)), pltpu.SemaphoreType.DMA((n, 4))],
    )(*gsends, *([after] if after is not None else []))


HBM =pl.BlockSpec(memory_space=pltpu.HBM)
SEM = pl.BlockSpec(memory_space=pltpu.SEMAPHORE)
EFFECT = pltpu.SideEffectType.DATAFLOW_SIDE_EFFECTING


def _in_hbm(a):
    return pltpu.with_memory_space_constraint(a, pltpu.HBM)


def _gather_peers(x, y, c):
    to = [(x, y, 1 - c)] + [(px, py, c) for px, py in _other_chips(x, y)]
    return to, [4 * px + 2 * py + pc for px, py, pc in to]


def gather_start(arrs, after, name):
    n = len(arrs)
    n_in = 2 * n + (1 if after is not None else 0)

    def body(*refs):
        srcs, lands = refs[:n], refs[n:2 * n]
        send_sems, recv_sems = refs[n_in], refs[n_in + 1]
        token = refs[-1]
        x, y, c = _place()
        to, _ = _gather_peers(x, y, c)
        me = 4 * x + 2 * y + c
        for a in range(n):
            for k, dev in enumerate(to):
                pltpu.make_async_remote_copy(
                    src_ref=srcs[a], dst_ref=lands[a].at[me], send_sem=send_sems.at[4 * a + k], recv_sem=recv_sems.at[4 * a + k],
                    device_id=dev, device_id_type=MESH).start()
        token[...] = jnp.zeros_like(token)

    zones = [lax.empty((N_DEV,) + a.shape, a.dtype) for a in arrs]
    args = [_in_hbm(a) for a in arrs] + [_in_hbm(z) for z in zones] + ([after] if after is not None else [])
    outs = _pcall(
        body, name=name,
        out_shape=(pltpu.SemaphoreType.DMA((4 * n,)), pltpu.SemaphoreType.DMA((4 * n,)),
                   *[pltpu.HBM(a.shape, a.dtype) for a in arrs], *[pltpu.HBM(z.shape, z.dtype) for z in zones],
                   jax.ShapeDtypeStruct((8, LANES), F32)),
        in_specs=[HBM] * (2 * n) + ([ANY] if after is not None else []),
        out_specs=(SEM, SEM, *[HBM] * (2 * n), pl.BlockSpec(memory_space=pltpu.VMEM)),
        input_output_aliases={i: 2 + i for i in range(2 * n)},
        compiler_params=pltpu.CompilerParams(has_side_effects=EFFECT),
    )(*args)
    return dict(send=outs[0], recv=outs[1], srcs=list(outs[2:2 + n]), lands=list(outs[2 + n:2 + 2 * n]), token=outs[-1])


def gather_wait(st, after, name):
    n = len(st["srcs"])

    def body(*refs):
        srcs, lands = refs[:n], refs[n:2 * n]
        send_sems, recv_sems = refs[2 * n], refs[2 * n + 1]
        x, y, c = _place()
        to, slots = _gather_peers(x, y, c)
        for a in range(n):
            for k, dev in enumerate(to):
                cp = pltpu.make_async_remote_copy(
                    src_ref=srcs[a], dst_ref=lands[a].at[slots[k]], send_sem=send_sems.at[4 * a + k],
                    recv_sem=recv_sems.at[4 * a + k], device_id=dev, device_id_type=MESH)
                cp.wait_send()
                cp.wait_recv()

    outs = _pcall(
        body, name=name,
        out_shape=(*[pltpu.HBM(a.shape, a.dtype) for a in st["srcs"]], *[pltpu.HBM(z.shape, z.dtype) for z in st["lands"]]),
        in_specs=[HBM] * (2 * n) + [SEM, SEM, ANY], out_specs=tuple([HBM] * (2 * n)),
        input_output_aliases={i: i for i in range(2 * n)},
        compiler_params=pltpu.CompilerParams(has_side_effects=EFFECT),
    )(*st["srcs"], *st["lands"], st["send"], st["recv"], after)
    return list(outs[n:])


def pass_start(zones, name):
    n = len(zones)

    def body(*refs):
        zs = refs[:n]
        send_sems, recv_sems = refs[n], refs[n + 1]
        token = refs[-1]
        x, y, c = _place()
        for a in range(n):
            for j, (px, py) in enumerate(_other_chips(x, y)):
                blk = zs[a].at[4 * px + 2 * py + c]
                pltpu.make_async_remote_copy(
                    src_ref=blk, dst_ref=blk, send_sem=send_sems.at[3 * a + j], recv_sem=recv_sems.at[3 * a + j],
                    device_id=(x, y, 1 - c), device_id_type=MESH).start()
        token[...] = jnp.zeros_like(token)

    outs = _pcall(
        body, name=name,
        out_shape=(pltpu.SemaphoreType.DMA((3 * n,)), pltpu.SemaphoreType.DMA((3 * n,)),
                   *[pltpu.HBM(z.shape, z.dtype) for z in zones], jax.ShapeDtypeStruct((8, LANES), F32)),
        in_specs=[HBM] * n, out_specs=(SEM, SEM, *[HBM] * n, pl.BlockSpec(memory_space=pltpu.VMEM)),
        input_output_aliases={i: 2 + i for i in range(n)},
        compiler_params=pltpu.CompilerParams(has_side_effects=EFFECT),
    )(*zones)
    return dict(send=outs[0], recv=outs[1], zones=list(outs[2:2 + n]), token=outs[-1])


def pass_wait(st, after, name):
    n = len(st["zones"])

    def body(*refs):
        zs = refs[:n]
        send_sems, recv_sems = refs[n], refs[n + 1]
        x, y, c = _place()
        for a in range(n):
            for j, (px, py) in enumerate(_other_chips(x, y)):
                cp = pltpu.make_async_remote_copy(
                    src_ref=zs[a].at[4 * px + 2 * py + c], dst_ref=zs[a].at[4 * px + 2 * py + 1 - c],
                    send_sem=send_sems.at[3 * a + j], recv_sem=recv_sems.at[3 * a + j],
                    device_id=(x, y, 1 - c), device_id_type=MESH)
                cp.wait_send()
                cp.wait_recv()

    outs = _pcall(
        body, name=name, out_shape=tuple(pltpu.HBM(z.shape, z.dtype) for z in st["zones"]),
        in_specs=[HBM] * n + [SEM, SEM, ANY], out_specs=tuple([HBM] * n),
        input_output_aliases={i: i for i in range(n)},
        compiler_params=pltpu.CompilerParams(has_side_effects=EFFECT),
    )(*st["zones"], st["send"], st["recv"], after)
    return list(outs)


def sibling_start(gsends, name):
    n = len(gsends)

    def body(*refs):
        srcs, lands = refs[:n], refs[n:2 * n]
        send_sems, recv_sems = refs[2 * n], refs[2 * n + 1]
        token = refs[-1]
        x, y, c = _place()
        for a in range(n):
            for q in range(4):
                pltpu.make_async_remote_copy(
                    src_ref=srcs[a].at[2 * q + 1 - c], dst_ref=lands[a].at[q], send_sem=send_sems.at[4 * a + q],
                    recv_sem=recv_sems.at[4 * a + q], device_id=(x, y, 1 - c), device_id_type=MESH).start()
        token[...] = jnp.zeros_like(token)

    zones = [lax.empty((4,) + g.shape[1:], g.dtype) for g in gsends]
    outs = _pcall(
        body, name=name,
        out_shape=(pltpu.SemaphoreType.DMA((4 * n,)), pltpu.SemaphoreType.DMA((4 * n,)),
                   *[pltpu.HBM(g.shape, g.dtype) for g in gsends], *[pltpu.HBM(z.shape, z.dtype) for z in zones],
                   jax.ShapeDtypeStruct((8, LANES), F32)),
        in_specs=[HBM] * (2 * n), out_specs=(SEM, SEM, *[HBM] * (2 * n), pl.BlockSpec(memory_space=pltpu.VMEM)),
        input_output_aliases={i: 2 + i for i in range(2 * n)},
        compiler_params=pltpu.CompilerParams(has_side_effects=EFFECT),
    )(*[_in_hbm(g) for g in gsends], *[_in_hbm(z) for z in zones])
    return dict(send=outs[0], recv=outs[1], srcs=list(outs[2:2 + n]), lands=list(outs[2 + n:2 + 2 * n]), token=outs[-1])


def sibling_wait(st, after, name):
    n = len(st["srcs"])

    def body(*refs):
        srcs, lands = refs[:n], refs[n:2 * n]
        send_sems, recv_sems = refs[2 * n], refs[2 * n + 1]
        x, y, c = _place()
        for a in range(n):
            for q in range(4):
                cp = pltpu.make_async_remote_copy(
                    src_ref=srcs[a].at[2 * q + 1 - c], dst_ref=lands[a].at[q], send_sem=send_sems.at[4 * a + q],
                    recv_sem=recv_sems.at[4 * a + q], device_id=(x, y, 1 - c), device_id_type=MESH)
                cp.wait_send()
                cp.wait_recv()

    outs = _pcall(
        body, name=name,
        out_shape=(*[pltpu.HBM(a.shape, a.dtype) for a in st["srcs"]], *[pltpu.HBM(z.shape, z.dtype) for z in st["lands"]]),
        in_specs=[HBM] * (2 * n) + [SEM, SEM, ANY], out_specs=tuple([HBM] * (2 * n)),
        input_output_aliases={i: i for i in range(2 * n)},
        compiler_params=pltpu.CompilerParams(has_side_effects=EFFECT),
    )(*st["srcs"], *st["lands"], st["send"], st["recv"], after)
    return list(outs[:n]), list(outs[n:])


def scatter_start(parts, name):
    n = len(parts)

    def body(*refs):
        srcs, lands = refs[:n], refs[n:2 * n]
        send_sems, recv_sems = refs[2 * n], refs[2 * n + 1]
        token = refs[-1]
        x, y, c = _place()
        for a in range(n):
            for j, (px, py) in enumerate(_other_chips(x, y)):
                pltpu.make_async_remote_copy(
                    src_ref=srcs[a].at[2 * px + py], dst_ref=lands[a].at[j], send_sem=send_sems.at[3 * a + j],
                    recv_sem=recv_sems.at[3 * a + j], device_id=(px, py, c), device_id_type=MESH).start()
        token[...] = jnp.zeros_like(token)

    zones = [lax.empty((3,) + p.shape[1:], p.dtype) for p in parts]
    outs = _pcall(
        body, name=name,
        out_shape=(pltpu.SemaphoreType.DMA((3 * n,)), pltpu.SemaphoreType.DMA((3 * n,)),
                   *[pltpu.HBM(p.shape, p.dtype) for p in parts], *[pltpu.HBM(z.shape, z.dtype) for z in zones],
                   jax.ShapeDtypeStruct((8, LANES), F32)),
        in_specs=[HBM] * (2 * n), out_specs=(SEM, SEM, *[HBM] * (2 * n), pl.BlockSpec(memory_space=pltpu.VMEM)),
        input_output_aliases={i: 2 + i for i in range(2 * n)},
        compiler_params=pltpu.CompilerParams(has_side_effects=EFFECT),
    )(*[_in_hbm(p) for p in parts], *[_in_hbm(z) for z in zones])
    return dict(send=outs[0], recv=outs[1], srcs=list(outs[2:2 + n]), lands=list(outs[2 + n:2 + 2 * n]), token=outs[-1])


def scatter_wait(st, after, name):
    n = len(st["srcs"])

    def body(*refs):
        srcs, lands = refs[:n], refs[n:2 * n]
        send_sems, recv_sems = refs[2 * n], refs[2 * n + 1]
        x, y, c = _place()
        for a in range(n):
            for j, (px, py) in enumerate(_other_chips(x, y)):
                cp = pltpu.make_async_remote_copy(
                    src_ref=srcs[a].at[2 * px + py], dst_ref=lands[a].at[j], send_sem=send_sems.at[3 * a + j],
                    recv_sem=recv_sems.at[3 * a + j], device_id=(px, py, c), device_id_type=MESH)
                cp.wait_send()
                cp.wait_recv()

    outs = _pcall(
        body, name=name,
        out_shape=(*[pltpu.HBM(a.shape, a.dtype) for a in st["srcs"]], *[pltpu.HBM(z.shape, z.dtype) for z in st["lands"]]),
        in_specs=[HBM] * (2 * n) + [SEM, SEM, ANY], out_specs=tuple([HBM] * (2 * n)),
        input_output_aliases={i: i for i in range(2 * n)},
        compiler_params=pltpu.CompilerParams(has_side_effects=EFFECT),
    )(*st["srcs"], *st["lands"], st["send"], st["recv"], after)
    return list(outs[:n]), list(outs[n:])


def _unshard(g, axis):
    nd = g.ndim - 1
    axis = axis % nd
    t = jnp.moveaxis(g, 0, axis)
    shp = list(g.shape[1:])
    shp[axis] *= N_DEV
    return t.reshape(shp)


def _to_shards(full, axis):
    axis = axis % full.ndim
    shp = list(full.shape)
    shp[axis:axis + 1] = [N_DEV, shp[axis] // N_DEV]
    return jnp.moveaxis(full.reshape(shp), axis, 0)


def _pack(arrs, rows):
    flat = jnp.concatenate([a.reshape(-1).astype(F32) for a in arrs])
    return jnp.pad(flat, (0, rows * LANES - flat.shape[0])).reshape(rows, LANES)


def _unpack(buf, shapes):
    flat = buf.reshape(-1)
    out, off = [], 0
    for s in shapes:
        n = math.prod(s)
        out.append(flat[off:off + n].reshape(s))
        off += n
    return out


def _rows_for(shapes):
    n = sum(math.prod(s) for s in shapes)
    return -(-n // (8 * LANES)) * 8


def _row(v, width=None):
    v = v.reshape(1, -1).astype(F32)
    if width is not None and v.shape[1] < width:
        v = jnp.pad(v, ((0, 0), (0, width - v.shape[1])))
    return v


def _after(order, width):
    if not order:
        return None
    t = order[0][0:1, 0:1]
    for o in order[1:]:
        t = t + o[0:1, 0:1]
    return jnp.broadcast_to(t, (1, width))


def _norm_after(norm, order):
    row = _after(order, norm.shape[1])
    return norm if row is None else norm + row


FFN_TN = 256


def ffn_in(h, norm, w_gate, w_up, name):
    def body(h_ref, n_ref, wg_ref, wu_ref, u_ref, g_ref, up_ref, act_ref, u_s):
        @pl.when(pl.program_id(0) == 0)
        def _():
            x = h_ref[...]
            r = lax.rsqrt(jnp.mean(x * x, axis=-1, keepdims=True) + EPS)
            u_s[...] = (x * r * n_ref[...]).astype(BF16)
            u_ref[...] = u_s[...]

        u = u_s[...]
        g = _dot(u, wg_ref[...], _NT)
        up = _dot(u, wu_ref[...], _NT)
        g_ref[...] = g.astype(BF16)
        up_ref[...] = up.astype(BF16)
        act_ref[...] = (_silu(g) * up).astype(BF16)

    whole = pl.BlockSpec((T, D), lambda j: (0, 0))
    wspec = pl.BlockSpec((FFN_TN, D), lambda j: (j, 0))
    col = pl.BlockSpec((T, FFN_TN), lambda j: (0, j))
    return _pcall(
        body, name=name, grid=(DFF // FFN_TN,), in_specs=[whole, pl.BlockSpec((1, D), lambda j: (0, 0)), wspec, wspec],
        out_specs=[whole, col, col, col],
        out_shape=[jax.ShapeDtypeStruct((T, D), BF16)] + [jax.ShapeDtypeStruct((T, DFF), BF16)] * 3,
        scratch_shapes=[pltpu.VMEM((T, D), BF16)], compiler_params=_cparams(("arbitrary",)),
    )(h, norm, w_gate, w_up)


def ffn_back(dh_b, w_down, g, up, after_row, name):
    has_row = after_row is not None

    def body(*refs):
        dh_ref, wd_ref, g_ref, up_ref = refs[:4]
        dg_ref, dup_ref = refs[-2:]
        da = _dot(dh_ref[...], wd_ref[...], _NT)
        if has_row:
            da = da + refs[4][...]
        g = g_ref[...].astype(F32)
        dg_ref[...] = (da * up_ref[...].astype(F32) * _dsilu(g)).astype(BF16)
        dup_ref[...] = (da * _silu(g)).astype(BF16)

    col = pl.BlockSpec((T, FFN_TN), lambda j: (0, j))
    in_specs = [pl.BlockSpec((T, D), lambda j: (0, 0)), pl.BlockSpec((FFN_TN, D), lambda j: (j, 0)), col, col]
    args = [dh_b, w_down, g, up]
    if has_row:
        in_specs.append(pl.BlockSpec((1, FFN_TN), lambda j: (0, j)))
        args.append(after_row)
    return _pcall(
        body, name=name, grid=(DFF // FFN_TN,), in_specs=in_specs, out_specs=[col, col],
        out_shape=[jax.ShapeDtypeStruct((T, DFF), BF16)] * 2, compiler_params=_cparams(("parallel",)),
    )(*args)


def ffn_layer_fwd(h, p, tag, order=()):
    u, g, up, act = ffn_in(h, _norm_after(p["norm"], order), p["w_gate"], p["w_up"], f"{tag}_in")
    h2 = matmul(act, p["w_down"], "nn", residual=h, name=f"{tag}_down")
    return h2, (h, u, g, up, act)


def ffn_layer_bwd(dh, dh_b, saved, p, tag, order=()):
    h, u, g, up, act = saved
    d_down = matmul(act, dh_b, "tn", out_dtype=BF16, name=f"{tag}_dwd")
    dg, dup = ffn_back(dh_b, p["w_down"], g, up, _after(order, DFF), f"{tag}_back")
    du = matmul(dg, p["w_gate"], "nn", name=f"{tag}_dug")
    du = matmul(dup, p["w_up"], "nn", residual=du, name=f"{tag}_duu")
    d_gate = matmul(dg, u, "tn", out_dtype=BF16, name=f"{tag}_dwg")
    d_up = matmul(dup, u, "tn", out_dtype=BF16, name=f"{tag}_dwu")
    dh2, dh2_b, d_norm = rms_bwd(du, h, p["norm"], dh, f"{tag}_drms")
    return dh2, dh2_b, dict(norm=d_norm, w_gate=d_gate, w_up=d_up, w_down=d_down)


def conv_layer_fwd(h, p, tag, order=()):
    u = rms_fwd(h, _norm_after(p["norm"], order), f"{tag}_rms")
    hh = matmul(u, p["w_pw1"], "nn", bias=p["b_pw1"], name=f"{tag}_pw1")
    gl = glu_fwd(hh, f"{tag}_glu")
    c2 = dwconv_fwd(gl, 0, p["dw_w"], p["dw_b"], KCV, 128, False, f"{tag}_dw")[0]
    s = ln_silu_fwd(c2, p["ln_g"], p["ln_b"], f"{tag}_ln")
    h2 = matmul(s, p["w_pw2"], "nn", bias=p["b_pw2"], residual=h, name=f"{tag}_pw2")
    return h2, (h, u, hh, gl, c2, s)


def conv_layer_bwd(dh, dh_b, saved, p, tag, order=()):
    h, u, hh, gl, c2, s = saved
    ds = matmul(dh_b, p["w_pw2"], "nt", bias=_after(order, D), name=f"{tag}_ds")
    d_pw2 = matmul(s, dh_b, "tn", out_dtype=BF16, name=f"{tag}_dwpw2")
    dc2, d_lng, d_lnb, d_bpw2 = ln_silu_bwd(ds, c2, dh, p["ln_g"], p["ln_b"], f"{tag}_dln")
    dgl, d_dww, d_dwb = dwconv_bwd(dc2, None, gl, 0, p["dw_w"], KCV, 128, False, F32, f"{tag}_ddw")
    dhh, d_bpw1 = glu_bwd(dgl, hh, f"{tag}_dglu")
    du = matmul(dhh, p["w_pw1"], "nt", name=f"{tag}_du")
    d_pw1 = matmul(u, dhh, "tn", out_dtype=BF16, name=f"{tag}_dwpw1")
    dh2, dh2_b, d_norm = rms_bwd(du, h, p["norm"], dh, f"{tag}_drms")
    grads = dict(norm=d_norm, w_pw1=d_pw1, b_pw1=d_bpw1, dw_w=d_dww[:KCV], dw_b=d_dwb, ln_g=d_lng, ln_b=d_lnb,
                 w_pw2=d_pw2, b_pw2=d_bpw2)
    return dh2, dh2_b, grads


def ssm_layer_fwd(h, p, tag, order=(), mid=None):
    u = rms_fwd(h, _norm_after(p["norm"], order), f"{tag}_rms")
    zx = matmul(u, p["w_in"], "nn", name=f"{tag}_in")
    cpre, xbc = dwconv_fwd(zx, DI // 512, p["conv_w"], p["conv_b"], KSSM, 512, True, f"{tag}_conv")
    dt, da, dtx, dax = dt_fwd(zx, p["dt_bias"], p["a_log"], f"{tag}_dt")
    daT = da[:, :NH].T
    y, states = ssd_fwd(xbc, dtx, dax, daT, p["d_full"], f"{tag}_ssd")
    gate_norm = p["gate_norm"] if mid is None else _norm_after(p["gate_norm"], mid(y))
    yn = gatenorm_fwd(y, zx, gate_norm, f"{tag}_gn")
    h2 = matmul(yn, p["w_out"], "nn", residual=h, name=f"{tag}_out")
    return h2, (h, u, zx, cpre, xbc, dt, dtx, dax, daT, y, states, yn)


def ssm_layer_bwd(dh, dh_b, saved, p, tag, order=()):
    h, u, zx, cpre, xbc, dt, dtx, dax, daT, y, states, yn = saved
    dyn = matmul(dh_b, p["w_out"], "nt", bias=_after(order, DI), name=f"{tag}_dyn")
    d_wout = matmul(yn, dh_b, "tn", out_dtype=BF16, name=f"{tag}_dwout")
    dy, dzx, d_gn = gatenorm_bwd(dyn, y, zx, p["gate_norm"], f"{tag}_dgn")
    dxbc, ddt, dda, dD = ssd_bwd(dy, xbc, dtx, dax, daT, p["d_full"], states, f"{tag}_dssd")
    dzx, d_dtb, d_alog = dt_bwd(ddt, dda, dt, zx, p["dt_bias"], p["a_log"], dzx, f"{tag}_ddt")
    dzx, d_cw, d_cb = dwconv_bwd(dxbc, cpre, zx, DI // 512, p["conv_w"], KSSM, 512, True, BF16, f"{tag}_dconv",
                                 into=(dzx, DI // 512))
    du = matmul(dzx, p["w_in"], "nt", name=f"{tag}_du")
    d_win = matmul(u, dzx, "tn", out_dtype=BF16, name=f"{tag}_dwin")
    dh2, dh2_b, d_norm = rms_bwd(du, h, p["norm"], dh, f"{tag}_drms")
    d_d = headsum(dD.reshape(NH, HD), f"{tag}_dD").reshape(NH)
    grads = dict(norm=d_norm, w_in=d_win[:, :DINP], conv_w=d_cw[:KSSM], conv_b=d_cb, dt_bias=d_dtb[0, :NH],
                 a_log=d_alog[0, :NH], d=d_d, gate_norm=d_gn, w_out=d_wout)
    return dh2, dh2_b, grads


BIG = ["ssm_w_in", "ssm_w_out", "cv_w_pw1", "cv_w_pw2", "ffn_w_gate", "ffn_w_up", "ffn_w_down"]
TRANSPOSED = ("ffn_w_gate", "ffn_w_up")
LAYER_AXIS = {"ssm_w_in": -1, "ssm_w_out": 0, "cv_w_pw1": -1, "cv_w_pw2": 0, "ffn_w_gate": 0, "ffn_w_up": 0,
              "ffn_w_down": 0}
SMALL_SHARDED = ["ssm_conv_w", "cv_norm", "cv_b_pw1", "cv_dw_w", "cv_dw_b", "cv_ln_g", "cv_ln_b", "cv_b_pw2"]
SMALL_REPL = ["ssm_norm", "ssm_conv_b", "ssm_dt_bias", "ssm_a_log", "ssm_d", "ssm_gate_norm", "ffn_norm", "final_norm"]
WEIGHTS = ["ssm_norm", "ssm_w_in", "ssm_conv_w", "ssm_conv_b", "ssm_dt_bias", "ssm_a_log", "ssm_d", "ssm_gate_norm",
           "ssm_w_out", "cv_norm", "cv_w_pw1", "cv_b_pw1", "cv_dw_w", "cv_dw_b", "cv_ln_g", "cv_ln_b", "cv_w_pw2",
           "cv_b_pw2", "ffn_norm", "ffn_w_gate", "ffn_w_up", "ffn_w_down", "final_norm"]
SMALL = [n for n in WEIGHTS if n not in BIG]


N_STAGES = 8
SYNC_SIBLING_STAGES = 2


def _stage_layer(s):
    i = s // 2
    if s % 2:
        return "ffn", i
    return ("ssm" if i % 2 == 0 else "cv"), i // 2


def _stage_group(s):
    fam, l = _stage_layer(s)
    names = {"ffn": ["ffn_w_gate", "ffn_w_up", "ffn_w_down"], "ssm": ["ssm_w_in", "ssm_w_out"],
             "cv": ["cv_w_pw1", "cv_w_pw2"]}[fam]
    return [(n, l) for n in names]


def _stage_params(s, big, small):
    fam, l = _stage_layer(s)
    if fam == "ffn":
        return dict(norm=_row(small["ffn_norm"][l]), w_gate=big["ffn_w_gate"], w_up=big["ffn_w_up"],
                    w_down=big["ffn_w_down"])
    if fam == "ssm":
        return dict(norm=_row(small["ssm_norm"][l]), w_in=jnp.pad(big["ssm_w_in"], ((0, 0), (0, DINP_PAD - DINP))),
                    conv_w=jnp.pad(small["ssm_conv_w"][l], ((0, 8 - KSSM), (0, 0))), conv_b=_row(small["ssm_conv_b"][l]),
                    dt_bias=_row(small["ssm_dt_bias"][l], LANES), a_log=_row(small["ssm_a_log"][l], LANES),
                    d_full=_row(jnp.repeat(small["ssm_d"][l], HD)), gate_norm=_row(small["ssm_gate_norm"][l]),
                    w_out=big["ssm_w_out"])
    return dict(norm=_row(small["cv_norm"][l]), w_pw1=big["cv_w_pw1"], b_pw1=_row(small["cv_b_pw1"][l]),
                dw_w=jnp.pad(small["cv_dw_w"][l], ((0, 32 - KCV), (0, 0))), dw_b=_row(small["cv_dw_b"][l]),
                ln_g=_row(small["cv_ln_g"][l]), ln_b=_row(small["cv_ln_b"][l]), w_pw2=big["cv_w_pw2"],
                b_pw2=_row(small["cv_b_pw2"][l]))


_STAGE_FWD = {"ffn": ffn_layer_fwd, "ssm": ssm_layer_fwd, "cv": conv_layer_fwd}
_STAGE_BWD = {"ffn": ffn_layer_bwd, "ssm": ssm_layer_bwd, "cv": conv_layer_bwd}


def _stage_fwd(s, h, p, order=(), mid=None):
    fam, l = _stage_layer(s)
    if mid is not None:
        return ssm_layer_fwd(h, p, f"{fam}{l}", order, mid)
    return _STAGE_FWD[fam](h, p, f"{fam}{l}", order)


def _stage_bwd(s, dh, dh_b, p, saved, order=()):
    fam, l = _stage_layer(s)
    dh, dh_b, g = _STAGE_BWD[fam](dh, dh_b, saved, p, f"{fam}{l}", order)
    return dh, dh_b, {f"{fam}_{k}": val for k, val in g.items()}


def _local(x, tgt, full):
    h, tape = x, []
    for s in range(N_STAGES):
        big = {n: (full[n][l].T if n in TRANSPOSED else full[n][l]) for n, l in _stage_group(s)}
        p = _stage_params(s, big, full)
        h, saved = _stage_fwd(s, h, p)
        tape.append((p, saved))
    dh, dh_b, d_final, loss_row = loss_head(h, _row(full["final_norm"]), tgt, "loss_head")
    gl = {n: [None] * full[n].shape[0] for n in WEIGHTS if n != "final_norm"}
    for s in reversed(range(N_STAGES)):
        dh, dh_b, g = _stage_bwd(s, dh, dh_b, *tape[s])
        for n, val in g.items():
            val = val.T if n in TRANSPOSED else val
            gl[n][_stage_layer(s)[1]] = val.reshape(full[n].shape[1:])
    grads = {n: jnp.stack(vs) for n, vs in gl.items()}
    grads["final_norm"] = d_final.reshape(D)
    return loss_row, dh, grads


def _step(x, tgt, w, m, v):
    idx = 4 * lax.axis_index("x") + 2 * lax.axis_index("y") + lax.axis_index("c")
    small_shapes = [w[n].shape for n in SMALL_SHARDED]
    small_pack = _pack([w[n] for n in SMALL_SHARDED], _rows_for(small_shapes))

    def view(n, a):
        return jnp.swapaxes(a, 1, 2) if n in TRANSPOSED else a

    wv, mv, vv = ({n: view(n, t[n]) for n in BIG} for t in (w, m, v))

    def blocks(s, zero=None):
        own = [wv[n][l] if zero is None else wv[n][l] + zero for n, l in _stage_group(s)]
        return [a.astype(BF16) for a in own] + ([small_pack] if s == 0 else [])

    arrs = [blocks(0)]
    first = gather_start(arrs[0], None, "gather0_start")
    arrs += [blocks(s, first["token"][0, 0]) for s in range(1, N_STAGES)]
    passing = pass_start(gather_wait(first, first["token"], "gather0_wait"), "pass0_start")
    crossing = {1: gather_start(arrs[1], passing["token"], "gather1_start")}
    crossing[2] = gather_start(arrs[2], crossing[1]["token"], "gather2_start")
    small = {n: w[n] for n in SMALL_REPL}
    flight = dict(passing=passing)

    def advance(s, after):
        tokens = []
        if s + 1 < N_STAGES:
            landed = gather_wait(crossing.pop(s + 1), after, f"gather{s + 1}_wait")
            flight["passing"] = pass_start(landed, f"pass{s + 1}_start")
            tokens.append(flight["passing"]["token"])
        if s + 3 < N_STAGES:
            crossing[s + 3] = gather_start(arrs[s + 3], flight["passing"]["token"], f"gather{s + 3}_start")
            tokens.append(crossing[s + 3]["token"])
        return tokens

    h, tape, after = x, [], crossing[2]["token"]
    for s in range(N_STAGES):
        zones = pass_wait(flight["passing"], after, f"pass{s}_wait")
        order = advance(s, zones[0]) if s else []
        mid = functools.partial(advance, 0) if s == 0 else None
        zones = [lax.dynamic_update_slice_in_dim(z, a[None], idx, 0) for z, a in zip(zones, arrs[s])]
        if s == 0:
            per_dev = [_unpack(zones[-1][k], small_shapes) for k in range(N_DEV)]
            for q, n in enumerate(SMALL_SHARDED):
                small[n] = _unshard(jnp.stack([per_dev[k][q] for k in range(N_DEV)]), -1)
        big = {n: _unshard(z, LAYER_AXIS[n]) for (n, _), z in zip(_stage_group(s), zones)}
        p = _stage_params(s, big, small)
        h, saved = _stage_fwd(s, h, p, order, mid)
        tape.append((p, saved))
        after = h

    dh, dh_b, d_final, loss_row = loss_head(h, _row(w["final_norm"]), tgt, "loss_head")

    out = {}
    small_g = {n: [None] * w[n].shape[0] for n in SMALL if n != "final_norm"}

    def finish(s, st, after):
        by_chip, recv = scatter_wait(st, after, f"scatter{s}_wait")
        names = [n for n, _ in _stage_group(s)]
        res = adamw_stage(recv, by_chip, [wv[n] for n in names], [mv[n] for n in names], [vv[n] for n in names],
                          [l for _, l in _stage_group(s)], [out.get(n) for n in names], f"adamw_stage{s}")
        out.update(zip(names, res))

    started, order, summed, to_sibling = [], [], None, None
    for s in reversed(range(N_STAGES)):
        dh, dh_b, g = _stage_bwd(s, dh, dh_b, *tape[s], order)
        order = []
        if to_sibling is not None:
            gsend, from_sibling = sibling_wait(to_sibling, dh, f"sibling{s + 1}_wait")
            by_chip = pair_sum(gsend, from_sibling, f"pair_sum{s + 1}")
            started.append((s + 1, scatter_start(by_chip, f"scatter{s + 1}_start")))
            order.append(started[-1][1]["token"])
            to_sibling = None
        for n, val in g.items():
            if n not in BIG:
                small_g[n][_stage_layer(s)[1]] = val.reshape(small[n].shape[1:])
        if s == 0:
            grads = {n: jnp.stack(vs) for n, vs in small_g.items()}
            grads["final_norm"] = d_final.reshape(D)
            small_full_shapes = [grads[n].shape for n in SMALL] + [(1,)]
            packed = _pack([grads[n] for n in SMALL] + [loss_row[0, :1]], _rows_for(small_full_shapes))
            summed = sum_leading(all_gather([packed], "gather_small_grads")[0], "sum_small_grads")
        gsend = [_to_shards(g[n], LAYER_AXIS[n]) for n, _ in _stage_group(s)]
        if s >= SYNC_SIBLING_STAGES:
            to_sibling = sibling_start(gsend, f"sibling{s}_start")
            order.append(to_sibling["token"])
            continue
        from_sibling = sibling_exchange(gsend, f"scatter{s}_sibling", summed)
        by_chip = pair_sum(gsend, from_sibling, f"pair_sum{s}")
        started.append((s, scatter_start(by_chip, f"scatter{s}_start")))
        order.append(started[-1][1]["token"])
    last = started[-1][1]["token"]
    for s, st in started[:-1]:
        finish(s, st, last)
    parts = _unpack(summed + last[0:1, 0:1], small_full_shapes)
    loss = parts[-1][0]
    mine = []
    for n, g in zip(SMALL, parts[:-1]):
        if n in SMALL_SHARDED:
            s = w[n].shape[-1]
            g = lax.dynamic_slice_in_dim(g, idx * s, s, axis=g.ndim - 1)
        mine.append(g)
    out.update(zip(SMALL, adamw_small(mine, [w[n] for n in SMALL], [m[n] for n in SMALL], [v[n] for n in SMALL],
                                      "adamw_small")))
    done = [out[n][0].reshape(-1)[:1] for n in out]
    finish(*started[-1], functools.reduce(jnp.add, done))
    for n in TRANSPOSED:
        out[n] = [view(n, a) for a in out[n]]
    return loss, dh, out


def kernel(x, ssm_norm, ssm_w_in, ssm_conv_w, ssm_conv_b, ssm_dt_bias, ssm_a_log, ssm_d, ssm_gate_norm, ssm_w_out, cv_norm, cv_w_pw1, cv_b_pw1, cv_dw_w, cv_dw_b, cv_ln_g, cv_ln_b, cv_w_pw2, cv_b_pw2, ffn_norm, ffn_w_gate, ffn_w_up, ffn_w_down, final_norm, loss_target, m_ssm_norm, m_ssm_w_in, m_ssm_conv_w, m_ssm_conv_b, m_ssm_dt_bias, m_ssm_a_log, m_ssm_d, m_ssm_gate_norm, m_ssm_w_out, m_cv_norm, m_cv_w_pw1, m_cv_b_pw1, m_cv_dw_w, m_cv_dw_b, m_cv_ln_g, m_cv_ln_b, m_cv_w_pw2, m_cv_b_pw2, m_ffn_norm, m_ffn_w_gate, m_ffn_w_up, m_ffn_w_down, m_final_norm, v_ssm_norm, v_ssm_w_in, v_ssm_conv_w, v_ssm_conv_b, v_ssm_dt_bias, v_ssm_a_log, v_ssm_d, v_ssm_gate_norm, v_ssm_w_out, v_cv_norm, v_cv_w_pw1, v_cv_b_pw1, v_cv_dw_w, v_cv_dw_b, v_cv_ln_g, v_cv_ln_b, v_cv_w_pw2, v_cv_b_pw2, v_ffn_norm, v_ffn_w_gate, v_ffn_w_up, v_ffn_w_down, v_final_norm):
    args = locals()
    w = {n: args[n] for n in WEIGHTS}
    m = {n: args["m_" + n] for n in WEIGHTS}
    v = {n: args["v_" + n] for n in WEIGHTS}
    loss, grad_x, out = _step(x[0], loss_target[0], w, m, v)
    res = [loss, grad_x[None]]
    for k in range(4):
        res += [out[n][k] for n in WEIGHTS]
    return tuple(res)
```

```python
import functools
import math

import jax
import jax.numpy as jnp
from jax import lax
from jax.experimental import pallas as pl
from jax.experimental.pallas import tpu as pltpu

F32 = jnp.float32
BF16 = jnp.bfloat16

N_DEV = 8
T = 2048
D = 1024
DI = 2048
NH = 32
HD = 64
NG = 4
GW = DI // NG
DS = 128
CONVD = DI + 2 * NG * DS
DINP = 2 * DI + 2 * NG * DS + NH
DINP_PAD = 5376
CH = 128
NCH = T // CH
DFF = 2816
KSSM = 4
KCV = 31
EPS = 1e-5
LANES = 128
VMEM_LIMIT = 56 * 1024 * 1024

ADAM_LR = 0.001
ADAM_B1 = 0.9
ADAM_B2 = 0.999
ADAM_EPS = 1e-08
ADAM_WD = 0.01
ADAM_STEP = 10

MESH = pl.DeviceIdType.MESH
ANY = pl.BlockSpec(memory_space=pl.ANY)


def _pcall(body, **kw):
    return pl.pallas_call(body, **kw)


def _cparams(sem):
    return pltpu.CompilerParams(dimension_semantics=sem, vmem_limit_bytes=VMEM_LIMIT)


def _pick(n, cands):
    for c in cands:
        if n % c == 0:
            return c
    raise ValueError(f"no tile for {n}")


def _sigmoid(x):
    return 1.0 / (1.0 + jnp.exp(-x))


def _silu(x):
    return x * _sigmoid(x)


def _dsilu(x):
    s = _sigmoid(x)
    return s * (1.0 + x * (1.0 - s))


_DIMS = {"nn": (((1,), (0,)), ((), ())), "nt": (((1,), (1,)), ((), ())), "tn": (((0,), (0,)), ((), ()))}


MM_VMEM_BUDGET = 40 * 1024 * 1024
MM_MAX_K = 3072


def _mm_tiles(M, N, K, out_bytes, has_res):
    tk = K if K <= MM_MAX_K else K // 2
    assert K % tk == 0 and tk % LANES == 0
    nk = K // tk
    best = None
    for tm in (2048, 1792, 1408, 1024, 768, 512, 256, 128):
        if M % tm:
            continue
        for tn in (1408, 1024, 768, 512, 384, 256, 128):
            if N % tn:
                continue
            blocks = tm * tk * 2 + tk * tn * 2 + tm * tn * out_bytes + (tm * tn * 4 if has_res else 0)
            vmem = 2 * blocks + tm * tn * 4 * (2 if nk > 1 else 1)
            if vmem > MM_VMEM_BUDGET:
                continue
            traffic = (N // tn if nk > 1 else 1) * M * K + (M // tm) * N * K
            key = (-traffic, tm * tn)
            if best is None or key > best[0]:
                best = (key, tm, tn)
    assert best is not None, (M, N, K)
    return best[1], best[2], tk


def matmul(a, b, mode, *, name, bias=None, residual=None, out_dtype=F32):
    assert a.dtype == BF16 and b.dtype == BF16
    if mode == "nn":
        (M, K), (K2, N) = a.shape, b.shape
    elif mode == "nt":
        (M, K), (N, K2) = a.shape, b.shape
    else:
        (K, M), (K2, N) = a.shape, b.shape
    assert K == K2
    has_bias, has_res = bias is not None, residual is not None
    tm, tn, tk = _mm_tiles(M, N, K, jnp.dtype(out_dtype).itemsize, has_res)
    nk = K // tk
    dims = _DIMS[mode]

    def body(*refs):
        a_ref, b_ref = refs[0], refs[1]
        pos = 2
        bias_ref = res_ref = None
        if has_bias:
            bias_ref = refs[pos]
            pos += 1
        if has_res:
            res_ref = refs[pos]
            pos += 1
        o_ref = refs[pos]

        def finish(out):
            if has_bias:
                out = out + bias_ref[...]
            if has_res:
                out = out + res_ref[...]
            o_ref[...] = out.astype(o_ref.dtype)

        part = lax.dot_general(a_ref[...], b_ref[...], dims, preferred_element_type=F32)
        if nk == 1:
            finish(part)
            return
        acc = refs[pos + 1]
        k = pl.program_id(2)

        @pl.when(k == 0)
        def _():
            acc[...] = part

        @pl.when(jnp.logical_and(k > 0, k < nk - 1))
        def _():
            acc[...] += part

        @pl.when(k == nk - 1)
        def _():
            finish(acc[...] + part)

    if mode == "tn":
        a_spec = pl.BlockSpec((tk, tm), lambda i, j, k: (k, i))
    else:
        a_spec = pl.BlockSpec((tm, tk), lambda i, j, k: (i, k))
    if mode == "nt":
        b_spec = pl.BlockSpec((tn, tk), lambda i, j, k: (j, k))
    else:
        b_spec = pl.BlockSpec((tk, tn), lambda i, j, k: (k, j))
    in_specs, args = [a_spec, b_spec], [a, b]
    if has_bias:
        in_specs.append(pl.BlockSpec((1, tn), lambda i, j, k: (0, j)))
        args.append(bias.reshape(1, N).astype(F32))
    if has_res:
        in_specs.append(pl.BlockSpec((tm, tn), lambda i, j, k: (i, j)))
        args.append(residual)
    return _pcall(
        body, name=name, grid=(M // tm, N // tn, nk), in_specs=in_specs,
        out_specs=pl.BlockSpec((tm, tn), lambda i, j, k: (i, j)),
        out_shape=jax.ShapeDtypeStruct((M, N), out_dtype),
        scratch_shapes=[pltpu.VMEM((tm, tn), F32)] if nk > 1 else [],
        compiler_params=_cparams(("parallel", "parallel", "arbitrary")),
    )(*args)


def rowwise(fn, rows, bcasts, outs, accs=(), *, name, tm=256, fill=None):
    n_rows, n_b, n_o, n_a = len(rows), len(bcasts), len(outs), len(accs)
    n_in = n_rows + n_b + (1 if fill is not None else 0)
    outs = [o if len(o) == 4 else (o[0], o[1], o[0], 0) for o in outs]
    nt = T // tm

    def body(*refs):
        ins = [r[...] for r in refs[:n_rows + n_b]]
        res = fn(*ins)
        o_refs = refs[n_in:n_in + n_o]
        a_refs = refs[n_in + n_o:]
        for r, v in zip(o_refs, res[:n_o]):
            r[...] = v.astype(r.dtype)
        if n_a:
            i = pl.program_id(0)

            @pl.when(i == 0)
            def _():
                for r in a_refs:
                    r[...] = jnp.zeros_like(r)

            for r, v in zip(a_refs, res[n_o:]):
                r[...] += v

    in_specs = [pl.BlockSpec((tm, w), functools.partial(lambda i, cb: (i, cb), cb=cb)) for (_, w, cb) in rows]
    in_specs += [pl.BlockSpec(b.shape, lambda i: (0, 0)) for b in bcasts]
    out_specs = [pl.BlockSpec((tm, w), functools.partial(lambda i, cb: (i, cb), cb=cb)) for (w, _, _, cb) in outs]
    out_specs += [pl.BlockSpec((1, w), lambda i: (0, 0)) for w in accs]
    out_shape = [jax.ShapeDtypeStruct((T, whole), dt) for (_, dt, whole, _) in outs]
    out_shape += [jax.ShapeDtypeStruct((1, w), F32) for w in accs]
    args = [r[0] for r in rows] + list(bcasts)
    aliases = {}
    if fill is not None:
        in_specs.append(ANY)
        args.append(fill[0])
        aliases = {n_in - 1: fill[1]}
    return _pcall(
        body, name=name, grid=(nt,), in_specs=in_specs, out_specs=out_specs, out_shape=out_shape,
        input_output_aliases=aliases, compiler_params=_cparams(("arbitrary",)),
    )(*args)


def _full(a):
    return (a, a.shape[1], 0)


def _rsum(v):
    return jnp.sum(v, axis=0, keepdims=True)


def rms_fwd(h, g, name):
    def fn(x, g):
        r = lax.rsqrt(jnp.mean(x * x, axis=-1, keepdims=True) + EPS)
        return (x * r * g,)
    return rowwise(fn, [_full(h)], [g], [(D, BF16)], name=name)[0]


def rms_bwd(du, h, g, dres, name):
    def fn(du, x, dres, g):
        r = lax.rsqrt(jnp.mean(x * x, axis=-1, keepdims=True) + EPS)
        xh = x * r
        dxh = du * g
        dx = r * (dxh - xh * jnp.mean(dxh * xh, axis=-1, keepdims=True))
        dh = dres + dx
        return dh, dh, _rsum(du * xh)
    return rowwise(fn, [_full(du), _full(h), _full(dres)], [g], [(D, F32), (D, BF16)], [D], name=name)


def loss_head(h, g, tgt, name):
    def fn(x, tgt, g):
        r = lax.rsqrt(jnp.mean(x * x, axis=-1, keepdims=True) + EPS)
        xh = x * r
        err = xh * g - tgt
        lsum = jnp.sum(jnp.sum(err * err, axis=-1, keepdims=True), axis=0, keepdims=True) * (0.5 / D)
        dy = err * (1.0 / D)
        dxh = dy * g
        dx = r * (dxh - xh * jnp.mean(dxh * xh, axis=-1, keepdims=True))
        return dx, dx, _rsum(dy * xh), jnp.broadcast_to(lsum, (1, LANES))
    return rowwise(fn, [_full(h), _full(tgt)], [g], [(D, F32), (D, BF16)], [D, LANES], name=name)


def glu_fwd(hh, name):
    def fn(a, g):
        return (a * _sigmoid(g),)
    return rowwise(fn, [(hh, D, 0), (hh, D, 1)], [], [(D, F32)], name=name)[0]


def glu_bwd(dgl, hh, name):
    def fn(dgl, a, g):
        s = _sigmoid(g)
        dhh = jnp.concatenate([dgl * s, dgl * a * s * (1.0 - s)], axis=1)
        return dhh, _rsum(dhh)
    return rowwise(fn, [_full(dgl), (hh, D, 0), (hh, D, 1)], [], [(2 * D, BF16)], [2 * D], name=name)


def ln_silu_fwd(c2, g, b, name):
    def fn(x, g, b):
        mu = jnp.mean(x, axis=-1, keepdims=True)
        xc = x - mu
        r = lax.rsqrt(jnp.mean(xc * xc, axis=-1, keepdims=True) + EPS)
        return (_silu(xc * r * g + b),)
    return rowwise(fn, [_full(c2)], [g, b], [(D, BF16)], name=name)[0]


def ln_silu_bwd(ds, c2, dh, g, b, name):
    def fn(ds, x, dh, g, b):
        mu = jnp.mean(x, axis=-1, keepdims=True)
        xc = x - mu
        r = lax.rsqrt(jnp.mean(xc * xc, axis=-1, keepdims=True) + EPS)
        xh = xc * r
        dn = ds * _dsilu(xh * g + b)
        dxh = dn * g
        dx = r * (dxh - jnp.mean(dxh, axis=-1, keepdims=True) - xh * jnp.mean(dxh * xh, axis=-1, keepdims=True))
        return dx, _rsum(dn * xh), _rsum(dn), _rsum(dh)
    return rowwise(fn, [_full(ds), _full(c2), _full(dh)], [g, b], [(D, F32)], [D, D, D], name=name)


def gatenorm_fwd(y, zx, gn, name):
    def fn(y, z, gn):
        hg = y * _silu(z)
        parts = []
        for k in range(NG):
            hk = hg[:, k * GW:(k + 1) * GW]
            parts.append(hk * lax.rsqrt(jnp.mean(hk * hk, axis=-1, keepdims=True) + EPS))
        return (jnp.concatenate(parts, axis=1) * gn,)
    return rowwise(fn, [_full(y), (zx, DI, 0)], [gn], [(DI, BF16)], name=name)[0]


def gatenorm_bwd(dyn, y, zx, gn, name):
    def fn(dyn, y, z, gn):
        sz = _silu(z)
        hg = y * sz
        dxh = dyn * gn
        dhg, xhs = [], []
        for k in range(NG):
            sl = slice(k * GW, (k + 1) * GW)
            hk = hg[:, sl]
            r = lax.rsqrt(jnp.mean(hk * hk, axis=-1, keepdims=True) + EPS)
            xh = hk * r
            dk = dxh[:, sl]
            dhg.append(r * (dk - xh * jnp.mean(dk * xh, axis=-1, keepdims=True)))
            xhs.append(xh)
        dhg = jnp.concatenate(dhg, axis=1)
        xh = jnp.concatenate(xhs, axis=1)
        return dhg * sz, dhg * y * _dsilu(z), _rsum(dyn * xh)
    return rowwise(fn, [_full(dyn), _full(y), (zx, DI, 0)], [gn], [(DI, F32), (DI, BF16, DINP_PAD, 0)], [DI], name=name)


def _softplus(x):
    return jnp.maximum(x, 0.0) + jnp.log(1.0 + jnp.exp(-jnp.abs(x)))


def _spread(v, e):
    hi = v.astype(BF16)
    r = v - hi.astype(F32)
    mid = r.astype(BF16)
    lo = (r - mid.astype(F32)).astype(BF16)
    return _dot(hi, e) + _dot(mid, e) + _dot(lo, e)


def _spread2(v, e):
    hi = v.astype(BF16)
    lo = (v - hi.astype(F32)).astype(BF16)
    return _dot(hi, e) + _dot(lo, e)


def dt_fwd(zx, dt_bias, a_log, name):
    heads = (jnp.arange(DI)[None, :] // HD == jnp.arange(LANES)[:, None]).astype(BF16)

    def fn(raw, bias, a_log, e):
        dt = _softplus(raw + bias)
        da = dt * (-jnp.exp(a_log))
        return dt, da, _spread(dt, e), _spread(da, e)

    return rowwise(fn, [(zx, LANES, (2 * DI + 2 * NG * DS) // LANES)], [dt_bias, a_log, heads],
                   [(LANES, F32), (LANES, F32), (DI, F32), (DI, F32)], name=name)


def dt_bwd(ddt, dda, dt, zx, dt_bias, a_log, dzx, name):
    def fn(ddt, dda, dt, raw, bias, a_log):
        a = -jnp.exp(a_log)
        draw = (ddt + dda * a) * _sigmoid(raw + bias)
        return jnp.concatenate([draw, jnp.zeros_like(draw)], axis=1), _rsum(draw), _rsum(dda * dt) * a
    return rowwise(fn, [_full(ddt), _full(dda), _full(dt), (zx, LANES, (2 * DI + 2 * NG * DS) // LANES)],
                   [dt_bias, a_log], [(2 * LANES, BF16, DINP_PAD, DINP_PAD // (2 * LANES) - 1)], [LANES, LANES],
                   name=name, fill=(dzx, 0))


def headsum(v, name):
    def body(v_ref, o_ref):
        o_ref[...] = jnp.sum(v_ref[...], axis=1, keepdims=True)
    return _pcall(body, name=name, out_shape=jax.ShapeDtypeStruct((v.shape[0], 1), F32))(v)


CONV_ROWS = 256


def _shifted(win, o, rows):
    if o == 0:
        return win[0:rows]
    n = win.shape[0]
    return pltpu.roll(win, shift=n - o, axis=0)[0:rows]


def dwconv_fwd(x, x_cb0, w, b, K, ct, act, name):
    C = w.shape[1]
    pad = 8 if K <= 8 else 32
    KP = w.shape[0]
    n_out = 2 if act else 1

    def body(x_ref, w_ref, b_ref, *rest):
        o_refs, px = rest[:n_out], rest[n_out]
        px[0:pad, :] = jnp.zeros((pad, ct), F32)
        px[pad:pad + T, :] = x_ref[...]
        wv = w_ref[...]
        bv = b_ref[...]
        for r0 in range(0, T, CONV_ROWS):
            win = px[r0:r0 + CONV_ROWS + pad, :]
            acc = jnp.broadcast_to(bv, (CONV_ROWS, ct))
            for k in range(K):
                acc = acc + wv[k:k + 1, :] * _shifted(win, pad - (K - 1) + k, CONV_ROWS)
            o_refs[0][r0:r0 + CONV_ROWS, :] = acc
            if act:
                o_refs[1][r0:r0 + CONV_ROWS, :] = _silu(acc)

    return _pcall(
        body, name=name, grid=(C // ct,),
        in_specs=[pl.BlockSpec((T, ct), lambda j: (0, x_cb0 + j)), pl.BlockSpec((KP, ct), lambda j: (0, j)),
                  pl.BlockSpec((1, ct), lambda j: (0, j))],
        out_specs=[pl.BlockSpec((T, ct), lambda j: (0, j))] * n_out,
        out_shape=[jax.ShapeDtypeStruct((T, C), F32)] * n_out,
        scratch_shapes=[pltpu.VMEM((T + pad, ct), F32)],
        compiler_params=_cparams(("parallel",)),
    )(x, w, b)


def dwconv_bwd(dout, cpre, x, x_cb0, w, K, ct, act, out_dtype, name, into=None):
    C = w.shape[1]
    pad = 8 if K <= 8 else 32
    KP = w.shape[0]

    def body(*refs):
        dx_ref, dw_ref, db_ref, px, pd = refs[-5:]
        if act:
            d_ref, c_ref, x_ref, w_ref = refs[:4]
        else:
            d_ref, x_ref, w_ref = refs[:3]
        px[0:pad, :] = jnp.zeros((pad, ct), F32)
        px[pad:pad + T, :] = x_ref[...]
        pd[T:T + pad, :] = jnp.zeros((pad, ct), F32)
        if act:
            pd[0:T, :] = d_ref[...] * _dsilu(c_ref[...])
        else:
            pd[0:T, :] = d_ref[...]
        wv = w_ref[...]
        dws = [jnp.zeros((1, ct), F32) for _ in range(K)]
        db = jnp.zeros((1, ct), F32)
        for r0 in range(0, T, CONV_ROWS):
            dwin = pd[r0:r0 + CONV_ROWS + pad, :]
            xwin = px[r0:r0 + CONV_ROWS + pad, :]
            dc = dwin[0:CONV_ROWS]
            db = db + _rsum(dc)
            acc = jnp.zeros((CONV_ROWS, ct), F32)
            for k in range(K):
                acc = acc + wv[k:k + 1, :] * _shifted(dwin, K - 1 - k, CONV_ROWS)
                dws[k] = dws[k] + _rsum(dc * _shifted(xwin, pad - (K - 1) + k, CONV_ROWS))
            dx_ref[r0:r0 + CONV_ROWS, :] = acc.astype(dx_ref.dtype)
        dw_ref[...] = jnp.zeros((KP, ct), F32)
        for k in range(K):
            dw_ref[k:k + 1, :] = dws[k]
        db_ref[...] = db

    col = pl.BlockSpec((T, ct), lambda j: (0, j))
    in_specs = [col] + ([col] if act else []) + [pl.BlockSpec((T, ct), lambda j: (0, x_cb0 + j)),
                                                 pl.BlockSpec((KP, ct), lambda j: (0, j))]
    args = [dout] + ([cpre] if act else []) + [x, w]
    dx_spec, dx_shape, aliases = col, jax.ShapeDtypeStruct((T, C), out_dtype), {}
    if into is not None:
        dx_spec = pl.BlockSpec((T, ct), lambda j: (0, into[1] + j))
        dx_shape = jax.ShapeDtypeStruct(into[0].shape, into[0].dtype)
        aliases = {len(args): 0}
        in_specs.append(ANY)
        args.append(into[0])
    return _pcall(
        body, name=name, grid=(C // ct,), in_specs=in_specs,
        out_specs=[dx_spec, pl.BlockSpec((KP, ct), lambda j: (0, j)), pl.BlockSpec((1, ct), lambda j: (0, j))],
        out_shape=[dx_shape, jax.ShapeDtypeStruct((KP, C), F32), jax.ShapeDtypeStruct((1, C), F32)],
        input_output_aliases=aliases,
        scratch_shapes=[pltpu.VMEM((T + pad, ct), F32), pltpu.VMEM((T + pad, ct), F32)],
        compiler_params=_cparams(("parallel",)),
    )(*args)


def _scan(a, axis, reverse=False):
    n = a.shape[axis]
    idx = lax.broadcasted_iota(jnp.int32, a.shape, axis)
    s = 1
    while s < n:
        if reverse:
            a = a + jnp.where(idx < n - s, pltpu.roll(a, shift=n - s, axis=axis), 0.0)
        else:
            a = a + jnp.where(idx >= s, pltpu.roll(a, shift=s, axis=axis), 0.0)
        s *= 2
    return a


_NT = _DIMS["nt"]
_TN = _DIMS["tn"]


def _dot(a, b, dims=_DIMS["nn"]):
    return lax.dot_general(a, b, dims, preferred_element_type=F32)


def ssd_fwd(xbc, dtx, dax, daT, dfull, name):
    def body(xbc_ref, dtx_ref, dax_ref, daT_ref, df_ref, y_ref, st_ref, S):
        ci = pl.program_id(0)

        @pl.when(ci == 0)
        def _():
            S[...] = jnp.zeros_like(S)

        row = lax.broadcasted_iota(jnp.int32, (CH, CH), 0)
        lane = lax.broadcasted_iota(jnp.int32, (CH, CH), 1)
        acsT = _scan(daT_ref[...], 1)
        for g in range(NG):
            c0 = g * GW
            xs = xbc_ref[:, c0:c0 + GW]
            acs = _scan(dax_ref[:, c0:c0 + GW], 0)
            Bm = xbc_ref[:, DI + g * DS:DI + (g + 1) * DS].astype(BF16)
            Cm = xbc_ref[:, DI + NG * DS + g * DS:DI + NG * DS + (g + 1) * DS].astype(BF16)
            xdt = xs * dtx_ref[:, c0:c0 + GW]
            atot = acs[CH - 1:CH, :]
            Sg = S[:, c0:c0 + GW]
            st_ref[:, c0:c0 + GW] = Sg
            CB = _dot(Cm, Bm, _NT)
            yg = jnp.exp(acs) * _dot(Cm, Sg.astype(BF16)) + xs * df_ref[:, c0:c0 + GW]
            xd = (xdt * jnp.exp(atot - acs)).astype(BF16)
            S[:, c0:c0 + GW] = jnp.exp(atot) * Sg + _dot(Bm, xd, _TN)
            xdt_b = xdt.astype(BF16)
            for r in range(NH // NG):
                h = g * (NH // NG) + r
                hs = slice(r * HD, (r + 1) * HD)
                seg = acs[:, r * HD:r * HD + 1] - acsT[h:h + 1, :]
                Lm = jnp.where(row >= lane, jnp.exp(jnp.minimum(seg, 0.0)), 0.0)
                yd = _dot((CB * Lm).astype(BF16), xdt_b[:, hs])
                y_ref[:, c0 + r * HD:c0 + (r + 1) * HD] = yg[:, hs] + yd

    return _pcall(
        body, name=name, grid=(NCH,),
        in_specs=[pl.BlockSpec((CH, CONVD), lambda i: (i, 0)), pl.BlockSpec((CH, DI), lambda i: (i, 0)),
                  pl.BlockSpec((CH, DI), lambda i: (i, 0)), pl.BlockSpec((NH, CH), lambda i: (0, i)),
                  pl.BlockSpec((1, DI), lambda i: (0, 0))],
        out_specs=[pl.BlockSpec((CH, DI), lambda i: (i, 0)), pl.BlockSpec((None, DS, DI), lambda i: (i, 0, 0))],
        out_shape=[jax.ShapeDtypeStruct((T, DI), F32), jax.ShapeDtypeStruct((NCH, DS, DI), F32)],
        scratch_shapes=[pltpu.VMEM((DS, DI), F32)],
        compiler_params=_cparams(("arbitrary",)),
    )(xbc, dtx, dax, daT, dfull)


def ssd_bwd(dy, xbc, dtx, dax, daT, dfull, states, name):
    hsum = (jnp.arange(DI)[:, None] // HD == jnp.arange(LANES)[None, :]).astype(BF16).reshape(NG, GW, LANES)

    def body(dy_ref, xbc_ref, dtx_ref, dax_ref, daT_ref, df_ref, st_ref, hsum_ref, dxbc_ref, ddt_ref, dda_ref, dD_ref, dS):
        i = pl.program_id(0)

        @pl.when(i == 0)
        def _():
            dS[...] = jnp.zeros_like(dS)
            dD_ref[...] = jnp.zeros_like(dD_ref)

        row = lax.broadcasted_iota(jnp.int32, (CH, CH), 0)
        lane = lax.broadcasted_iota(jnp.int32, (CH, CH), 1)
        acsT = _scan(daT_ref[...], 1)
        ddt_all = jnp.zeros((CH, LANES), F32)
        dacs_all = jnp.zeros((CH, LANES), F32)
        colacc = jnp.zeros((CH, CH), F32)
        for g in range(NG):
            c0 = g * GW
            xs = xbc_ref[:, c0:c0 + GW]
            dtx = dtx_ref[:, c0:c0 + GW]
            acs = _scan(dax_ref[:, c0:c0 + GW], 0)
            Bm = xbc_ref[:, DI + g * DS:DI + (g + 1) * DS].astype(BF16)
            Cm = xbc_ref[:, DI + NG * DS + g * DS:DI + NG * DS + (g + 1) * DS].astype(BF16)
            xdt = xs * dtx
            atot = acs[CH - 1:CH, :]
            Sin = st_ref[:, c0:c0 + GW]
            dyg = dy_ref[:, c0:c0 + GW]
            dSo = dS[:, c0:c0 + GW]
            E = jnp.exp(acs)
            Etot = jnp.exp(atot)
            dec = jnp.exp(atot - acs)
            dD_ref[:, c0:c0 + GW] += _rsum(dyg * xs)
            dxs = dyg * df_ref[:, c0:c0 + GW]
            Sin_b = Sin.astype(BF16)
            dSo_b = dSo.astype(BF16)
            dY0 = dyg * E
            dY0_b = dY0.astype(BF16)
            dC = _dot(dY0_b, Sin_b, _NT)
            dS[:, c0:c0 + GW] = _dot(Cm, dY0_b, _TN) + Etot * dSo
            XD = xdt * dec
            dXD = _dot(Bm, dSo_b)
            dB = _dot(XD.astype(BF16), dSo_b, _NT)
            dxdt = dXD * dec
            Gq = dXD * XD
            dacs_x = dY0 * _dot(Cm, Sin_b) - Gq
            datot_x = _rsum(dSo * Sin) * Etot + _rsum(Gq)
            dacs_all = dacs_all + _spread2(dacs_x, hsum_ref[g])
            dtot8 = _spread2(jnp.broadcast_to(datot_x, (8, GW)), hsum_ref[g])
            dacs_all = dacs_all + jnp.where(row == CH - 1, jnp.broadcast_to(dtot8[0:1, :], (CH, LANES)), 0.0)
            CB = _dot(Cm, Bm, _NT)
            dCB = jnp.zeros((CH, CH), F32)
            xdt_b = xdt.astype(BF16)
            dy_b = dyg.astype(BF16)
            for r in range(NH // NG):
                h = g * (NH // NG) + r
                hs = slice(r * HD, (r + 1) * HD)
                seg = acs[:, r * HD:r * HD + 1] - acsT[h:h + 1, :]
                Lm = jnp.where(row >= lane, jnp.exp(jnp.minimum(seg, 0.0)), 0.0)
                dyr = dy_b[:, hs]
                dML = _dot(dyr, xdt_b[:, hs], _NT) * Lm
                dxbc_ref[:, c0 + r * HD:c0 + (r + 1) * HD] = _dot((CB * Lm).astype(BF16), dyr, _TN)
                dCB = dCB + dML
                dseg = dML * CB
                dacs_all = dacs_all + _spread2(dseg, (lane == h).astype(BF16))
                colacc = colacc + jnp.where(row == h, jnp.sum(dseg, axis=0, keepdims=True), 0.0)
            dxdt = dxdt + dxbc_ref[:, c0:c0 + GW]
            ddt_all = ddt_all + _spread2(dxdt * xs, hsum_ref[g])
            dxbc_ref[:, c0:c0 + GW] = dxs + dxdt * dtx
            dCB_b = dCB.astype(BF16)
            dxbc_ref[:, DI + g * DS:DI + (g + 1) * DS] = dB + _dot(dCB_b, Cm, _TN)
            dxbc_ref[:, DI + NG * DS + g * DS:DI + NG * DS + (g + 1) * DS] = dC + _dot(dCB_b, Bm)
        ddt_ref[...] = ddt_all
        dda_ref[...] = _scan(dacs_all - colacc.T, 0, reverse=True)

    last = NCH - 1
    return _pcall(
        body, name=name, grid=(NCH,),
        in_specs=[pl.BlockSpec((CH, DI), lambda i: (last - i, 0)), pl.BlockSpec((CH, CONVD), lambda i: (last - i, 0)),
                  pl.BlockSpec((CH, DI), lambda i: (last - i, 0)), pl.BlockSpec((CH, DI), lambda i: (last - i, 0)),
                  pl.BlockSpec((NH, CH), lambda i: (0, last - i)), pl.BlockSpec((1, DI), lambda i: (0, 0)),
                  pl.BlockSpec((None, DS, DI), lambda i: (last - i, 0, 0)),
                  pl.BlockSpec((NG, GW, LANES), lambda i: (0, 0, 0))],
        out_specs=[pl.BlockSpec((CH, CONVD), lambda i: (last - i, 0)), pl.BlockSpec((CH, LANES), lambda i: (last - i, 0)),
                   pl.BlockSpec((CH, LANES), lambda i: (last - i, 0)), pl.BlockSpec((1, DI), lambda i: (0, 0))],
        out_shape=[jax.ShapeDtypeStruct((T, CONVD), F32), jax.ShapeDtypeStruct((T, LANES), F32),
                   jax.ShapeDtypeStruct((T, LANES), F32), jax.ShapeDtypeStruct((1, DI), F32)],
        scratch_shapes=[pltpu.VMEM((DS, DI), F32)],
        compiler_params=_cparams(("arbitrary",)),
    )(dy, xbc, dtx, dax, daT, dfull, states, hsum)


def _as3d(shape):
    if len(shape) == 1:
        return (1, 1, shape[0])
    if len(shape) == 2:
        return (1, shape[0], shape[1])
    return (math.prod(shape[:-2]), shape[-2], shape[-1])


def adamw_small(gs, ws, ms, vs, name):
    n = len(ws)
    bc1 = 1.0 - ADAM_B1 ** ADAM_STEP
    bc2 = 1.0 - ADAM_B2 ** ADAM_STEP

    def body(*refs):
        for a in range(n):
            g_ref, w_ref, m_ref, v_ref = refs[4 * a:4 * a + 4]
            g_out, d_out, m_out, v_out = refs[4 * n + 4 * a:4 * n + 4 * a + 4]
            g = g_ref[...]
            mn = ADAM_B1 * m_ref[...] + (1.0 - ADAM_B1) * g
            vn = ADAM_B2 * v_ref[...] + (1.0 - ADAM_B2) * (g * g)
            g_out[...] = g
            m_out[...] = mn
            v_out[...] = vn
            d_out[...] = -ADAM_LR * ((mn / bc1) / (jnp.sqrt(vn / bc2) + ADAM_EPS) + ADAM_WD * w_ref[...])

    args, out_shape = [], []
    for g, w, m, v in zip(gs, ws, ms, vs):
        s3 = _as3d(w.shape)
        args += [t.reshape(s3) for t in (g, w, m, v)]
        out_shape += [jax.ShapeDtypeStruct(s3, F32)] * 4
    outs = _pcall(body, name=name, out_shape=out_shape)(*args)
    return [[o.reshape(w.shape) for o in outs[4 * a:4 * a + 4]] for a, w in enumerate(ws)]


ADAMW_STEPS = 4


def adamw_stage(recvs, owns, ws, ms, vs, layers, prevs, name):
    n = len(ws)
    chained = prevs[0] is not None
    assert all((p is not None) == chained for p in prevs)
    bc1 = 1.0 - ADAM_B1 ** ADAM_STEP
    bc2 = 1.0 - ADAM_B2 ** ADAM_STEP
    n_in = (9 if chained else 5) * n

    def body(*refs):
        for a in range(n):
            r_ref, o_ref, w_ref, m_ref, v_ref = refs[5 * a:5 * a + 5]
            g_out, d_out, m_out, v_out = refs[n_in + 4 * a:n_in + 4 * a + 4]
            g = o_ref[...].astype(F32)
            for k in range(r_ref.shape[0]):
                g = g + r_ref[k].astype(F32)
            mn = ADAM_B1 * m_ref[...] + (1.0 - ADAM_B1) * g
            vn = ADAM_B2 * v_ref[...] + (1.0 - ADAM_B2) * (g * g)
            g_out[...] = g
            m_out[...] = mn
            v_out[...] = vn
            d_out[...] = -ADAM_LR * ((mn / bc1) / (jnp.sqrt(vn / bc2) + ADAM_EPS) + ADAM_WD * w_ref[...])

    in_specs, args, out_specs, out_shape = [], [], [], []
    for a in range(n):
        _, R, C = ws[a].shape
        tr = R // ADAMW_STEPS
        assert tr * ADAMW_STEPS == R and tr % 8 == 0
        slot = pl.BlockSpec((None, tr, C), functools.partial(lambda r, l: (l, r, 0), l=layers[a]))
        own = pl.BlockSpec((None, tr, C), lambda r: (2 * lax.axis_index("x") + lax.axis_index("y"), r, 0))
        in_specs += [pl.BlockSpec((recvs[a].shape[0], tr, C), lambda r: (0, r, 0)), own, slot, slot, slot]
        args += [recvs[a], owns[a], ws[a], ms[a], vs[a]]
        out_specs += [slot] * 4
        out_shape += [jax.ShapeDtypeStruct(ws[a].shape, F32)] * 4
    aliases = {}
    if chained:
        for a in range(n):
            in_specs += [ANY] * 4
            args += list(prevs[a])
            aliases.update({5 * n + 4 * a + k: 4 * a + k for k in range(4)})
    outs = _pcall(
        body, name=name, grid=(ADAMW_STEPS,), in_specs=in_specs, out_specs=out_specs, out_shape=out_shape,
        input_output_aliases=aliases, compiler_params=_cparams(("parallel",)),
    )(*args)
    return [list(outs[4 * a:4 * a + 4]) for a in range(n)]


def sum_leading(parts, name):
    P, R, C = parts.shape

    def body(p_ref, o_ref):
        s = p_ref[0]
        for k in range(1, P):
            s = s + p_ref[k]
        o_ref[...] = s

    return _pcall(body, name=name, out_shape=jax.ShapeDtypeStruct((R, C), F32))(parts)


def pair_sum(gsends, recvs, name):
    n = len(gsends)

    def body(*refs):
        for g_ref, r_ref, o_ref in zip(refs[:n], refs[n:2 * n], refs[2 * n:]):
            o_ref[...] = (g_ref[...].astype(F32) + r_ref[...].astype(F32)).astype(o_ref.dtype)

    def slot(a):
        return pl.BlockSpec((None,) + a.shape[1:], lambda q: (q, 0, 0))

    def own(a):
        return pl.BlockSpec((None,) + a.shape[1:], lambda q: (2 * q + lax.axis_index("c"), 0, 0))

    return _pcall(
        body, name=name, grid=(4,), in_specs=[own(g) for g in gsends] + [slot(r) for r in recvs],
        out_specs=[slot(r) for r in recvs], out_shape=[jax.ShapeDtypeStruct(r.shape, BF16) for r in recvs],
        compiler_params=_cparams(("parallel",)),
    )(*gsends, *recvs)


def _place():
    return lax.axis_index("x"), lax.axis_index("y"), lax.axis_index("c")


def _other_chips(x, y):
    return [(1 - x, y), (x, 1 - y), (1 - x, 1 - y)]


def all_gather(arrs, name):
    n = len(arrs)

    def body(*refs):
        ins, outs = refs[:n], refs[n:2 * n]
        send_sems, recv_sems, local_sems = refs[2 * n:]
        x, y, c = _place()
        me, sibling = (x, y, c), (x, y, 1 - c)
        chips = _other_chips(x, y)

        def slot(a, px, py, pc):
            return outs[a].at[4 * px + 2 * py + pc]

        def copy(a, k, block, to, src=None):
            return pltpu.make_async_remote_copy(
                src_ref=slot(a, *block) if src is None else src, dst_ref=slot(a, *block),
                send_sem=send_sems.at[a, k], recv_sem=recv_sems.at[a, k], device_id=to, device_id_type=MESH)

        mine, first, passed = [], [], []
        for a in range(n):
            cp = pltpu.make_async_copy(ins[a], slot(a, *me), local_sems.at[a])
            cp.start()
            mine.append(cp)
            first.append(copy(a, 0, me, sibling, src=ins[a]))
            first += [copy(a, 1 + j, me, (*chip, c), src=ins[a]) for j, chip in enumerate(chips)]
        for cp in first:
            cp.start()
        for j, chip in enumerate(chips):
            for a in range(n):
                copy(a, 1 + j, (*chip, c), me).wait_recv()
                cp = copy(a, 4 + j, (*chip, c), sibling)
                cp.start()
                passed.append(cp)
        for a in range(n):
            copy(a, 0, sibling, me).wait_recv()
            for j, chip in enumerate(chips):
                copy(a, 4 + j, (*chip, 1 - c), me).wait_recv()
        for cp in first + passed:
            cp.wait_send()
        for cp in mine:
            cp.wait()

    return _pcall(
        body, name=name, in_specs=[ANY] * n, out_specs=[ANY] * n,
        out_shape=[jax.ShapeDtypeStruct((N_DEV,) + a.shape, a.dtype) for a in arrs],
        scratch_shapes=[pltpu.SemaphoreType.DMA((n, 7)), pltpu.SemaphoreType.DMA((n, 7)), pltpu.SemaphoreType.DMA((n,))],
    )(*arrs)


def sibling_exchange(gsends, name, after=None):
    n = len(gsends)
    n_in = n + (1 if after is not None else 0)

    def body(*refs):
        ins, outs = refs[:n], refs[n_in:n_in + n]
        send_sems, recv_sems = refs[n_in + n:]
        x, y, c = _place()
        copies = []
        for a in range(n):
            for q in range(4):
                cp = pltpu.make_async_remote_copy(
                    src_ref=ins[a].at[2 * q + 1 - c], dst_ref=outs[a].at[q],
                    send_sem=send_sems.at[a, q], recv_sem=recv_sems.at[a, q],
                    device_id=(x, y, 1 - c), device_id_type=MESH)
                cp.start()
                copies.append(cp)
        for cp in copies:
            cp.wait()

    return _pcall(
        body, name=name, in_specs=[ANY] * n_in, out_specs=[ANY] * n,
        out_shape=[jax.ShapeDtypeStruct((4,) + g.shape[1:], g.dtype) for g in gsends],
        scratch_shapes=[pltpu.SemaphoreType.DMA((n, 4)), pltpu.SemaphoreType.DMA((n, 4))],
    )(*gsends, *([after] if after is not None else []))


HBM =pl.BlockSpec(memory_space=pltpu.HBM)
SEM = pl.BlockSpec(memory_space=pltpu.SEMAPHORE)
EFFECT = pltpu.SideEffectType.DATAFLOW_SIDE_EFFECTING


def _in_hbm(a):
    return pltpu.with_memory_space_constraint(a, pltpu.HBM)


def _gather_peers(x, y, c):
    to = [(x, y, 1 - c)] + [(px, py, c) for px, py in _other_chips(x, y)]
    return to, [4 * px + 2 * py + pc for px, py, pc in to]


def gather_start(arrs, after, name):
    n = len(arrs)
    n_in = 2 * n + (1 if after is not None else 0)

    def body(*refs):
        srcs, lands = refs[:n], refs[n:2 * n]
        send_sems, recv_sems = refs[n_in], refs[n_in + 1]
        token = refs[-1]
        x, y, c = _place()
        to, _ = _gather_peers(x, y, c)
        me = 4 * x + 2 * y + c
        for a in range(n):
            for k, dev in enumerate(to):
                pltpu.make_async_remote_copy(
                    src_ref=srcs[a], dst_ref=lands[a].at[me], send_sem=send_sems.at[4 * a + k], recv_sem=recv_sems.at[4 * a + k],
                    device_id=dev, device_id_type=MESH).start()
        token[...] = jnp.zeros_like(token)

    zones = [lax.empty((N_DEV,) + a.shape, a.dtype) for a in arrs]
    args = [_in_hbm(a) for a in arrs] + [_in_hbm(z) for z in zones] + ([after] if after is not None else [])
    outs = _pcall(
        body, name=name,
        out_shape=(pltpu.SemaphoreType.DMA((4 * n,)), pltpu.SemaphoreType.DMA((4 * n,)),
                   *[pltpu.HBM(a.shape, a.dtype) for a in arrs], *[pltpu.HBM(z.shape, z.dtype) for z in zones],
                   jax.ShapeDtypeStruct((8, LANES), F32)),
        in_specs=[HBM] * (2 * n) + ([ANY] if after is not None else []),
        out_specs=(SEM, SEM, *[HBM] * (2 * n), pl.BlockSpec(memory_space=pltpu.VMEM)),
        input_output_aliases={i: 2 + i for i in range(2 * n)},
        compiler_params=pltpu.CompilerParams(has_side_effects=EFFECT),
    )(*args)
    return dict(send=outs[0], recv=outs[1], srcs=list(outs[2:2 + n]), lands=list(outs[2 + n:2 + 2 * n]), token=outs[-1])


def gather_wait(st, after, name):
    n = len(st["srcs"])

    def body(*refs):
        srcs, lands = refs[:n], refs[n:2 * n]
        send_sems, recv_sems = refs[2 * n], refs[2 * n + 1]
        x, y, c = _place()
        to, slots = _gather_peers(x, y, c)
        for a in range(n):
            for k, dev in enumerate(to):
                cp = pltpu.make_async_remote_copy(
                    src_ref=srcs[a], dst_ref=lands[a].at[slots[k]], send_sem=send_sems.at[4 * a + k],
                    recv_sem=recv_sems.at[4 * a + k], device_id=dev, device_id_type=MESH)
                cp.wait_send()
                cp.wait_recv()

    outs = _pcall(
        body, name=name,
        out_shape=(*[pltpu.HBM(a.shape, a.dtype) for a in st["srcs"]], *[pltpu.HBM(z.shape, z.dtype) for z in st["lands"]]),
        in_specs=[HBM] * (2 * n) + [SEM, SEM, ANY], out_specs=tuple([HBM] * (2 * n)),
        input_output_aliases={i: i for i in range(2 * n)},
        compiler_params=pltpu.CompilerParams(has_side_effects=EFFECT),
    )(*st["srcs"], *st["lands"], st["send"], st["recv"], after)
    return list(outs[n:])


def pass_start(zones, name):
    n = len(zones)

    def body(*refs):
        zs = refs[:n]
        send_sems, recv_sems = refs[n], refs[n + 1]
        token = refs[-1]
        x, y, c = _place()
        for a in range(n):
            for j, (px, py) in enumerate(_other_chips(x, y)):
                blk = zs[a].at[4 * px + 2 * py + c]
                pltpu.make_async_remote_copy(
                    src_ref=blk, dst_ref=blk, send_sem=send_sems.at[3 * a + j], recv_sem=recv_sems.at[3 * a + j],
                    device_id=(x, y, 1 - c), device_id_type=MESH).start()
        token[...] = jnp.zeros_like(token)

    outs = _pcall(
        body, name=name,
        out_shape=(pltpu.SemaphoreType.DMA((3 * n,)), pltpu.SemaphoreType.DMA((3 * n,)),
                   *[pltpu.HBM(z.shape, z.dtype) for z in zones], jax.ShapeDtypeStruct((8, LANES), F32)),
        in_specs=[HBM] * n, out_specs=(SEM, SEM, *[HBM] * n, pl.BlockSpec(memory_space=pltpu.VMEM)),
        input_output_aliases={i: 2 + i for i in range(n)},
        compiler_params=pltpu.CompilerParams(has_side_effects=EFFECT),
    )(*zones)
    return dict(send=outs[0], recv=outs[1], zones=list(outs[2:2 + n]), token=outs[-1])


def pass_wait(st, after, name):
    n = len(st["zones"])

    def body(*refs):
        zs = refs[:n]
        send_sems, recv_sems = refs[n], refs[n + 1]
        x, y, c = _place()
        for a in range(n):
            for j, (px, py) in enumerate(_other_chips(x, y)):
                cp = pltpu.make_async_remote_copy(
                    src_ref=zs[a].at[4 * px + 2 * py + c], dst_ref=zs[a].at[4 * px + 2 * py + 1 - c],
                    send_sem=send_sems.at[3 * a + j], recv_sem=recv_sems.at[3 * a + j],
                    device_id=(x, y, 1 - c), device_id_type=MESH)
                cp.wait_send()
                cp.wait_recv()

    outs = _pcall(
        body, name=name, out_shape=tuple(pltpu.HBM(z.shape, z.dtype) for z in st["zones"]),
        in_specs=[HBM] * n + [SEM, SEM, ANY], out_specs=tuple([HBM] * n),
        input_output_aliases={i: i for i in range(n)},
        compiler_params=pltpu.CompilerParams(has_side_effects=EFFECT),
    )(*st["zones"], st["send"], st["recv"], after)
    return list(outs)


def sibling_start(gsends, name):
    n = len(gsends)

    def body(*refs):
        srcs, lands = refs[:n], refs[n:2 * n]
        send_sems, recv_sems = refs[2 * n], refs[2 * n + 1]
        token = refs[-1]
        x, y, c = _place()
        for a in range(n):
            for q in range(4):
                pltpu.make_async_remote_copy(
                    src_ref=srcs[a].at[2 * q + 1 - c], dst_ref=lands[a].at[q], send_sem=send_sems.at[4 * a + q],
                    recv_sem=recv_sems.at[4 * a + q], device_id=(x, y, 1 - c), device_id_type=MESH).start()
        token[...] = jnp.zeros_like(token)

    zones = [lax.empty((4,) + g.shape[1:], g.dtype) for g in gsends]
    outs = _pcall(
        body, name=name,
        out_shape=(pltpu.SemaphoreType.DMA((4 * n,)), pltpu.SemaphoreType.DMA((4 * n,)),
                   *[pltpu.HBM(g.shape, g.dtype) for g in gsends], *[pltpu.HBM(z.shape, z.dtype) for z in zones],
                   jax.ShapeDtypeStruct((8, LANES), F32)),
        in_specs=[HBM] * (2 * n), out_specs=(SEM, SEM, *[HBM] * (2 * n), pl.BlockSpec(memory_space=pltpu.VMEM)),
        input_output_aliases={i: 2 + i for i in range(2 * n)},
        compiler_params=pltpu.CompilerParams(has_side_effects=EFFECT),
    )(*[_in_hbm(g) for g in gsends], *[_in_hbm(z) for z in zones])
    return dict(send=outs[0], recv=outs[1], srcs=list(outs[2:2 + n]), lands=list(outs[2 + n:2 + 2 * n]), token=outs[-1])


def sibling_wait(st, after, name):
    n = len(st["srcs"])

    def body(*refs):
        srcs, lands = refs[:n], refs[n:2 * n]
        send_sems, recv_sems = refs[2 * n], refs[2 * n + 1]
        x, y, c = _place()
        for a in range(n):
            for q in range(4):
                cp = pltpu.make_async_remote_copy(
                    src_ref=srcs[a].at[2 * q + 1 - c], dst_ref=lands[a].at[q], send_sem=send_sems.at[4 * a + q],
                    recv_sem=recv_sems.at[4 * a + q], device_id=(x, y, 1 - c), device_id_type=MESH)
                cp.wait_send()
                cp.wait_recv()

    outs = _pcall(
        body, name=name,
        out_shape=(*[pltpu.HBM(a.shape, a.dtype) for a in st["srcs"]], *[pltpu.HBM(z.shape, z.dtype) for z in st["lands"]]),
        in_specs=[HBM] * (2 * n) + [SEM, SEM, ANY], out_specs=tuple([HBM] * (2 * n)),
        input_output_aliases={i: i for i in range(2 * n)},
        compiler_params=pltpu.CompilerParams(has_side_effects=EFFECT),
    )(*st["srcs"], *st["lands"], st["send"], st["recv"], after)
    return list(outs[:n]), list(outs[n:])


def scatter_start(parts, name):
    n = len(parts)

    def body(*refs):
        srcs, lands = refs[:n], refs[n:2 * n]
        send_sems, recv_sems = refs[2 * n], refs[2 * n + 1]
        token = refs[-1]
        x, y, c = _place()
        for a in range(n):
            for j, (px, py) in enumerate(_other_chips(x, y)):
                pltpu.make_async_remote_copy(
                    src_ref=srcs[a].at[2 * px + py], dst_ref=lands[a].at[j], send_sem=send_sems.at[3 * a + j],
                    recv_sem=recv_sems.at[3 * a + j], device_id=(px, py, c), device_id_type=MESH).start()
        token[...] = jnp.zeros_like(token)

    zones = [lax.empty((3,) + p.shape[1:], p.dtype) for p in parts]
    outs = _pcall(
        body, name=name,
        out_shape=(pltpu.SemaphoreType.DMA((3 * n,)), pltpu.SemaphoreType.DMA((3 * n,)),
                   *[pltpu.HBM(p.shape, p.dtype) for p in parts], *[pltpu.HBM(z.shape, z.dtype) for z in zones],
                   jax.ShapeDtypeStruct((8, LANES), F32)),
        in_specs=[HBM] * (2 * n), out_specs=(SEM, SEM, *[HBM] * (2 * n), pl.BlockSpec(memory_space=pltpu.VMEM)),
        input_output_aliases={i: 2 + i for i in range(2 * n)},
        compiler_params=pltpu.CompilerParams(has_side_effects=EFFECT),
    )(*[_in_hbm(p) for p in parts], *[_in_hbm(z) for z in zones])
    return dict(send=outs[0], recv=outs[1], srcs=list(outs[2:2 + n]), lands=list(outs[2 + n:2 + 2 * n]), token=outs[-1])


def scatter_wait(st, after, name):
    n = len(st["srcs"])

    def body(*refs):
        srcs, lands = refs[:n], refs[n:2 * n]
        send_sems, recv_sems = refs[2 * n], refs[2 * n + 1]
        x, y, c = _place()
        for a in range(n):
            for j, (px, py) in enumerate(_other_chips(x, y)):
                cp = pltpu.make_async_remote_copy(
                    src_ref=srcs[a].at[2 * px + py], dst_ref=lands[a].at[j], send_sem=send_sems.at[3 * a + j],
                    recv_sem=recv_sems.at[3 * a + j], device_id=(px, py, c), device_id_type=MESH)
                cp.wait_send()
                cp.wait_recv()

    outs = _pcall(
        body, name=name,
        out_shape=(*[pltpu.HBM(a.shape, a.dtype) for a in st["srcs"]], *[pltpu.HBM(z.shape, z.dtype) for z in st["lands"]]),
        in_specs=[HBM] * (2 * n) + [SEM, SEM, ANY], out_specs=tuple([HBM] * (2 * n)),
        input_output_aliases={i: i for i in range(2 * n)},
        compiler_params=pltpu.CompilerParams(has_side_effects=EFFECT),
    )(*st["srcs"], *st["lands"], st["send"], st["recv"], after)
    return list(outs[:n]), list(outs[n:])


def _unshard(g, axis):
    nd = g.ndim - 1
    axis = axis % nd
    t = jnp.moveaxis(g, 0, axis)
    shp = list(g.shape[1:])
    shp[axis] *= N_DEV
    return t.reshape(shp)


def _to_shards(full, axis):
    axis = axis % full.ndim
    shp = list(full.shape)
    shp[axis:axis + 1] = [N_DEV, shp[axis] // N_DEV]
    return jnp.moveaxis(full.reshape(shp), axis, 0)


def _pack(arrs, rows):
    flat = jnp.concatenate([a.reshape(-1).astype(F32) for a in arrs])
    return jnp.pad(flat, (0, rows * LANES - flat.shape[0])).reshape(rows, LANES)


def _unpack(buf, shapes):
    flat = buf.reshape(-1)
    out, off = [], 0
    for s in shapes:
        n = math.prod(s)
        out.append(flat[off:off + n].reshape(s))
        off += n
    return out


def _rows_for(shapes):
    n = sum(math.prod(s) for s in shapes)
    return -(-n // (8 * LANES)) * 8


def _row(v, width=None):
    v = v.reshape(1, -1).astype(F32)
    if width is not None and v.shape[1] < width:
        v = jnp.pad(v, ((0, 0), (0, width - v.shape[1])))
    return v


def _after(order, width):
    if not order:
        return None
    t = order[0][0:1, 0:1]
    for o in order[1:]:
        t = t + o[0:1, 0:1]
    return jnp.broadcast_to(t, (1, width))


def _norm_after(norm, order):
    row = _after(order, norm.shape[1])
    return norm if row is None else norm + row


FFN_TN = 256


def ffn_in(h, norm, w_gate, w_up, name):
    def body(h_ref, n_ref, wg_ref, wu_ref, u_ref, g_ref, up_ref, act_ref, u_s):
        @pl.when(pl.program_id(0) == 0)
        def _():
            x = h_ref[...]
            r = lax.rsqrt(jnp.mean(x * x, axis=-1, keepdims=True) + EPS)
            u_s[...] = (x * r * n_ref[...]).astype(BF16)
            u_ref[...] = u_s[...]

        for r0 in range(0, T, T // 2):
            rows = slice(r0, r0 + T // 2)
            u = u_s[rows, :]
            g = _dot(u, wg_ref[...], _NT)
            up = _dot(u, wu_ref[...], _NT)
            g_ref[rows, :] = g.astype(BF16)
            up_ref[rows, :] = up.astype(BF16)
            act_ref[rows, :] = (_silu(g) * up).astype(BF16)

    whole = pl.BlockSpec((T, D), lambda j: (0, 0))
    wspec = pl.BlockSpec((FFN_TN, D), lambda j: (j, 0))
    col = pl.BlockSpec((T, FFN_TN), lambda j: (0, j))
    return _pcall(
        body, name=name, grid=(DFF // FFN_TN,), in_specs=[whole, pl.BlockSpec((1, D), lambda j: (0, 0)), wspec, wspec],
        out_specs=[whole, col, col, col],
        out_shape=[jax.ShapeDtypeStruct((T, D), BF16)] + [jax.ShapeDtypeStruct((T, DFF), BF16)] * 3,
        scratch_shapes=[pltpu.VMEM((T, D), BF16)], compiler_params=_cparams(("arbitrary",)),
    )(h, norm, w_gate, w_up)


def ffn_back(dh_b, w_down, g, up, after_row, name):
    has_row = after_row is not None

    def body(*refs):
        dh_ref, wd_ref, g_ref, up_ref = refs[:4]
        dg_ref, dup_ref = refs[-2:]
        for r0 in range(0, T, T // 2):
            rows = slice(r0, r0 + T // 2)
            da = _dot(dh_ref[rows, :], wd_ref[...], _NT)
            if has_row:
                da = da + refs[4][...]
            g = g_ref[rows, :].astype(F32)
            dg_ref[rows, :] = (da * up_ref[rows, :].astype(F32) * _dsilu(g)).astype(BF16)
            dup_ref[rows, :] = (da * _silu(g)).astype(BF16)

    col = pl.BlockSpec((T, FFN_TN), lambda j: (0, j))
    in_specs = [pl.BlockSpec((T, D), lambda j: (0, 0)), pl.BlockSpec((FFN_TN, D), lambda j: (j, 0)), col, col]
    args = [dh_b, w_down, g, up]
    if has_row:
        in_specs.append(pl.BlockSpec((1, FFN_TN), lambda j: (0, j)))
        args.append(after_row)
    return _pcall(
        body, name=name, grid=(DFF // FFN_TN,), in_specs=in_specs, out_specs=[col, col],
        out_shape=[jax.ShapeDtypeStruct((T, DFF), BF16)] * 2, compiler_params=_cparams(("parallel",)),
    )(*args)


def ffn_layer_fwd(h, p, tag, order=()):
    u, g, up, act = ffn_in(h, _norm_after(p["norm"], order), p["w_gate"], p["w_up"], f"{tag}_in")
    h2 = matmul(act, p["w_down"], "nn", residual=h, name=f"{tag}_down")
    return h2, (h, u, g, up, act)


def ffn_layer_bwd(dh, dh_b, saved, p, tag, order=()):
    h, u, g, up, act = saved
    d_down = matmul(act, dh_b, "tn", out_dtype=BF16, name=f"{tag}_dwd")
    dg, dup = ffn_back(dh_b, p["w_down"], g, up, _after(order, DFF), f"{tag}_back")
    du = matmul(dg, p["w_gate"], "nn", name=f"{tag}_dug")
    du = matmul(dup, p["w_up"], "nn", residual=du, name=f"{tag}_duu")
    d_gate = matmul(dg, u, "tn", out_dtype=BF16, name=f"{tag}_dwg")
    d_up = matmul(dup, u, "tn", out_dtype=BF16, name=f"{tag}_dwu")
    dh2, dh2_b, d_norm = rms_bwd(du, h, p["norm"], dh, f"{tag}_drms")
    return dh2, dh2_b, dict(norm=d_norm, w_gate=d_gate, w_up=d_up, w_down=d_down)


def conv_layer_fwd(h, p, tag, order=()):
    u = rms_fwd(h, _norm_after(p["norm"], order), f"{tag}_rms")
    hh = matmul(u, p["w_pw1"], "nn", bias=p["b_pw1"], name=f"{tag}_pw1")
    gl = glu_fwd(hh, f"{tag}_glu")
    c2 = dwconv_fwd(gl, 0, p["dw_w"], p["dw_b"], KCV, 128, False, f"{tag}_dw")[0]
    s = ln_silu_fwd(c2, p["ln_g"], p["ln_b"], f"{tag}_ln")
    h2 = matmul(s, p["w_pw2"], "nn", bias=p["b_pw2"], residual=h, name=f"{tag}_pw2")
    return h2, (h, u, hh, gl, c2, s)


def conv_layer_bwd(dh, dh_b, saved, p, tag, order=()):
    h, u, hh, gl, c2, s = saved
    ds = matmul(dh_b, p["w_pw2"], "nt", bias=_after(order, D), name=f"{tag}_ds")
    d_pw2 = matmul(s, dh_b, "tn", out_dtype=BF16, name=f"{tag}_dwpw2")
    dc2, d_lng, d_lnb, d_bpw2 = ln_silu_bwd(ds, c2, dh, p["ln_g"], p["ln_b"], f"{tag}_dln")
    dgl, d_dww, d_dwb = dwconv_bwd(dc2, None, gl, 0, p["dw_w"], KCV, 128, False, F32, f"{tag}_ddw")
    dhh, d_bpw1 = glu_bwd(dgl, hh, f"{tag}_dglu")
    du = matmul(dhh, p["w_pw1"], "nt", name=f"{tag}_du")
    d_pw1 = matmul(u, dhh, "tn", out_dtype=BF16, name=f"{tag}_dwpw1")
    dh2, dh2_b, d_norm = rms_bwd(du, h, p["norm"], dh, f"{tag}_drms")
    grads = dict(norm=d_norm, w_pw1=d_pw1, b_pw1=d_bpw1, dw_w=d_dww[:KCV], dw_b=d_dwb, ln_g=d_lng, ln_b=d_lnb,
                 w_pw2=d_pw2, b_pw2=d_bpw2)
    return dh2, dh2_b, grads


def ssm_layer_fwd(h, p, tag, order=(), mid=None):
    u = rms_fwd(h, _norm_after(p["norm"], order), f"{tag}_rms")
    zx = matmul(u, p["w_in"], "nn", name=f"{tag}_in")
    cpre, xbc = dwconv_fwd(zx, DI // 512, p["conv_w"], p["conv_b"], KSSM, 512, True, f"{tag}_conv")
    dt, da, dtx, dax = dt_fwd(zx, p["dt_bias"], p["a_log"], f"{tag}_dt")
    daT = da[:, :NH].T
    y, states = ssd_fwd(xbc, dtx, dax, daT, p["d_full"], f"{tag}_ssd")
    gate_norm = p["gate_norm"] if mid is None else _norm_after(p["gate_norm"], mid(y))
    yn = gatenorm_fwd(y, zx, gate_norm, f"{tag}_gn")
    h2 = matmul(yn, p["w_out"], "nn", residual=h, name=f"{tag}_out")
    return h2, (h, u, zx, cpre, xbc, dt, dtx, dax, daT, y, states, yn)


def ssm_layer_bwd(dh, dh_b, saved, p, tag, order=()):
    h, u, zx, cpre, xbc, dt, dtx, dax, daT, y, states, yn = saved
    dyn = matmul(dh_b, p["w_out"], "nt", bias=_after(order, DI), name=f"{tag}_dyn")
    d_wout = matmul(yn, dh_b, "tn", out_dtype=BF16, name=f"{tag}_dwout")
    dy, dzx, d_gn = gatenorm_bwd(dyn, y, zx, p["gate_norm"], f"{tag}_dgn")
    dxbc, ddt, dda, dD = ssd_bwd(dy, xbc, dtx, dax, daT, p["d_full"], states, f"{tag}_dssd")
    dzx, d_dtb, d_alog = dt_bwd(ddt, dda, dt, zx, p["dt_bias"], p["a_log"], dzx, f"{tag}_ddt")
    dzx, d_cw, d_cb = dwconv_bwd(dxbc, cpre, zx, DI // 512, p["conv_w"], KSSM, 512, True, BF16, f"{tag}_dconv",
                                 into=(dzx, DI // 512))
    du = matmul(dzx, p["w_in"], "nt", name=f"{tag}_du")
    d_win = matmul(u, dzx, "tn", out_dtype=BF16, name=f"{tag}_dwin")
    dh2, dh2_b, d_norm = rms_bwd(du, h, p["norm"], dh, f"{tag}_drms")
    d_d = headsum(dD.reshape(NH, HD), f"{tag}_dD").reshape(NH)
    grads = dict(norm=d_norm, w_in=d_win[:, :DINP], conv_w=d_cw[:KSSM], conv_b=d_cb, dt_bias=d_dtb[0, :NH],
                 a_log=d_alog[0, :NH], d=d_d, gate_norm=d_gn, w_out=d_wout)
    return dh2, dh2_b, grads


BIG = ["ssm_w_in", "ssm_w_out", "cv_w_pw1", "cv_w_pw2", "ffn_w_gate", "ffn_w_up", "ffn_w_down"]
TRANSPOSED = ("ffn_w_gate", "ffn_w_up")
LAYER_AXIS = {"ssm_w_in": -1, "ssm_w_out": 0, "cv_w_pw1": -1, "cv_w_pw2": 0, "ffn_w_gate": 0, "ffn_w_up": 0,
              "ffn_w_down": 0}
SMALL_SHARDED = ["ssm_conv_w", "cv_norm", "cv_b_pw1", "cv_dw_w", "cv_dw_b", "cv_ln_g", "cv_ln_b", "cv_b_pw2"]
SMALL_REPL = ["ssm_norm", "ssm_conv_b", "ssm_dt_bias", "ssm_a_log", "ssm_d", "ssm_gate_norm", "ffn_norm", "final_norm"]
WEIGHTS = ["ssm_norm", "ssm_w_in", "ssm_conv_w", "ssm_conv_b", "ssm_dt_bias", "ssm_a_log", "ssm_d", "ssm_gate_norm",
           "ssm_w_out", "cv_norm", "cv_w_pw1", "cv_b_pw1", "cv_dw_w", "cv_dw_b", "cv_ln_g", "cv_ln_b", "cv_w_pw2",
           "cv_b_pw2", "ffn_norm", "ffn_w_gate", "ffn_w_up", "ffn_w_down", "final_norm"]
SMALL = [n for n in WEIGHTS if n not in BIG]


N_STAGES = 8
SYNC_SIBLING_STAGES = 2


def _stage_layer(s):
    i = s // 2
    if s % 2:
        return "ffn", i
    return ("ssm" if i % 2 == 0 else "cv"), i // 2


def _stage_group(s):
    fam, l = _stage_layer(s)
    names = {"ffn": ["ffn_w_gate", "ffn_w_up", "ffn_w_down"], "ssm": ["ssm_w_in", "ssm_w_out"],
             "cv": ["cv_w_pw1", "cv_w_pw2"]}[fam]
    return [(n, l) for n in names]


def _stage_params(s, big, small):
    fam, l = _stage_layer(s)
    if fam == "ffn":
        return dict(norm=_row(small["ffn_norm"][l]), w_gate=big["ffn_w_gate"], w_up=big["ffn_w_up"],
                    w_down=big["ffn_w_down"])
    if fam == "ssm":
        return dict(norm=_row(small["ssm_norm"][l]), w_in=jnp.pad(big["ssm_w_in"], ((0, 0), (0, DINP_PAD - DINP))),
                    conv_w=jnp.pad(small["ssm_conv_w"][l], ((0, 8 - KSSM), (0, 0))), conv_b=_row(small["ssm_conv_b"][l]),
                    dt_bias=_row(small["ssm_dt_bias"][l], LANES), a_log=_row(small["ssm_a_log"][l], LANES),
                    d_full=_row(jnp.repeat(small["ssm_d"][l], HD)), gate_norm=_row(small["ssm_gate_norm"][l]),
                    w_out=big["ssm_w_out"])
    return dict(norm=_row(small["cv_norm"][l]), w_pw1=big["cv_w_pw1"], b_pw1=_row(small["cv_b_pw1"][l]),
                dw_w=jnp.pad(small["cv_dw_w"][l], ((0, 32 - KCV), (0, 0))), dw_b=_row(small["cv_dw_b"][l]),
                ln_g=_row(small["cv_ln_g"][l]), ln_b=_row(small["cv_ln_b"][l]), w_pw2=big["cv_w_pw2"],
                b_pw2=_row(small["cv_b_pw2"][l]))


_STAGE_FWD = {"ffn": ffn_layer_fwd, "ssm": ssm_layer_fwd, "cv": conv_layer_fwd}
_STAGE_BWD = {"ffn": ffn_layer_bwd, "ssm": ssm_layer_bwd, "cv": conv_layer_bwd}


def _stage_fwd(s, h, p, order=(), mid=None):
    fam, l = _stage_layer(s)
    if mid is not None:
        return ssm_layer_fwd(h, p, f"{fam}{l}", order, mid)
    return _STAGE_FWD[fam](h, p, f"{fam}{l}", order)


def _stage_bwd(s, dh, dh_b, p, saved, order=()):
    fam, l = _stage_layer(s)
    dh, dh_b, g = _STAGE_BWD[fam](dh, dh_b, saved, p, f"{fam}{l}", order)
    return dh, dh_b, {f"{fam}_{k}": val for k, val in g.items()}


def _local(x, tgt, full):
    h, tape = x, []
    for s in range(N_STAGES):
        big = {n: (full[n][l].T if n in TRANSPOSED else full[n][l]) for n, l in _stage_group(s)}
        p = _stage_params(s, big, full)
        h, saved = _stage_fwd(s, h, p)
        tape.append((p, saved))
    dh, dh_b, d_final, loss_row = loss_head(h, _row(full["final_norm"]), tgt, "loss_head")
    gl = {n: [None] * full[n].shape[0] for n in WEIGHTS if n != "final_norm"}
    for s in reversed(range(N_STAGES)):
        dh, dh_b, g = _stage_bwd(s, dh, dh_b, *tape[s])
        for n, val in g.items():
            val = val.T if n in TRANSPOSED else val
            gl[n][_stage_layer(s)[1]] = val.reshape(full[n].shape[1:])
    grads = {n: jnp.stack(vs) for n, vs in gl.items()}
    grads["final_norm"] = d_final.reshape(D)
    return loss_row, dh, grads


def _step(x, tgt, w, m, v):
    idx = 4 * lax.axis_index("x") + 2 * lax.axis_index("y") + lax.axis_index("c")
    small_shapes = [w[n].shape for n in SMALL_SHARDED]
    small_pack = _pack([w[n] for n in SMALL_SHARDED], _rows_for(small_shapes))

    def view(n, a):
        return jnp.swapaxes(a, 1, 2) if n in TRANSPOSED else a

    wv, mv, vv = ({n: view(n, t[n]) for n in BIG} for t in (w, m, v))

    def blocks(s, zero=None):
        own = [wv[n][l] if zero is None else wv[n][l] + zero for n, l in _stage_group(s)]
        return [a.astype(BF16) for a in own] + ([small_pack] if s == 0 else [])

    arrs = [blocks(0)]
    first = gather_start(arrs[0], None, "gather0_start")
    arrs += [blocks(s, first["token"][0, 0]) for s in range(1, N_STAGES)]
    passing = pass_start(gather_wait(first, first["token"], "gather0_wait"), "pass0_start")
    crossing = {1: gather_start(arrs[1], passing["token"], "gather1_start")}
    crossing[2] = gather_start(arrs[2], crossing[1]["token"], "gather2_start")
    small = {n: w[n] for n in SMALL_REPL}
    flight = dict(passing=passing)

    def advance(s, after):
        tokens = []
        if s + 1 < N_STAGES:
            landed = gather_wait(crossing.pop(s + 1), after, f"gather{s + 1}_wait")
            flight["passing"] = pass_start(landed, f"pass{s + 1}_start")
            tokens.append(flight["passing"]["token"])
        if s + 3 < N_STAGES:
            crossing[s + 3] = gather_start(arrs[s + 3], flight["passing"]["token"], f"gather{s + 3}_start")
            tokens.append(crossing[s + 3]["token"])
        return tokens

    h, tape, after = x, [], crossing[2]["token"]
    for s in range(N_STAGES):
        zones = pass_wait(flight["passing"], after, f"pass{s}_wait")
        order = advance(s, zones[0]) if s else []
        mid = functools.partial(advance, 0) if s == 0 else None
        zones = [lax.dynamic_update_slice_in_dim(z, a[None], idx, 0) for z, a in zip(zones, arrs[s])]
        if s == 0:
            per_dev = [_unpack(zones[-1][k], small_shapes) for k in range(N_DEV)]
            for q, n in enumerate(SMALL_SHARDED):
                small[n] = _unshard(jnp.stack([per_dev[k][q] for k in range(N_DEV)]), -1)
        big = {n: _unshard(z, LAYER_AXIS[n]) for (n, _), z in zip(_stage_group(s), zones)}
        p = _stage_params(s, big, small)
        h, saved = _stage_fwd(s, h, p, order, mid)
        tape.append((p, saved))
        after = h

    dh, dh_b, d_final, loss_row = loss_head(h, _row(w["final_norm"]), tgt, "loss_head")

    out = {}
    small_g = {n: [None] * w[n].shape[0] for n in SMALL if n != "final_norm"}

    def finish(s, st, after):
        by_chip, recv = scatter_wait(st, after, f"scatter{s}_wait")
        names = [n for n, _ in _stage_group(s)]
        res = adamw_stage(recv, by_chip, [wv[n] for n in names], [mv[n] for n in names], [vv[n] for n in names],
                          [l for _, l in _stage_group(s)], [out.get(n) for n in names], f"adamw_stage{s}")
        out.update(zip(names, res))

    started, order, summed, to_sibling = [], [], None, None
    for s in reversed(range(N_STAGES)):
        dh, dh_b, g = _stage_bwd(s, dh, dh_b, *tape[s], order)
        order = []
        if to_sibling is not None:
            gsend, from_sibling = sibling_wait(to_sibling, dh, f"sibling{s + 1}_wait")
            by_chip = pair_sum(gsend, from_sibling, f"pair_sum{s + 1}")
            started.append((s + 1, scatter_start(by_chip, f"scatter{s + 1}_start")))
            order.append(started[-1][1]["token"])
            to_sibling = None
        for n, val in g.items():
            if n not in BIG:
                small_g[n][_stage_layer(s)[1]] = val.reshape(small[n].shape[1:])
        if s == 0:
            grads = {n: jnp.stack(vs) for n, vs in small_g.items()}
            grads["final_norm"] = d_final.reshape(D)
            small_full_shapes = [grads[n].shape for n in SMALL] + [(1,)]
            packed = _pack([grads[n] for n in SMALL] + [loss_row[0, :1]], _rows_for(small_full_shapes))
            summed = sum_leading(all_gather([packed], "gather_small_grads")[0], "sum_small_grads")
        gsend = [_to_shards(g[n], LAYER_AXIS[n]) for n, _ in _stage_group(s)]
        if s >= SYNC_SIBLING_STAGES:
            to_sibling = sibling_start(gsend, f"sibling{s}_start")
            order.append(to_sibling["token"])
            continue
        from_sibling = sibling_exchange(gsend, f"scatter{s}_sibling", summed)
        by_chip = pair_sum(gsend, from_sibling, f"pair_sum{s}")
        started.append((s, scatter_start(by_chip, f"scatter{s}_start")))
        order.append(started[-1][1]["token"])
    last = started[-1][1]["token"]
    for s, st in started[:-1]:
        finish(s, st, last)
    parts = _unpack(summed + last[0:1, 0:1], small_full_shapes)
    loss = parts[-1][0]
    mine = []
    for n, g in zip(SMALL, parts[:-1]):
        if n in SMALL_SHARDED:
            s = w[n].shape[-1]
            g = lax.dynamic_slice_in_dim(g, idx * s, s, axis=g.ndim - 1)
        mine.append(g)
    out.update(zip(SMALL, adamw_small(mine, [w[n] for n in SMALL], [m[n] for n in SMALL], [v[n] for n in SMALL],
                                      "adamw_small")))
    done = [out[n][0].reshape(-1)[:1] for n in out]
    finish(*started[-1], functools.reduce(jnp.add, done))
    for n in TRANSPOSED:
        out[n] = [view(n, a) for a in out[n]]
    return loss, dh, out


def kernel(x, ssm_norm, ssm_w_in, ssm_conv_w, ssm_conv_b, ssm_dt_bias, ssm_a_log, ssm_d, ssm_gate_norm, ssm_w_out, cv_norm, cv_w_pw1, cv_b_pw1, cv_dw_w, cv_dw_b, cv_ln_g, cv_ln_b, cv_w_pw2, cv_b_pw2, ffn_norm, ffn_w_gate, ffn_w_up, ffn_w_down, final_norm, loss_target, m_ssm_norm, m_ssm_w_in, m_ssm_conv_w, m_ssm_conv_b, m_ssm_dt_bias, m_ssm_a_log, m_ssm_d, m_ssm_gate_norm, m_ssm_w_out, m_cv_norm, m_cv_w_pw1, m_cv_b_pw1, m_cv_dw_w, m_cv_dw_b, m_cv_ln_g, m_cv_ln_b, m_cv_w_pw2, m_cv_b_pw2, m_ffn_norm, m_ffn_w_gate, m_ffn_w_up, m_ffn_w_down, m_final_norm, v_ssm_norm, v_ssm_w_in, v_ssm_conv_w, v_ssm_conv_b, v_ssm_dt_bias, v_ssm_a_log, v_ssm_d, v_ssm_gate_norm, v_ssm_w_out, v_cv_norm, v_cv_w_pw1, v_cv_b_pw1, v_cv_dw_w, v_cv_dw_b, v_cv_ln_g, v_cv_ln_b, v_cv_w_pw2, v_cv_b_pw2, v_ffn_norm, v_ffn_w_gate, v_ffn_w_up, v_ffn_w_down, v_final_norm):
    args = locals()
    w = {n: args[n] for n in WEIGHTS}
    m = {n: args["m_" + n] for n in WEIGHTS}
    v = {n: args["v_" + n] for n in WEIGHTS}
    loss, grad_x, out = _step(x[0], loss_target[0], w, m, v)
    res = [loss, grad_x[None]]
    for k in range(4):
        res += [out[n][k] for n in WEIGHTS]
    return tuple(res)
```
